```python
import jax, jax.numpy as jnp
from jax import lax
import numpy as np

D_MODEL = 1024
BATCH = 2
SEQ = 8192
DEPTH = 1
DEC_BATCH = 16
DEC_SEQ = 16
PAST_LEN = 2048

CHUNK = 64
N_HEADS_M = 4
HEAD_DIM = 128
M_WIDTH = N_HEADS_M * HEAD_DIM
CONV_CH = 512
CONV_WIDTH = 31
MIX_WIDTH = M_WIDTH + CONV_CH
D_FF = 2816
N_IN = 4 * M_WIDTH + 2 * N_HEADS_M + 2 * CONV_CH
EPS = 1e-6

kernel_name = "mlstm_conformer_conv_hybrid_stream_step"


def rmsnorm(x, g):
    xf = x.astype(jnp.float32)
    y = xf * lax.rsqrt(jnp.mean(xf * xf, axis=-1, keepdims=True) + EPS)
    return (y * g.astype(jnp.float32)).astype(x.dtype)


def layernorm(x, g, b):
    xf = x.astype(jnp.float32)
    mu = jnp.mean(xf, axis=-1, keepdims=True)
    var = jnp.mean(jnp.square(xf - mu), axis=-1, keepdims=True)
    y = (xf - mu) * lax.rsqrt(var + EPS)
    return (y * g.astype(jnp.float32) + b.astype(jnp.float32)).astype(x.dtype)


def swiglu(x, wg, wu, wd):
    return (jax.nn.silu(x @ wg) * (x @ wu)) @ wd


def mlstm_block(state, inp):
    C, n, m = state
    q, k, v, ig, lf = inp
    L = q.shape[2]
    b = jnp.cumsum(lf, axis=-1)
    D = b[..., :, None] - b[..., None, :] + ig[..., None, :]
    causal = jnp.tril(jnp.ones((L, L), dtype=bool))
    D = jnp.where(causal, D, -jnp.inf)
    inter = b + m[..., None]
    m_t = jnp.maximum(inter, jnp.max(D, axis=-1))
    w_inter = jnp.exp(inter - m_t)
    S = jnp.einsum('bhtd,bhsd->bhts', q, k) * jnp.exp(D - m_t[..., None])
    num = w_inter[..., None] * jnp.einsum('bhtd,bhde->bhte', q, C) + jnp.einsum('bhts,bhse->bhte', S, v)
    den = w_inter * jnp.einsum('bhtd,bhd->bht', q, n) + jnp.sum(S, axis=-1)
    h = num / jnp.maximum(jnp.abs(den), jnp.exp(-m_t))[..., None]
    m_new = m_t[..., -1]
    g_state = jnp.exp(b[..., -1] + m - m_new)
    g_rows = jnp.exp(b[..., -1:] - b + ig - m_new[..., None])
    C_new = g_state[..., None, None] * C + jnp.einsum('bhs,bhsd,bhse->bhde', g_rows, k, v)
    n_new = g_state[..., None] * n + jnp.einsum('bhs,bhsd->bhd', g_rows, k)
    return (C_new, n_new, m_new), h


def mlstm(q, k, v, ig, lf, state):
    B, H, T, _ = q.shape
    L = min(CHUNK, T)
    nc = T // L

    def blocks(a):
        return jnp.moveaxis(a.reshape(a.shape[:2] + (nc, L) + a.shape[3:]), 2, 0)

    new_state, h = lax.scan(mlstm_block, state, (blocks(q), blocks(k), blocks(v), blocks(ig), blocks(lf)))
    h = jnp.moveaxis(h, 0, 2).reshape(B, H, T, HEAD_DIM)
    return h, new_state


def causal_depthwise(full, w, bias):
    y = lax.conv_general_dilated(full, w[:, None, :].astype(full.dtype), window_strides=(1,), padding='VALID',
                                 dimension_numbers=('NWC', 'WIO', 'NWC'), feature_group_count=full.shape[-1])
    return y + bias


def encoder_layer(x, mstate, conv_hist, p):
    Bn, T, _ = x.shape
    h = rmsnorm(x, p['ffn1_pre_g'])
    x = x + 0.5 * rmsnorm(swiglu(h, p['ffn1_wg'], p['ffn1_wu'], p['ffn1_wd']), p['ffn1_post_g'])
    h = rmsnorm(x, p['mix_pre_g'])
    proj = h @ p['w_in']
    cuts = np.cumsum([M_WIDTH, M_WIDTH, M_WIDTH, M_WIDTH, N_HEADS_M, N_HEADS_M, CONV_CH])
    q, k, v, o, ig, fg, cv, cg = jnp.split(proj, cuts, axis=-1)

    def heads(a):
        return a.reshape(Bn, T, N_HEADS_M, HEAD_DIM).transpose(0, 2, 1, 3).astype(jnp.float32)

    qh = heads(q)
    kh = heads(k) * (HEAD_DIM ** -0.5)
    vh = heads(v)
    ig_f = (ig + p['b_igate']).astype(jnp.float32).transpose(0, 2, 1)
    lf = jax.nn.log_sigmoid((fg + p['b_fgate']).astype(jnp.float32)).transpose(0, 2, 1)
    hm, new_mstate = mlstm(qh, kh, vh, ig_f, lf, mstate)
    hm = hm.transpose(0, 2, 1, 3).reshape(Bn, T, M_WIDTH).astype(x.dtype)
    y_m = jax.nn.sigmoid(o) * hm
    u = cv * jax.nn.sigmoid(cg)
    full = jnp.concatenate([conv_hist.astype(u.dtype), u], axis=1)
    new_hist = full[:, -(CONV_WIDTH - 1):]
    c = causal_depthwise(full, p['conv_w'], p['conv_b'])
    c = jax.nn.silu(layernorm(c, p['conv_ln_g'], p['conv_ln_b']))
    mix = jnp.concatenate([y_m, c], axis=-1) @ p['w_out']
    x = x + rmsnorm(mix, p['mix_post_g'])
    h = rmsnorm(x, p['ffn2_pre_g'])
    x = x + 0.5 * rmsnorm(swiglu(h, p['ffn2_wg'], p['ffn2_wu'], p['ffn2_wd']), p['ffn2_post_g'])
    x = rmsnorm(x, p['final_g'])
    return x, new_mstate, new_hist


def setup_inputs(seed: int = 0) -> dict:
    key = jax.random.key(seed)
    ks = iter(jax.random.split(key, 40))

    def nrm(shape, scale):
        return jax.random.normal(next(ks), shape, jnp.float32) * scale

    def gain(n):
        return 1.0 + nrm((DEPTH, n), 0.01)

    f_bias = jnp.broadcast_to(jnp.linspace(3.0, 6.0, N_HEADS_M, dtype=jnp.float32), (DEPTH, N_HEADS_M))
    return {
        "x_prompt": nrm((BATCH, SEQ, D_MODEL), 1.0),
        "x_sample": nrm((DEC_BATCH, DEC_SEQ, D_MODEL), 1.0),
        "state_mlstm_C": nrm((DEPTH, DEC_BATCH, N_HEADS_M, HEAD_DIM, HEAD_DIM), 0.1),
        "state_mlstm_n": nrm((DEPTH, DEC_BATCH, N_HEADS_M, HEAD_DIM), 0.1),
        "state_mlstm_m": nrm((DEPTH, DEC_BATCH, N_HEADS_M), 1.0),
        "cache_conv": nrm((DEPTH, DEC_BATCH, CONV_WIDTH - 1, CONV_CH), 0.5),
        "ffn1_pre_g": gain(D_MODEL),
        "ffn1_wg": nrm((DEPTH, D_MODEL, D_FF), D_MODEL ** -0.5),
        "ffn1_wu": nrm((DEPTH, D_MODEL, D_FF), D_MODEL ** -0.5),
        "ffn1_wd": nrm((DEPTH, D_FF, D_MODEL), D_FF ** -0.5),
        "ffn1_post_g": gain(D_MODEL),
        "mix_pre_g": gain(D_MODEL),
        "w_in": nrm((DEPTH, D_MODEL, N_IN), D_MODEL ** -0.5),
        "b_igate": nrm((DEPTH, N_HEADS_M), 0.1),
        "b_fgate": f_bias + nrm((DEPTH, N_HEADS_M), 0.01),
        "conv_w": nrm((DEPTH, CONV_WIDTH, CONV_CH), CONV_WIDTH ** -0.5),
        "conv_b": nrm((DEPTH, CONV_CH), 0.01),
        "conv_ln_g": gain(CONV_CH),
        "conv_ln_b": nrm((DEPTH, CONV_CH), 0.01),
        "w_out": nrm((DEPTH, MIX_WIDTH, D_MODEL), MIX_WIDTH ** -0.5),
        "mix_post_g": gain(D_MODEL),
        "ffn2_pre_g": gain(D_MODEL),
        "ffn2_wg": nrm((DEPTH, D_MODEL, D_FF), D_MODEL ** -0.5),
        "ffn2_wu": nrm((DEPTH, D_MODEL, D_FF), D_MODEL ** -0.5),
        "ffn2_wd": nrm((DEPTH, D_FF, D_MODEL), D_FF ** -0.5),
        "ffn2_post_g": gain(D_MODEL),
        "final_g": gain(D_MODEL),
    }


def reference(x_prompt, x_sample, state_mlstm_C, state_mlstm_n, state_mlstm_m, cache_conv,
              ffn1_pre_g, ffn1_wg, ffn1_wu, ffn1_wd, ffn1_post_g, mix_pre_g, w_in, b_igate, b_fgate,
              conv_w, conv_b, conv_ln_g, conv_ln_b, w_out, mix_post_g,
              ffn2_pre_g, ffn2_wg, ffn2_wu, ffn2_wd, ffn2_post_g, final_g):
    yp, ys = x_prompt, x_sample
    Cp_l, np_l, mp_l, hp_l, Cs_l, ns_l, ms_l, hs_l = [], [], [], [], [], [], [], []
    for l in range(DEPTH):
        p = {
            'ffn1_pre_g': ffn1_pre_g[l], 'ffn1_wg': ffn1_wg[l], 'ffn1_wu': ffn1_wu[l], 'ffn1_wd': ffn1_wd[l],
            'ffn1_post_g': ffn1_post_g[l], 'mix_pre_g': mix_pre_g[l], 'w_in': w_in[l],
            'b_igate': b_igate[l], 'b_fgate': b_fgate[l], 'conv_w': conv_w[l], 'conv_b': conv_b[l],
            'conv_ln_g': conv_ln_g[l], 'conv_ln_b': conv_ln_b[l], 'w_out': w_out[l], 'mix_post_g': mix_post_g[l],
            'ffn2_pre_g': ffn2_pre_g[l], 'ffn2_wg': ffn2_wg[l], 'ffn2_wu': ffn2_wu[l], 'ffn2_wd': ffn2_wd[l],
            'ffn2_post_g': ffn2_post_g[l], 'final_g': final_g[l],
        }
        init_p = (jnp.zeros((BATCH, N_HEADS_M, HEAD_DIM, HEAD_DIM), jnp.float32),
                  jnp.zeros((BATCH, N_HEADS_M, HEAD_DIM), jnp.float32),
                  jnp.zeros((BATCH, N_HEADS_M), jnp.float32))
        hist_p = jnp.zeros((BATCH, CONV_WIDTH - 1, CONV_CH), yp.dtype)
        yp, (Cp, n_p, mp), hp = encoder_layer(yp, init_p, hist_p, p)
        init_s = (state_mlstm_C[l].astype(jnp.float32), state_mlstm_n[l].astype(jnp.float32),
                  state_mlstm_m[l].astype(jnp.float32))
        ys, (Cs, n_s, ms), hs = encoder_layer(ys, init_s, cache_conv[l], p)
        Cp_l.append(Cp); np_l.append(n_p); mp_l.append(mp); hp_l.append(hp)
        Cs_l.append(Cs); ns_l.append(n_s); ms_l.append(ms); hs_l.append(hs)
    new_C_prompt = jnp.stack(Cp_l)
    new_n_prompt = jnp.stack(np_l)
    new_m_prompt = jnp.stack(mp_l)
    new_conv_prompt = jnp.stack(hp_l)
    new_C_sample = jnp.stack(Cs_l)
    new_n_sample = jnp.stack(ns_l)
    new_m_sample = jnp.stack(ms_l)
    new_conv_sample = jnp.stack(hs_l)
    return (yp, ys, new_C_prompt, new_n_prompt, new_m_prompt, new_conv_prompt,
            new_C_sample, new_n_sample, new_m_sample, new_conv_sample)
```

```python
import functools

import jax
import jax.numpy as jnp
from jax import lax
from jax.experimental import pallas as pl
from jax.experimental.pallas import tpu as pltpu

D_MODEL = 1024
D_FF = 2816
N_HEADS = 4
HEAD_DIM = 128
M_WIDTH = N_HEADS * HEAD_DIM
CONV_CH = 512
CONV_WIDTH = 31
HIST = CONV_WIDTH - 1
HIST_PAD = 32
EPS = 1e-6
CHUNK = 128
AUG = 2 * HEAD_DIM

F32 = jnp.float32
BF16 = jnp.bfloat16
HIGHEST = lax.Precision.HIGHEST
NT_DIMS = (((1,), (1,)), ((), ()))

TOKEN_TILE = 256
MIXER_TILE = 512
VMEM_LIMIT_TOKENWISE = 56 * 1024 * 1024
VMEM_LIMIT_MIXER = 40 * 1024 * 1024


def _rms(x, g):
    return x * lax.rsqrt(jnp.mean(x * x, axis=-1, keepdims=True) + EPS) * g


def _dot(a, b):
    return jnp.dot(a, b, preferred_element_type=F32)


def _swiglu(h, wg_ref, wu_ref, wd_ref):
    a = _dot(h, wg_ref[...])
    b = _dot(h, wu_ref[...])
    s = (a * jax.nn.sigmoid(a) * b).astype(BF16)
    return _dot(s, wd_ref[...])


def _ffn_in_body(x_ref, g1_ref, wg_ref, wu_ref, wd_ref, gpost_ref, gmix_ref, wqvo_ref, wkt_ref, wconv_ref,
                 wgt_ref, gbias_ref, x1_ref, q_ref, kt_ref, v_ref, og_ref, u_ref, gt_ref):
    x = x_ref[...]
    h = _rms(x, g1_ref[...]).astype(BF16)
    d = _swiglu(h, wg_ref, wu_ref, wd_ref)
    x1 = x + 0.5 * _rms(d, gpost_ref[...])
    x1_ref[...] = x1
    h2 = _rms(x1, gmix_ref[...]).astype(BF16)
    qvo = _dot(h2, wqvo_ref[...])
    q_ref[...] = qvo[:, :M_WIDTH].astype(BF16)
    v_ref[...] = qvo[:, M_WIDTH:2 * M_WIDTH]
    og_ref[...] = jax.nn.sigmoid(qvo[:, 2 * M_WIDTH:])
    kt = lax.dot_general(wkt_ref[...], h2, NT_DIMS, preferred_element_type=F32) * (HEAD_DIM ** -0.5)
    kt_ref[...] = kt.astype(BF16)
    cc = _dot(h2, wconv_ref[...])
    u_ref[...] = cc[:, :CONV_CH] * jax.nn.sigmoid(cc[:, CONV_CH:])
    g = lax.dot_general(wgt_ref[...], h2, NT_DIMS, preferred_element_type=F32) + gbias_ref[...]
    row = lax.broadcasted_iota(jnp.int32, g.shape, 0)
    gt_ref[...] = jnp.where(row < N_HEADS, g, jax.nn.log_sigmoid(g))


def _ffn_out_body(mix_ref, x1_ref, wout_ref, gmp_ref, g2_ref, wg_ref, wu_ref, wd_ref, gpost_ref, gfin_ref, y_ref):
    o = _dot(mix_ref[...].astype(BF16), wout_ref[...])
    x2 = x1_ref[...] + _rms(o, gmp_ref[...])
    h = _rms(x2, g2_ref[...]).astype(BF16)
    d = _swiglu(h, wg_ref, wu_ref, wd_ref)
    x3 = x2 + 0.5 * _rms(d, gpost_ref[...])
    y_ref[...] = _rms(x3, gfin_ref[...])


def _const_spec(shape):
    nd = len(shape)
    return pl.BlockSpec(shape, lambda *_: (0,) * nd, pipeline_mode=pl.Buffered(1))


def _ffn_in(x2d, p, tm):
    m = x2d.shape[0]
    grid = (m // tm,)
    row = lambda w: pl.BlockSpec((tm, w), lambda i: (i, 0))
    col = lambda h: pl.BlockSpec((h, tm), lambda i: (0, i))
    gbias = jnp.broadcast_to(p["gbias"][:, None], (2 * N_HEADS, tm))
    consts = [p["ffn1_pre_g"], p["ffn1_wg"], p["ffn1_wu"], p["ffn1_wd"], p["ffn1_post_g"], p["mix_pre_g"],
              p["w_qvo"], p["w_kt"], p["w_conv"], p["w_gt"], gbias]
    return pl.pallas_call(
        _ffn_in_body,
        grid=grid,
        in_specs=[row(D_MODEL)] + [_const_spec(c.shape) for c in consts],
        out_specs=[row(D_MODEL), row(M_WIDTH), col(M_WIDTH), row(M_WIDTH), row(M_WIDTH), row(CONV_CH),
                   col(2 * N_HEADS)],
        out_shape=[jax.ShapeDtypeStruct((m, D_MODEL), F32), jax.ShapeDtypeStruct((m, M_WIDTH), BF16),
                   jax.ShapeDtypeStruct((M_WIDTH, m), BF16), jax.ShapeDtypeStruct((m, M_WIDTH), F32),
                   jax.ShapeDtypeStruct((m, M_WIDTH), F32), jax.ShapeDtypeStruct((m, CONV_CH), F32),
                   jax.ShapeDtypeStruct((2 * N_HEADS, m), F32)],
        compiler_params=pltpu.CompilerParams(dimension_semantics=("arbitrary",),
                                             vmem_limit_bytes=VMEM_LIMIT_TOKENWISE),
        name="ffn_in",
    )(x2d, *consts)


def _ffn_out(mix, x1, p, tm):
    m = mix.shape[0]
    row = pl.BlockSpec((tm, D_MODEL), lambda i: (i, 0))
    consts = [p["w_out"], p["mix_post_g"], p["ffn2_pre_g"], p["ffn2_wg"], p["ffn2_wu"], p["ffn2_wd"],
              p["ffn2_post_g"], p["final_g"]]
    return pl.pallas_call(
        _ffn_out_body,
        grid=(m // tm,),
        in_specs=[row, row] + [_const_spec(c.shape) for c in consts],
        out_specs=row,
        out_shape=jax.ShapeDtypeStruct((m, D_MODEL), F32),
        compiler_params=pltpu.CompilerParams(dimension_semantics=("arbitrary",),
                                             vmem_limit_bytes=VMEM_LIMIT_TOKENWISE),
        name="ffn_out",
    )(mix, x1, *consts)


def _conv_ln_swish(ubuf, row0, nrows, cw_ref, cb_ref, lg_ref, lb_ref):
    acc = jnp.broadcast_to(cb_ref[...], (nrows, CONV_CH))
    for j in range(CONV_WIDTH):
        start = row0 + HIST_PAD - HIST + j
        acc = acc + ubuf[start:start + nrows, :] * cw_ref[j:j + 1, :]
    mu = jnp.mean(acc, axis=-1, keepdims=True)
    xc = acc - mu
    var = jnp.mean(xc * xc, axis=-1, keepdims=True)
    y = xc * lax.rsqrt(var + EPS) * lg_ref[...] + lb_ref[...]
    return y * jax.nn.sigmoid(y)


def _mixer_prompt_body(q_ref, kt_ref, v_ref, og_ref, u_ref, gt_ref, cw_ref, cb_ref, lg_ref, lb_ref,
                       mix_ref, caug_out, m_out, hist_out, caug_s, m_s, ubuf, *, tt):
    L = CHUNK

    @pl.when(pl.program_id(1) == 0)
    def _init():
        caug_s[...] = jnp.zeros_like(caug_s)
        m_s[...] = jnp.zeros_like(m_s)
        ubuf[0:HIST_PAD, :] = jnp.zeros((HIST_PAD, CONV_CH), F32)

    r = lax.broadcasted_iota(jnp.int32, (L, L), 0)
    c = lax.broadcasted_iota(jnp.int32, (L, L), 1)
    caus = c <= r
    triu = (r <= c).astype(F32)
    ones = jnp.ones((L, HEAD_DIM), F32)
    caug = [caug_s[h] for h in range(N_HEADS)]
    mprev = [m_s[h] for h in range(N_HEADS)]
    for ck in range(tt // L):
        rows = slice(ck * L, (ck + 1) * L)
        gt = gt_ref[:, rows]
        b_rows = jnp.dot(gt, triu, precision=HIGHEST, preferred_element_type=F32)
        for h in range(N_HEADS):
            hs = slice(h * HEAD_DIM, (h + 1) * HEAD_DIM)
            b_r = b_rows[N_HEADS + h:N_HEADS + h + 1, :]
            a_r = gt[h:h + 1, :] - b_r
            arow = jnp.broadcast_to(a_r, (L, L))
            bcol = jnp.broadcast_to(b_r, (L, L)).T
            acol = arow.T
            dmat = jnp.where(caus, bcol + arow, -jnp.inf)
            m_t = jnp.maximum(bcol + mprev[h], jnp.max(dmat, axis=1, keepdims=True))
            qh = q_ref[rows, hs]
            kth = kt_ref[hs, rows]
            s = _dot(qh, kth) * jnp.exp(dmat - m_t)
            vaug = jnp.concatenate([v_ref[rows, hs], ones], axis=1)
            w_inter = jnp.exp(bcol + mprev[h] - m_t)
            qc = _dot(qh, caug[h].astype(BF16))
            sv = _dot(s.astype(BF16), vaug.astype(BF16))
            num = w_inter * qc[:, :HEAD_DIM] + sv[:, :HEAD_DIM]
            den = w_inter * qc[:, HEAD_DIM:] + sv[:, HEAD_DIM:]
            hh = num / jnp.maximum(jnp.abs(den), jnp.exp(-m_t))
            mix_ref[rows, hs] = og_ref[rows, hs] * hh
            m_new = m_t[L - 1:L, :]
            b_last = bcol[L - 1:L, :]
            g_state = jnp.exp(b_last + mprev[h] - m_new)
            g_rows = jnp.exp(b_last + acol - m_new)
            gv = (jnp.concatenate([g_rows, g_rows], axis=1) * vaug).astype(BF16)
            caug[h] = jnp.concatenate([g_state, g_state], axis=1) * caug[h] + _dot(kth, gv)
            mprev[h] = m_new
    for h in range(N_HEADS):
        caug_s[h] = caug[h]
        m_s[h] = mprev[h]
        caug_out[0, h] = caug[h]
        m_out[0, h] = mprev[h]

    ubuf[HIST_PAD:HIST_PAD + tt, :] = u_ref[...]
    rb = 64
    for k in range(tt // rb):
        mix_ref[k * rb:(k + 1) * rb, M_WIDTH:] = _conv_ln_swish(ubuf, k * rb, rb, cw_ref, cb_ref, lg_ref, lb_ref)
    tail = ubuf[tt:tt + HIST_PAD, :]
    hist_out[0] = tail
    ubuf[0:HIST_PAD, :] = tail


def _mixer_sample_body(q_ref, kt_ref, v_ref, og_ref, u_ref, gt_ref, caug0_ref, m0_ref, hist0_ref,
                       cw_ref, cb_ref, lg_ref, lb_ref, mix_ref, caug_out, m_out, hist_out, ubuf, *, nseq, L):
    R = nseq * L
    r = lax.broadcasted_iota(jnp.int32, (R, R), 0)
    c = lax.broadcasted_iota(jnp.int32, (R, R), 1)
    same = lax.div(r, L) == lax.div(c, L)
    caus = same & (c <= r)
    segtriu = jnp.where(same & (r <= c), 1.0, 0.0).astype(F32)
    segones = jnp.where(same, 1.0, 0.0).astype(F32)
    rowseq = lax.div(lax.broadcasted_iota(jnp.int32, (R, AUG), 0), L)
    ones = jnp.ones((R, HEAD_DIM), F32)
    gt = gt_ref[...]
    b_rows = jnp.dot(gt, segtriu, precision=HIGHEST, preferred_element_type=F32)
    tot_rows = jnp.dot(gt, segones, precision=HIGHEST, preferred_element_type=F32)
    for h in range(N_HEADS):
        hs = slice(h * HEAD_DIM, (h + 1) * HEAD_DIM)
        b_r = b_rows[N_HEADS + h:N_HEADS + h + 1, :]
        a_r = gt[h:h + 1, :] - b_r
        t_r = tot_rows[N_HEADS + h:N_HEADS + h + 1, :]
        arow = jnp.broadcast_to(a_r, (R, R))
        bcol = jnp.broadcast_to(b_r, (R, R)).T
        acol = arow.T
        tcol = jnp.broadcast_to(t_r, (R, R)).T
        mprev = m0_ref[h]
        mprev = jnp.concatenate([mprev, mprev], axis=1)
        dmat = jnp.where(caus, bcol + arow, -jnp.inf)
        m_t = jnp.maximum(bcol + mprev, jnp.max(dmat, axis=1, keepdims=True))
        dend = jnp.where(same, tcol + arow, -jnp.inf)
        m_new = jnp.maximum(tcol + mprev, jnp.max(dend, axis=1, keepdims=True))
        qh = q_ref[:, hs]
        kth = kt_ref[hs, :]
        s = _dot(qh, kth) * jnp.exp(dmat - m_t)
        vaug = jnp.concatenate([v_ref[:, hs], ones], axis=1)
        w_inter = jnp.exp(bcol + mprev - m_t)
        qc = jnp.zeros((R, AUG), F32)
        for i in range(nseq):
            qc = jnp.where(rowseq == i, _dot(qh, caug0_ref[i, h].astype(BF16)), qc)
        sv = _dot(s.astype(BF16), vaug.astype(BF16))
        num = w_inter[:, :HEAD_DIM] * qc[:, :HEAD_DIM] + sv[:, :HEAD_DIM]
        den = w_inter[:, :HEAD_DIM] * qc[:, HEAD_DIM:] + sv[:, HEAD_DIM:]
        hh = num / jnp.maximum(jnp.abs(den), jnp.exp(-m_t[:, :HEAD_DIM]))
        mix_ref[:, hs] = og_ref[:, hs] * hh
        g_state = jnp.exp(tcol + mprev - m_new)
        g_rows = jnp.exp(tcol + acol - m_new)
        gv = g_rows * vaug
        for i in range(nseq):
            gvi = jnp.where(rowseq == i, gv, 0.0).astype(BF16)
            caug_out[i, h] = g_state[i * L:i * L + 1, :] * caug0_ref[i, h] + _dot(kth, gvi)
            m_out[i, h] = m_new[i * L:i * L + 1, :HEAD_DIM]

    for i in range(nseq):
        ubuf[0:HIST_PAD, :] = hist0_ref[i]
        ubuf[HIST_PAD:HIST_PAD + L, :] = u_ref[i * L:(i + 1) * L, :]
        mix_ref[i * L:(i + 1) * L, M_WIDTH:] = _conv_ln_swish(ubuf, 0, L, cw_ref, cb_ref, lg_ref, lb_ref)
        hist_out[i] = ubuf[L:L + HIST_PAD, :]


def _mixer_prompt(q, kt, v, og, u, gt, p, batch, seq, tt):
    nt = seq // tt
    m = batch * seq
    row = lambda w: pl.BlockSpec((tt, w), lambda b, i: (b * nt + i, 0))
    col = lambda h: pl.BlockSpec((h, tt), lambda b, i: (0, b * nt + i))
    consts = [p["conv_w"], p["conv_b"], p["conv_ln_g"], p["conv_ln_b"]]
    return pl.pallas_call(
        functools.partial(_mixer_prompt_body, tt=tt),
        grid=(batch, nt),
        in_specs=[row(M_WIDTH), col(M_WIDTH), row(M_WIDTH), row(M_WIDTH), row(CONV_CH), col(2 * N_HEADS)]
                 + [_const_spec(c.shape) for c in consts],
        out_specs=[row(D_MODEL),
                   pl.BlockSpec((1, N_HEADS, HEAD_DIM, AUG), lambda b, i: (b, 0, 0, 0)),
                   pl.BlockSpec((1, N_HEADS, 1, HEAD_DIM), lambda b, i: (b, 0, 0, 0)),
                   pl.BlockSpec((1, HIST_PAD, CONV_CH), lambda b, i: (b, 0, 0))],
        out_shape=[jax.ShapeDtypeStruct((m, D_MODEL), F32),
                   jax.ShapeDtypeStruct((batch, N_HEADS, HEAD_DIM, AUG), F32),
                   jax.ShapeDtypeStruct((batch, N_HEADS, 1, HEAD_DIM), F32),
                   jax.ShapeDtypeStruct((batch, HIST_PAD, CONV_CH), F32)],
        scratch_shapes=[pltpu.VMEM((N_HEADS, HEAD_DIM, AUG), F32), pltpu.VMEM((N_HEADS, 1, HEAD_DIM), F32),
                        pltpu.VMEM((HIST_PAD + tt, CONV_CH), F32)],
        compiler_params=pltpu.CompilerParams(dimension_semantics=("arbitrary", "arbitrary"),
                                             vmem_limit_bytes=VMEM_LIMIT_MIXER),
        name="mixer_prompt",
    )(q, kt, v, og, u, gt, *consts)


def _mixer_sample(q, kt, v, og, u, gt, caug0, m0, hist0, p, nseq, L):
    rows = nseq * L
    args = [q, kt, v, og, u, gt, caug0, m0, hist0, p["conv_w"], p["conv_b"], p["conv_ln_g"], p["conv_ln_b"]]
    full = lambda a: pl.BlockSpec(a.shape, lambda i, nd=a.ndim: (0,) * nd)
    out_shape = [jax.ShapeDtypeStruct((rows, D_MODEL), F32),
                 jax.ShapeDtypeStruct((nseq, N_HEADS, HEAD_DIM, AUG), F32),
                 jax.ShapeDtypeStruct((nseq, N_HEADS, 1, HEAD_DIM), F32),
                 jax.ShapeDtypeStruct((nseq, HIST_PAD, CONV_CH), F32)]
    return pl.pallas_call(
        functools.partial(_mixer_sample_body, nseq=nseq, L=L),
        grid=(1,),
        in_specs=[full(a) for a in args],
        out_specs=[full(s) for s in out_shape],
        out_shape=out_shape,
        scratch_shapes=[pltpu.VMEM((HIST_PAD + L, CONV_CH), F32)],
        compiler_params=pltpu.CompilerParams(dimension_semantics=("arbitrary",),
                                             vmem_limit_bytes=VMEM_LIMIT_MIXER),
        name="mixer_sample",
    )(*args)


def _layer_params(l, ffn1_pre_g, ffn1_wg, ffn1_wu, ffn1_wd, ffn1_post_g, mix_pre_g, w_in, b_igate, b_fgate,
                  conv_w, conv_b, conv_ln_g, conv_ln_b, w_out, mix_post_g, ffn2_pre_g, ffn2_wg, ffn2_wu, ffn2_wd,
                  ffn2_post_g, final_g):
    vec = lambda a: a[l].astype(F32).reshape(1, -1)
    w = w_in[l]
    cuts = [0, M_WIDTH, 2 * M_WIDTH, 3 * M_WIDTH, 4 * M_WIDTH, 4 * M_WIDTH + N_HEADS, 4 * M_WIDTH + 2 * N_HEADS,
            4 * M_WIDTH + 2 * N_HEADS + CONV_CH, 4 * M_WIDTH + 2 * N_HEADS + 2 * CONV_CH]
    wq, wk, wv, wo, wi, wf, wcv, wcg = [w[:, a:b] for a, b in zip(cuts[:-1], cuts[1:])]
    return {
        "ffn1_pre_g": vec(ffn1_pre_g), "ffn1_post_g": vec(ffn1_post_g), "mix_pre_g": vec(mix_pre_g),
        "mix_post_g": vec(mix_post_g), "ffn2_pre_g": vec(ffn2_pre_g), "ffn2_post_g": vec(ffn2_post_g),
        "final_g": vec(final_g),
        "ffn1_wg": ffn1_wg[l].astype(BF16), "ffn1_wu": ffn1_wu[l].astype(BF16), "ffn1_wd": ffn1_wd[l].astype(BF16),
        "ffn2_wg": ffn2_wg[l].astype(BF16), "ffn2_wu": ffn2_wu[l].astype(BF16), "ffn2_wd": ffn2_wd[l].astype(BF16),
        "w_qvo": jnp.concatenate([wq, wv, wo], axis=1).astype(BF16),
        "w_kt": wk.T.astype(BF16),
        "w_conv": jnp.concatenate([wcv, wcg], axis=1).astype(BF16),
        "w_gt": jnp.concatenate([wi, wf], axis=1).T.astype(BF16),
        "gbias": jnp.concatenate([b_igate[l], b_fgate[l]]).astype(F32),
        "w_out": w_out[l].astype(BF16),
        "conv_w": conv_w[l].astype(F32), "conv_b": vec(conv_b), "conv_ln_g": vec(conv_ln_g),
        "conv_ln_b": vec(conv_ln_b),
    }


def _split_state(caug, m, hist):
    return caug[..., :HEAD_DIM], caug[..., HEAD_DIM], m[:, :, 0, 0], hist[:, HIST_PAD - HIST:, :]


def kernel(x_prompt, x_sample, state_mlstm_C, state_mlstm_n, state_mlstm_m, cache_conv, ffn1_pre_g, ffn1_wg,
           ffn1_wu, ffn1_wd, ffn1_post_g, mix_pre_g, w_in, b_igate, b_fgate, conv_w, conv_b, conv_ln_g, conv_ln_b,
           w_out, mix_post_g, ffn2_pre_g, ffn2_wg, ffn2_wu, ffn2_wd, ffn2_post_g, final_g):
    batch, seq, _ = x_prompt.shape
    nseq, dseq, _ = x_sample.shape
    depth = w_in.shape[0]
    yp = x_prompt.reshape(batch * seq, D_MODEL)
    ys = x_sample.reshape(nseq * dseq, D_MODEL)
    outs_p, outs_s = [], []
    for l in range(depth):
        p = _layer_params(l, ffn1_pre_g, ffn1_wg, ffn1_wu, ffn1_wd, ffn1_post_g, mix_pre_g, w_in, b_igate, b_fgate,
                          conv_w, conv_b, conv_ln_g, conv_ln_b, w_out, mix_post_g, ffn2_pre_g, ffn2_wg, ffn2_wu,
                          ffn2_wd, ffn2_post_g, final_g)
        x1, q, kt, v, og, u, gt = _ffn_in(yp, p, TOKEN_TILE)
        mix, caug, m, hist = _mixer_prompt(q, kt, v, og, u, gt, p, batch, seq, MIXER_TILE)
        yp = _ffn_out(mix, x1, p, TOKEN_TILE)
        outs_p.append(_split_state(caug, m, hist))
        x1, q, kt, v, og, u, gt = _ffn_in(ys, p, TOKEN_TILE)
        n0 = jnp.broadcast_to(state_mlstm_n[l].astype(F32)[..., None], (nseq, N_HEADS, HEAD_DIM, HEAD_DIM))
        caug0 = jnp.concatenate([state_mlstm_C[l].astype(F32), n0], axis=-1)
        m0 = jnp.broadcast_to(state_mlstm_m[l].astype(F32).T[:, :, None, None], (N_HEADS, nseq, dseq, HEAD_DIM))
        m0 = m0.reshape(N_HEADS, nseq * dseq, HEAD_DIM)
        hist0 = jnp.pad(cache_conv[l].astype(F32), ((0, 0), (HIST_PAD - HIST, 0), (0, 0)))
        mix, caug, m, hist = _mixer_sample(q, kt, v, og, u, gt, caug0, m0, hist0, p, nseq, dseq)
        ys = _ffn_out(mix, x1, p, TOKEN_TILE)
        outs_s.append(_split_state(caug, m, hist))
    stack = lambda outs, k: jnp.stack([o[k] for o in outs])
    return (yp.reshape(batch, seq, D_MODEL), ys.reshape(nseq, dseq, D_MODEL),
            stack(outs_p, 0), stack(outs_p, 1), stack(outs_p, 2), stack(outs_p, 3),
            stack(outs_s, 0), stack(outs_s, 1), stack(outs_s, 2), stack(outs_s, 3))
```

```python
import functools

import jax
import jax.numpy as jnp
from jax import lax
from jax.experimental import pallas as pl
from jax.experimental.pallas import tpu as pltpu

D_MODEL = 1024
D_FF = 2816
N_HEADS = 4
HEAD_DIM = 128
M_WIDTH = N_HEADS * HEAD_DIM
CONV_CH = 512
CONV_WIDTH = 31
HIST = CONV_WIDTH - 1
HIST_PAD = 32
SUBLANES = 8
EPS = 1e-6
CHUNK = 128
AUG = 2 * HEAD_DIM

F32 = jnp.float32
BF16 = jnp.bfloat16
HIGHEST = lax.Precision.HIGHEST
NT_DIMS = (((1,), (1,)), ((), ()))

TOKEN_TILE = 256
MIXER_TILE = 512
VMEM_LIMIT_TOKENWISE = 56 * 1024 * 1024
VMEM_LIMIT_MIXER = 40 * 1024 * 1024


def _rms(x, g):
    return x * lax.rsqrt(jnp.mean(x * x, axis=-1, keepdims=True) + EPS) * g


def _dot(a, b):
    return jnp.dot(a, b, preferred_element_type=F32)


def _swiglu(h, wg_ref, wu_ref, wd_ref):
    a = _dot(h, wg_ref[...])
    b = _dot(h, wu_ref[...])
    s = (a * jax.nn.sigmoid(a) * b).astype(BF16)
    return _dot(s, wd_ref[...])


def _ffn_in_body(x_ref, g1_ref, wg_ref, wu_ref, wd_ref, gpost_ref, gmix_ref, wqvo_ref, wkt_ref, wconv_ref,
                 wgt_ref, gbias_ref, x1_ref, q_ref, kt_ref, v_ref, og_ref, u_ref, gt_ref):
    x = x_ref[...]
    h = _rms(x, g1_ref[...]).astype(BF16)
    d = _swiglu(h, wg_ref, wu_ref, wd_ref)
    x1 = x + 0.5 * _rms(d, gpost_ref[...])
    x1_ref[...] = x1
    h2 = _rms(x1, gmix_ref[...]).astype(BF16)
    qvo = _dot(h2, wqvo_ref[...])
    q_ref[...] = qvo[:, :M_WIDTH].astype(BF16)
    v_ref[...] = qvo[:, M_WIDTH:2 * M_WIDTH]
    og_ref[...] = jax.nn.sigmoid(qvo[:, 2 * M_WIDTH:])
    kt = lax.dot_general(wkt_ref[...], h2, NT_DIMS, preferred_element_type=F32) * (HEAD_DIM ** -0.5)
    kt_ref[...] = kt.astype(BF16)
    cc = _dot(h2, wconv_ref[...])
    u_ref[...] = cc[:, :CONV_CH] * jax.nn.sigmoid(cc[:, CONV_CH:])
    g = lax.dot_general(wgt_ref[...], h2, NT_DIMS, preferred_element_type=F32) + gbias_ref[...]
    row = lax.broadcasted_iota(jnp.int32, g.shape, 0)
    gt_ref[...] = jnp.where(row < N_HEADS, g, jax.nn.log_sigmoid(g))


def _ffn_out_body(mix_ref, x1_ref, wout_ref, gmp_ref, g2_ref, wg_ref, wu_ref, wd_ref, gpost_ref, gfin_ref, y_ref):
    o = _dot(mix_ref[...].astype(BF16), wout_ref[...])
    x2 = x1_ref[...] + _rms(o, gmp_ref[...])
    h = _rms(x2, g2_ref[...]).astype(BF16)
    d = _swiglu(h, wg_ref, wu_ref, wd_ref)
    x3 = x2 + 0.5 * _rms(d, gpost_ref[...])
    y_ref[...] = _rms(x3, gfin_ref[...])


def _const_spec(shape):
    nd = len(shape)
    return pl.BlockSpec(shape, lambda *_: (0,) * nd, pipeline_mode=pl.Buffered(1))


def _ffn_in(x2d, p, tm):
    m = x2d.shape[0]
    grid = (m // tm,)
    row = lambda w: pl.BlockSpec((tm, w), lambda i: (i, 0))
    col = lambda h: pl.BlockSpec((h, tm), lambda i: (0, i))
    gbias = jnp.broadcast_to(p["gbias"][:, None], (2 * N_HEADS, tm))
    consts = [p["ffn1_pre_g"], p["ffn1_wg"], p["ffn1_wu"], p["ffn1_wd"], p["ffn1_post_g"], p["mix_pre_g"],
              p["w_qvo"], p["w_kt"], p["w_conv"], p["w_gt"], gbias]
    return pl.pallas_call(
        _ffn_in_body,
        grid=grid,
        in_specs=[row(D_MODEL)] + [_const_spec(c.shape) for c in consts],
        out_specs=[row(D_MODEL), row(M_WIDTH), col(M_WIDTH), row(M_WIDTH), row(M_WIDTH), row(CONV_CH),
                   col(2 * N_HEADS)],
        out_shape=[jax.ShapeDtypeStruct((m, D_MODEL), F32), jax.ShapeDtypeStruct((m, M_WIDTH), BF16),
                   jax.ShapeDtypeStruct((M_WIDTH, m), BF16), jax.ShapeDtypeStruct((m, M_WIDTH), F32),
                   jax.ShapeDtypeStruct((m, M_WIDTH), F32), jax.ShapeDtypeStruct((m, CONV_CH), F32),
                   jax.ShapeDtypeStruct((2 * N_HEADS, m), F32)],
        compiler_params=pltpu.CompilerParams(dimension_semantics=("arbitrary",),
                                             vmem_limit_bytes=VMEM_LIMIT_TOKENWISE),
        name="ffn_in",
    )(x2d, *consts)


def _ffn_out(mix, x1, p, tm):
    m = mix.shape[0]
    row = pl.BlockSpec((tm, D_MODEL), lambda i: (i, 0))
    consts = [p["w_out"], p["mix_post_g"], p["ffn2_pre_g"], p["ffn2_wg"], p["ffn2_wu"], p["ffn2_wd"],
              p["ffn2_post_g"], p["final_g"]]
    return pl.pallas_call(
        _ffn_out_body,
        grid=(m // tm,),
        in_specs=[row, row] + [_const_spec(c.shape) for c in consts],
        out_specs=row,
        out_shape=jax.ShapeDtypeStruct((m, D_MODEL), F32),
        compiler_params=pltpu.CompilerParams(dimension_semantics=("arbitrary",),
                                             vmem_limit_bytes=VMEM_LIMIT_TOKENWISE),
        name="ffn_out",
    )(mix, x1, *consts)


def _conv_ln_swish(ubuf, row0, nrows, cw_ref, cb_ref, lg_ref, lb_ref):
    acc = jnp.broadcast_to(cb_ref[...], (nrows, CONV_CH))
    first = HIST_PAD - HIST
    for res in range(SUBLANES):
        taps = [j for j in range(CONV_WIDTH) if (first + j) % SUBLANES == res]
        lo = (first + taps[0]) // SUBLANES * SUBLANES
        hi = (first + taps[-1]) // SUBLANES * SUBLANES
        win = ubuf[row0 + lo:row0 + hi + nrows + (SUBLANES if res else 0), :]
        if res:
            win = pltpu.roll(win, win.shape[0] - res, 0)
        for j in taps:
            off = first + j - res - lo
            acc = acc + win[off:off + nrows, :] * cw_ref[j:j + 1, :]
    mu = jnp.mean(acc, axis=-1, keepdims=True)
    xc = acc - mu
    var = jnp.mean(xc * xc, axis=-1, keepdims=True)
    y = xc * lax.rsqrt(var + EPS) * lg_ref[...] + lb_ref[...]
    return y * jax.nn.sigmoid(y)


def _mixer_prompt_body(q_ref, kt_ref, v_ref, og_ref, u_ref, gt_ref, cw_ref, cb_ref, lg_ref, lb_ref,
                       mix_ref, caug_out, m_out, hist_out, caug_s, m_s, ubuf, *, tt):
    L = CHUNK

    @pl.when(pl.program_id(1) == 0)
    def _init():
        caug_s[...] = jnp.zeros_like(caug_s)
        m_s[...] = jnp.zeros_like(m_s)
        ubuf[0:HIST_PAD, :] = jnp.zeros((HIST_PAD, CONV_CH), F32)

    r = lax.broadcasted_iota(jnp.int32, (L, L), 0)
    c = lax.broadcasted_iota(jnp.int32, (L, L), 1)
    caus = c <= r
    triu = (r <= c).astype(F32)
    ones = jnp.ones((L, HEAD_DIM), F32)
    caug = [caug_s[h] for h in range(N_HEADS)]
    mprev = [m_s[h] for h in range(N_HEADS)]
    for ck in range(tt // L):
        rows = slice(ck * L, (ck + 1) * L)
        gt = gt_ref[:, rows]
        b_rows = jnp.dot(gt, triu, precision=HIGHEST, preferred_element_type=F32)
        for h in range(N_HEADS):
            hs = slice(h * HEAD_DIM, (h + 1) * HEAD_DIM)
            b_r = b_rows[N_HEADS + h:N_HEADS + h + 1, :]
            a_r = gt[h:h + 1, :] - b_r
            arow = jnp.broadcast_to(a_r, (L, L))
            bcol = jnp.broadcast_to(b_r, (L, L)).T
            acol = arow.T
            dmat = jnp.where(caus, bcol + arow, -jnp.inf)
            m_t = jnp.maximum(bcol + mprev[h], jnp.max(dmat, axis=1, keepdims=True))
            qh = q_ref[rows, hs]
            kth = kt_ref[hs, rows]
            s = _dot(qh, kth) * jnp.exp(dmat - m_t)
            vaug = jnp.concatenate([v_ref[rows, hs], ones], axis=1)
            w_inter = jnp.exp(bcol + mprev[h] - m_t)
            qc = _dot(qh, caug[h].astype(BF16))
            sv = _dot(s.astype(BF16), vaug.astype(BF16))
            num = w_inter * qc[:, :HEAD_DIM] + sv[:, :HEAD_DIM]
            den = w_inter * qc[:, HEAD_DIM:] + sv[:, HEAD_DIM:]
            hh = num / jnp.maximum(jnp.abs(den), jnp.exp(-m_t))
            mix_ref[rows, hs] = og_ref[rows, hs] * hh
            m_new = m_t[L - 1:L, :]
            b_last = bcol[L - 1:L, :]
            g_state = jnp.exp(b_last + mprev[h] - m_new)
            g_rows = jnp.exp(b_last + acol - m_new)
            gv = (jnp.concatenate([g_rows, g_rows], axis=1) * vaug).astype(BF16)
            caug[h] = jnp.concatenate([g_state, g_state], axis=1) * caug[h] + _dot(kth, gv)
            mprev[h] = m_new
    for h in range(N_HEADS):
        caug_s[h] = caug[h]
        m_s[h] = mprev[h]
        caug_out[0, h] = caug[h]
        m_out[0, h] = mprev[h]

    ubuf[HIST_PAD:HIST_PAD + tt, :] = u_ref[...]
    rb = 128
    for k in range(tt // rb):
        mix_ref[k * rb:(k + 1) * rb, M_WIDTH:] = _conv_ln_swish(ubuf, k * rb, rb, cw_ref, cb_ref, lg_ref, lb_ref)
    tail = ubuf[tt:tt + HIST_PAD, :]
    hist_out[0] = tail
    ubuf[0:HIST_PAD, :] = tail


def _mixer_sample_body(q_ref, kt_ref, v_ref, og_ref, u_ref, gt_ref, caug0_ref, m0_ref, hist0_ref,
                       cw_ref, cb_ref, lg_ref, lb_ref, mix_ref, caug_out, m_out, hist_out, ubuf, *, nseq, L):
    R = nseq * L
    r = lax.broadcasted_iota(jnp.int32, (R, R), 0)
    c = lax.broadcasted_iota(jnp.int32, (R, R), 1)
    same = lax.div(r, L) == lax.div(c, L)
    caus = same & (c <= r)
    segtriu = jnp.where(same & (r <= c), 1.0, 0.0).astype(F32)
    segones = jnp.where(same, 1.0, 0.0).astype(F32)
    rowseq = lax.div(lax.broadcasted_iota(jnp.int32, (R, AUG), 0), L)
    ones = jnp.ones((R, HEAD_DIM), F32)
    gt = gt_ref[...]
    b_rows = jnp.dot(gt, segtriu, precision=HIGHEST, preferred_element_type=F32)
    tot_rows = jnp.dot(gt, segones, precision=HIGHEST, preferred_element_type=F32)
    for h in range(N_HEADS):
        hs = slice(h * HEAD_DIM, (h + 1) * HEAD_DIM)
        b_r = b_rows[N_HEADS + h:N_HEADS + h + 1, :]
        a_r = gt[h:h + 1, :] - b_r
        t_r = tot_rows[N_HEADS + h:N_HEADS + h + 1, :]
        arow = jnp.broadcast_to(a_r, (R, R))
        bcol = jnp.broadcast_to(b_r, (R, R)).T
        acol = arow.T
        tcol = jnp.broadcast_to(t_r, (R, R)).T
        mprev = m0_ref[h]
        mprev = jnp.concatenate([mprev, mprev], axis=1)
        dmat = jnp.where(caus, bcol + arow, -jnp.inf)
        m_t = jnp.maximum(bcol + mprev, jnp.max(dmat, axis=1, keepdims=True))
        dend = jnp.where(same, tcol + arow, -jnp.inf)
        m_new = jnp.maximum(tcol + mprev, jnp.max(dend, axis=1, keepdims=True))
        qh = q_ref[:, hs]
        kth = kt_ref[hs, :]
        s = _dot(qh, kth) * jnp.exp(dmat - m_t)
        vaug = jnp.concatenate([v_ref[:, hs], ones], axis=1)
        w_inter = jnp.exp(bcol + mprev - m_t)
        qc = jnp.zeros((R, AUG), F32)
        for i in range(nseq):
            qc = jnp.where(rowseq == i, _dot(qh, caug0_ref[i, h].astype(BF16)), qc)
        sv = _dot(s.astype(BF16), vaug.astype(BF16))
        num = w_inter[:, :HEAD_DIM] * qc[:, :HEAD_DIM] + sv[:, :HEAD_DIM]
        den = w_inter[:, :HEAD_DIM] * qc[:, HEAD_DIM:] + sv[:, HEAD_DIM:]
        hh = num / jnp.maximum(jnp.abs(den), jnp.exp(-m_t[:, :HEAD_DIM]))
        mix_ref[:, hs] = og_ref[:, hs] * hh
        g_state = jnp.exp(tcol + mprev - m_new)
        g_rows = jnp.exp(tcol + acol - m_new)
        gv = g_rows * vaug
        for i in range(nseq):
            gvi = jnp.where(rowseq == i, gv, 0.0).astype(BF16)
            caug_out[i, h] = g_state[i * L:i * L + 1, :] * caug0_ref[i, h] + _dot(kth, gvi)
            m_out[i, h] = m_new[i * L:i * L + 1, :HEAD_DIM]

    for i in range(nseq):
        ubuf[0:HIST_PAD, :] = hist0_ref[i]
        ubuf[HIST_PAD:HIST_PAD + L, :] = u_ref[i * L:(i + 1) * L, :]
        mix_ref[i * L:(i + 1) * L, M_WIDTH:] = _conv_ln_swish(ubuf, 0, L, cw_ref, cb_ref, lg_ref, lb_ref)
        hist_out[i] = ubuf[L:L + HIST_PAD, :]


def _mixer_prompt(q, kt, v, og, u, gt, p, batch, seq, tt):
    nt = seq // tt
    m = batch * seq
    row = lambda w: pl.BlockSpec((tt, w), lambda b, i: (b * nt + i, 0))
    col = lambda h: pl.BlockSpec((h, tt), lambda b, i: (0, b * nt + i))
    consts = [p["conv_w"], p["conv_b"], p["conv_ln_g"], p["conv_ln_b"]]
    return pl.pallas_call(
        functools.partial(_mixer_prompt_body, tt=tt),
        grid=(batch, nt),
        in_specs=[row(M_WIDTH), col(M_WIDTH), row(M_WIDTH), row(M_WIDTH), row(CONV_CH), col(2 * N_HEADS)]
                 + [_const_spec(c.shape) for c in consts],
        out_specs=[row(D_MODEL),
                   pl.BlockSpec((1, N_HEADS, HEAD_DIM, AUG), lambda b, i: (b, 0, 0, 0)),
                   pl.BlockSpec((1, N_HEADS, 1, HEAD_DIM), lambda b, i: (b, 0, 0, 0)),
                   pl.BlockSpec((1, HIST_PAD, CONV_CH), lambda b, i: (b, 0, 0))],
        out_shape=[jax.ShapeDtypeStruct((m, D_MODEL), F32),
                   jax.ShapeDtypeStruct((batch, N_HEADS, HEAD_DIM, AUG), F32),
                   jax.ShapeDtypeStruct((batch, N_HEADS, 1, HEAD_DIM), F32),
                   jax.ShapeDtypeStruct((batch, HIST_PAD, CONV_CH), F32)],
        scratch_shapes=[pltpu.VMEM((N_HEADS, HEAD_DIM, AUG), F32), pltpu.VMEM((N_HEADS, 1, HEAD_DIM), F32),
                        pltpu.VMEM((HIST_PAD + tt, CONV_CH), F32)],
        compiler_params=pltpu.CompilerParams(dimension_semantics=("arbitrary", "arbitrary"),
                                             vmem_limit_bytes=VMEM_LIMIT_MIXER),
        name="mixer_prompt",
    )(q, kt, v, og, u, gt, *consts)


def _mixer_sample(q, kt, v, og, u, gt, caug0, m0, hist0, p, nseq, L):
    rows = nseq * L
    args = [q, kt, v, og, u, gt, caug0, m0, hist0, p["conv_w"], p["conv_b"], p["conv_ln_g"], p["conv_ln_b"]]
    full = lambda a: pl.BlockSpec(a.shape, lambda i, nd=a.ndim: (0,) * nd)
    out_shape = [jax.ShapeDtypeStruct((rows, D_MODEL), F32),
                 jax.ShapeDtypeStruct((nseq, N_HEADS, HEAD_DIM, AUG), F32),
                 jax.ShapeDtypeStruct((nseq, N_HEADS, 1, HEAD_DIM), F32),
                 jax.ShapeDtypeStruct((nseq, HIST_PAD, CONV_CH), F32)]
    return pl.pallas_call(
        functools.partial(_mixer_sample_body, nseq=nseq, L=L),
        grid=(1,),
        in_specs=[full(a) for a in args],
        out_specs=[full(s) for s in out_shape],
        out_shape=out_shape,
        scratch_shapes=[pltpu.VMEM((HIST_PAD + L, CONV_CH), F32)],
        compiler_params=pltpu.CompilerParams(dimension_semantics=("arbitrary",),
                                             vmem_limit_bytes=VMEM_LIMIT_MIXER),
        name="mixer_sample",
    )(*args)


def _layer_params(l, ffn1_pre_g, ffn1_wg, ffn1_wu, ffn1_wd, ffn1_post_g, mix_pre_g, w_in, b_igate, b_fgate,
                  conv_w, conv_b, conv_ln_g, conv_ln_b, w_out, mix_post_g, ffn2_pre_g, ffn2_wg, ffn2_wu, ffn2_wd,
                  ffn2_post_g, final_g):
    vec = lambda a: a[l].astype(F32).reshape(1, -1)
    w = w_in[l]
    cuts = [0, M_WIDTH, 2 * M_WIDTH, 3 * M_WIDTH, 4 * M_WIDTH, 4 * M_WIDTH + N_HEADS, 4 * M_WIDTH + 2 * N_HEADS,
            4 * M_WIDTH + 2 * N_HEADS + CONV_CH, 4 * M_WIDTH + 2 * N_HEADS + 2 * CONV_CH]
    wq, wk, wv, wo, wi, wf, wcv, wcg = [w[:, a:b] for a, b in zip(cuts[:-1], cuts[1:])]
    return {
        "ffn1_pre_g": vec(ffn1_pre_g), "ffn1_post_g": vec(ffn1_post_g), "mix_pre_g": vec(mix_pre_g),
        "mix_post_g": vec(mix_post_g), "ffn2_pre_g": vec(ffn2_pre_g), "ffn2_post_g": vec(ffn2_post_g),
        "final_g": vec(final_g),
        "ffn1_wg": ffn1_wg[l].astype(BF16), "ffn1_wu": ffn1_wu[l].astype(BF16), "ffn1_wd": ffn1_wd[l].astype(BF16),
        "ffn2_wg": ffn2_wg[l].astype(BF16), "ffn2_wu": ffn2_wu[l].astype(BF16), "ffn2_wd": ffn2_wd[l].astype(BF16),
        "w_qvo": jnp.concatenate([wq, wv, wo], axis=1).astype(BF16),
        "w_kt": wk.T.astype(BF16),
        "w_conv": jnp.concatenate([wcv, wcg], axis=1).astype(BF16),
        "w_gt": jnp.concatenate([wi, wf], axis=1).T.astype(BF16),
        "gbias": jnp.concatenate([b_igate[l], b_fgate[l]]).astype(F32),
        "w_out": w_out[l].astype(BF16),
        "conv_w": conv_w[l].astype(F32), "conv_b": vec(conv_b), "conv_ln_g": vec(conv_ln_g),
        "conv_ln_b": vec(conv_ln_b),
    }


def _split_state(caug, m, hist):
    return caug[..., :HEAD_DIM], caug[..., HEAD_DIM], m[:, :, 0, 0], hist[:, HIST_PAD - HIST:, :]


def kernel(x_prompt, x_sample, state_mlstm_C, state_mlstm_n, state_mlstm_m, cache_conv, ffn1_pre_g, ffn1_wg,
           ffn1_wu, ffn1_wd, ffn1_post_g, mix_pre_g, w_in, b_igate, b_fgate, conv_w, conv_b, conv_ln_g, conv_ln_b,
           w_out, mix_post_g, ffn2_pre_g, ffn2_wg, ffn2_wu, ffn2_wd, ffn2_post_g, final_g):
    batch, seq, _ = x_prompt.shape
    nseq, dseq, _ = x_sample.shape
    depth = w_in.shape[0]
    yp = x_prompt.reshape(batch * seq, D_MODEL)
    ys = x_sample.reshape(nseq * dseq, D_MODEL)
    outs_p, outs_s = [], []
    for l in range(depth):
        p = _layer_params(l, ffn1_pre_g, ffn1_wg, ffn1_wu, ffn1_wd, ffn1_post_g, mix_pre_g, w_in, b_igate, b_fgate,
                          conv_w, conv_b, conv_ln_g, conv_ln_b, w_out, mix_post_g, ffn2_pre_g, ffn2_wg, ffn2_wu,
                          ffn2_wd, ffn2_post_g, final_g)
        x1, q, kt, v, og, u, gt = _ffn_in(yp, p, TOKEN_TILE)
        mix, caug, m, hist = _mixer_prompt(q, kt, v, og, u, gt, p, batch, seq, MIXER_TILE)
        yp = _ffn_out(mix, x1, p, TOKEN_TILE)
        outs_p.append(_split_state(caug, m, hist))
        x1, q, kt, v, og, u, gt = _ffn_in(ys, p, TOKEN_TILE)
        n0 = jnp.broadcast_to(state_mlstm_n[l].astype(F32)[..., None], (nseq, N_HEADS, HEAD_DIM, HEAD_DIM))
        caug0 = jnp.concatenate([state_mlstm_C[l].astype(F32), n0], axis=-1)
        m0 = jnp.broadcast_to(state_mlstm_m[l].astype(F32).T[:, :, None, None], (N_HEADS, nseq, dseq, HEAD_DIM))
        m0 = m0.reshape(N_HEADS, nseq * dseq, HEAD_DIM)
        hist0 = jnp.pad(cache_conv[l].astype(F32), ((0, 0), (HIST_PAD - HIST, 0), (0, 0)))
        mix, caug, m, hist = _mixer_sample(q, kt, v, og, u, gt, caug0, m0, hist0, p, nseq, dseq)
        ys = _ffn_out(mix, x1, p, TOKEN_TILE)
        outs_s.append(_split_state(caug, m, hist))
    stack = lambda outs, k: jnp.stack([o[k] for o in outs])
    return (yp.reshape(batch, seq, D_MODEL), ys.reshape(nseq, dseq, D_MODEL),
            stack(outs_p, 0), stack(outs_p, 1), stack(outs_p, 2), stack(outs_p, 3),
            stack(outs_s, 0), stack(outs_s, 1), stack(outs_s, 2), stack(outs_s, 3))
```

```python
import functools

import jax
import jax.numpy as jnp
from jax import lax
from jax.experimental import pallas as pl
from jax.experimental.pallas import tpu as pltpu

D_MODEL = 1024
D_FF = 2816
N_HEADS = 4
HEAD_DIM = 128
M_WIDTH = N_HEADS * HEAD_DIM
CONV_CH = 512
CONV_WIDTH = 31
HIST = CONV_WIDTH - 1
HIST_PAD = 32
SUBLANES = 8
EPS = 1e-6
CHUNK = 128
AUG = 2 * HEAD_DIM
CONV_ROWS = 128
LANES = 128
LN_ROWS = 32

F32 = jnp.float32
BF16 = jnp.bfloat16
HIGHEST = lax.Precision.HIGHEST
NT_DIMS = (((1,), (1,)), ((), ()))

TOKEN_TILE = 256
VMEM_LIMIT_TOKENWISE = 56 * 1024 * 1024
VMEM_LIMIT_MIXER = 40 * 1024 * 1024


def _rms(x, g):
    return x * lax.rsqrt(jnp.mean(x * x, axis=-1, keepdims=True) + EPS) * g


def _dot(a, b):
    return jnp.dot(a, b, preferred_element_type=F32)


def _swiglu(h, wg_ref, wu_ref, wd_ref):
    a = _dot(h, wg_ref[...])
    b = _dot(h, wu_ref[...])
    s = (a * jax.nn.sigmoid(a) * b).astype(BF16)
    return _dot(s, wd_ref[...])


def _ffn_in_values(x, g1_ref, wg_ref, wu_ref, wd_ref, gpost_ref, gmix_ref, wqvo_ref, wkt_ref, wconv_ref,
                   wgt_ref, gbias_ref):
    h = _rms(x, g1_ref[...]).astype(BF16)
    d = _swiglu(h, wg_ref, wu_ref, wd_ref)
    x1 = x + 0.5 * _rms(d, gpost_ref[...])
    h2 = _rms(x1, gmix_ref[...]).astype(BF16)
    qvo = _dot(h2, wqvo_ref[...])
    q = qvo[:, :M_WIDTH].astype(BF16)
    v = qvo[:, M_WIDTH:2 * M_WIDTH]
    og = jax.nn.sigmoid(qvo[:, 2 * M_WIDTH:])
    kt = lax.dot_general(wkt_ref[...], h2, NT_DIMS, preferred_element_type=F32) * (HEAD_DIM ** -0.5)
    cc = _dot(h2, wconv_ref[...])
    u = cc[:, :CONV_CH] * jax.nn.sigmoid(cc[:, CONV_CH:])
    g = lax.dot_general(wgt_ref[...], h2, NT_DIMS, preferred_element_type=F32) + gbias_ref[...]
    row = lax.broadcasted_iota(jnp.int32, g.shape, 0)
    gt = jnp.where(row < N_HEADS, g, jax.nn.log_sigmoid(g))
    return x1, q, kt.astype(BF16), v, og, u, gt


def _ffn_out_values(ym, c, x1, wout_ref, gmp_ref, g2_ref, wg_ref, wu_ref, wd_ref, gpost_ref, gfin_ref):
    o = _dot(ym.astype(BF16), wout_ref[0:M_WIDTH, :]) + _dot(c.astype(BF16), wout_ref[M_WIDTH:, :])
    x2 = x1 + _rms(o, gmp_ref[...])
    h = _rms(x2, g2_ref[...]).astype(BF16)
    d = _swiglu(h, wg_ref, wu_ref, wd_ref)
    x3 = x2 + 0.5 * _rms(d, gpost_ref[...])
    return _rms(x3, gfin_ref[...])


def _conv_taps(ubuf, row0, nrows, lanes, cw_ref, cb_ref):
    acc = jnp.broadcast_to(cb_ref[:, lanes], (nrows, lanes.stop - lanes.start))
    first = HIST_PAD - HIST
    for res in range(SUBLANES):
        taps = [j for j in range(CONV_WIDTH) if (first + j) % SUBLANES == res]
        lo = (first + taps[0]) // SUBLANES * SUBLANES
        hi = (first + taps[-1]) // SUBLANES * SUBLANES
        win = ubuf[row0 + lo:row0 + hi + nrows + (SUBLANES if res else 0), lanes]
        if res:
            win = pltpu.roll(win, win.shape[0] - res, 0)
        for j in taps:
            off = first + j - res - lo
            acc = acc + win[off:off + nrows, :] * cw_ref[j:j + 1, lanes]
    return acc


def _ln_swish(acc, lg_ref, lb_ref):
    mu = jnp.mean(acc, axis=-1, keepdims=True)
    xc = acc - mu
    var = jnp.mean(xc * xc, axis=-1, keepdims=True)
    y = xc * lax.rsqrt(var + EPS) * lg_ref[...] + lb_ref[...]
    return y * jax.nn.sigmoid(y)


def _mlstm_chunk_head(b_r, a_r, caus, qh, kth, vh, caug, mprev):
    L = b_r.shape[1]
    arow = jnp.broadcast_to(a_r, (L, L))
    bcol = jnp.broadcast_to(b_r, (L, L)).T
    acol = arow.T
    dmat = jnp.where(caus, bcol + arow, -jnp.inf)
    m_t = jnp.maximum(bcol + mprev, jnp.max(dmat, axis=1, keepdims=True))
    s = _dot(qh, kth) * jnp.exp(dmat - m_t)
    vaug = jnp.concatenate([vh, jnp.ones_like(vh)], axis=1)
    w_inter = jnp.exp(bcol + mprev - m_t)
    qc = _dot(qh, caug.astype(BF16))
    sv = _dot(s.astype(BF16), vaug.astype(BF16))
    num = w_inter * qc[:, :HEAD_DIM] + sv[:, :HEAD_DIM]
    den = w_inter * qc[:, HEAD_DIM:] + sv[:, HEAD_DIM:]
    hh = num / jnp.maximum(jnp.abs(den), jnp.exp(-m_t))
    m_new = m_t[L - 1:L, :]
    b_last = bcol[L - 1:L, :]
    g_state = jnp.exp(b_last + mprev - m_new)
    g_rows = jnp.exp(b_last + acol - m_new)
    gv = (jnp.concatenate([g_rows, g_rows], axis=1) * vaug).astype(BF16)
    caug_new = jnp.concatenate([g_state, g_state], axis=1) * caug + _dot(kth, gv)
    return hh, caug_new, m_new


def _ffn_in_mlstm_body(x_ref, g1_ref, wg_ref, wu_ref, wd_ref, gpost_ref, gmix_ref, wqvo_ref, wkt_ref, wconv_ref,
                       wgt_ref, gbias_ref, x1_ref, u_ref, ym_ref, caug_out, m_out,
                       q_st, kt_st, v_st, og_st, g_st, caug_s, m_s, *, tm, nt):
    g = pl.program_id(0)
    wslot = lax.rem(g, 2)
    rslot = 1 - wslot
    L = CHUNK

    @pl.when(g == 0)
    def _init():
        for st in (q_st, kt_st, v_st, og_st, g_st, caug_s, m_s):
            st[...] = jnp.zeros_like(st)

    fresh = lax.rem(g - 1, nt) == 0
    caug = [jnp.where(fresh, 0.0, caug_s[h]) for h in range(N_HEADS)]
    mprev = [jnp.where(fresh, 0.0, m_s[h]) for h in range(N_HEADS)]
    r = lax.broadcasted_iota(jnp.int32, (L, L), 0)
    c = lax.broadcasted_iota(jnp.int32, (L, L), 1)
    caus = c <= r
    triu = (r <= c).astype(F32)
    for ck in range(tm // L):
        rows = slice(ck * L, (ck + 1) * L)
        gt = g_st[rslot, :, rows]
        b_rows = jnp.dot(gt, triu, precision=HIGHEST, preferred_element_type=F32)
        for h in range(N_HEADS):
            hs = slice(h * HEAD_DIM, (h + 1) * HEAD_DIM)
            b_r = b_rows[N_HEADS + h:N_HEADS + h + 1, :]
            a_r = gt[h:h + 1, :] - b_r
            hh, caug[h], mprev[h] = _mlstm_chunk_head(b_r, a_r, caus, q_st[rslot, rows, hs], kt_st[rslot, hs, rows],
                                                      v_st[rslot, rows, hs], caug[h], mprev[h])
            ym_ref[rows, hs] = og_st[rslot, rows, hs] * hh
    for h in range(N_HEADS):
        caug_s[h] = caug[h]
        m_s[h] = mprev[h]
        caug_out[0, h] = caug[h]
        m_out[0, h] = mprev[h]

    x1, q, kt, v, og, u, gt = _ffn_in_values(x_ref[...], g1_ref, wg_ref, wu_ref, wd_ref, gpost_ref, gmix_ref,
                                             wqvo_ref, wkt_ref, wconv_ref, wgt_ref, gbias_ref)
    x1_ref[...] = x1
    u_ref[...] = u
    q_st[wslot] = q
    kt_st[wslot] = kt
    v_st[wslot] = v
    og_st[wslot] = og
    g_st[wslot] = gt


def _conv_ffn_out_body(u_ref, ym_ref, x1_ref, cw_ref, cb_ref, lg_ref, lb_ref, wout_ref, gmp_ref, g2_ref,
                       wg_ref, wu_ref, wd_ref, gpost_ref, gfin_ref, y_ref, hist_out, ubuf, cbuf, *, tm, nt, ntiles):
    g = pl.program_id(0)
    wslot = lax.rem(g, 2)
    rslot = 1 - wslot

    @pl.when(g == 0)
    def _init():
        cbuf[...] = jnp.zeros_like(cbuf)

    @pl.when(lax.rem(jnp.minimum(g, ntiles - 1), nt) == 0)
    def _new_sequence():
        ubuf[0:HIST_PAD, :] = jnp.zeros((HIST_PAD, CONV_CH), F32)

    y_ref[...] = _ffn_out_values(ym_ref[...], cbuf[rslot], x1_ref[...], wout_ref, gmp_ref, g2_ref, wg_ref, wu_ref,
                                 wd_ref, gpost_ref, gfin_ref)

    ubuf[HIST_PAD:HIST_PAD + tm, :] = u_ref[...]
    for k in range(tm // CONV_ROWS):
        for lb in range(CONV_CH // LANES):
            lanes = slice(lb * LANES, (lb + 1) * LANES)
            cbuf[wslot, k * CONV_ROWS:(k + 1) * CONV_ROWS, lanes] = _conv_taps(ubuf, k * CONV_ROWS, CONV_ROWS, lanes,
                                                                               cw_ref, cb_ref)
    for k in range(tm // LN_ROWS):
        rows = slice(k * LN_ROWS, (k + 1) * LN_ROWS)
        cbuf[wslot, rows, :] = _ln_swish(cbuf[wslot, rows, :], lg_ref, lb_ref)
    tail = ubuf[tm:tm + HIST_PAD, :]
    hist_out[0] = tail
    ubuf[0:HIST_PAD, :] = tail


def _const_spec(shape):
    nd = len(shape)
    return pl.BlockSpec(shape, lambda *_: (0,) * nd, pipeline_mode=pl.Buffered(1))


def _ffn_in_consts(p, tm):
    gbias = jnp.broadcast_to(p["gbias"][:, None], (2 * N_HEADS, tm))
    return [p["ffn1_pre_g"], p["ffn1_wg"], p["ffn1_wu"], p["ffn1_wd"], p["ffn1_post_g"], p["mix_pre_g"],
            p["w_qvo"], p["w_kt"], p["w_conv"], p["w_gt"], gbias]


def _ffn_out_consts(p):
    return [p["w_out"], p["mix_post_g"], p["ffn2_pre_g"], p["ffn2_wg"], p["ffn2_wu"], p["ffn2_wd"],
            p["ffn2_post_g"], p["final_g"]]


def _ffn_in_mlstm(x2d, p, batch, seq, tm):
    m = batch * seq
    nt = seq // tm
    ntiles = m // tm
    cur = lambda w: pl.BlockSpec((tm, w), lambda g: (jnp.minimum(g, ntiles - 1), 0))
    prev = lambda w: pl.BlockSpec((tm, w), lambda g: (jnp.maximum(g - 1, 0), 0))
    prev_seq = lambda g: jnp.maximum(g - 1, 0) // nt
    consts = _ffn_in_consts(p, tm)
    return pl.pallas_call(
        functools.partial(_ffn_in_mlstm_body, tm=tm, nt=nt),
        grid=(ntiles + 1,),
        in_specs=[cur(D_MODEL)] + [_const_spec(c.shape) for c in consts],
        out_specs=[cur(D_MODEL), cur(CONV_CH), prev(M_WIDTH),
                   pl.BlockSpec((1, N_HEADS, HEAD_DIM, AUG), lambda g: (prev_seq(g), 0, 0, 0)),
                   pl.BlockSpec((1, N_HEADS, 1, HEAD_DIM), lambda g: (prev_seq(g), 0, 0, 0))],
        out_shape=[jax.ShapeDtypeStruct((m, D_MODEL), F32), jax.ShapeDtypeStruct((m, CONV_CH), F32),
                   jax.ShapeDtypeStruct((m, M_WIDTH), F32),
                   jax.ShapeDtypeStruct((batch, N_HEADS, HEAD_DIM, AUG), F32),
                   jax.ShapeDtypeStruct((batch, N_HEADS, 1, HEAD_DIM), F32)],
        scratch_shapes=[pltpu.VMEM((2, tm, M_WIDTH), BF16), pltpu.VMEM((2, M_WIDTH, tm), BF16),
                        pltpu.VMEM((2, tm, M_WIDTH), F32), pltpu.VMEM((2, tm, M_WIDTH), F32),
                        pltpu.VMEM((2, 2 * N_HEADS, tm), F32),
                        pltpu.VMEM((N_HEADS, HEAD_DIM, AUG), F32), pltpu.VMEM((N_HEADS, 1, HEAD_DIM), F32)],
        compiler_params=pltpu.CompilerParams(dimension_semantics=("arbitrary",),
                                             vmem_limit_bytes=VMEM_LIMIT_TOKENWISE),
        name="ffn_in_mlstm",
    )(x2d, *consts)


def _conv_ffn_out(u, ym, x1, p, batch, seq, tm):
    m = batch * seq
    nt = seq // tm
    ntiles = m // tm
    cur = lambda w: pl.BlockSpec((tm, w), lambda g: (jnp.minimum(g, ntiles - 1), 0))
    prev = lambda w: pl.BlockSpec((tm, w), lambda g: (jnp.maximum(g - 1, 0), 0))
    consts = [p["conv_w"], p["conv_b"], p["conv_ln_g"], p["conv_ln_b"]] + _ffn_out_consts(p)
    return pl.pallas_call(
        functools.partial(_conv_ffn_out_body, tm=tm, nt=nt, ntiles=ntiles),
        grid=(ntiles + 1,),
        in_specs=[cur(CONV_CH), prev(M_WIDTH), prev(D_MODEL)] + [_const_spec(c.shape) for c in consts],
        out_specs=[prev(D_MODEL),
                   pl.BlockSpec((1, HIST_PAD, CONV_CH), lambda g: (jnp.minimum(g, ntiles - 1) // nt, 0, 0))],
        out_shape=[jax.ShapeDtypeStruct((m, D_MODEL), F32), jax.ShapeDtypeStruct((batch, HIST_PAD, CONV_CH), F32)],
        scratch_shapes=[pltpu.VMEM((HIST_PAD + tm, CONV_CH), F32), pltpu.VMEM((2, tm, CONV_CH), F32)],
        compiler_params=pltpu.CompilerParams(dimension_semantics=("arbitrary",),
                                             vmem_limit_bytes=VMEM_LIMIT_TOKENWISE),
        name="conv_ffn_out",
    )(u, ym, x1, *consts)


def _ffn_in_body(x_ref, g1_ref, wg_ref, wu_ref, wd_ref, gpost_ref, gmix_ref, wqvo_ref, wkt_ref, wconv_ref,
                 wgt_ref, gbias_ref, x1_ref, q_ref, kt_ref, v_ref, og_ref, u_ref, gt_ref):
    outs = _ffn_in_values(x_ref[...], g1_ref, wg_ref, wu_ref, wd_ref, gpost_ref, gmix_ref, wqvo_ref, wkt_ref,
                          wconv_ref, wgt_ref, gbias_ref)
    for ref, val in zip((x1_ref, q_ref, kt_ref, v_ref, og_ref, u_ref, gt_ref), outs):
        ref[...] = val


def _ffn_out_body(mix_ref, x1_ref, wout_ref, gmp_ref, g2_ref, wg_ref, wu_ref, wd_ref, gpost_ref, gfin_ref, y_ref):
    y_ref[...] = _ffn_out_values(mix_ref[:, :M_WIDTH], mix_ref[:, M_WIDTH:], x1_ref[...], wout_ref, gmp_ref, g2_ref,
                                 wg_ref, wu_ref, wd_ref, gpost_ref, gfin_ref)


def _mixer_sample_body(q_ref, kt_ref, v_ref, og_ref, u_ref, gt_ref, caug0_ref, m0_ref, hist0_ref,
                       cw_ref, cb_ref, lg_ref, lb_ref, mix_ref, caug_out, m_out, hist_out, ubuf, *, nseq, L):
    R = nseq * L
    r = lax.broadcasted_iota(jnp.int32, (R, R), 0)
    c = lax.broadcasted_iota(jnp.int32, (R, R), 1)
    same = lax.div(r, L) == lax.div(c, L)
    caus = same & (c <= r)
    segtriu = jnp.where(same & (r <= c), 1.0, 0.0).astype(F32)
    segones = jnp.where(same, 1.0, 0.0).astype(F32)
    rowseq = lax.div(lax.broadcasted_iota(jnp.int32, (R, AUG), 0), L)
    ones = jnp.ones((R, HEAD_DIM), F32)
    gt = gt_ref[...]
    b_rows = jnp.dot(gt, segtriu, precision=HIGHEST, preferred_element_type=F32)
    tot_rows = jnp.dot(gt, segones, precision=HIGHEST, preferred_element_type=F32)
    for h in range(N_HEADS):
        hs = slice(h * HEAD_DIM, (h + 1) * HEAD_DIM)
        b_r = b_rows[N_HEADS + h:N_HEADS + h + 1, :]
        a_r = gt[h:h + 1, :] - b_r
        t_r = tot_rows[N_HEADS + h:N_HEADS + h + 1, :]
        arow = jnp.broadcast_to(a_r, (R, R))
        bcol = jnp.broadcast_to(b_r, (R, R)).T
        acol = arow.T
        tcol = jnp.broadcast_to(t_r, (R, R)).T
        mprev = m0_ref[h]
        mprev = jnp.concatenate([mprev, mprev], axis=1)
        dmat = jnp.where(caus, bcol + arow, -jnp.inf)
        m_t = jnp.maximum(bcol + mprev, jnp.max(dmat, axis=1, keepdims=True))
        dend = jnp.where(same, tcol + arow, -jnp.inf)
        m_new = jnp.maximum(tcol + mprev, jnp.max(dend, axis=1, keepdims=True))
        qh = q_ref[:, hs]
        kth = kt_ref[hs, :]
        s = _dot(qh, kth) * jnp.exp(dmat - m_t)
        vaug = jnp.concatenate([v_ref[:, hs], ones], axis=1)
        w_inter = jnp.exp(bcol + mprev - m_t)
        qc = jnp.zeros((R, AUG), F32)
        for i in range(nseq):
            qc = jnp.where(rowseq == i, _dot(qh, caug0_ref[i, h].astype(BF16)), qc)
        sv = _dot(s.astype(BF16), vaug.astype(BF16))
        num = w_inter[:, :HEAD_DIM] * qc[:, :HEAD_DIM] + sv[:, :HEAD_DIM]
        den = w_inter[:, :HEAD_DIM] * qc[:, HEAD_DIM:] + sv[:, HEAD_DIM:]
        hh = num / jnp.maximum(jnp.abs(den), jnp.exp(-m_t[:, :HEAD_DIM]))
        mix_ref[:, hs] = og_ref[:, hs] * hh
        g_state = jnp.exp(tcol + mprev - m_new)
        g_rows = jnp.exp(tcol + acol - m_new)
        gv = g_rows * vaug
        for i in range(nseq):
            gvi = jnp.where(rowseq == i, gv, 0.0).astype(BF16)
            caug_out[i, h] = g_state[i * L:i * L + 1, :] * caug0_ref[i, h] + _dot(kth, gvi)
            m_out[i, h] = m_new[i * L:i * L + 1, :HEAD_DIM]

    for i in range(nseq):
        ubuf[0:HIST_PAD, :] = hist0_ref[i]
        ubuf[HIST_PAD:HIST_PAD + L, :] = u_ref[i * L:(i + 1) * L, :]
        acc = _conv_taps(ubuf, 0, L, slice(0, CONV_CH), cw_ref, cb_ref)
        mix_ref[i * L:(i + 1) * L, M_WIDTH:] = _ln_swish(acc, lg_ref, lb_ref)
        hist_out[i] = ubuf[L:L + HIST_PAD, :]


def _ffn_in(x2d, p, tm):
    m = x2d.shape[0]
    row = lambda w: pl.BlockSpec((tm, w), lambda i: (i, 0))
    col = lambda h: pl.BlockSpec((h, tm), lambda i: (0, i))
    consts = _ffn_in_consts(p, tm)
    return pl.pallas_call(
        _ffn_in_body,
        grid=(m // tm,),
        in_specs=[row(D_MODEL)] + [_const_spec(c.shape) for c in consts],
        out_specs=[row(D_MODEL), row(M_WIDTH), col(M_WIDTH), row(M_WIDTH), row(M_WIDTH), row(CONV_CH),
                   col(2 * N_HEADS)],
        out_shape=[jax.ShapeDtypeStruct((m, D_MODEL), F32), jax.ShapeDtypeStruct((m, M_WIDTH), BF16),
                   jax.ShapeDtypeStruct((M_WIDTH, m), BF16), jax.ShapeDtypeStruct((m, M_WIDTH), F32),
                   jax.ShapeDtypeStruct((m, M_WIDTH), F32), jax.ShapeDtypeStruct((m, CONV_CH), F32),
                   jax.ShapeDtypeStruct((2 * N_HEADS, m), F32)],
        compiler_params=pltpu.CompilerParams(dimension_semantics=("arbitrary",),
                                             vmem_limit_bytes=VMEM_LIMIT_TOKENWISE),
        name="ffn_in",
    )(x2d, *consts)


def _ffn_out(mix, x1, p, tm):
    m = mix.shape[0]
    row = pl.BlockSpec((tm, D_MODEL), lambda i: (i, 0))
    consts = _ffn_out_consts(p)
    return pl.pallas_call(
        _ffn_out_body,
        grid=(m // tm,),
        in_specs=[row, row] + [_const_spec(c.shape) for c in consts],
        out_specs=row,
        out_shape=jax.ShapeDtypeStruct((m, D_MODEL), F32),
        compiler_params=pltpu.CompilerParams(dimension_semantics=("arbitrary",),
                                             vmem_limit_bytes=VMEM_LIMIT_TOKENWISE),
        name="ffn_out",
    )(mix, x1, *consts)


def _mixer_sample(q, kt, v, og, u, gt, caug0, m0, hist0, p, nseq, L):
    rows = nseq * L
    args = [q, kt, v, og, u, gt, caug0, m0, hist0, p["conv_w"], p["conv_b"], p["conv_ln_g"], p["conv_ln_b"]]
    full = lambda a: pl.BlockSpec(a.shape, lambda i, nd=a.ndim: (0,) * nd)
    out_shape = [jax.ShapeDtypeStruct((rows, D_MODEL), F32),
                 jax.ShapeDtypeStruct((nseq, N_HEADS, HEAD_DIM, AUG), F32),
                 jax.ShapeDtypeStruct((nseq, N_HEADS, 1, HEAD_DIM), F32),
                 jax.ShapeDtypeStruct((nseq, HIST_PAD, CONV_CH), F32)]
    return pl.pallas_call(
        functools.partial(_mixer_sample_body, nseq=nseq, L=L),
        grid=(1,),
        in_specs=[full(a) for a in args],
        out_specs=[full(s) for s in out_shape],
        out_shape=out_shape,
        scratch_shapes=[pltpu.VMEM((HIST_PAD + L, CONV_CH), F32)],
        compiler_params=pltpu.CompilerParams(dimension_semantics=("arbitrary",),
                                             vmem_limit_bytes=VMEM_LIMIT_MIXER),
        name="mixer_sample",
    )(*args)


def _layer_params(l, ffn1_pre_g, ffn1_wg, ffn1_wu, ffn1_wd, ffn1_post_g, mix_pre_g, w_in, b_igate, b_fgate,
                  conv_w, conv_b, conv_ln_g, conv_ln_b, w_out, mix_post_g, ffn2_pre_g, ffn2_wg, ffn2_wu, ffn2_wd,
                  ffn2_post_g, final_g):
    vec = lambda a: a[l].astype(F32).reshape(1, -1)
    w = w_in[l]
    cuts = [0, M_WIDTH, 2 * M_WIDTH, 3 * M_WIDTH, 4 * M_WIDTH, 4 * M_WIDTH + N_HEADS, 4 * M_WIDTH + 2 * N_HEADS,
            4 * M_WIDTH + 2 * N_HEADS + CONV_CH, 4 * M_WIDTH + 2 * N_HEADS + 2 * CONV_CH]
    wq, wk, wv, wo, wi, wf, wcv, wcg = [w[:, a:b] for a, b in zip(cuts[:-1], cuts[1:])]
    return {
        "ffn1_pre_g": vec(ffn1_pre_g), "ffn1_post_g": vec(ffn1_post_g), "mix_pre_g": vec(mix_pre_g),
        "mix_post_g": vec(mix_post_g), "ffn2_pre_g": vec(ffn2_pre_g), "ffn2_post_g": vec(ffn2_post_g),
        "final_g": vec(final_g),
        "ffn1_wg": ffn1_wg[l].astype(BF16), "ffn1_wu": ffn1_wu[l].astype(BF16), "ffn1_wd": ffn1_wd[l].astype(BF16),
        "ffn2_wg": ffn2_wg[l].astype(BF16), "ffn2_wu": ffn2_wu[l].astype(BF16), "ffn2_wd": ffn2_wd[l].astype(BF16),
        "w_qvo": jnp.concatenate([wq, wv, wo], axis=1).astype(BF16),
        "w_kt": wk.T.astype(BF16),
        "w_conv": jnp.concatenate([wcv, wcg], axis=1).astype(BF16),
        "w_gt": jnp.concatenate([wi, wf], axis=1).T.astype(BF16),
        "gbias": jnp.concatenate([b_igate[l], b_fgate[l]]).astype(F32),
        "w_out": w_out[l].astype(BF16),
        "conv_w": conv_w[l].astype(F32), "conv_b": vec(conv_b), "conv_ln_g": vec(conv_ln_g),
        "conv_ln_b": vec(conv_ln_b),
    }


def _split_state(caug, m, hist):
    return caug[..., :HEAD_DIM], caug[..., HEAD_DIM], m[:, :, 0, 0], hist[:, HIST_PAD - HIST:, :]


def kernel(x_prompt, x_sample, state_mlstm_C, state_mlstm_n, state_mlstm_m, cache_conv, ffn1_pre_g, ffn1_wg,
           ffn1_wu, ffn1_wd, ffn1_post_g, mix_pre_g, w_in, b_igate, b_fgate, conv_w, conv_b, conv_ln_g, conv_ln_b,
           w_out, mix_post_g, ffn2_pre_g, ffn2_wg, ffn2_wu, ffn2_wd, ffn2_post_g, final_g):
    batch, seq, _ = x_prompt.shape
    nseq, dseq, _ = x_sample.shape
    depth = w_in.shape[0]
    assert seq % TOKEN_TILE == 0 and TOKEN_TILE % CHUNK == 0 and TOKEN_TILE % CONV_ROWS == 0
    assert (nseq * dseq) % TOKEN_TILE == 0 and dseq <= HIST_PAD
    yp = x_prompt.reshape(batch * seq, D_MODEL)
    ys = x_sample.reshape(nseq * dseq, D_MODEL)
    outs_p, outs_s = [], []
    for l in range(depth):
        p = _layer_params(l, ffn1_pre_g, ffn1_wg, ffn1_wu, ffn1_wd, ffn1_post_g, mix_pre_g, w_in, b_igate, b_fgate,
                          conv_w, conv_b, conv_ln_g, conv_ln_b, w_out, mix_post_g, ffn2_pre_g, ffn2_wg, ffn2_wu,
                          ffn2_wd, ffn2_post_g, final_g)
        x1, u, ym, caug, m = _ffn_in_mlstm(yp, p, batch, seq, TOKEN_TILE)
        yp, hist = _conv_ffn_out(u, ym, x1, p, batch, seq, TOKEN_TILE)
        outs_p.append(_split_state(caug, m, hist))
        x1, q, kt, v, og, u, gt = _ffn_in(ys, p, TOKEN_TILE)
        n0 = jnp.broadcast_to(state_mlstm_n[l].astype(F32)[..., None], (nseq, N_HEADS, HEAD_DIM, HEAD_DIM))
        caug0 = jnp.concatenate([state_mlstm_C[l].astype(F32), n0], axis=-1)
        m0 = jnp.broadcast_to(state_mlstm_m[l].astype(F32).T[:, :, None, None], (N_HEADS, nseq, dseq, HEAD_DIM))
        m0 = m0.reshape(N_HEADS, nseq * dseq, HEAD_DIM)
        hist0 = jnp.pad(cache_conv[l].astype(F32), ((0, 0), (HIST_PAD - HIST, 0), (0, 0)))
        mix, caug, m, hist = _mixer_sample(q, kt, v, og, u, gt, caug0, m0, hist0, p, nseq, dseq)
        ys = _ffn_out(mix, x1, p, TOKEN_TILE)
        outs_s.append(_split_state(caug, m, hist))
    stack = lambda outs, k: jnp.stack([o[k] for o in outs])
    return (yp.reshape(batch, seq, D_MODEL), ys.reshape(nseq, dseq, D_MODEL),
            stack(outs_p, 0), stack(outs_p, 1), stack(outs_p, 2), stack(outs_p, 3),
            stack(outs_s, 0), stack(outs_s, 1), stack(outs_s, 2), stack(outs_s, 3))
```

```python
import functools

import jax
import jax.numpy as jnp
from jax import lax
from jax.experimental import pallas as pl
from jax.experimental.pallas import tpu as pltpu

D_MODEL = 1024
D_FF = 2816
N_HEADS = 4
HEAD_DIM = 128
M_WIDTH = N_HEADS * HEAD_DIM
CONV_CH = 512
CONV_WIDTH = 31
HIST = CONV_WIDTH - 1
HIST_PAD = 32
SUBLANES = 8
EPS = 1e-6
CHUNK = 128
AUG = 2 * HEAD_DIM
CONV_ROWS = 128
LANES = 128
LN_ROWS = 32
NCH = 256

F32 = jnp.float32
BF16 = jnp.bfloat16
HIGHEST = lax.Precision.HIGHEST
NT_DIMS = (((1,), (1,)), ((), ()))

TOKEN_TILE = 256
PROMPT_TILE = 256
VMEM_LIMIT_TOKENWISE = 56 * 1024 * 1024
VMEM_LIMIT_MIXER = 40 * 1024 * 1024


def _rms(x, g):
    return x * lax.rsqrt(jnp.mean(x * x, axis=-1, keepdims=True) + EPS) * g


def _dot(a, b):
    return jnp.dot(a, b, preferred_element_type=F32)


def _swiglu(h, wg_ref, wu_ref, wd_ref):
    a = _dot(h, wg_ref[...])
    b = _dot(h, wu_ref[...])
    s = (a * jax.nn.sigmoid(a) * b).astype(BF16)
    return _dot(s, wd_ref[...])


class _Side:
    def __init__(self, make_pieces):
        self.zero = None
        self.pieces = make_pieces(self)

    def run(self, after, n=1):
        bits = pltpu.bitcast(after[after.shape[0] - SUBLANES:, 0:LANES], jnp.int32)
        self.zero = lax.shift_right_logical(lax.shift_right_logical(bits, 16), 16)[0, 0]
        if n is None:
            for _ in self.pieces:
                pass
        else:
            for _ in range(n):
                next(self.pieces, None)


def _swiglu_chunked(h_s, s_s, wg_ref, wu_ref, wd_ref, side, chunk, after_gating, down_pieces=1):
    for c0 in range(0, D_FF, chunk):
        cols = slice(c0, min(c0 + chunk, D_FF))
        a = _dot(h_s[...], wg_ref[:, cols])
        b = _dot(h_s[...], wu_ref[:, cols])
        s = a * jax.nn.sigmoid(a) * b
        s_s[:, cols] = s.astype(BF16)
        for t0 in range(0, cols.stop - cols.start, NCH):
            side.run((s if after_gating else a)[:, t0:t0 + LANES])
    d = []
    for c0 in range(0, D_MODEL, chunk):
        d.append(_dot(s_s[...], wd_ref[:, c0:c0 + chunk]))
        for t0 in range(0, chunk, NCH):
            side.run(d[-1][:, t0:t0 + LANES], down_pieces)
    return jnp.concatenate(d, axis=1)


def _ffn_in_values(x, g1_ref, wg_ref, wu_ref, wd_ref, gpost_ref, gmix_ref, wqvo_ref, wkt_ref, wconv_ref,
                   wgt_ref, gbias_ref):
    h = _rms(x, g1_ref[...]).astype(BF16)
    d = _swiglu(h, wg_ref, wu_ref, wd_ref)
    x1 = x + 0.5 * _rms(d, gpost_ref[...])
    h2 = _rms(x1, gmix_ref[...]).astype(BF16)
    qvo = _dot(h2, wqvo_ref[...])
    q = qvo[:, :M_WIDTH].astype(BF16)
    v = qvo[:, M_WIDTH:2 * M_WIDTH]
    og = jax.nn.sigmoid(qvo[:, 2 * M_WIDTH:])
    kt = lax.dot_general(wkt_ref[...], h2, NT_DIMS, preferred_element_type=F32) * (HEAD_DIM ** -0.5)
    cc = _dot(h2, wconv_ref[...])
    u = cc[:, :CONV_CH] * jax.nn.sigmoid(cc[:, CONV_CH:])
    g = lax.dot_general(wgt_ref[...], h2, NT_DIMS, preferred_element_type=F32) + gbias_ref[...]
    row = lax.broadcasted_iota(jnp.int32, g.shape, 0)
    gt = jnp.where(row < N_HEADS, g, jax.nn.log_sigmoid(g))
    return x1, q, kt.astype(BF16), v, og, u, gt


def _ffn_out_values(ym, c, x1, wout_ref, gmp_ref, g2_ref, wg_ref, wu_ref, wd_ref, gpost_ref, gfin_ref):
    o = _dot(ym.astype(BF16), wout_ref[0:M_WIDTH, :]) + _dot(c.astype(BF16), wout_ref[M_WIDTH:, :])
    x2 = x1 + _rms(o, gmp_ref[...])
    h = _rms(x2, g2_ref[...]).astype(BF16)
    d = _swiglu(h, wg_ref, wu_ref, wd_ref)
    x3 = x2 + 0.5 * _rms(d, gpost_ref[...])
    return _rms(x3, gfin_ref[...])


def _conv_taps(ubuf, row0, nrows, lanes, cw_ref, cb_ref):
    acc = jnp.broadcast_to(cb_ref[:, lanes], (nrows, lanes.stop - lanes.start))
    first = HIST_PAD - HIST
    for res in range(SUBLANES):
        taps = [j for j in range(CONV_WIDTH) if (first + j) % SUBLANES == res]
        lo = (first + taps[0]) // SUBLANES * SUBLANES
        hi = (first + taps[-1]) // SUBLANES * SUBLANES
        win = ubuf[row0 + lo:row0 + hi + nrows + (SUBLANES if res else 0), lanes]
        if res:
            win = pltpu.roll(win, win.shape[0] - res, 0)
        for j in taps:
            off = first + j - res - lo
            acc = acc + win[off:off + nrows, :] * cw_ref[j:j + 1, lanes]
    return acc


def _ln_swish(acc, lg_ref, lb_ref):
    mu = jnp.mean(acc, axis=-1, keepdims=True)
    xc = acc - mu
    var = jnp.mean(xc * xc, axis=-1, keepdims=True)
    y = xc * lax.rsqrt(var + EPS) * lg_ref[...] + lb_ref[...]
    return y * jax.nn.sigmoid(y)


def _mlstm_gates(b_r, a_r, caus, qh, kth, caug):
    L = b_r.shape[1]
    arow = jnp.broadcast_to(a_r, (L, L))
    bcol = jnp.broadcast_to(b_r, (L, L)).T
    acol = arow.T
    dmat = jnp.where(caus, bcol + arow, -jnp.inf)
    return dict(bcol=bcol, acol=acol, dmat=dmat, rowmax=jnp.max(dmat, axis=1, keepdims=True),
                s_raw=_dot(qh, kth), qc=_dot(qh, caug.astype(BF16)), caug=caug, kth=kth)


def _mlstm_scores(st, mprev, vh):
    m_t = jnp.maximum(st["bcol"] + mprev, st["rowmax"])
    s = st["s_raw"] * jnp.exp(st["dmat"] - m_t)
    vaug = jnp.concatenate([vh, jnp.ones_like(vh)], axis=1)
    st.update(m_t=m_t, mprev=mprev, vaug=vaug, sv=_dot(s.astype(BF16), vaug.astype(BF16)),
              w_inter=jnp.exp(st["bcol"] + mprev - m_t))
    return m_t[m_t.shape[0] - 1:, :]


def _mlstm_output(st):
    L = st["m_t"].shape[0]
    m_t, qc, sv, w_inter = st["m_t"], st["qc"], st["sv"], st["w_inter"]
    num = w_inter * qc[:, :HEAD_DIM] + sv[:, :HEAD_DIM]
    den = w_inter * qc[:, HEAD_DIM:] + sv[:, HEAD_DIM:]
    hh = num / jnp.maximum(jnp.abs(den), jnp.exp(-m_t))
    m_new = m_t[L - 1:L, :]
    b_last = st["bcol"][L - 1:L, :]
    g_state = jnp.exp(b_last + st["mprev"] - m_new)
    g_rows = jnp.exp(b_last + st["acol"] - m_new)
    gv = (jnp.concatenate([g_rows, g_rows], axis=1) * st["vaug"]).astype(BF16)
    caug_new = jnp.concatenate([g_state, g_state], axis=1) * st["caug"] + _dot(st["kth"], gv)
    return hh, caug_new


def _ffn_in_mlstm_body(x_ref, g1_ref, wg_ref, wu_ref, wd_ref, gpost_ref, gmix_ref, wqvo_ref, wkt_ref, wconv_ref,
                       wgt_ref, gbias_ref, x1_ref, u_ref, ym_ref, caug_out, m_out,
                       q_st, kt_st, v_st, og_st, g_st, caug_s, m_s, h_s, s_s, *, tm, nt):
    g = pl.program_id(0)
    wslot = lax.rem(g, 2)
    rslot = 1 - wslot
    L = CHUNK

    @pl.when(g == 0)
    def _init():
        for st in (q_st, kt_st, v_st, og_st, g_st, caug_s, m_s):
            st[...] = jnp.zeros_like(st)

    def mlstm_pieces(side):
        fresh = lax.rem(g - 1, nt) == 0
        r = lax.broadcasted_iota(jnp.int32, (L, L), 0)
        c = lax.broadcasted_iota(jnp.int32, (L, L), 1)
        caus = c <= r
        triu = (r <= c).astype(F32)
        nck = tm // L
        items = [(ck, h) for ck in range(nck) for h in range(N_HEADS)]
        gates = {}
        stages = {}

        def stage1(ck, h):
            rows, hs, slot = slice(ck * L, (ck + 1) * L), slice(h * HEAD_DIM, (h + 1) * HEAD_DIM), rslot + side.zero
            if ck not in gates:
                gt = g_st[slot, :, rows]
                gates[ck] = gt, jnp.dot(gt, triu, precision=HIGHEST, preferred_element_type=F32)
            gt, b_rows = gates[ck]
            b_r = b_rows[N_HEADS + h:N_HEADS + h + 1, :]
            caug = caug_s[h + side.zero]
            if ck == 0:
                caug = jnp.where(fresh, 0.0, caug)
            stages[ck, h] = _mlstm_gates(b_r, gt[h:h + 1, :] - b_r, caus, q_st[slot, rows, hs], kt_st[slot, hs, rows],
                                         caug)

        def stage2(ck, h):
            rows, hs, slot = slice(ck * L, (ck + 1) * L), slice(h * HEAD_DIM, (h + 1) * HEAD_DIM), rslot + side.zero
            mprev = m_s[h + side.zero]
            if ck == 0:
                mprev = jnp.where(fresh, 0.0, mprev)
            m_new = _mlstm_scores(stages[ck, h], mprev, v_st[slot, rows, hs])
            m_s[h] = m_new
            if ck == nck - 1:
                m_out[0, h] = m_new

        def stage3(ck, h):
            rows, hs, slot = slice(ck * L, (ck + 1) * L), slice(h * HEAD_DIM, (h + 1) * HEAD_DIM), rslot + side.zero
            hh, caug = _mlstm_output(stages.pop((ck, h)))
            ym_ref[rows, hs] = og_st[slot, rows, hs] * hh
            caug_s[h] = caug
            if ck == nck - 1:
                caug_out[0, h] = caug

        for t in range(len(items) + 2):
            for lag, stage in enumerate((stage1, stage2, stage3)):
                if 0 <= t - lag < len(items):
                    stage(*items[t - lag])
            yield

    side = _Side(mlstm_pieces)

    h_s[...] = _rms(x_ref[...], g1_ref[...]).astype(BF16)
    d = _swiglu_chunked(h_s, s_s, wg_ref, wu_ref, wd_ref, side, 2 * NCH, False)
    x1 = x_ref[...] + 0.5 * _rms(d, gpost_ref[...])
    x1_ref[...] = x1
    h_s[...] = _rms(x1, gmix_ref[...]).astype(BF16)
    for c0 in range(0, 3 * M_WIDTH, NCH):
        r = _dot(h_s[...], wqvo_ref[:, c0:c0 + NCH])
        cols = slice(c0 % M_WIDTH, c0 % M_WIDTH + NCH)
        if c0 < M_WIDTH:
            q_st[wslot, :, cols] = r.astype(BF16)
        elif c0 < 2 * M_WIDTH:
            v_st[wslot, :, cols] = r
        else:
            og_st[wslot, :, cols] = jax.nn.sigmoid(r)
        side.run(r)
    for c0 in range(0, M_WIDTH, NCH):
        kt = lax.dot_general(wkt_ref[c0:c0 + NCH, :], h_s[...], NT_DIMS, preferred_element_type=F32)
        kt_st[wslot, c0:c0 + NCH, :] = (kt * (HEAD_DIM ** -0.5)).astype(BF16)
        side.run(kt)
    for c0 in range(0, CONV_CH, NCH):
        cv = _dot(h_s[...], wconv_ref[:, c0:c0 + NCH])
        cg = _dot(h_s[...], wconv_ref[:, CONV_CH + c0:CONV_CH + c0 + NCH])
        u_ref[:, c0:c0 + NCH] = cv * jax.nn.sigmoid(cg)
        side.run(cv)
    gg = lax.dot_general(wgt_ref[...], h_s[...], NT_DIMS, preferred_element_type=F32) + gbias_ref[...]
    row = lax.broadcasted_iota(jnp.int32, gg.shape, 0)
    g_st[wslot] = jnp.where(row < N_HEADS, gg, jax.nn.log_sigmoid(gg))
    side.run(gg, None)


def _conv_ffn_out_body(u_ref, ym_ref, x1_ref, cw_ref, cb_ref, lg_ref, lb_ref, wout_ref, gmp_ref, g2_ref,
                       wg_ref, wu_ref, wd_ref, gpost_ref, gfin_ref, y_ref, hist_out, ubuf, cbuf, lhs_s, h_s, s_s, x2_s,
                       *, tm, nt, ntiles):
    g = pl.program_id(0)
    wslot = lax.rem(g, 2)
    rslot = 1 - wslot

    @pl.when(g == 0)
    def _init():
        cbuf[...] = jnp.zeros_like(cbuf)

    @pl.when(lax.rem(jnp.minimum(g, ntiles - 1), nt) == 0)
    def _new_sequence():
        ubuf[0, 0:HIST_PAD, :] = jnp.zeros((HIST_PAD, CONV_CH), F32)

    ubuf[0, HIST_PAD:HIST_PAD + tm, :] = u_ref[...]

    def conv_pieces(side):
        def layer_norm(k, j):
            rows = slice(k * CONV_ROWS + j * LN_ROWS, k * CONV_ROWS + (j + 1) * LN_ROWS)
            cbuf[wslot, rows, :] = _ln_swish(cbuf[wslot + side.zero, rows, :], lg_ref, lb_ref)

        nblk = tm // CONV_ROWS
        for k in range(nblk):
            for lb in range(CONV_CH // LANES):
                lanes = slice(lb * LANES, (lb + 1) * LANES)
                cbuf[wslot, k * CONV_ROWS:(k + 1) * CONV_ROWS, lanes] = _conv_taps(
                    ubuf.at[side.zero], k * CONV_ROWS, CONV_ROWS, lanes, cw_ref, cb_ref)
                if k > 0:
                    layer_norm(k - 1, lb)
                yield
        for j in range(CONV_ROWS // LN_ROWS):
            layer_norm(nblk - 1, j)
        tail = ubuf[side.zero, tm:tm + HIST_PAD, :]
        hist_out[0] = tail
        ubuf[0, 0:HIST_PAD, :] = tail
        yield

    side = _Side(conv_pieces)

    lhs_s[:, :M_WIDTH] = ym_ref[...].astype(BF16)
    lhs_s[:, M_WIDTH:] = cbuf[rslot].astype(BF16)
    o = []
    for c0 in range(0, D_MODEL, NCH):
        o.append(_dot(lhs_s[...], wout_ref[:, c0:c0 + NCH]))
    x2 = x1_ref[...] + _rms(jnp.concatenate(o, axis=1), gmp_ref[...])
    x2_s[...] = x2
    h_s[...] = _rms(x2, g2_ref[...]).astype(BF16)
    d = _swiglu_chunked(h_s, s_s, wg_ref, wu_ref, wd_ref, side, NCH, True, down_pieces=2)
    x3 = x2_s[...] + 0.5 * _rms(d, gpost_ref[...])
    y = _rms(x3, gfin_ref[...])
    y_ref[...] = y
    side.run(y, None)


def _const_spec(shape):
    nd = len(shape)
    return pl.BlockSpec(shape, lambda *_: (0,) * nd, pipeline_mode=pl.Buffered(1))


def _ffn_in_consts(p, tm):
    gbias = jnp.broadcast_to(p["gbias"][:, None], (2 * N_HEADS, tm))
    return [p["ffn1_pre_g"], p["ffn1_wg"], p["ffn1_wu"], p["ffn1_wd"], p["ffn1_post_g"], p["mix_pre_g"],
            p["w_qvo"], p["w_kt"], p["w_conv"], p["w_gt"], gbias]


def _ffn_out_consts(p):
    return [p["w_out"], p["mix_post_g"], p["ffn2_pre_g"], p["ffn2_wg"], p["ffn2_wu"], p["ffn2_wd"],
            p["ffn2_post_g"], p["final_g"]]


def _ffn_in_mlstm(x2d, p, batch, seq, tm):
    m = batch * seq
    nt = seq // tm
    ntiles = m // tm
    cur = lambda w: pl.BlockSpec((tm, w), lambda g: (jnp.minimum(g, ntiles - 1), 0))
    prev = lambda w: pl.BlockSpec((tm, w), lambda g: (jnp.maximum(g - 1, 0), 0))
    prev_seq = lambda g: jnp.maximum(g - 1, 0) // nt
    consts = _ffn_in_consts(p, tm)
    return pl.pallas_call(
        functools.partial(_ffn_in_mlstm_body, tm=tm, nt=nt),
        grid=(ntiles + 1,),
        in_specs=[cur(D_MODEL)] + [_const_spec(c.shape) for c in consts],
        out_specs=[cur(D_MODEL), cur(CONV_CH), prev(M_WIDTH),
                   pl.BlockSpec((1, N_HEADS, HEAD_DIM, AUG), lambda g: (prev_seq(g), 0, 0, 0)),
                   pl.BlockSpec((1, N_HEADS, 1, HEAD_DIM), lambda g: (prev_seq(g), 0, 0, 0))],
        out_shape=[jax.ShapeDtypeStruct((m, D_MODEL), F32), jax.ShapeDtypeStruct((m, CONV_CH), F32),
                   jax.ShapeDtypeStruct((m, M_WIDTH), F32),
                   jax.ShapeDtypeStruct((batch, N_HEADS, HEAD_DIM, AUG), F32),
                   jax.ShapeDtypeStruct((batch, N_HEADS, 1, HEAD_DIM), F32)],
        scratch_shapes=[pltpu.VMEM((2, tm, M_WIDTH), BF16), pltpu.VMEM((2, M_WIDTH, tm), BF16),
                        pltpu.VMEM((2, tm, M_WIDTH), F32), pltpu.VMEM((2, tm, M_WIDTH), F32),
                        pltpu.VMEM((2, 2 * N_HEADS, tm), F32),
                        pltpu.VMEM((N_HEADS, HEAD_DIM, AUG), F32), pltpu.VMEM((N_HEADS, 1, HEAD_DIM), F32),
                        pltpu.VMEM((tm, D_MODEL), BF16), pltpu.VMEM((tm, D_FF), BF16)],
        compiler_params=pltpu.CompilerParams(dimension_semantics=("arbitrary",),
                                             vmem_limit_bytes=VMEM_LIMIT_TOKENWISE),
        name="ffn_in_mlstm",
    )(x2d, *consts)


def _conv_ffn_out(u, ym, x1, p, batch, seq, tm):
    m = batch * seq
    nt = seq // tm
    ntiles = m // tm
    cur = lambda w: pl.BlockSpec((tm, w), lambda g: (jnp.minimum(g, ntiles - 1), 0))
    prev = lambda w: pl.BlockSpec((tm, w), lambda g: (jnp.maximum(g - 1, 0), 0))
    consts = [p["conv_w"], p["conv_b"], p["conv_ln_g"], p["conv_ln_b"]] + _ffn_out_consts(p)
    return pl.pallas_call(
        functools.partial(_conv_ffn_out_body, tm=tm, nt=nt, ntiles=ntiles),
        grid=(ntiles + 1,),
        in_specs=[cur(CONV_CH), prev(M_WIDTH), prev(D_MODEL)] + [_const_spec(c.shape) for c in consts],
        out_specs=[prev(D_MODEL),
                   pl.BlockSpec((1, HIST_PAD, CONV_CH), lambda g: (jnp.minimum(g, ntiles - 1) // nt, 0, 0))],
        out_shape=[jax.ShapeDtypeStruct((m, D_MODEL), F32), jax.ShapeDtypeStruct((batch, HIST_PAD, CONV_CH), F32)],
        scratch_shapes=[pltpu.VMEM((1, HIST_PAD + tm, CONV_CH), F32), pltpu.VMEM((2, tm, CONV_CH), F32),
                        pltpu.VMEM((tm, D_MODEL), BF16), pltpu.VMEM((tm, D_MODEL), BF16), pltpu.VMEM((tm, D_FF), BF16),
                        pltpu.VMEM((tm, D_MODEL), F32)],
        compiler_params=pltpu.CompilerParams(dimension_semantics=("arbitrary",),
                                             vmem_limit_bytes=VMEM_LIMIT_TOKENWISE),
        name="conv_ffn_out",
    )(u, ym, x1, *consts)


def _ffn_in_body(x_ref, g1_ref, wg_ref, wu_ref, wd_ref, gpost_ref, gmix_ref, wqvo_ref, wkt_ref, wconv_ref,
                 wgt_ref, gbias_ref, x1_ref, q_ref, kt_ref, v_ref, og_ref, u_ref, gt_ref):
    outs = _ffn_in_values(x_ref[...], g1_ref, wg_ref, wu_ref, wd_ref, gpost_ref, gmix_ref, wqvo_ref, wkt_ref,
                          wconv_ref, wgt_ref, gbias_ref)
    for ref, val in zip((x1_ref, q_ref, kt_ref, v_ref, og_ref, u_ref, gt_ref), outs):
        ref[...] = val


def _ffn_out_body(mix_ref, x1_ref, wout_ref, gmp_ref, g2_ref, wg_ref, wu_ref, wd_ref, gpost_ref, gfin_ref, y_ref):
    y_ref[...] = _ffn_out_values(mix_ref[:, :M_WIDTH], mix_ref[:, M_WIDTH:], x1_ref[...], wout_ref, gmp_ref, g2_ref,
                                 wg_ref, wu_ref, wd_ref, gpost_ref, gfin_ref)


def _mixer_sample_body(q_ref, kt_ref, v_ref, og_ref, u_ref, gt_ref, caug0_ref, m0_ref, hist0_ref,
                       cw_ref, cb_ref, lg_ref, lb_ref, mix_ref, caug_out, m_out, hist_out, ubuf, *, nseq, L):
    R = nseq * L
    r = lax.broadcasted_iota(jnp.int32, (R, R), 0)
    c = lax.broadcasted_iota(jnp.int32, (R, R), 1)
    same = lax.div(r, L) == lax.div(c, L)
    caus = same & (c <= r)
    segtriu = jnp.where(same & (r <= c), 1.0, 0.0).astype(F32)
    segones = jnp.where(same, 1.0, 0.0).astype(F32)
    rowseq = lax.div(lax.broadcasted_iota(jnp.int32, (R, AUG), 0), L)
    ones = jnp.ones((R, HEAD_DIM), F32)
    gt = gt_ref[...]
    b_rows = jnp.dot(gt, segtriu, precision=HIGHEST, preferred_element_type=F32)
    tot_rows = jnp.dot(gt, segones, precision=HIGHEST, preferred_element_type=F32)
    for h in range(N_HEADS):
        hs = slice(h * HEAD_DIM, (h + 1) * HEAD_DIM)
        b_r = b_rows[N_HEADS + h:N_HEADS + h + 1, :]
        a_r = gt[h:h + 1, :] - b_r
        t_r = tot_rows[N_HEADS + h:N_HEADS + h + 1, :]
        arow = jnp.broadcast_to(a_r, (R, R))
        bcol = jnp.broadcast_to(b_r, (R, R)).T
        acol = arow.T
        tcol = jnp.broadcast_to(t_r, (R, R)).T
        mprev = m0_ref[h]
        mprev = jnp.concatenate([mprev, mprev], axis=1)
        dmat = jnp.where(caus, bcol + arow, -jnp.inf)
        m_t = jnp.maximum(bcol + mprev, jnp.max(dmat, axis=1, keepdims=True))
        dend = jnp.where(same, tcol + arow, -jnp.inf)
        m_new = jnp.maximum(tcol + mprev, jnp.max(dend, axis=1, keepdims=True))
        qh = q_ref[:, hs]
        kth = kt_ref[hs, :]
        s = _dot(qh, kth) * jnp.exp(dmat - m_t)
        vaug = jnp.concatenate([v_ref[:, hs], ones], axis=1)
        w_inter = jnp.exp(bcol + mprev - m_t)
        qc = jnp.zeros((R, AUG), F32)
        for i in range(nseq):
            qc = jnp.where(rowseq == i, _dot(qh, caug0_ref[i, h].astype(BF16)), qc)
        sv = _dot(s.astype(BF16), vaug.astype(BF16))
        num = w_inter[:, :HEAD_DIM] * qc[:, :HEAD_DIM] + sv[:, :HEAD_DIM]
        den = w_inter[:, :HEAD_DIM] * qc[:, HEAD_DIM:] + sv[:, HEAD_DIM:]
        hh = num / jnp.maximum(jnp.abs(den), jnp.exp(-m_t[:, :HEAD_DIM]))
        mix_ref[:, hs] = og_ref[:, hs] * hh
        g_state = jnp.exp(tcol + mprev - m_new)
        g_rows = jnp.exp(tcol + acol - m_new)
        gv = g_rows * vaug
        for i in range(nseq):
            gvi = jnp.where(rowseq == i, gv, 0.0).astype(BF16)
            caug_out[i, h] = g_state[i * L:i * L + 1, :] * caug0_ref[i, h] + _dot(kth, gvi)
            m_out[i, h] = m_new[i * L:i * L + 1, :HEAD_DIM]

    for i in range(nseq):
        ubuf[0:HIST_PAD, :] = hist0_ref[i]
        ubuf[HIST_PAD:HIST_PAD + L, :] = u_ref[i * L:(i + 1) * L, :]
        acc = _conv_taps(ubuf, 0, L, slice(0, CONV_CH), cw_ref, cb_ref)
        mix_ref[i * L:(i + 1) * L, M_WIDTH:] = _ln_swish(acc, lg_ref, lb_ref)
        hist_out[i] = ubuf[L:L + HIST_PAD, :]


def _ffn_in(x2d, p, tm):
    m = x2d.shape[0]
    row = lambda w: pl.BlockSpec((tm, w), lambda i: (i, 0))
    col = lambda h: pl.BlockSpec((h, tm), lambda i: (0, i))
    consts = _ffn_in_consts(p, tm)
    return pl.pallas_call(
        _ffn_in_body,
        grid=(m // tm,),
        in_specs=[row(D_MODEL)] + [_const_spec(c.shape) for c in consts],
        out_specs=[row(D_MODEL), row(M_WIDTH), col(M_WIDTH), row(M_WIDTH), row(M_WIDTH), row(CONV_CH),
                   col(2 * N_HEADS)],
        out_shape=[jax.ShapeDtypeStruct((m, D_MODEL), F32), jax.ShapeDtypeStruct((m, M_WIDTH), BF16),
                   jax.ShapeDtypeStruct((M_WIDTH, m), BF16), jax.ShapeDtypeStruct((m, M_WIDTH), F32),
                   jax.ShapeDtypeStruct((m, M_WIDTH), F32), jax.ShapeDtypeStruct((m, CONV_CH), F32),
                   jax.ShapeDtypeStruct((2 * N_HEADS, m), F32)],
        compiler_params=pltpu.CompilerParams(dimension_semantics=("arbitrary",),
                                             vmem_limit_bytes=VMEM_LIMIT_TOKENWISE),
        name="ffn_in",
    )(x2d, *consts)


def _ffn_out(mix, x1, p, tm):
    m = mix.shape[0]
    row = pl.BlockSpec((tm, D_MODEL), lambda i: (i, 0))
    consts = _ffn_out_consts(p)
    return pl.pallas_call(
        _ffn_out_body,
        grid=(m // tm,),
        in_specs=[row, row] + [_const_spec(c.shape) for c in consts],
        out_specs=row,
        out_shape=jax.ShapeDtypeStruct((m, D_MODEL), F32),
        compiler_params=pltpu.CompilerParams(dimension_semantics=("arbitrary",),
                                             vmem_limit_bytes=VMEM_LIMIT_TOKENWISE),
        name="ffn_out",
    )(mix, x1, *consts)


def _mixer_sample(q, kt, v, og, u, gt, caug0, m0, hist0, p, nseq, L):
    rows = nseq * L
    args = [q, kt, v, og, u, gt, caug0, m0, hist0, p["conv_w"], p["conv_b"], p["conv_ln_g"], p["conv_ln_b"]]
    full = lambda a: pl.BlockSpec(a.shape, lambda i, nd=a.ndim: (0,) * nd)
    out_shape = [jax.ShapeDtypeStruct((rows, D_MODEL), F32),
                 jax.ShapeDtypeStruct((nseq, N_HEADS, HEAD_DIM, AUG), F32),
                 jax.ShapeDtypeStruct((nseq, N_HEADS, 1, HEAD_DIM), F32),
                 jax.ShapeDtypeStruct((nseq, HIST_PAD, CONV_CH), F32)]
    return pl.pallas_call(
        functools.partial(_mixer_sample_body, nseq=nseq, L=L),
        grid=(1,),
        in_specs=[full(a) for a in args],
        out_specs=[full(s) for s in out_shape],
        out_shape=out_shape,
        scratch_shapes=[pltpu.VMEM((HIST_PAD + L, CONV_CH), F32)],
        compiler_params=pltpu.CompilerParams(dimension_semantics=("arbitrary",),
                                             vmem_limit_bytes=VMEM_LIMIT_MIXER),
        name="mixer_sample",
    )(*args)


def _layer_params(l, ffn1_pre_g, ffn1_wg, ffn1_wu, ffn1_wd, ffn1_post_g, mix_pre_g, w_in, b_igate, b_fgate,
                  conv_w, conv_b, conv_ln_g, conv_ln_b, w_out, mix_post_g, ffn2_pre_g, ffn2_wg, ffn2_wu, ffn2_wd,
                  ffn2_post_g, final_g):
    vec = lambda a: a[l].astype(F32).reshape(1, -1)
    w = w_in[l]
    cuts = [0, M_WIDTH, 2 * M_WIDTH, 3 * M_WIDTH, 4 * M_WIDTH, 4 * M_WIDTH + N_HEADS, 4 * M_WIDTH + 2 * N_HEADS,
            4 * M_WIDTH + 2 * N_HEADS + CONV_CH, 4 * M_WIDTH + 2 * N_HEADS + 2 * CONV_CH]
    wq, wk, wv, wo, wi, wf, wcv, wcg = [w[:, a:b] for a, b in zip(cuts[:-1], cuts[1:])]
    return {
        "ffn1_pre_g": vec(ffn1_pre_g), "ffn1_post_g": vec(ffn1_post_g), "mix_pre_g": vec(mix_pre_g),
        "mix_post_g": vec(mix_post_g), "ffn2_pre_g": vec(ffn2_pre_g), "ffn2_post_g": vec(ffn2_post_g),
        "final_g": vec(final_g),
        "ffn1_wg": ffn1_wg[l].astype(BF16), "ffn1_wu": ffn1_wu[l].astype(BF16), "ffn1_wd": ffn1_wd[l].astype(BF16),
        "ffn2_wg": ffn2_wg[l].astype(BF16), "ffn2_wu": ffn2_wu[l].astype(BF16), "ffn2_wd": ffn2_wd[l].astype(BF16),
        "w_qvo": jnp.concatenate([wq, wv, wo], axis=1).astype(BF16),
        "w_kt": wk.T.astype(BF16),
        "w_conv": jnp.concatenate([wcv, wcg], axis=1).astype(BF16),
        "w_gt": jnp.concatenate([wi, wf], axis=1).T.astype(BF16),
        "gbias": jnp.concatenate([b_igate[l], b_fgate[l]]).astype(F32),
        "w_out": w_out[l].astype(BF16),
        "conv_w": conv_w[l].astype(F32), "conv_b": vec(conv_b), "conv_ln_g": vec(conv_ln_g),
        "conv_ln_b": vec(conv_ln_b),
    }


def _split_state(caug, m, hist):
    return caug[..., :HEAD_DIM], caug[..., HEAD_DIM], m[:, :, 0, 0], hist[:, HIST_PAD - HIST:, :]


def kernel(x_prompt, x_sample, state_mlstm_C, state_mlstm_n, state_mlstm_m, cache_conv, ffn1_pre_g, ffn1_wg,
           ffn1_wu, ffn1_wd, ffn1_post_g, mix_pre_g, w_in, b_igate, b_fgate, conv_w, conv_b, conv_ln_g, conv_ln_b,
           w_out, mix_post_g, ffn2_pre_g, ffn2_wg, ffn2_wu, ffn2_wd, ffn2_post_g, final_g):
    batch, seq, _ = x_prompt.shape
    nseq, dseq, _ = x_sample.shape
    depth = w_in.shape[0]
    assert seq % PROMPT_TILE == 0 and PROMPT_TILE % CHUNK == 0 and PROMPT_TILE % CONV_ROWS == 0
    assert (nseq * dseq) % TOKEN_TILE == 0 and dseq <= HIST_PAD
    yp = x_prompt.reshape(batch * seq, D_MODEL)
    ys = x_sample.reshape(nseq * dseq, D_MODEL)
    outs_p, outs_s = [], []
    for l in range(depth):
        p = _layer_params(l, ffn1_pre_g, ffn1_wg, ffn1_wu, ffn1_wd, ffn1_post_g, mix_pre_g, w_in, b_igate, b_fgate,
                          conv_w, conv_b, conv_ln_g, conv_ln_b, w_out, mix_post_g, ffn2_pre_g, ffn2_wg, ffn2_wu,
                          ffn2_wd, ffn2_post_g, final_g)
        x1, u, ym, caug, m = _ffn_in_mlstm(yp, p, batch, seq, PROMPT_TILE)
        yp, hist = _conv_ffn_out(u, ym, x1, p, batch, seq, PROMPT_TILE)
        outs_p.append(_split_state(caug, m, hist))
        x1, q, kt, v, og, u, gt = _ffn_in(ys, p, TOKEN_TILE)
        n0 = jnp.broadcast_to(state_mlstm_n[l].astype(F32)[..., None], (nseq, N_HEADS, HEAD_DIM, HEAD_DIM))
        caug0 = jnp.concatenate([state_mlstm_C[l].astype(F32), n0], axis=-1)
        m0 = jnp.broadcast_to(state_mlstm_m[l].astype(F32).T[:, :, None, None], (N_HEADS, nseq, dseq, HEAD_DIM))
        m0 = m0.reshape(N_HEADS, nseq * dseq, HEAD_DIM)
        hist0 = jnp.pad(cache_conv[l].astype(F32), ((0, 0), (HIST_PAD - HIST, 0), (0, 0)))
        mix, caug, m, hist = _mixer_sample(q, kt, v, og, u, gt, caug0, m0, hist0, p, nseq, dseq)
        ys = _ffn_out(mix, x1, p, TOKEN_TILE)
        outs_s.append(_split_state(caug, m, hist))
    stack = lambda outs, k: jnp.stack([o[k] for o in outs])
    return (yp.reshape(batch, seq, D_MODEL), ys.reshape(nseq, dseq, D_MODEL),
            stack(outs_p, 0), stack(outs_p, 1), stack(outs_p, 2), stack(outs_p, 3),
            stack(outs_s, 0), stack(outs_s, 1), stack(outs_s, 2), stack(outs_s, 3))
```

```python
import functools

import jax
import jax.numpy as jnp
from jax import lax
from jax.experimental import pallas as pl
from jax.experimental.pallas import tpu as pltpu

D_MODEL = 1024
D_FF = 2816
N_HEADS = 4
HEAD_DIM = 128
M_WIDTH = N_HEADS * HEAD_DIM
CONV_CH = 512
CONV_WIDTH = 31
HIST = CONV_WIDTH - 1
HIST_PAD = 32
SUBLANES = 8
EPS = 1e-6
CHUNK = 128
AUG = 2 * HEAD_DIM
CONV_ROWS = 128
LANES = 128
LN_ROWS = 32
NCH = 256

F32 = jnp.float32
BF16 = jnp.bfloat16
HIGHEST = lax.Precision.HIGHEST
NT_DIMS = (((1,), (1,)), ((), ()))

TOKEN_TILE = 256
PROMPT_TILE = 256
VMEM_LIMIT_TOKENWISE = 56 * 1024 * 1024
VMEM_LIMIT_MIXER = 40 * 1024 * 1024


def _rms(x, g):
    return x * lax.rsqrt(jnp.mean(x * x, axis=-1, keepdims=True) + EPS) * g


def _dot(a, b):
    return jnp.dot(a, b, preferred_element_type=F32)


def _swiglu(h, wg_ref, wu_ref, wd_ref):
    a = _dot(h, wg_ref[...])
    b = _dot(h, wu_ref[...])
    s = (a * jax.nn.sigmoid(a) * b).astype(BF16)
    return _dot(s, wd_ref[...])


class _Side:
    def __init__(self, make_pieces):
        self.zero = None
        self.pieces = make_pieces(self)

    def run(self, after, n=1):
        bits = pltpu.bitcast(after[after.shape[0] - SUBLANES:, 0:LANES], jnp.int32)
        self.zero = lax.shift_right_logical(lax.shift_right_logical(bits, 16), 16)[0, 0]
        if n is None:
            for _ in self.pieces:
                pass
        else:
            for _ in range(n):
                next(self.pieces, None)


def _swiglu_chunked(h_s, s_s, wg_ref, wu_ref, wd_ref, side, chunk, after_gating, down_pieces=1):
    for c0 in range(0, D_FF, chunk):
        cols = slice(c0, min(c0 + chunk, D_FF))
        a = _dot(h_s[...], wg_ref[:, cols])
        b = _dot(h_s[...], wu_ref[:, cols])
        s = a * jax.nn.sigmoid(a) * b
        s_s[:, cols] = s.astype(BF16)
        for t0 in range(0, cols.stop - cols.start, NCH):
            side.run((s if after_gating else a)[:, t0:t0 + LANES])
    d = []
    for c0 in range(0, D_MODEL, chunk):
        d.append(_dot(s_s[...], wd_ref[:, c0:c0 + chunk]))
        for t0 in range(0, chunk, NCH):
            side.run(d[-1][:, t0:t0 + LANES], down_pieces)
    return jnp.concatenate(d, axis=1)


def _ffn_in_values(x, g1_ref, wg_ref, wu_ref, wd_ref, gpost_ref, gmix_ref, wqvo_ref, wkt_ref, wconv_ref,
                   wgt_ref, gbias_ref):
    h = _rms(x, g1_ref[...]).astype(BF16)
    d = _swiglu(h, wg_ref, wu_ref, wd_ref)
    x1 = x + 0.5 * _rms(d, gpost_ref[...])
    h2 = _rms(x1, gmix_ref[...]).astype(BF16)
    qvo = _dot(h2, wqvo_ref[...])
    q = qvo[:, :M_WIDTH].astype(BF16)
    v = qvo[:, M_WIDTH:2 * M_WIDTH]
    og = jax.nn.sigmoid(qvo[:, 2 * M_WIDTH:])
    kt = lax.dot_general(wkt_ref[...], h2, NT_DIMS, preferred_element_type=F32) * (HEAD_DIM ** -0.5)
    cc = _dot(h2, wconv_ref[...])
    u = cc[:, :CONV_CH] * jax.nn.sigmoid(cc[:, CONV_CH:])
    g = lax.dot_general(wgt_ref[...], h2, NT_DIMS, preferred_element_type=F32) + gbias_ref[...]
    row = lax.broadcasted_iota(jnp.int32, g.shape, 0)
    gt = jnp.where(row < N_HEADS, g, jax.nn.log_sigmoid(g))
    return x1, q, kt.astype(BF16), v, og, u, gt


def _ffn_out_values(ym, c, x1, wout_ref, gmp_ref, g2_ref, wg_ref, wu_ref, wd_ref, gpost_ref, gfin_ref):
    o = _dot(ym.astype(BF16), wout_ref[0:M_WIDTH, :]) + _dot(c.astype(BF16), wout_ref[M_WIDTH:, :])
    x2 = x1 + _rms(o, gmp_ref[...])
    h = _rms(x2, g2_ref[...]).astype(BF16)
    d = _swiglu(h, wg_ref, wu_ref, wd_ref)
    x3 = x2 + 0.5 * _rms(d, gpost_ref[...])
    return _rms(x3, gfin_ref[...])


def _conv_taps(ubuf, row0, nrows, lanes, cw_ref, cb_ref):
    acc = jnp.broadcast_to(cb_ref[:, lanes], (nrows, lanes.stop - lanes.start))
    first = HIST_PAD - HIST
    for res in range(SUBLANES):
        taps = [j for j in range(CONV_WIDTH) if (first + j) % SUBLANES == res]
        lo = (first + taps[0]) // SUBLANES * SUBLANES
        hi = (first + taps[-1]) // SUBLANES * SUBLANES
        win = ubuf[row0 + lo:row0 + hi + nrows + (SUBLANES if res else 0), lanes]
        if res:
            win = pltpu.roll(win, win.shape[0] - res, 0)
        for j in taps:
            off = first + j - res - lo
            acc = acc + win[off:off + nrows, :] * cw_ref[j:j + 1, lanes]
    return acc


def _conv_taps_interleaved(uslab, row0, nrows, lanes, cw_ref, cb_ref):
    first = HIST_PAD - HIST
    half = nrows // 2
    acc = [jnp.broadcast_to(cb_ref[:, lanes], (half, LANES))] * 2
    for j in range(CONV_WIDTH):
        for parity in range(2):
            x = uslab[pl.ds(row0 + first + j + parity, half, stride=2), :]
            acc[parity] = acc[parity] + x * cw_ref[j:j + 1, lanes]
    return acc


def _ln_swish(acc, lg_ref, lb_ref):
    mu = jnp.mean(acc, axis=-1, keepdims=True)
    xc = acc - mu
    var = jnp.mean(xc * xc, axis=-1, keepdims=True)
    y = xc * lax.rsqrt(var + EPS) * lg_ref[...] + lb_ref[...]
    return y * jax.nn.sigmoid(y)


def _mlstm_gates(b_r, a_r, caus, qh, kth, caug):
    L = b_r.shape[1]
    arow = jnp.broadcast_to(a_r, (L, L))
    bcol = jnp.broadcast_to(b_r, (L, L)).T
    acol = arow.T
    dmat = jnp.where(caus, bcol + arow, -jnp.inf)
    return dict(bcol=bcol, acol=acol, dmat=dmat, rowmax=jnp.max(dmat, axis=1, keepdims=True),
                s_raw=_dot(qh, kth), qc=_dot(qh, caug.astype(BF16)), caug=caug, kth=kth)


def _mlstm_scores(st, mprev, vh):
    m_t = jnp.maximum(st["bcol"] + mprev, st["rowmax"])
    s = st["s_raw"] * jnp.exp(st["dmat"] - m_t)
    vaug = jnp.concatenate([vh, jnp.ones_like(vh)], axis=1)
    st.update(m_t=m_t, mprev=mprev, vaug=vaug, sv=_dot(s.astype(BF16), vaug.astype(BF16)),
              w_inter=jnp.exp(st["bcol"] + mprev - m_t))
    return m_t[m_t.shape[0] - 1:, :]


def _mlstm_output(st):
    L = st["m_t"].shape[0]
    m_t, qc, sv, w_inter = st["m_t"], st["qc"], st["sv"], st["w_inter"]
    num = w_inter * qc[:, :HEAD_DIM] + sv[:, :HEAD_DIM]
    den = w_inter * qc[:, HEAD_DIM:] + sv[:, HEAD_DIM:]
    hh = num / jnp.maximum(jnp.abs(den), jnp.exp(-m_t))
    m_new = m_t[L - 1:L, :]
    b_last = st["bcol"][L - 1:L, :]
    g_state = jnp.exp(b_last + st["mprev"] - m_new)
    g_rows = jnp.exp(b_last + st["acol"] - m_new)
    gv = (jnp.concatenate([g_rows, g_rows], axis=1) * st["vaug"]).astype(BF16)
    caug_new = jnp.concatenate([g_state, g_state], axis=1) * st["caug"] + _dot(st["kth"], gv)
    return hh, caug_new


def _ffn_in_mlstm_body(x_ref, g1_ref, wg_ref, wu_ref, wd_ref, gpost_ref, gmix_ref, wqvo_ref, wkt_ref, wconv_ref,
                       wgt_ref, gbias_ref, x1_ref, u_ref, ym_ref, caug_out, m_out,
                       q_st, kt_st, v_st, og_st, g_st, caug_s, m_s, h_s, s_s, *, tm, nt):
    g = pl.program_id(0)
    wslot = lax.rem(g, 2)
    rslot = 1 - wslot
    L = CHUNK

    @pl.when(g == 0)
    def _init():
        for st in (q_st, kt_st, v_st, og_st, g_st, caug_s, m_s):
            st[...] = jnp.zeros_like(st)

    def mlstm_pieces(side):
        fresh = lax.rem(g - 1, nt) == 0
        r = lax.broadcasted_iota(jnp.int32, (L, L), 0)
        c = lax.broadcasted_iota(jnp.int32, (L, L), 1)
        caus = c <= r
        triu = (r <= c).astype(F32)
        nck = tm // L
        items = [(ck, h) for ck in range(nck) for h in range(N_HEADS)]
        gates = {}
        stages = {}

        def stage1(ck, h):
            rows, hs, slot = slice(ck * L, (ck + 1) * L), slice(h * HEAD_DIM, (h + 1) * HEAD_DIM), rslot + side.zero
            if ck not in gates:
                gt = g_st[slot, :, rows]
                gates[ck] = gt, jnp.dot(gt, triu, precision=HIGHEST, preferred_element_type=F32)
            gt, b_rows = gates[ck]
            b_r = b_rows[N_HEADS + h:N_HEADS + h + 1, :]
            caug = caug_s[h + side.zero]
            if ck == 0:
                caug = jnp.where(fresh, 0.0, caug)
            stages[ck, h] = _mlstm_gates(b_r, gt[h:h + 1, :] - b_r, caus, q_st[slot, rows, hs], kt_st[slot, hs, rows],
                                         caug)

        def stage2(ck, h):
            rows, hs, slot = slice(ck * L, (ck + 1) * L), slice(h * HEAD_DIM, (h + 1) * HEAD_DIM), rslot + side.zero
            mprev = m_s[h + side.zero]
            if ck == 0:
                mprev = jnp.where(fresh, 0.0, mprev)
            m_new = _mlstm_scores(stages[ck, h], mprev, v_st[slot, rows, hs])
            m_s[h] = m_new
            if ck == nck - 1:
                m_out[0, h] = m_new

        def stage3(ck, h):
            rows, hs, slot = slice(ck * L, (ck + 1) * L), slice(h * HEAD_DIM, (h + 1) * HEAD_DIM), rslot + side.zero
            hh, caug = _mlstm_output(stages.pop((ck, h)))
            ym_ref[rows, hs] = og_st[slot, rows, hs] * hh
            caug_s[h] = caug
            if ck == nck - 1:
                caug_out[0, h] = caug

        for t in range(len(items) + 2):
            for lag, stage in enumerate((stage1, stage2, stage3)):
                if 0 <= t - lag < len(items):
                    stage(*items[t - lag])
            yield

    side = _Side(mlstm_pieces)

    h_s[...] = _rms(x_ref[...], g1_ref[...]).astype(BF16)
    d = _swiglu_chunked(h_s, s_s, wg_ref, wu_ref, wd_ref, side, 2 * NCH, False)
    x1 = x_ref[...] + 0.5 * _rms(d, gpost_ref[...])
    x1_ref[...] = x1
    h_s[...] = _rms(x1, gmix_ref[...]).astype(BF16)
    for c0 in range(0, 3 * M_WIDTH, NCH):
        r = _dot(h_s[...], wqvo_ref[:, c0:c0 + NCH])
        cols = slice(c0 % M_WIDTH, c0 % M_WIDTH + NCH)
        if c0 < M_WIDTH:
            q_st[wslot, :, cols] = r.astype(BF16)
        elif c0 < 2 * M_WIDTH:
            v_st[wslot, :, cols] = r
        else:
            og_st[wslot, :, cols] = jax.nn.sigmoid(r)
        side.run(r)
    for c0 in range(0, M_WIDTH, NCH):
        kt = lax.dot_general(wkt_ref[c0:c0 + NCH, :], h_s[...], NT_DIMS, preferred_element_type=F32)
        kt_st[wslot, c0:c0 + NCH, :] = (kt * (HEAD_DIM ** -0.5)).astype(BF16)
        side.run(kt)
    for c0 in range(0, CONV_CH, NCH):
        cv = _dot(h_s[...], wconv_ref[:, c0:c0 + NCH])
        cg = _dot(h_s[...], wconv_ref[:, CONV_CH + c0:CONV_CH + c0 + NCH])
        u_ref[:, c0:c0 + NCH] = cv * jax.nn.sigmoid(cg)
        side.run(cv)
    gg = lax.dot_general(wgt_ref[...], h_s[...], NT_DIMS, preferred_element_type=F32) + gbias_ref[...]
    row = lax.broadcasted_iota(jnp.int32, gg.shape, 0)
    g_st[wslot] = jnp.where(row < N_HEADS, gg, jax.nn.log_sigmoid(gg))
    side.run(gg, None)


def _conv_ffn_out_body(u_ref, ym_ref, x1_ref, cw_ref, cb_ref, lg_ref, lb_ref, wout_ref, gmp_ref, g2_ref,
                       wg_ref, wu_ref, wd_ref, gpost_ref, gfin_ref, y_ref, hist_out, ubuf, cbuf, *, tm, nt, ntiles):
    g = pl.program_id(0)
    wslot = lax.rem(g, 2)
    rslot = 1 - wslot

    @pl.when(g == 0)
    def _init():
        cbuf[...] = jnp.zeros_like(cbuf)

    nslab = CONV_CH // LANES

    @pl.when(lax.rem(jnp.minimum(g, ntiles - 1), nt) == 0)
    def _new_sequence():
        ubuf[:, 0:HIST_PAD, :] = jnp.zeros((nslab, HIST_PAD, LANES), F32)

    c_prev = jnp.concatenate([cbuf[rslot * nslab + lb] for lb in range(nslab)], axis=1)
    y_ref[...] = _ffn_out_values(ym_ref[...], c_prev, x1_ref[...], wout_ref, gmp_ref, g2_ref, wg_ref, wu_ref,
                                 wd_ref, gpost_ref, gfin_ref)

    for lb in range(nslab):
        ubuf[lb, HIST_PAD:HIST_PAD + tm, :] = u_ref[:, lb * LANES:(lb + 1) * LANES]
    for k in range(tm // CONV_ROWS):
        for lb in range(nslab):
            acc = _conv_taps_interleaved(ubuf.at[lb], k * CONV_ROWS, CONV_ROWS, slice(lb * LANES, (lb + 1) * LANES),
                                         cw_ref, cb_ref)
            for parity in range(2):
                cbuf[wslot * nslab + lb, pl.ds(k * CONV_ROWS + parity, CONV_ROWS // 2, stride=2), :] = acc[parity]
    for k in range(tm // LN_ROWS):
        rows = slice(k * LN_ROWS, (k + 1) * LN_ROWS)
        y = _ln_swish(jnp.concatenate([cbuf[wslot * nslab + lb, rows, :] for lb in range(nslab)], axis=1),
                      lg_ref, lb_ref)
        for lb in range(nslab):
            cbuf[wslot * nslab + lb, rows, :] = y[:, lb * LANES:(lb + 1) * LANES]
    tail = jnp.concatenate([ubuf[lb, tm:tm + HIST_PAD, :] for lb in range(nslab)], axis=1)
    hist_out[0] = tail
    for lb in range(nslab):
        ubuf[lb, 0:HIST_PAD, :] = tail[:, lb * LANES:(lb + 1) * LANES]


def _const_spec(shape):
    nd = len(shape)
    return pl.BlockSpec(shape, lambda *_: (0,) * nd, pipeline_mode=pl.Buffered(1))


def _ffn_in_consts(p, tm):
    gbias = jnp.broadcast_to(p["gbias"][:, None], (2 * N_HEADS, tm))
    return [p["ffn1_pre_g"], p["ffn1_wg"], p["ffn1_wu"], p["ffn1_wd"], p["ffn1_post_g"], p["mix_pre_g"],
            p["w_qvo"], p["w_kt"], p["w_conv"], p["w_gt"], gbias]


def _ffn_out_consts(p):
    return [p["w_out"], p["mix_post_g"], p["ffn2_pre_g"], p["ffn2_wg"], p["ffn2_wu"], p["ffn2_wd"],
            p["ffn2_post_g"], p["final_g"]]


def _ffn_in_mlstm(x2d, p, batch, seq, tm):
    m = batch * seq
    nt = seq // tm
    ntiles = m // tm
    cur = lambda w: pl.BlockSpec((tm, w), lambda g: (jnp.minimum(g, ntiles - 1), 0))
    prev = lambda w: pl.BlockSpec((tm, w), lambda g: (jnp.maximum(g - 1, 0), 0))
    prev_seq = lambda g: jnp.maximum(g - 1, 0) // nt
    consts = _ffn_in_consts(p, tm)
    return pl.pallas_call(
        functools.partial(_ffn_in_mlstm_body, tm=tm, nt=nt),
        grid=(ntiles + 1,),
        in_specs=[cur(D_MODEL)] + [_const_spec(c.shape) for c in consts],
        out_specs=[cur(D_MODEL), cur(CONV_CH), prev(M_WIDTH),
                   pl.BlockSpec((1, N_HEADS, HEAD_DIM, AUG), lambda g: (prev_seq(g), 0, 0, 0)),
                   pl.BlockSpec((1, N_HEADS, 1, HEAD_DIM), lambda g: (prev_seq(g), 0, 0, 0))],
        out_shape=[jax.ShapeDtypeStruct((m, D_MODEL), F32), jax.ShapeDtypeStruct((m, CONV_CH), F32),
                   jax.ShapeDtypeStruct((m, M_WIDTH), F32),
                   jax.ShapeDtypeStruct((batch, N_HEADS, HEAD_DIM, AUG), F32),
                   jax.ShapeDtypeStruct((batch, N_HEADS, 1, HEAD_DIM), F32)],
        scratch_shapes=[pltpu.VMEM((2, tm, M_WIDTH), BF16), pltpu.VMEM((2, M_WIDTH, tm), BF16),
                        pltpu.VMEM((2, tm, M_WIDTH), F32), pltpu.VMEM((2, tm, M_WIDTH), F32),
                        pltpu.VMEM((2, 2 * N_HEADS, tm), F32),
                        pltpu.VMEM((N_HEADS, HEAD_DIM, AUG), F32), pltpu.VMEM((N_HEADS, 1, HEAD_DIM), F32),
                        pltpu.VMEM((tm, D_MODEL), BF16), pltpu.VMEM((tm, D_FF), BF16)],
        compiler_params=pltpu.CompilerParams(dimension_semantics=("arbitrary",),
                                             vmem_limit_bytes=VMEM_LIMIT_TOKENWISE),
        name="ffn_in_mlstm",
    )(x2d, *consts)


def _conv_ffn_out(u, ym, x1, p, batch, seq, tm):
    m = batch * seq
    nt = seq // tm
    ntiles = m // tm
    cur = lambda w: pl.BlockSpec((tm, w), lambda g: (jnp.minimum(g, ntiles - 1), 0))
    prev = lambda w: pl.BlockSpec((tm, w), lambda g: (jnp.maximum(g - 1, 0), 0))
    consts = [p["conv_w"], p["conv_b"], p["conv_ln_g"], p["conv_ln_b"]] + _ffn_out_consts(p)
    return pl.pallas_call(
        functools.partial(_conv_ffn_out_body, tm=tm, nt=nt, ntiles=ntiles),
        grid=(ntiles + 1,),
        in_specs=[cur(CONV_CH), prev(M_WIDTH), prev(D_MODEL)] + [_const_spec(c.shape) for c in consts],
        out_specs=[prev(D_MODEL),
                   pl.BlockSpec((1, HIST_PAD, CONV_CH), lambda g: (jnp.minimum(g, ntiles - 1) // nt, 0, 0))],
        out_shape=[jax.ShapeDtypeStruct((m, D_MODEL), F32), jax.ShapeDtypeStruct((batch, HIST_PAD, CONV_CH), F32)],
        scratch_shapes=[pltpu.VMEM((CONV_CH // LANES, HIST_PAD + tm, LANES), F32),
                        pltpu.VMEM((2 * CONV_CH // LANES, tm, LANES), F32)],
        compiler_params=pltpu.CompilerParams(dimension_semantics=("arbitrary",),
                                             vmem_limit_bytes=VMEM_LIMIT_TOKENWISE),
        name="conv_ffn_out",
    )(u, ym, x1, *consts)


def _ffn_in_body(x_ref, g1_ref, wg_ref, wu_ref, wd_ref, gpost_ref, gmix_ref, wqvo_ref, wkt_ref, wconv_ref,
                 wgt_ref, gbias_ref, x1_ref, q_ref, kt_ref, v_ref, og_ref, u_ref, gt_ref):
    outs = _ffn_in_values(x_ref[...], g1_ref, wg_ref, wu_ref, wd_ref, gpost_ref, gmix_ref, wqvo_ref, wkt_ref,
                          wconv_ref, wgt_ref, gbias_ref)
    for ref, val in zip((x1_ref, q_ref, kt_ref, v_ref, og_ref, u_ref, gt_ref), outs):
        ref[...] = val


def _ffn_out_body(mix_ref, x1_ref, wout_ref, gmp_ref, g2_ref, wg_ref, wu_ref, wd_ref, gpost_ref, gfin_ref, y_ref):
    y_ref[...] = _ffn_out_values(mix_ref[:, :M_WIDTH], mix_ref[:, M_WIDTH:], x1_ref[...], wout_ref, gmp_ref, g2_ref,
                                 wg_ref, wu_ref, wd_ref, gpost_ref, gfin_ref)


def _mixer_sample_body(q_ref, kt_ref, v_ref, og_ref, u_ref, gt_ref, caug0_ref, m0_ref, hist0_ref,
                       cw_ref, cb_ref, lg_ref, lb_ref, mix_ref, caug_out, m_out, hist_out, ubuf, *, nseq, L):
    R = nseq * L
    r = lax.broadcasted_iota(jnp.int32, (R, R), 0)
    c = lax.broadcasted_iota(jnp.int32, (R, R), 1)
    same = lax.div(r, L) == lax.div(c, L)
    caus = same & (c <= r)
    segtriu = jnp.where(same & (r <= c), 1.0, 0.0).astype(F32)
    segones = jnp.where(same, 1.0, 0.0).astype(F32)
    rowseq = lax.div(lax.broadcasted_iota(jnp.int32, (R, AUG), 0), L)
    ones = jnp.ones((R, HEAD_DIM), F32)
    gt = gt_ref[...]
    b_rows = jnp.dot(gt, segtriu, precision=HIGHEST, preferred_element_type=F32)
    tot_rows = jnp.dot(gt, segones, precision=HIGHEST, preferred_element_type=F32)
    for h in range(N_HEADS):
        hs = slice(h * HEAD_DIM, (h + 1) * HEAD_DIM)
        b_r = b_rows[N_HEADS + h:N_HEADS + h + 1, :]
        a_r = gt[h:h + 1, :] - b_r
        t_r = tot_rows[N_HEADS + h:N_HEADS + h + 1, :]
        arow = jnp.broadcast_to(a_r, (R, R))
        bcol = jnp.broadcast_to(b_r, (R, R)).T
        acol = arow.T
        tcol = jnp.broadcast_to(t_r, (R, R)).T
        mprev = m0_ref[h]
        mprev = jnp.concatenate([mprev, mprev], axis=1)
        dmat = jnp.where(caus, bcol + arow, -jnp.inf)
        m_t = jnp.maximum(bcol + mprev, jnp.max(dmat, axis=1, keepdims=True))
        dend = jnp.where(same, tcol + arow, -jnp.inf)
        m_new = jnp.maximum(tcol + mprev, jnp.max(dend, axis=1, keepdims=True))
        qh = q_ref[:, hs]
        kth = kt_ref[hs, :]
        s = _dot(qh, kth) * jnp.exp(dmat - m_t)
        vaug = jnp.concatenate([v_ref[:, hs], ones], axis=1)
        w_inter = jnp.exp(bcol + mprev - m_t)
        qc = jnp.zeros((R, AUG), F32)
        for i in range(nseq):
            qc = jnp.where(rowseq == i, _dot(qh, caug0_ref[i, h].astype(BF16)), qc)
        sv = _dot(s.astype(BF16), vaug.astype(BF16))
        num = w_inter[:, :HEAD_DIM] * qc[:, :HEAD_DIM] + sv[:, :HEAD_DIM]
        den = w_inter[:, :HEAD_DIM] * qc[:, HEAD_DIM:] + sv[:, HEAD_DIM:]
        hh = num / jnp.maximum(jnp.abs(den), jnp.exp(-m_t[:, :HEAD_DIM]))
        mix_ref[:, hs] = og_ref[:, hs] * hh
        g_state = jnp.exp(tcol + mprev - m_new)
        g_rows = jnp.exp(tcol + acol - m_new)
        gv = g_rows * vaug
        for i in range(nseq):
            gvi = jnp.where(rowseq == i, gv, 0.0).astype(BF16)
            caug_out[i, h] = g_state[i * L:i * L + 1, :] * caug0_ref[i, h] + _dot(kth, gvi)
            m_out[i, h] = m_new[i * L:i * L + 1, :HEAD_DIM]

    for i in range(nseq):
        ubuf[0:HIST_PAD, :] = hist0_ref[i]
        ubuf[HIST_PAD:HIST_PAD + L, :] = u_ref[i * L:(i + 1) * L, :]
        acc = _conv_taps(ubuf, 0, L, slice(0, CONV_CH), cw_ref, cb_ref)
        mix_ref[i * L:(i + 1) * L, M_WIDTH:] = _ln_swish(acc, lg_ref, lb_ref)
        hist_out[i] = ubuf[L:L + HIST_PAD, :]


def _ffn_in(x2d, p, tm):
    m = x2d.shape[0]
    row = lambda w: pl.BlockSpec((tm, w), lambda i: (i, 0))
    col = lambda h: pl.BlockSpec((h, tm), lambda i: (0, i))
    consts = _ffn_in_consts(p, tm)
    return pl.pallas_call(
        _ffn_in_body,
        grid=(m // tm,),
        in_specs=[row(D_MODEL)] + [_const_spec(c.shape) for c in consts],
        out_specs=[row(D_MODEL), row(M_WIDTH), col(M_WIDTH), row(M_WIDTH), row(M_WIDTH), row(CONV_CH),
                   col(2 * N_HEADS)],
        out_shape=[jax.ShapeDtypeStruct((m, D_MODEL), F32), jax.ShapeDtypeStruct((m, M_WIDTH), BF16),
                   jax.ShapeDtypeStruct((M_WIDTH, m), BF16), jax.ShapeDtypeStruct((m, M_WIDTH), F32),
                   jax.ShapeDtypeStruct((m, M_WIDTH), F32), jax.ShapeDtypeStruct((m, CONV_CH), F32),
                   jax.ShapeDtypeStruct((2 * N_HEADS, m), F32)],
        compiler_params=pltpu.CompilerParams(dimension_semantics=("arbitrary",),
                                             vmem_limit_bytes=VMEM_LIMIT_TOKENWISE),
        name="ffn_in",
    )(x2d, *consts)


def _ffn_out(mix, x1, p, tm):
    m = mix.shape[0]
    row = pl.BlockSpec((tm, D_MODEL), lambda i: (i, 0))
    consts = _ffn_out_consts(p)
    return pl.pallas_call(
        _ffn_out_body,
        grid=(m // tm,),
        in_specs=[row, row] + [_const_spec(c.shape) for c in consts],
        out_specs=row,
        out_shape=jax.ShapeDtypeStruct((m, D_MODEL), F32),
        compiler_params=pltpu.CompilerParams(dimension_semantics=("arbitrary",),
                                             vmem_limit_bytes=VMEM_LIMIT_TOKENWISE),
        name="ffn_out",
    )(mix, x1, *consts)


def _mixer_sample(q, kt, v, og, u, gt, caug0, m0, hist0, p, nseq, L):
    rows = nseq * L
    args = [q, kt, v, og, u, gt, caug0, m0, hist0, p["conv_w"], p["conv_b"], p["conv_ln_g"], p["conv_ln_b"]]
    full = lambda a: pl.BlockSpec(a.shape, lambda i, nd=a.ndim: (0,) * nd)
    out_shape = [jax.ShapeDtypeStruct((rows, D_MODEL), F32),
                 jax.ShapeDtypeStruct((nseq, N_HEADS, HEAD_DIM, AUG), F32),
                 jax.ShapeDtypeStruct((nseq, N_HEADS, 1, HEAD_DIM), F32),
                 jax.ShapeDtypeStruct((nseq, HIST_PAD, CONV_CH), F32)]
    return pl.pallas_call(
        functools.partial(_mixer_sample_body, nseq=nseq, L=L),
        grid=(1,),
        in_specs=[full(a) for a in args],
        out_specs=[full(s) for s in out_shape],
        out_shape=out_shape,
        scratch_shapes=[pltpu.VMEM((HIST_PAD + L, CONV_CH), F32)],
        compiler_params=pltpu.CompilerParams(dimension_semantics=("arbitrary",),
                                             vmem_limit_bytes=VMEM_LIMIT_MIXER),
        name="mixer_sample",
    )(*args)


def _layer_params(l, ffn1_pre_g, ffn1_wg, ffn1_wu, ffn1_wd, ffn1_post_g, mix_pre_g, w_in, b_igate, b_fgate,
                  conv_w, conv_b, conv_ln_g, conv_ln_b, w_out, mix_post_g, ffn2_pre_g, ffn2_wg, ffn2_wu, ffn2_wd,
                  ffn2_post_g, final_g):
    vec = lambda a: a[l].astype(F32).reshape(1, -1)
    w = w_in[l]
    cuts = [0, M_WIDTH, 2 * M_WIDTH, 3 * M_WIDTH, 4 * M_WIDTH, 4 * M_WIDTH + N_HEADS, 4 * M_WIDTH + 2 * N_HEADS,
            4 * M_WIDTH + 2 * N_HEADS + CONV_CH, 4 * M_WIDTH + 2 * N_HEADS + 2 * CONV_CH]
    wq, wk, wv, wo, wi, wf, wcv, wcg = [w[:, a:b] for a, b in zip(cuts[:-1], cuts[1:])]
    return {
        "ffn1_pre_g": vec(ffn1_pre_g), "ffn1_post_g": vec(ffn1_post_g), "mix_pre_g": vec(mix_pre_g),
        "mix_post_g": vec(mix_post_g), "ffn2_pre_g": vec(ffn2_pre_g), "ffn2_post_g": vec(ffn2_post_g),
        "final_g": vec(final_g),
        "ffn1_wg": ffn1_wg[l].astype(BF16), "ffn1_wu": ffn1_wu[l].astype(BF16), "ffn1_wd": ffn1_wd[l].astype(BF16),
        "ffn2_wg": ffn2_wg[l].astype(BF16), "ffn2_wu": ffn2_wu[l].astype(BF16), "ffn2_wd": ffn2_wd[l].astype(BF16),
        "w_qvo": jnp.concatenate([wq, wv, wo], axis=1).astype(BF16),
        "w_kt": wk.T.astype(BF16),
        "w_conv": jnp.concatenate([wcv, wcg], axis=1).astype(BF16),
        "w_gt": jnp.concatenate([wi, wf], axis=1).T.astype(BF16),
        "gbias": jnp.concatenate([b_igate[l], b_fgate[l]]).astype(F32),
        "w_out": w_out[l].astype(BF16),
        "conv_w": conv_w[l].astype(F32), "conv_b": vec(conv_b), "conv_ln_g": vec(conv_ln_g),
        "conv_ln_b": vec(conv_ln_b),
    }


def _split_state(caug, m, hist):
    return caug[..., :HEAD_DIM], caug[..., HEAD_DIM], m[:, :, 0, 0], hist[:, HIST_PAD - HIST:, :]


def kernel(x_prompt, x_sample, state_mlstm_C, state_mlstm_n, state_mlstm_m, cache_conv, ffn1_pre_g, ffn1_wg,
           ffn1_wu, ffn1_wd, ffn1_post_g, mix_pre_g, w_in, b_igate, b_fgate, conv_w, conv_b, conv_ln_g, conv_ln_b,
           w_out, mix_post_g, ffn2_pre_g, ffn2_wg, ffn2_wu, ffn2_wd, ffn2_post_g, final_g):
    batch, seq, _ = x_prompt.shape
    nseq, dseq, _ = x_sample.shape
    depth = w_in.shape[0]
    assert seq % PROMPT_TILE == 0 and PROMPT_TILE % CHUNK == 0 and PROMPT_TILE % CONV_ROWS == 0
    assert (nseq * dseq) % TOKEN_TILE == 0 and dseq <= HIST_PAD
    yp = x_prompt.reshape(batch * seq, D_MODEL)
    ys = x_sample.reshape(nseq * dseq, D_MODEL)
    outs_p, outs_s = [], []
    for l in range(depth):
        p = _layer_params(l, ffn1_pre_g, ffn1_wg, ffn1_wu, ffn1_wd, ffn1_post_g, mix_pre_g, w_in, b_igate, b_fgate,
                          conv_w, conv_b, conv_ln_g, conv_ln_b, w_out, mix_post_g, ffn2_pre_g, ffn2_wg, ffn2_wu,
                          ffn2_wd, ffn2_post_g, final_g)
        x1, u, ym, caug, m = _ffn_in_mlstm(yp, p, batch, seq, PROMPT_TILE)
        yp, hist = _conv_ffn_out(u, ym, x1, p, batch, seq, PROMPT_TILE)
        outs_p.append(_split_state(caug, m, hist))
        x1, q, kt, v, og, u, gt = _ffn_in(ys, p, TOKEN_TILE)
        n0 = jnp.broadcast_to(state_mlstm_n[l].astype(F32)[..., None], (nseq, N_HEADS, HEAD_DIM, HEAD_DIM))
        caug0 = jnp.concatenate([state_mlstm_C[l].astype(F32), n0], axis=-1)
        m0 = jnp.broadcast_to(state_mlstm_m[l].astype(F32).T[:, :, None, None], (N_HEADS, nseq, dseq, HEAD_DIM))
        m0 = m0.reshape(N_HEADS, nseq * dseq, HEAD_DIM)
        hist0 = jnp.pad(cache_conv[l].astype(F32), ((0, 0), (HIST_PAD - HIST, 0), (0, 0)))
        mix, caug, m, hist = _mixer_sample(q, kt, v, og, u, gt, caug0, m0, hist0, p, nseq, dseq)
        ys = _ffn_out(mix, x1, p, TOKEN_TILE)
        outs_s.append(_split_state(caug, m, hist))
    stack = lambda outs, k: jnp.stack([o[k] for o in outs])
    return (yp.reshape(batch, seq, D_MODEL), ys.reshape(nseq, dseq, D_MODEL),
            stack(outs_p, 0), stack(outs_p, 1), stack(outs_p, 2), stack(outs_p, 3),
            stack(outs_s, 0), stack(outs_s, 1), stack(outs_s, 2), stack(outs_s, 3))
```

```python
import functools

import jax
import jax.numpy as jnp
from jax import lax
from jax.experimental import pallas as pl
from jax.experimental.pallas import tpu as pltpu

D_MODEL = 1024
D_FF = 2816
N_HEADS = 4
HEAD_DIM = 128
M_WIDTH = N_HEADS * HEAD_DIM
CONV_CH = 512
CONV_WIDTH = 31
HIST = CONV_WIDTH - 1
HIST_PAD = 32
SUBLANES = 8
EPS = 1e-6
CHUNK = 128
AUG = 2 * HEAD_DIM
CONV_ROWS = 128
LANES = 128
LN_ROWS = 32
NCH = 256

F32 = jnp.float32
BF16 = jnp.bfloat16
HIGHEST = lax.Precision.HIGHEST
NT_DIMS = (((1,), (1,)), ((), ()))

TOKEN_TILE = 256
PROMPT_TILE = 256
VMEM_LIMIT_TOKENWISE = 56 * 1024 * 1024
VMEM_LIMIT_MIXER = 40 * 1024 * 1024


def _rms(x, g):
    return x * lax.rsqrt(jnp.mean(x * x, axis=-1, keepdims=True) + EPS) * g


def _dot(a, b):
    return jnp.dot(a, b, preferred_element_type=F32)


def _swiglu(h, wg_ref, wu_ref, wd_ref):
    a = _dot(h, wg_ref[...])
    b = _dot(h, wu_ref[...])
    s = (a * jax.nn.sigmoid(a) * b).astype(BF16)
    return _dot(s, wd_ref[...])


class _Side:
    def __init__(self, make_pieces):
        self.zero = None
        self.pieces = make_pieces(self)

    def run(self, after, n=1):
        bits = pltpu.bitcast(after[after.shape[0] - SUBLANES:, 0:LANES], jnp.int32)
        self.zero = lax.shift_right_logical(lax.shift_right_logical(bits, 16), 16)[0, 0]
        if n is None:
            for _ in self.pieces:
                pass
        else:
            for _ in range(n):
                next(self.pieces, None)


def _swiglu_chunked(h_s, s_s, wg_ref, wu_ref, wd_ref, side, chunk, after_gating, down_pieces=1):
    for c0 in range(0, D_FF, chunk):
        cols = slice(c0, min(c0 + chunk, D_FF))
        a = _dot(h_s[...], wg_ref[:, cols])
        b = _dot(h_s[...], wu_ref[:, cols])
        s = a * jax.nn.sigmoid(a) * b
        s_s[:, cols] = s.astype(BF16)
        for t0 in range(0, cols.stop - cols.start, NCH):
            side.run((s if after_gating else a)[:, t0:t0 + LANES])
    d = []
    for c0 in range(0, D_MODEL, chunk):
        d.append(_dot(s_s[...], wd_ref[:, c0:c0 + chunk]))
        for t0 in range(0, chunk, NCH):
            side.run(d[-1][:, t0:t0 + LANES], down_pieces)
    return jnp.concatenate(d, axis=1)


def _ffn_in_values(x, g1_ref, wg_ref, wu_ref, wd_ref, gpost_ref, gmix_ref, wqvo_ref, wkt_ref, wconv_ref,
                   wgt_ref, gbias_ref):
    h = _rms(x, g1_ref[...]).astype(BF16)
    d = _swiglu(h, wg_ref, wu_ref, wd_ref)
    x1 = x + 0.5 * _rms(d, gpost_ref[...])
    h2 = _rms(x1, gmix_ref[...]).astype(BF16)
    qvo = _dot(h2, wqvo_ref[...])
    q = qvo[:, :M_WIDTH].astype(BF16)
    v = qvo[:, M_WIDTH:2 * M_WIDTH]
    og = jax.nn.sigmoid(qvo[:, 2 * M_WIDTH:])
    kt = lax.dot_general(wkt_ref[...], h2, NT_DIMS, preferred_element_type=F32) * (HEAD_DIM ** -0.5)
    cc = _dot(h2, wconv_ref[...])
    u = cc[:, :CONV_CH] * jax.nn.sigmoid(cc[:, CONV_CH:])
    g = lax.dot_general(wgt_ref[...], h2, NT_DIMS, preferred_element_type=F32) + gbias_ref[...]
    row = lax.broadcasted_iota(jnp.int32, g.shape, 0)
    gt = jnp.where(row < N_HEADS, g, jax.nn.log_sigmoid(g))
    return x1, q, kt.astype(BF16), v, og, u, gt


def _ffn_out_values(ym, c, x1, wout_ref, gmp_ref, g2_ref, wg_ref, wu_ref, wd_ref, gpost_ref, gfin_ref):
    o = _dot(ym.astype(BF16), wout_ref[0:M_WIDTH, :]) + _dot(c.astype(BF16), wout_ref[M_WIDTH:, :])
    x2 = x1 + _rms(o, gmp_ref[...])
    h = _rms(x2, g2_ref[...]).astype(BF16)
    d = _swiglu(h, wg_ref, wu_ref, wd_ref)
    x3 = x2 + 0.5 * _rms(d, gpost_ref[...])
    return _rms(x3, gfin_ref[...])


def _conv_taps(ubuf, row0, nrows, lanes, cw_ref, cb_ref):
    acc = jnp.broadcast_to(cb_ref[:, lanes], (nrows, lanes.stop - lanes.start))
    first = HIST_PAD - HIST
    for res in range(SUBLANES):
        taps = [j for j in range(CONV_WIDTH) if (first + j) % SUBLANES == res]
        lo = (first + taps[0]) // SUBLANES * SUBLANES
        hi = (first + taps[-1]) // SUBLANES * SUBLANES
        win = ubuf[row0 + lo:row0 + hi + nrows + (SUBLANES if res else 0), lanes]
        if res:
            win = pltpu.roll(win, win.shape[0] - res, 0)
        for j in taps:
            off = first + j - res - lo
            acc = acc + win[off:off + nrows, :] * cw_ref[j:j + 1, lanes]
    return acc


def _conv_taps_interleaved(uslab, row0, nrows, lanes, cw_ref, cb_ref):
    first = HIST_PAD - HIST
    half = nrows // 2
    acc = [jnp.broadcast_to(cb_ref[:, lanes], (half, LANES))] * 2
    for j in range(CONV_WIDTH):
        for parity in range(2):
            x = uslab[pl.ds(row0 + first + j + parity, half, stride=2), :]
            acc[parity] = acc[parity] + x * cw_ref[j:j + 1, lanes]
    return acc


def _ln_swish(acc, lg_ref, lb_ref):
    mu = jnp.mean(acc, axis=-1, keepdims=True)
    xc = acc - mu
    var = jnp.mean(xc * xc, axis=-1, keepdims=True)
    y = xc * lax.rsqrt(var + EPS) * lg_ref[...] + lb_ref[...]
    return y * jax.nn.sigmoid(y)


def _mlstm_gates(b_r, a_r, caus, qh, kth, caug):
    L = b_r.shape[1]
    arow = jnp.broadcast_to(a_r, (L, L))
    bcol = jnp.broadcast_to(b_r, (L, L)).T
    acol = arow.T
    dmat = jnp.where(caus, bcol + arow, -jnp.inf)
    return dict(bcol=bcol, acol=acol, dmat=dmat, rowmax=jnp.max(dmat, axis=1, keepdims=True),
                s_raw=_dot(qh, kth), qc=_dot(qh, caug.astype(BF16)), caug=caug, kth=kth)


def _mlstm_scores(st, mprev, vh):
    m_t = jnp.maximum(st["bcol"] + mprev, st["rowmax"])
    s = st["s_raw"] * jnp.exp(st["dmat"] - m_t)
    vaug = jnp.concatenate([vh, jnp.ones_like(vh)], axis=1)
    st.update(m_t=m_t, mprev=mprev, vaug=vaug, sv=_dot(s.astype(BF16), vaug.astype(BF16)),
              w_inter=jnp.exp(st["bcol"] + mprev - m_t))
    return m_t[m_t.shape[0] - 1:, :]


def _mlstm_output(st):
    L = st["m_t"].shape[0]
    m_t, qc, sv, w_inter = st["m_t"], st["qc"], st["sv"], st["w_inter"]
    num = w_inter * qc[:, :HEAD_DIM] + sv[:, :HEAD_DIM]
    den = w_inter * qc[:, HEAD_DIM:] + sv[:, HEAD_DIM:]
    hh = num / jnp.maximum(jnp.abs(den), jnp.exp(-m_t))
    m_new = m_t[L - 1:L, :]
    b_last = st["bcol"][L - 1:L, :]
    g_state = jnp.exp(b_last + st["mprev"] - m_new)
    g_rows = jnp.exp(b_last + st["acol"] - m_new)
    gv = (jnp.concatenate([g_rows, g_rows], axis=1) * st["vaug"]).astype(BF16)
    caug_new = jnp.concatenate([g_state, g_state], axis=1) * st["caug"] + _dot(st["kth"], gv)
    return hh, caug_new


def _ffn_in_mlstm_body(x_ref, g1_ref, wg_ref, wu_ref, wd_ref, gpost_ref, gmix_ref, wqvo_ref, wkt_ref, wconv_ref,
                       wgt_ref, gbias_ref, x1_ref, u_ref, ym_ref, caug_out, m_out,
                       q_st, kt_st, v_st, og_st, g_st, caug_s, m_s, h_s, s_s, *, tm, nt):
    g = pl.program_id(0)
    wslot = lax.rem(g, 2)
    rslot = 1 - wslot
    L = CHUNK

    @pl.when(g == 0)
    def _init():
        for st in (q_st, kt_st, v_st, og_st, g_st, caug_s, m_s):
            st[...] = jnp.zeros_like(st)

    def mlstm_pieces(side):
        fresh = lax.rem(g - 1, nt) == 0
        r = lax.broadcasted_iota(jnp.int32, (L, L), 0)
        c = lax.broadcasted_iota(jnp.int32, (L, L), 1)
        caus = c <= r
        triu = (r <= c).astype(F32)
        nck = tm // L
        items = [(ck, h) for ck in range(nck) for h in range(N_HEADS)]
        gates = {}
        stages = {}

        def stage1(ck, h):
            rows, hs, slot = slice(ck * L, (ck + 1) * L), slice(h * HEAD_DIM, (h + 1) * HEAD_DIM), rslot + side.zero
            if ck not in gates:
                gt = g_st[slot, :, rows]
                gates[ck] = gt, jnp.dot(gt, triu, precision=HIGHEST, preferred_element_type=F32)
            gt, b_rows = gates[ck]
            b_r = b_rows[N_HEADS + h:N_HEADS + h + 1, :]
            caug = caug_s[h + side.zero]
            if ck == 0:
                caug = jnp.where(fresh, 0.0, caug)
            stages[ck, h] = _mlstm_gates(b_r, gt[h:h + 1, :] - b_r, caus, q_st[slot, rows, hs], kt_st[slot, hs, rows],
                                         caug)

        def stage2(ck, h):
            rows, hs, slot = slice(ck * L, (ck + 1) * L), slice(h * HEAD_DIM, (h + 1) * HEAD_DIM), rslot + side.zero
            mprev = m_s[h + side.zero]
            if ck == 0:
                mprev = jnp.where(fresh, 0.0, mprev)
            m_new = _mlstm_scores(stages[ck, h], mprev, v_st[slot, rows, hs])
            m_s[h] = m_new
            if ck == nck - 1:
                m_out[0, h] = m_new

        def stage3(ck, h):
            rows, hs, slot = slice(ck * L, (ck + 1) * L), slice(h * HEAD_DIM, (h + 1) * HEAD_DIM), rslot + side.zero
            hh, caug = _mlstm_output(stages.pop((ck, h)))
            ym_ref[rows, hs] = og_st[slot, rows, hs] * hh
            caug_s[h] = caug
            if ck == nck - 1:
                caug_out[0, h] = caug

        for t in range(len(items) + 2):
            for lag, stage in enumerate((stage1, stage2, stage3)):
                if 0 <= t - lag < len(items):
                    stage(*items[t - lag])
            yield

    side = _Side(mlstm_pieces)

    h_s[...] = _rms(x_ref[...], g1_ref[...]).astype(BF16)
    d = _swiglu_chunked(h_s, s_s, wg_ref, wu_ref, wd_ref, side, 2 * NCH, False)
    x1 = x_ref[...] + 0.5 * _rms(d, gpost_ref[...])
    x1_ref[...] = x1
    h_s[...] = _rms(x1, gmix_ref[...]).astype(BF16)
    for c0 in range(0, 3 * M_WIDTH, NCH):
        r = _dot(h_s[...], wqvo_ref[:, c0:c0 + NCH])
        cols = slice(c0 % M_WIDTH, c0 % M_WIDTH + NCH)
        if c0 < M_WIDTH:
            q_st[wslot, :, cols] = r.astype(BF16)
        elif c0 < 2 * M_WIDTH:
            v_st[wslot, :, cols] = r
        else:
            og_st[wslot, :, cols] = jax.nn.sigmoid(r)
        side.run(r)
    for c0 in range(0, M_WIDTH, NCH):
        kt = lax.dot_general(wkt_ref[c0:c0 + NCH, :], h_s[...], NT_DIMS, preferred_element_type=F32)
        kt_st[wslot, c0:c0 + NCH, :] = (kt * (HEAD_DIM ** -0.5)).astype(BF16)
        side.run(kt)
    for c0 in range(0, CONV_CH, NCH):
        cv = _dot(h_s[...], wconv_ref[:, c0:c0 + NCH])
        cg = _dot(h_s[...], wconv_ref[:, CONV_CH + c0:CONV_CH + c0 + NCH])
        u_ref[:, c0:c0 + NCH] = cv * jax.nn.sigmoid(cg)
        side.run(cv)
    gg = lax.dot_general(wgt_ref[...], h_s[...], NT_DIMS, preferred_element_type=F32) + gbias_ref[...]
    row = lax.broadcasted_iota(jnp.int32, gg.shape, 0)
    g_st[wslot] = jnp.where(row < N_HEADS, gg, jax.nn.log_sigmoid(gg))
    side.run(gg, None)


def _conv_ffn_out_body(u_ref, ym_ref, x1_ref, cw_ref, cb_ref, lg_ref, lb_ref, wout_ref, gmp_ref, g2_ref,
                       wg_ref, wu_ref, wd_ref, gpost_ref, gfin_ref, y_ref, hist_out, ubuf, cbuf, *, tm, nt, ntiles):
    g = pl.program_id(0)
    wslot = lax.rem(g, 2)
    rslot = 1 - wslot

    @pl.when(g == 0)
    def _init():
        cbuf[...] = jnp.zeros_like(cbuf)

    nslab = CONV_CH // LANES

    @pl.when(lax.rem(jnp.minimum(g, ntiles - 1), nt) == 0)
    def _new_sequence():
        ubuf[:, 0:HIST_PAD, :] = jnp.zeros((nslab, HIST_PAD, LANES), F32)

    c_prev = jnp.concatenate([cbuf[rslot * nslab + lb] for lb in range(nslab)], axis=1)
    y_ref[...] = _ffn_out_values(ym_ref[...], c_prev, x1_ref[...], wout_ref, gmp_ref, g2_ref, wg_ref, wu_ref,
                                 wd_ref, gpost_ref, gfin_ref)

    for lb in range(nslab):
        ubuf[lb, HIST_PAD:HIST_PAD + tm, :] = u_ref[:, lb * LANES:(lb + 1) * LANES]
    for k in range(tm // CONV_ROWS):
        for lb in range(nslab):
            acc = _conv_taps_interleaved(ubuf.at[lb], k * CONV_ROWS, CONV_ROWS, slice(lb * LANES, (lb + 1) * LANES),
                                         cw_ref, cb_ref)
            for parity in range(2):
                cbuf[wslot * nslab + lb, pl.ds(k * CONV_ROWS + parity, CONV_ROWS // 2, stride=2), :] = acc[parity]
    for k in range(tm // LN_ROWS):
        rows = slice(k * LN_ROWS, (k + 1) * LN_ROWS)
        y = _ln_swish(jnp.concatenate([cbuf[wslot * nslab + lb, rows, :] for lb in range(nslab)], axis=1),
                      lg_ref, lb_ref)
        for lb in range(nslab):
            cbuf[wslot * nslab + lb, rows, :] = y[:, lb * LANES:(lb + 1) * LANES]
    tail = jnp.concatenate([ubuf[lb, tm:tm + HIST_PAD, :] for lb in range(nslab)], axis=1)
    hist_out[0] = tail
    for lb in range(nslab):
        ubuf[lb, 0:HIST_PAD, :] = tail[:, lb * LANES:(lb + 1) * LANES]


def _const_spec(shape):
    nd = len(shape)
    return pl.BlockSpec(shape, lambda *_: (0,) * nd, pipeline_mode=pl.Buffered(1))


def _ffn_in_consts(p, tm):
    gbias = jnp.broadcast_to(p["gbias"][:, None], (2 * N_HEADS, tm))
    return [p["ffn1_pre_g"], p["ffn1_wg"], p["ffn1_wu"], p["ffn1_wd"], p["ffn1_post_g"], p["mix_pre_g"],
            p["w_qvo"], p["w_kt"], p["w_conv"], p["w_gt"], gbias]


def _ffn_out_consts(p):
    return [p["w_out"], p["mix_post_g"], p["ffn2_pre_g"], p["ffn2_wg"], p["ffn2_wu"], p["ffn2_wd"],
            p["ffn2_post_g"], p["final_g"]]


def _ffn_in_mlstm(x2d, p, batch, seq, tm):
    m = batch * seq
    nt = seq // tm
    ntiles = m // tm
    cur = lambda w: pl.BlockSpec((tm, w), lambda g: (jnp.minimum(g, ntiles - 1), 0))
    prev = lambda w: pl.BlockSpec((tm, w), lambda g: (jnp.maximum(g - 1, 0), 0))
    prev_seq = lambda g: jnp.maximum(g - 1, 0) // nt
    consts = _ffn_in_consts(p, tm)
    return pl.pallas_call(
        functools.partial(_ffn_in_mlstm_body, tm=tm, nt=nt),
        grid=(ntiles + 1,),
        in_specs=[cur(D_MODEL)] + [_const_spec(c.shape) for c in consts],
        out_specs=[cur(D_MODEL), cur(CONV_CH), prev(M_WIDTH),
                   pl.BlockSpec((1, N_HEADS, HEAD_DIM, AUG), lambda g: (prev_seq(g), 0, 0, 0)),
                   pl.BlockSpec((1, N_HEADS, 1, HEAD_DIM), lambda g: (prev_seq(g), 0, 0, 0))],
        out_shape=[jax.ShapeDtypeStruct((m, D_MODEL), F32), jax.ShapeDtypeStruct((m, CONV_CH), F32),
                   jax.ShapeDtypeStruct((m, M_WIDTH), F32),
                   jax.ShapeDtypeStruct((batch, N_HEADS, HEAD_DIM, AUG), F32),
                   jax.ShapeDtypeStruct((batch, N_HEADS, 1, HEAD_DIM), F32)],
        scratch_shapes=[pltpu.VMEM((2, tm, M_WIDTH), BF16), pltpu.VMEM((2, M_WIDTH, tm), BF16),
                        pltpu.VMEM((2, tm, M_WIDTH), F32), pltpu.VMEM((2, tm, M_WIDTH), F32),
                        pltpu.VMEM((2, 2 * N_HEADS, tm), F32),
                        pltpu.VMEM((N_HEADS, HEAD_DIM, AUG), F32), pltpu.VMEM((N_HEADS, 1, HEAD_DIM), F32),
                        pltpu.VMEM((tm, D_MODEL), BF16), pltpu.VMEM((tm, D_FF), BF16)],
        compiler_params=pltpu.CompilerParams(dimension_semantics=("arbitrary",),
                                             vmem_limit_bytes=VMEM_LIMIT_TOKENWISE),
        name="ffn_in_mlstm",
    )(x2d, *consts)


def _conv_ffn_out(u, ym, x1, p, batch, seq, tm):
    m = batch * seq
    nt = seq // tm
    ntiles = m // tm
    cur = lambda w: pl.BlockSpec((tm, w), lambda g: (jnp.minimum(g, ntiles - 1), 0))
    prev = lambda w: pl.BlockSpec((tm, w), lambda g: (jnp.maximum(g - 1, 0), 0))
    consts = [p["conv_w"], p["conv_b"], p["conv_ln_g"], p["conv_ln_b"]] + _ffn_out_consts(p)
    return pl.pallas_call(
        functools.partial(_conv_ffn_out_body, tm=tm, nt=nt, ntiles=ntiles),
        grid=(ntiles + 1,),
        in_specs=[cur(CONV_CH), prev(M_WIDTH), prev(D_MODEL)] + [_const_spec(c.shape) for c in consts],
        out_specs=[prev(D_MODEL),
                   pl.BlockSpec((1, HIST_PAD, CONV_CH), lambda g: (jnp.minimum(g, ntiles - 1) // nt, 0, 0))],
        out_shape=[jax.ShapeDtypeStruct((m, D_MODEL), F32), jax.ShapeDtypeStruct((batch, HIST_PAD, CONV_CH), F32)],
        scratch_shapes=[pltpu.VMEM((CONV_CH // LANES, HIST_PAD + tm, LANES), F32),
                        pltpu.VMEM((2 * CONV_CH // LANES, tm, LANES), F32)],
        compiler_params=pltpu.CompilerParams(dimension_semantics=("arbitrary",),
                                             vmem_limit_bytes=VMEM_LIMIT_TOKENWISE),
        name="conv_ffn_out",
    )(u, ym, x1, *consts)


def _ffn_in_body(x_ref, g1_ref, wg_ref, wu_ref, wd_ref, gpost_ref, gmix_ref, wqvo_ref, wkt_ref, wconv_ref,
                 wgt_ref, gbias_ref, x1_ref, q_ref, kt_ref, v_ref, og_ref, u_ref, gt_ref):
    outs = _ffn_in_values(x_ref[...], g1_ref, wg_ref, wu_ref, wd_ref, gpost_ref, gmix_ref, wqvo_ref, wkt_ref,
                          wconv_ref, wgt_ref, gbias_ref)
    for ref, val in zip((x1_ref, q_ref, kt_ref, v_ref, og_ref, u_ref, gt_ref), outs):
        ref[...] = val


def _ffn_out_body(mix_ref, x1_ref, wout_ref, gmp_ref, g2_ref, wg_ref, wu_ref, wd_ref, gpost_ref, gfin_ref, y_ref):
    y_ref[...] = _ffn_out_values(mix_ref[:, :M_WIDTH], mix_ref[:, M_WIDTH:], x1_ref[...], wout_ref, gmp_ref, g2_ref,
                                 wg_ref, wu_ref, wd_ref, gpost_ref, gfin_ref)


def _mixer_sample_body(q_ref, kt_ref, v_ref, og_ref, u_ref, gt_ref, caug0_ref, m0_ref, hist0_ref,
                       cw_ref, cb_ref, lg_ref, lb_ref, mix_ref, caug_out, m_out, hist_out, ubuf, *, nseq, L):
    R = nseq * L
    r = lax.broadcasted_iota(jnp.int32, (R, R), 0)
    c = lax.broadcasted_iota(jnp.int32, (R, R), 1)
    same = lax.div(r, L) == lax.div(c, L)
    caus = same & (c <= r)
    segtriu = jnp.where(same & (r <= c), 1.0, 0.0).astype(F32)
    segones = jnp.where(same, 1.0, 0.0).astype(F32)
    rowseq = lax.div(lax.broadcasted_iota(jnp.int32, (R, AUG), 0), L)
    ones = jnp.ones((R, HEAD_DIM), F32)
    gt = gt_ref[...]
    b_rows = jnp.dot(gt, segtriu, precision=HIGHEST, preferred_element_type=F32)
    tot_rows = jnp.dot(gt, segones, precision=HIGHEST, preferred_element_type=F32)
    for h in range(N_HEADS):
        hs = slice(h * HEAD_DIM, (h + 1) * HEAD_DIM)
        b_r = b_rows[N_HEADS + h:N_HEADS + h + 1, :]
        a_r = gt[h:h + 1, :] - b_r
        t_r = tot_rows[N_HEADS + h:N_HEADS + h + 1, :]
        arow = jnp.broadcast_to(a_r, (R, R))
        bcol = jnp.broadcast_to(b_r, (R, R)).T
        acol = arow.T
        tcol = jnp.broadcast_to(t_r, (R, R)).T
        mprev = m0_ref[h]
        mprev = jnp.concatenate([mprev, mprev], axis=1)
        dmat = jnp.where(caus, bcol + arow, -jnp.inf)
        m_t = jnp.maximum(bcol + mprev, jnp.max(dmat, axis=1, keepdims=True))
        dend = jnp.where(same, tcol + arow, -jnp.inf)
        m_new = jnp.maximum(tcol + mprev, jnp.max(dend, axis=1, keepdims=True))
        qh = q_ref[:, hs]
        kth = kt_ref[hs, :]
        s = _dot(qh, kth) * jnp.exp(dmat - m_t)
        vaug = jnp.concatenate([v_ref[:, hs], ones], axis=1)
        w_inter = jnp.exp(bcol + mprev - m_t)
        qc = jnp.zeros((R, AUG), F32)
        for i in range(nseq):
            qc = jnp.where(rowseq == i, _dot(qh, caug0_ref[i, h].astype(BF16)), qc)
        sv = _dot(s.astype(BF16), vaug.astype(BF16))
        num = w_inter[:, :HEAD_DIM] * qc[:, :HEAD_DIM] + sv[:, :HEAD_DIM]
        den = w_inter[:, :HEAD_DIM] * qc[:, HEAD_DIM:] + sv[:, HEAD_DIM:]
        hh = num / jnp.maximum(jnp.abs(den), jnp.exp(-m_t[:, :HEAD_DIM]))
        mix_ref[:, hs] = og_ref[:, hs] * hh
        g_state = jnp.exp(tcol + mprev - m_new)
        g_rows = jnp.exp(tcol + acol - m_new)
        gv = g_rows * vaug
        for i in range(nseq):
            gvi = jnp.where(rowseq == i, gv, 0.0).astype(BF16)
            caug_out[i, h] = g_state[i * L:i * L + 1, :] * caug0_ref[i, h] + _dot(kth, gvi)
            m_out[i, h] = m_new[i * L:i * L + 1, :HEAD_DIM]

    for i in range(nseq):
        ubuf[0:HIST_PAD, :] = hist0_ref[i]
        ubuf[HIST_PAD:HIST_PAD + L, :] = u_ref[i * L:(i + 1) * L, :]
        acc = _conv_taps(ubuf, 0, L, slice(0, CONV_CH), cw_ref, cb_ref)
        mix_ref[i * L:(i + 1) * L, M_WIDTH:] = _ln_swish(acc, lg_ref, lb_ref)
        hist_out[i] = ubuf[L:L + HIST_PAD, :]


def _ffn_in(x2d, p, tm):
    m = x2d.shape[0]
    row = lambda w: pl.BlockSpec((tm, w), lambda i: (i, 0))
    col = lambda h: pl.BlockSpec((h, tm), lambda i: (0, i))
    consts = _ffn_in_consts(p, tm)
    return pl.pallas_call(
        _ffn_in_body,
        grid=(m // tm,),
        in_specs=[row(D_MODEL)] + [_const_spec(c.shape) for c in consts],
        out_specs=[row(D_MODEL), row(M_WIDTH), col(M_WIDTH), row(M_WIDTH), row(M_WIDTH), row(CONV_CH),
                   col(2 * N_HEADS)],
        out_shape=[jax.ShapeDtypeStruct((m, D_MODEL), F32), jax.ShapeDtypeStruct((m, M_WIDTH), BF16),
                   jax.ShapeDtypeStruct((M_WIDTH, m), BF16), jax.ShapeDtypeStruct((m, M_WIDTH), F32),
                   jax.ShapeDtypeStruct((m, M_WIDTH), F32), jax.ShapeDtypeStruct((m, CONV_CH), F32),
                   jax.ShapeDtypeStruct((2 * N_HEADS, m), F32)],
        compiler_params=pltpu.CompilerParams(dimension_semantics=("arbitrary",),
                                             vmem_limit_bytes=VMEM_LIMIT_TOKENWISE),
        name="ffn_in",
    )(x2d, *consts)


def _ffn_out(mix, x1, p, tm):
    m = mix.shape[0]
    row = pl.BlockSpec((tm, D_MODEL), lambda i: (i, 0))
    consts = _ffn_out_consts(p)
    return pl.pallas_call(
        _ffn_out_body,
        grid=(m // tm,),
        in_specs=[row, row] + [_const_spec(c.shape) for c in consts],
        out_specs=row,
        out_shape=jax.ShapeDtypeStruct((m, D_MODEL), F32),
        compiler_params=pltpu.CompilerParams(dimension_semantics=("arbitrary",),
                                             vmem_limit_bytes=VMEM_LIMIT_TOKENWISE),
        name="ffn_out",
    )(mix, x1, *consts)


def _mixer_sample(q, kt, v, og, u, gt, caug0, m0, hist0, p, nseq, L):
    rows = nseq * L
    args = [q, kt, v, og, u, gt, caug0, m0, hist0, p["conv_w"], p["conv_b"], p["conv_ln_g"], p["conv_ln_b"]]
    full = lambda a: pl.BlockSpec(a.shape, lambda i, nd=a.ndim: (0,) * nd)
    out_shape = [jax.ShapeDtypeStruct((rows, D_MODEL), F32),
                 jax.ShapeDtypeStruct((nseq, N_HEADS, HEAD_DIM, AUG), F32),
                 jax.ShapeDtypeStruct((nseq, N_HEADS, 1, HEAD_DIM), F32),
                 jax.ShapeDtypeStruct((nseq, HIST_PAD, CONV_CH), F32)]
    return pl.pallas_call(
        functools.partial(_mixer_sample_body, nseq=nseq, L=L),
        grid=(1,),
        in_specs=[full(a) for a in args],
        out_specs=[full(s) for s in out_shape],
        out_shape=out_shape,
        scratch_shapes=[pltpu.VMEM((HIST_PAD + L, CONV_CH), F32)],
        compiler_params=pltpu.CompilerParams(dimension_semantics=("arbitrary",),
                                             vmem_limit_bytes=VMEM_LIMIT_MIXER),
        name="mixer_sample",
    )(*args)


def _layer_params(l, ffn1_pre_g, ffn1_wg, ffn1_wu, ffn1_wd, ffn1_post_g, mix_pre_g, w_in, b_igate, b_fgate,
                  conv_w, conv_b, conv_ln_g, conv_ln_b, w_out, mix_post_g, ffn2_pre_g, ffn2_wg, ffn2_wu, ffn2_wd,
                  ffn2_post_g, final_g):
    vec = lambda a: a[l].astype(F32).reshape(1, -1)
    w = w_in[l]
    cuts = [0, M_WIDTH, 2 * M_WIDTH, 3 * M_WIDTH, 4 * M_WIDTH, 4 * M_WIDTH + N_HEADS, 4 * M_WIDTH + 2 * N_HEADS,
            4 * M_WIDTH + 2 * N_HEADS + CONV_CH, 4 * M_WIDTH + 2 * N_HEADS + 2 * CONV_CH]
    wq, wk, wv, wo, wi, wf, wcv, wcg = [w[:, a:b] for a, b in zip(cuts[:-1], cuts[1:])]
    return {
        "ffn1_pre_g": vec(ffn1_pre_g), "ffn1_post_g": vec(ffn1_post_g), "mix_pre_g": vec(mix_pre_g),
        "mix_post_g": vec(mix_post_g), "ffn2_pre_g": vec(ffn2_pre_g), "ffn2_post_g": vec(ffn2_post_g),
        "final_g": vec(final_g),
        "ffn1_wg": ffn1_wg[l].astype(BF16), "ffn1_wu": ffn1_wu[l].astype(BF16), "ffn1_wd": ffn1_wd[l].astype(BF16),
        "ffn2_wg": ffn2_wg[l].astype(BF16), "ffn2_wu": ffn2_wu[l].astype(BF16), "ffn2_wd": ffn2_wd[l].astype(BF16),
        "w_qvo": jnp.concatenate([wq, wv, wo], axis=1).astype(BF16),
        "w_kt": wk.T.astype(BF16),
        "w_conv": jnp.concatenate([wcv, wcg], axis=1).astype(BF16),
        "w_gt": jnp.concatenate([wi, wf], axis=1).T.astype(BF16),
        "gbias": jnp.concatenate([b_igate[l], b_fgate[l]]).astype(F32),
        "w_out": w_out[l].astype(BF16),
        "conv_w": conv_w[l].astype(F32), "conv_b": vec(conv_b), "conv_ln_g": vec(conv_ln_g),
        "conv_ln_b": vec(conv_ln_b),
    }


def _split_state(caug, m, hist):
    return caug[..., :HEAD_DIM], caug[..., HEAD_DIM], m[:, :, 0, 0], hist[:, HIST_PAD - HIST:, :]


def kernel(x_prompt, x_sample, state_mlstm_C, state_mlstm_n, state_mlstm_m, cache_conv, ffn1_pre_g, ffn1_wg,
           ffn1_wu, ffn1_wd, ffn1_post_g, mix_pre_g, w_in, b_igate, b_fgate, conv_w, conv_b, conv_ln_g, conv_ln_b,
           w_out, mix_post_g, ffn2_pre_g, ffn2_wg, ffn2_wu, ffn2_wd, ffn2_post_g, final_g):
    batch, seq, _ = x_prompt.shape
    nseq, dseq, _ = x_sample.shape
    depth = w_in.shape[0]
    assert seq % PROMPT_TILE == 0 and PROMPT_TILE % CHUNK == 0 and PROMPT_TILE % CONV_ROWS == 0
    assert (nseq * dseq) % TOKEN_TILE == 0 and dseq <= HIST_PAD
    yp = x_prompt.reshape(batch * seq, D_MODEL)
    ys = x_sample.reshape(nseq * dseq, D_MODEL)
    outs_p, outs_s = [], []
    for l in range(depth):
        p = _layer_params(l, ffn1_pre_g, ffn1_wg, ffn1_wu, ffn1_wd, ffn1_post_g, mix_pre_g, w_in, b_igate, b_fgate,
                          conv_w, conv_b, conv_ln_g, conv_ln_b, w_out, mix_post_g, ffn2_pre_g, ffn2_wg, ffn2_wu,
                          ffn2_wd, ffn2_post_g, final_g)
        x1, u, ym, caug, m = _ffn_in_mlstm(yp, p, batch, seq, PROMPT_TILE)
        yp, hist = _conv_ffn_out(u, ym, x1, p, batch, seq, 2 * PROMPT_TILE)
        outs_p.append(_split_state(caug, m, hist))
        x1, q, kt, v, og, u, gt = _ffn_in(ys, p, TOKEN_TILE)
        n0 = jnp.broadcast_to(state_mlstm_n[l].astype(F32)[..., None], (nseq, N_HEADS, HEAD_DIM, HEAD_DIM))
        caug0 = jnp.concatenate([state_mlstm_C[l].astype(F32), n0], axis=-1)
        m0 = jnp.broadcast_to(state_mlstm_m[l].astype(F32).T[:, :, None, None], (N_HEADS, nseq, dseq, HEAD_DIM))
        m0 = m0.reshape(N_HEADS, nseq * dseq, HEAD_DIM)
        hist0 = jnp.pad(cache_conv[l].astype(F32), ((0, 0), (HIST_PAD - HIST, 0), (0, 0)))
        mix, caug, m, hist = _mixer_sample(q, kt, v, og, u, gt, caug0, m0, hist0, p, nseq, dseq)
        ys = _ffn_out(mix, x1, p, TOKEN_TILE)
        outs_s.append(_split_state(caug, m, hist))
    stack = lambda outs, k: jnp.stack([o[k] for o in outs])
    return (yp.reshape(batch, seq, D_MODEL), ys.reshape(nseq, dseq, D_MODEL),
            stack(outs_p, 0), stack(outs_p, 1), stack(outs_p, 2), stack(outs_p, 3),
            stack(outs_s, 0), stack(outs_s, 1), stack(outs_s, 2), stack(outs_s, 3))
```

```python
import functools

import jax
import jax.numpy as jnp
from jax import lax
from jax.experimental import pallas as pl
from jax.experimental.pallas import tpu as pltpu

D_MODEL = 1024
D_FF = 2816
N_HEADS = 4
HEAD_DIM = 128
M_WIDTH = N_HEADS * HEAD_DIM
CONV_CH = 512
CONV_WIDTH = 31
HIST = CONV_WIDTH - 1
HIST_PAD = 32
SUBLANES = 8
EPS = 1e-6
CHUNK = 128
AUG = 2 * HEAD_DIM
CONV_ROWS = 128
LANES = 128
LN_ROWS = 32
NCH = 256

F32 = jnp.float32
BF16 = jnp.bfloat16
HIGHEST = lax.Precision.HIGHEST
NT_DIMS = (((1,), (1,)), ((), ()))

TOKEN_TILE = 256
PROMPT_TILE = 256
FFN_ROWS = 256
VMEM_LIMIT_TOKENWISE = 56 * 1024 * 1024
VMEM_LIMIT_MIXER = 40 * 1024 * 1024


def _rms(x, g):
    return x * lax.rsqrt(jnp.mean(x * x, axis=-1, keepdims=True) + EPS) * g


def _dot(a, b):
    return jnp.dot(a, b, preferred_element_type=F32)


def _swiglu(h, wg_ref, wu_ref, wd_ref):
    a = _dot(h, wg_ref[...])
    b = _dot(h, wu_ref[...])
    s = (a * jax.nn.sigmoid(a) * b).astype(BF16)
    return _dot(s, wd_ref[...])


class _Side:
    def __init__(self, make_pieces):
        self.zero = None
        self.pieces = make_pieces(self)

    def run(self, after, n=1):
        bits = pltpu.bitcast(after[after.shape[0] - SUBLANES:, 0:LANES], jnp.int32)
        self.zero = lax.shift_right_logical(lax.shift_right_logical(bits, 16), 16)[0, 0]
        if n is None:
            for _ in self.pieces:
                pass
        else:
            for _ in range(n):
                next(self.pieces, None)


def _swiglu_chunked(h_s, s_s, wg_ref, wu_ref, wd_ref, side, chunk, after_gating, down_pieces=1):
    for c0 in range(0, D_FF, chunk):
        cols = slice(c0, min(c0 + chunk, D_FF))
        a = _dot(h_s[...], wg_ref[:, cols])
        b = _dot(h_s[...], wu_ref[:, cols])
        s = a * jax.nn.sigmoid(a) * b
        s_s[:, cols] = s.astype(BF16)
        for t0 in range(0, cols.stop - cols.start, NCH):
            side.run((s if after_gating else a)[:, t0:t0 + LANES])
    d = []
    for c0 in range(0, D_MODEL, chunk):
        d.append(_dot(s_s[...], wd_ref[:, c0:c0 + chunk]))
        for t0 in range(0, chunk, NCH):
            side.run(d[-1][:, t0:t0 + LANES], down_pieces)
    return jnp.concatenate(d, axis=1)


def _ffn_in_values(x, g1_ref, wg_ref, wu_ref, wd_ref, gpost_ref, gmix_ref, wqvo_ref, wkt_ref, wconv_ref,
                   wgt_ref, gbias_ref):
    h = _rms(x, g1_ref[...]).astype(BF16)
    d = _swiglu(h, wg_ref, wu_ref, wd_ref)
    x1 = x + 0.5 * _rms(d, gpost_ref[...])
    h2 = _rms(x1, gmix_ref[...]).astype(BF16)
    qvo = _dot(h2, wqvo_ref[...])
    q = qvo[:, :M_WIDTH].astype(BF16)
    v = qvo[:, M_WIDTH:2 * M_WIDTH]
    og = jax.nn.sigmoid(qvo[:, 2 * M_WIDTH:])
    kt = lax.dot_general(wkt_ref[...], h2, NT_DIMS, preferred_element_type=F32) * (HEAD_DIM ** -0.5)
    cc = _dot(h2, wconv_ref[...])
    u = cc[:, :CONV_CH] * jax.nn.sigmoid(cc[:, CONV_CH:])
    g = lax.dot_general(wgt_ref[...], h2, NT_DIMS, preferred_element_type=F32) + gbias_ref[...]
    row = lax.broadcasted_iota(jnp.int32, g.shape, 0)
    gt = jnp.where(row < N_HEADS, g, jax.nn.log_sigmoid(g))
    return x1, q, kt.astype(BF16), v, og, u, gt


def _ffn_out_values(ym, c, x1, wout_ref, gmp_ref, g2_ref, wg_ref, wu_ref, wd_ref, gpost_ref, gfin_ref):
    o = _dot(ym.astype(BF16), wout_ref[0:M_WIDTH, :]) + _dot(c.astype(BF16), wout_ref[M_WIDTH:, :])
    x2 = x1 + _rms(o, gmp_ref[...])
    h = _rms(x2, g2_ref[...]).astype(BF16)
    d = _swiglu(h, wg_ref, wu_ref, wd_ref)
    x3 = x2 + 0.5 * _rms(d, gpost_ref[...])
    return _rms(x3, gfin_ref[...])


def _ffn_out_phases(ym, c, x1, y_ref, rows, wout_ref, gmp_ref, g2_ref, wg_ref, wu_ref, wd_ref, gpost_ref, gfin_ref):
    o = _dot(ym.astype(BF16), wout_ref[0:M_WIDTH, :]) + _dot(c.astype(BF16), wout_ref[M_WIDTH:, :])
    yield
    x2 = x1 + _rms(o, gmp_ref[...])
    h = _rms(x2, g2_ref[...]).astype(BF16)
    a = _dot(h, wg_ref[...])
    yield
    b = _dot(h, wu_ref[...])
    yield
    s = (a * jax.nn.sigmoid(a) * b).astype(BF16)
    d = _dot(s, wd_ref[...])
    yield
    x3 = x2 + 0.5 * _rms(d, gpost_ref[...])
    y_ref[rows, :] = _rms(x3, gfin_ref[...])
    yield


def _conv_taps(ubuf, row0, nrows, lanes, cw_ref, cb_ref):
    acc = jnp.broadcast_to(cb_ref[:, lanes], (nrows, lanes.stop - lanes.start))
    first = HIST_PAD - HIST
    for res in range(SUBLANES):
        taps = [j for j in range(CONV_WIDTH) if (first + j) % SUBLANES == res]
        lo = (first + taps[0]) // SUBLANES * SUBLANES
        hi = (first + taps[-1]) // SUBLANES * SUBLANES
        win = ubuf[row0 + lo:row0 + hi + nrows + (SUBLANES if res else 0), lanes]
        if res:
            win = pltpu.roll(win, win.shape[0] - res, 0)
        for j in taps:
            off = first + j - res - lo
            acc = acc + win[off:off + nrows, :] * cw_ref[j:j + 1, lanes]
    return acc


def _conv_taps_interleaved(uslab, row0, nrows, lanes, cw_ref, cb_ref):
    first = HIST_PAD - HIST
    half = nrows // 2
    acc = [jnp.broadcast_to(cb_ref[:, lanes], (half, LANES))] * 2
    for j in range(CONV_WIDTH):
        for parity in range(2):
            x = uslab[pl.ds(row0 + first + j + parity, half, stride=2), :]
            acc[parity] = acc[parity] + x * cw_ref[j:j + 1, lanes]
    return acc


def _ln_swish(acc, lg_ref, lb_ref):
    mu = jnp.mean(acc, axis=-1, keepdims=True)
    xc = acc - mu
    var = jnp.mean(xc * xc, axis=-1, keepdims=True)
    y = xc * lax.rsqrt(var + EPS) * lg_ref[...] + lb_ref[...]
    return y * jax.nn.sigmoid(y)


def _mlstm_gates(b_r, a_r, caus, qh, kth, caug):
    L = b_r.shape[1]
    arow = jnp.broadcast_to(a_r, (L, L))
    bcol = jnp.broadcast_to(b_r, (L, L)).T
    acol = arow.T
    dmat = jnp.where(caus, bcol + arow, -jnp.inf)
    return dict(bcol=bcol, acol=acol, dmat=dmat, rowmax=jnp.max(dmat, axis=1, keepdims=True),
                s_raw=_dot(qh, kth), qc=_dot(qh, caug.astype(BF16)), caug=caug, kth=kth)


def _mlstm_scores(st, mprev, vh):
    m_t = jnp.maximum(st["bcol"] + mprev, st["rowmax"])
    s = st["s_raw"] * jnp.exp(st["dmat"] - m_t)
    vaug = jnp.concatenate([vh, jnp.ones_like(vh)], axis=1)
    st.update(m_t=m_t, mprev=mprev, vaug=vaug, sv=_dot(s.astype(BF16), vaug.astype(BF16)),
              w_inter=jnp.exp(st["bcol"] + mprev - m_t))
    return m_t[m_t.shape[0] - 1:, :]


def _mlstm_output(st):
    L = st["m_t"].shape[0]
    m_t, qc, sv, w_inter = st["m_t"], st["qc"], st["sv"], st["w_inter"]
    num = w_inter * qc[:, :HEAD_DIM] + sv[:, :HEAD_DIM]
    den = w_inter * qc[:, HEAD_DIM:] + sv[:, HEAD_DIM:]
    hh = num / jnp.maximum(jnp.abs(den), jnp.exp(-m_t))
    m_new = m_t[L - 1:L, :]
    b_last = st["bcol"][L - 1:L, :]
    g_state = jnp.exp(b_last + st["mprev"] - m_new)
    g_rows = jnp.exp(b_last + st["acol"] - m_new)
    gv = (jnp.concatenate([g_rows, g_rows], axis=1) * st["vaug"]).astype(BF16)
    caug_new = jnp.concatenate([g_state, g_state], axis=1) * st["caug"] + _dot(st["kth"], gv)
    return hh, caug_new


def _ffn_in_mlstm_body(x_ref, g1_ref, wg_ref, wu_ref, wd_ref, gpost_ref, gmix_ref, wqvo_ref, wkt_ref, wconv_ref,
                       wgt_ref, gbias_ref, x1_ref, u_ref, ym_ref, caug_out, m_out,
                       q_st, kt_st, v_st, og_st, g_st, caug_s, m_s, h_s, s_s, *, tm, nt):
    g = pl.program_id(0)
    wslot = lax.rem(g, 2)
    rslot = 1 - wslot
    L = CHUNK

    @pl.when(g == 0)
    def _init():
        for st in (q_st, kt_st, v_st, og_st, g_st, caug_s, m_s):
            st[...] = jnp.zeros_like(st)

    def mlstm_pieces(side):
        fresh = lax.rem(g - 1, nt) == 0
        r = lax.broadcasted_iota(jnp.int32, (L, L), 0)
        c = lax.broadcasted_iota(jnp.int32, (L, L), 1)
        caus = c <= r
        triu = (r <= c).astype(F32)
        nck = tm // L
        items = [(ck, h) for ck in range(nck) for h in range(N_HEADS)]
        gates = {}
        stages = {}

        def stage1(ck, h):
            rows, hs, slot = slice(ck * L, (ck + 1) * L), slice(h * HEAD_DIM, (h + 1) * HEAD_DIM), rslot + side.zero
            if ck not in gates:
                gt = g_st[slot, :, rows]
                gates[ck] = gt, jnp.dot(gt, triu, precision=HIGHEST, preferred_element_type=F32)
            gt, b_rows = gates[ck]
            b_r = b_rows[N_HEADS + h:N_HEADS + h + 1, :]
            caug = caug_s[h + side.zero]
            if ck == 0:
                caug = jnp.where(fresh, 0.0, caug)
            stages[ck, h] = _mlstm_gates(b_r, gt[h:h + 1, :] - b_r, caus, q_st[slot, rows, hs], kt_st[slot, hs, rows],
                                         caug)

        def stage2(ck, h):
            rows, hs, slot = slice(ck * L, (ck + 1) * L), slice(h * HEAD_DIM, (h + 1) * HEAD_DIM), rslot + side.zero
            mprev = m_s[h + side.zero]
            if ck == 0:
                mprev = jnp.where(fresh, 0.0, mprev)
            m_new = _mlstm_scores(stages[ck, h], mprev, v_st[slot, rows, hs])
            m_s[h] = m_new
            if ck == nck - 1:
                m_out[0, h] = m_new

        def stage3(ck, h):
            rows, hs, slot = slice(ck * L, (ck + 1) * L), slice(h * HEAD_DIM, (h + 1) * HEAD_DIM), rslot + side.zero
            hh, caug = _mlstm_output(stages.pop((ck, h)))
            ym_ref[rows, hs] = og_st[slot, rows, hs] * hh
            caug_s[h] = caug
            if ck == nck - 1:
                caug_out[0, h] = caug

        for t in range(len(items) + 2):
            for lag, stage in enumerate((stage1, stage2, stage3)):
                if 0 <= t - lag < len(items):
                    stage(*items[t - lag])
            yield

    side = _Side(mlstm_pieces)

    h_s[...] = _rms(x_ref[...], g1_ref[...]).astype(BF16)
    d = _swiglu_chunked(h_s, s_s, wg_ref, wu_ref, wd_ref, side, 2 * NCH, False)
    x1 = x_ref[...] + 0.5 * _rms(d, gpost_ref[...])
    x1_ref[...] = x1
    h_s[...] = _rms(x1, gmix_ref[...]).astype(BF16)
    for c0 in range(0, 3 * M_WIDTH, NCH):
        r = _dot(h_s[...], wqvo_ref[:, c0:c0 + NCH])
        cols = slice(c0 % M_WIDTH, c0 % M_WIDTH + NCH)
        if c0 < M_WIDTH:
            q_st[wslot, :, cols] = r.astype(BF16)
        elif c0 < 2 * M_WIDTH:
            v_st[wslot, :, cols] = r
        else:
            og_st[wslot, :, cols] = jax.nn.sigmoid(r)
        side.run(r)
    for c0 in range(0, M_WIDTH, NCH):
        kt = lax.dot_general(wkt_ref[c0:c0 + NCH, :], h_s[...], NT_DIMS, preferred_element_type=F32)
        kt_st[wslot, c0:c0 + NCH, :] = (kt * (HEAD_DIM ** -0.5)).astype(BF16)
        side.run(kt)
    for c0 in range(0, CONV_CH, NCH):
        cv = _dot(h_s[...], wconv_ref[:, c0:c0 + NCH])
        cg = _dot(h_s[...], wconv_ref[:, CONV_CH + c0:CONV_CH + c0 + NCH])
        u_ref[:, c0:c0 + NCH] = cv * jax.nn.sigmoid(cg)
        side.run(cv)
    gg = lax.dot_general(wgt_ref[...], h_s[...], NT_DIMS, preferred_element_type=F32) + gbias_ref[...]
    row = lax.broadcasted_iota(jnp.int32, gg.shape, 0)
    g_st[wslot] = jnp.where(row < N_HEADS, gg, jax.nn.log_sigmoid(gg))
    side.run(gg, None)


def _conv_ffn_out_body(u_ref, ym_ref, x1_ref, cw_ref, cb_ref, lg_ref, lb_ref, wout_ref, gmp_ref, g2_ref,
                       wg_ref, wu_ref, wd_ref, gpost_ref, gfin_ref, y_ref, hist_out, ubuf, cbuf, *, tm, nt, ntiles):
    g = pl.program_id(0)
    wslot = lax.rem(g, 2)
    rslot = 1 - wslot

    @pl.when(g == 0)
    def _init():
        cbuf[...] = jnp.zeros_like(cbuf)

    nslab = CONV_CH // LANES

    @pl.when(lax.rem(jnp.minimum(g, ntiles - 1), nt) == 0)
    def _new_sequence():
        ubuf[:, 0:HIST_PAD, :] = jnp.zeros((nslab, HIST_PAD, LANES), F32)

    halves = []
    for r0 in range(0, tm, FFN_ROWS):
        rows = slice(r0, r0 + FFN_ROWS)
        c_prev = jnp.concatenate([cbuf[rslot * nslab + lb, rows, :] for lb in range(nslab)], axis=1)
        halves.append(_ffn_out_phases(ym_ref[rows, :], c_prev, x1_ref[rows, :], y_ref, rows, wout_ref, gmp_ref, g2_ref,
                                      wg_ref, wu_ref, wd_ref, gpost_ref, gfin_ref))
    for _ in zip(*halves):
        pass

    for lb in range(nslab):
        ubuf[lb, HIST_PAD:HIST_PAD + tm, :] = u_ref[:, lb * LANES:(lb + 1) * LANES]
    for k in range(tm // CONV_ROWS):
        for lb in range(nslab):
            acc = _conv_taps_interleaved(ubuf.at[lb], k * CONV_ROWS, CONV_ROWS, slice(lb * LANES, (lb + 1) * LANES),
                                         cw_ref, cb_ref)
            for parity in range(2):
                cbuf[wslot * nslab + lb, pl.ds(k * CONV_ROWS + parity, CONV_ROWS // 2, stride=2), :] = acc[parity]
    for k in range(tm // LN_ROWS):
        rows = slice(k * LN_ROWS, (k + 1) * LN_ROWS)
        y = _ln_swish(jnp.concatenate([cbuf[wslot * nslab + lb, rows, :] for lb in range(nslab)], axis=1),
                      lg_ref, lb_ref)
        for lb in range(nslab):
            cbuf[wslot * nslab + lb, rows, :] = y[:, lb * LANES:(lb + 1) * LANES]
    tail = jnp.concatenate([ubuf[lb, tm:tm + HIST_PAD, :] for lb in range(nslab)], axis=1)
    hist_out[0] = tail
    for lb in range(nslab):
        ubuf[lb, 0:HIST_PAD, :] = tail[:, lb * LANES:(lb + 1) * LANES]


def _const_spec(shape):
    nd = len(shape)
    return pl.BlockSpec(shape, lambda *_: (0,) * nd, pipeline_mode=pl.Buffered(1))


def _ffn_in_consts(p, tm):
    gbias = jnp.broadcast_to(p["gbias"][:, None], (2 * N_HEADS, tm))
    return [p["ffn1_pre_g"], p["ffn1_wg"], p["ffn1_wu"], p["ffn1_wd"], p["ffn1_post_g"], p["mix_pre_g"],
            p["w_qvo"], p["w_kt"], p["w_conv"], p["w_gt"], gbias]


def _ffn_out_consts(p):
    return [p["w_out"], p["mix_post_g"], p["ffn2_pre_g"], p["ffn2_wg"], p["ffn2_wu"], p["ffn2_wd"],
            p["ffn2_post_g"], p["final_g"]]


def _ffn_in_mlstm(x2d, p, batch, seq, tm):
    m = batch * seq
    nt = seq // tm
    ntiles = m // tm
    cur = lambda w: pl.BlockSpec((tm, w), lambda g: (jnp.minimum(g, ntiles - 1), 0))
    prev = lambda w: pl.BlockSpec((tm, w), lambda g: (jnp.maximum(g - 1, 0), 0))
    prev_seq = lambda g: jnp.maximum(g - 1, 0) // nt
    consts = _ffn_in_consts(p, tm)
    return pl.pallas_call(
        functools.partial(_ffn_in_mlstm_body, tm=tm, nt=nt),
        grid=(ntiles + 1,),
        in_specs=[cur(D_MODEL)] + [_const_spec(c.shape) for c in consts],
        out_specs=[cur(D_MODEL), cur(CONV_CH), prev(M_WIDTH),
                   pl.BlockSpec((1, N_HEADS, HEAD_DIM, AUG), lambda g: (prev_seq(g), 0, 0, 0)),
                   pl.BlockSpec((1, N_HEADS, 1, HEAD_DIM), lambda g: (prev_seq(g), 0, 0, 0))],
        out_shape=[jax.ShapeDtypeStruct((m, D_MODEL), F32), jax.ShapeDtypeStruct((m, CONV_CH), F32),
                   jax.ShapeDtypeStruct((m, M_WIDTH), F32),
                   jax.ShapeDtypeStruct((batch, N_HEADS, HEAD_DIM, AUG), F32),
                   jax.ShapeDtypeStruct((batch, N_HEADS, 1, HEAD_DIM), F32)],
        scratch_shapes=[pltpu.VMEM((2, tm, M_WIDTH), BF16), pltpu.VMEM((2, M_WIDTH, tm), BF16),
                        pltpu.VMEM((2, tm, M_WIDTH), F32), pltpu.VMEM((2, tm, M_WIDTH), F32),
                        pltpu.VMEM((2, 2 * N_HEADS, tm), F32),
                        pltpu.VMEM((N_HEADS, HEAD_DIM, AUG), F32), pltpu.VMEM((N_HEADS, 1, HEAD_DIM), F32),
                        pltpu.VMEM((tm, D_MODEL), BF16), pltpu.VMEM((tm, D_FF), BF16)],
        compiler_params=pltpu.CompilerParams(dimension_semantics=("arbitrary",),
                                             vmem_limit_bytes=VMEM_LIMIT_TOKENWISE),
        name="ffn_in_mlstm",
    )(x2d, *consts)


def _conv_ffn_out(u, ym, x1, p, batch, seq, tm):
    m = batch * seq
    nt = seq // tm
    ntiles = m // tm
    cur = lambda w: pl.BlockSpec((tm, w), lambda g: (jnp.minimum(g, ntiles - 1), 0))
    prev = lambda w: pl.BlockSpec((tm, w), lambda g: (jnp.maximum(g - 1, 0), 0))
    consts = [p["conv_w"], p["conv_b"], p["conv_ln_g"], p["conv_ln_b"]] + _ffn_out_consts(p)
    return pl.pallas_call(
        functools.partial(_conv_ffn_out_body, tm=tm, nt=nt, ntiles=ntiles),
        grid=(ntiles + 1,),
        in_specs=[cur(CONV_CH), prev(M_WIDTH), prev(D_MODEL)] + [_const_spec(c.shape) for c in consts],
        out_specs=[prev(D_MODEL),
                   pl.BlockSpec((1, HIST_PAD, CONV_CH), lambda g: (jnp.minimum(g, ntiles - 1) // nt, 0, 0))],
        out_shape=[jax.ShapeDtypeStruct((m, D_MODEL), F32), jax.ShapeDtypeStruct((batch, HIST_PAD, CONV_CH), F32)],
        scratch_shapes=[pltpu.VMEM((CONV_CH // LANES, HIST_PAD + tm, LANES), F32),
                        pltpu.VMEM((2 * CONV_CH // LANES, tm, LANES), F32)],
        compiler_params=pltpu.CompilerParams(dimension_semantics=("arbitrary",),
                                             vmem_limit_bytes=VMEM_LIMIT_TOKENWISE),
        name="conv_ffn_out",
    )(u, ym, x1, *consts)


def _ffn_in_body(x_ref, g1_ref, wg_ref, wu_ref, wd_ref, gpost_ref, gmix_ref, wqvo_ref, wkt_ref, wconv_ref,
                 wgt_ref, gbias_ref, x1_ref, q_ref, kt_ref, v_ref, og_ref, u_ref, gt_ref):
    outs = _ffn_in_values(x_ref[...], g1_ref, wg_ref, wu_ref, wd_ref, gpost_ref, gmix_ref, wqvo_ref, wkt_ref,
                          wconv_ref, wgt_ref, gbias_ref)
    for ref, val in zip((x1_ref, q_ref, kt_ref, v_ref, og_ref, u_ref, gt_ref), outs):
        ref[...] = val


def _ffn_out_body(mix_ref, x1_ref, wout_ref, gmp_ref, g2_ref, wg_ref, wu_ref, wd_ref, gpost_ref, gfin_ref, y_ref):
    y_ref[...] = _ffn_out_values(mix_ref[:, :M_WIDTH], mix_ref[:, M_WIDTH:], x1_ref[...], wout_ref, gmp_ref, g2_ref,
                                 wg_ref, wu_ref, wd_ref, gpost_ref, gfin_ref)


def _mixer_sample_body(q_ref, kt_ref, v_ref, og_ref, u_ref, gt_ref, caug0_ref, m0_ref, hist0_ref,
                       cw_ref, cb_ref, lg_ref, lb_ref, mix_ref, caug_out, m_out, hist_out, ubuf, *, nseq, L):
    R = nseq * L
    r = lax.broadcasted_iota(jnp.int32, (R, R), 0)
    c = lax.broadcasted_iota(jnp.int32, (R, R), 1)
    same = lax.div(r, L) == lax.div(c, L)
    caus = same & (c <= r)
    segtriu = jnp.where(same & (r <= c), 1.0, 0.0).astype(F32)
    segones = jnp.where(same, 1.0, 0.0).astype(F32)
    rowseq = lax.div(lax.broadcasted_iota(jnp.int32, (R, AUG), 0), L)
    ones = jnp.ones((R, HEAD_DIM), F32)
    gt = gt_ref[...]
    b_rows = jnp.dot(gt, segtriu, precision=HIGHEST, preferred_element_type=F32)
    tot_rows = jnp.dot(gt, segones, precision=HIGHEST, preferred_element_type=F32)
    for h in range(N_HEADS):
        hs = slice(h * HEAD_DIM, (h + 1) * HEAD_DIM)
        b_r = b_rows[N_HEADS + h:N_HEADS + h + 1, :]
        a_r = gt[h:h + 1, :] - b_r
        t_r = tot_rows[N_HEADS + h:N_HEADS + h + 1, :]
        arow = jnp.broadcast_to(a_r, (R, R))
        bcol = jnp.broadcast_to(b_r, (R, R)).T
        acol = arow.T
        tcol = jnp.broadcast_to(t_r, (R, R)).T
        mprev = m0_ref[h]
        mprev = jnp.concatenate([mprev, mprev], axis=1)
        dmat = jnp.where(caus, bcol + arow, -jnp.inf)
        m_t = jnp.maximum(bcol + mprev, jnp.max(dmat, axis=1, keepdims=True))
        dend = jnp.where(same, tcol + arow, -jnp.inf)
        m_new = jnp.maximum(tcol + mprev, jnp.max(dend, axis=1, keepdims=True))
        qh = q_ref[:, hs]
        kth = kt_ref[hs, :]
        s = _dot(qh, kth) * jnp.exp(dmat - m_t)
        vaug = jnp.concatenate([v_ref[:, hs], ones], axis=1)
        w_inter = jnp.exp(bcol + mprev - m_t)
        qc = jnp.zeros((R, AUG), F32)
        for i in range(nseq):
            qc = jnp.where(rowseq == i, _dot(qh, caug0_ref[i, h].astype(BF16)), qc)
        sv = _dot(s.astype(BF16), vaug.astype(BF16))
        num = w_inter[:, :HEAD_DIM] * qc[:, :HEAD_DIM] + sv[:, :HEAD_DIM]
        den = w_inter[:, :HEAD_DIM] * qc[:, HEAD_DIM:] + sv[:, HEAD_DIM:]
        hh = num / jnp.maximum(jnp.abs(den), jnp.exp(-m_t[:, :HEAD_DIM]))
        mix_ref[:, hs] = og_ref[:, hs] * hh
        g_state = jnp.exp(tcol + mprev - m_new)
        g_rows = jnp.exp(tcol + acol - m_new)
        gv = g_rows * vaug
        for i in range(nseq):
            gvi = jnp.where(rowseq == i, gv, 0.0).astype(BF16)
            caug_out[i, h] = g_state[i * L:i * L + 1, :] * caug0_ref[i, h] + _dot(kth, gvi)
            m_out[i, h] = m_new[i * L:i * L + 1, :HEAD_DIM]

    for i in range(nseq):
        ubuf[0:HIST_PAD, :] = hist0_ref[i]
        ubuf[HIST_PAD:HIST_PAD + L, :] = u_ref[i * L:(i + 1) * L, :]
        acc = _conv_taps(ubuf, 0, L, slice(0, CONV_CH), cw_ref, cb_ref)
        mix_ref[i * L:(i + 1) * L, M_WIDTH:] = _ln_swish(acc, lg_ref, lb_ref)
        hist_out[i] = ubuf[L:L + HIST_PAD, :]


def _ffn_in(x2d, p, tm):
    m = x2d.shape[0]
    row = lambda w: pl.BlockSpec((tm, w), lambda i: (i, 0))
    col = lambda h: pl.BlockSpec((h, tm), lambda i: (0, i))
    consts = _ffn_in_consts(p, tm)
    return pl.pallas_call(
        _ffn_in_body,
        grid=(m // tm,),
        in_specs=[row(D_MODEL)] + [_const_spec(c.shape) for c in consts],
        out_specs=[row(D_MODEL), row(M_WIDTH), col(M_WIDTH), row(M_WIDTH), row(M_WIDTH), row(CONV_CH),
                   col(2 * N_HEADS)],
        out_shape=[jax.ShapeDtypeStruct((m, D_MODEL), F32), jax.ShapeDtypeStruct((m, M_WIDTH), BF16),
                   jax.ShapeDtypeStruct((M_WIDTH, m), BF16), jax.ShapeDtypeStruct((m, M_WIDTH), F32),
                   jax.ShapeDtypeStruct((m, M_WIDTH), F32), jax.ShapeDtypeStruct((m, CONV_CH), F32),
                   jax.ShapeDtypeStruct((2 * N_HEADS, m), F32)],
        compiler_params=pltpu.CompilerParams(dimension_semantics=("arbitrary",),
                                             vmem_limit_bytes=VMEM_LIMIT_TOKENWISE),
        name="ffn_in",
    )(x2d, *consts)


def _ffn_out(mix, x1, p, tm):
    m = mix.shape[0]
    row = pl.BlockSpec((tm, D_MODEL), lambda i: (i, 0))
    consts = _ffn_out_consts(p)
    return pl.pallas_call(
        _ffn_out_body,
        grid=(m // tm,),
        in_specs=[row, row] + [_const_spec(c.shape) for c in consts],
        out_specs=row,
        out_shape=jax.ShapeDtypeStruct((m, D_MODEL), F32),
        compiler_params=pltpu.CompilerParams(dimension_semantics=("arbitrary",),
                                             vmem_limit_bytes=VMEM_LIMIT_TOKENWISE),
        name="ffn_out",
    )(mix, x1, *consts)


def _mixer_sample(q, kt, v, og, u, gt, caug0, m0, hist0, p, nseq, L):
    rows = nseq * L
    args = [q, kt, v, og, u, gt, caug0, m0, hist0, p["conv_w"], p["conv_b"], p["conv_ln_g"], p["conv_ln_b"]]
    full = lambda a: pl.BlockSpec(a.shape, lambda i, nd=a.ndim: (0,) * nd)
    out_shape = [jax.ShapeDtypeStruct((rows, D_MODEL), F32),
                 jax.ShapeDtypeStruct((nseq, N_HEADS, HEAD_DIM, AUG), F32),
                 jax.ShapeDtypeStruct((nseq, N_HEADS, 1, HEAD_DIM), F32),
                 jax.ShapeDtypeStruct((nseq, HIST_PAD, CONV_CH), F32)]
    return pl.pallas_call(
        functools.partial(_mixer_sample_body, nseq=nseq, L=L),
        grid=(1,),
        in_specs=[full(a) for a in args],
        out_specs=[full(s) for s in out_shape],
        out_shape=out_shape,
        scratch_shapes=[pltpu.VMEM((HIST_PAD + L, CONV_CH), F32)],
        compiler_params=pltpu.CompilerParams(dimension_semantics=("arbitrary",),
                                             vmem_limit_bytes=VMEM_LIMIT_MIXER),
        name="mixer_sample",
    )(*args)


def _layer_params(l, ffn1_pre_g, ffn1_wg, ffn1_wu, ffn1_wd, ffn1_post_g, mix_pre_g, w_in, b_igate, b_fgate,
                  conv_w, conv_b, conv_ln_g, conv_ln_b, w_out, mix_post_g, ffn2_pre_g, ffn2_wg, ffn2_wu, ffn2_wd,
                  ffn2_post_g, final_g):
    vec = lambda a: a[l].astype(F32).reshape(1, -1)
    w = w_in[l]
    cuts = [0, M_WIDTH, 2 * M_WIDTH, 3 * M_WIDTH, 4 * M_WIDTH, 4 * M_WIDTH + N_HEADS, 4 * M_WIDTH + 2 * N_HEADS,
            4 * M_WIDTH + 2 * N_HEADS + CONV_CH, 4 * M_WIDTH + 2 * N_HEADS + 2 * CONV_CH]
    wq, wk, wv, wo, wi, wf, wcv, wcg = [w[:, a:b] for a, b in zip(cuts[:-1], cuts[1:])]
    return {
        "ffn1_pre_g": vec(ffn1_pre_g), "ffn1_post_g": vec(ffn1_post_g), "mix_pre_g": vec(mix_pre_g),
        "mix_post_g": vec(mix_post_g), "ffn2_pre_g": vec(ffn2_pre_g), "ffn2_post_g": vec(ffn2_post_g),
        "final_g": vec(final_g),
        "ffn1_wg": ffn1_wg[l].astype(BF16), "ffn1_wu": ffn1_wu[l].astype(BF16), "ffn1_wd": ffn1_wd[l].astype(BF16),
        "ffn2_wg": ffn2_wg[l].astype(BF16), "ffn2_wu": ffn2_wu[l].astype(BF16), "ffn2_wd": ffn2_wd[l].astype(BF16),
        "w_qvo": jnp.concatenate([wq, wv, wo], axis=1).astype(BF16),
        "w_kt": wk.T.astype(BF16),
        "w_conv": jnp.concatenate([wcv, wcg], axis=1).astype(BF16),
        "w_gt": jnp.concatenate([wi, wf], axis=1).T.astype(BF16),
        "gbias": jnp.concatenate([b_igate[l], b_fgate[l]]).astype(F32),
        "w_out": w_out[l].astype(BF16),
        "conv_w": conv_w[l].astype(F32), "conv_b": vec(conv_b), "conv_ln_g": vec(conv_ln_g),
        "conv_ln_b": vec(conv_ln_b),
    }


def _split_state(caug, m, hist):
    return caug[..., :HEAD_DIM], caug[..., HEAD_DIM], m[:, :, 0, 0], hist[:, HIST_PAD - HIST:, :]


def kernel(x_prompt, x_sample, state_mlstm_C, state_mlstm_n, state_mlstm_m, cache_conv, ffn1_pre_g, ffn1_wg,
           ffn1_wu, ffn1_wd, ffn1_post_g, mix_pre_g, w_in, b_igate, b_fgate, conv_w, conv_b, conv_ln_g, conv_ln_b,
           w_out, mix_post_g, ffn2_pre_g, ffn2_wg, ffn2_wu, ffn2_wd, ffn2_post_g, final_g):
    batch, seq, _ = x_prompt.shape
    nseq, dseq, _ = x_sample.shape
    depth = w_in.shape[0]
    assert seq % PROMPT_TILE == 0 and PROMPT_TILE % CHUNK == 0 and PROMPT_TILE % CONV_ROWS == 0
    assert (nseq * dseq) % TOKEN_TILE == 0 and dseq <= HIST_PAD
    yp = x_prompt.reshape(batch * seq, D_MODEL)
    ys = x_sample.reshape(nseq * dseq, D_MODEL)
    outs_p, outs_s = [], []
    for l in range(depth):
        p = _layer_params(l, ffn1_pre_g, ffn1_wg, ffn1_wu, ffn1_wd, ffn1_post_g, mix_pre_g, w_in, b_igate, b_fgate,
                          conv_w, conv_b, conv_ln_g, conv_ln_b, w_out, mix_post_g, ffn2_pre_g, ffn2_wg, ffn2_wu,
                          ffn2_wd, ffn2_post_g, final_g)
        x1, u, ym, caug, m = _ffn_in_mlstm(yp, p, batch, seq, PROMPT_TILE)
        yp, hist = _conv_ffn_out(u, ym, x1, p, batch, seq, 2 * PROMPT_TILE)
        outs_p.append(_split_state(caug, m, hist))
        x1, q, kt, v, og, u, gt = _ffn_in(ys, p, TOKEN_TILE)
        n0 = jnp.broadcast_to(state_mlstm_n[l].astype(F32)[..., None], (nseq, N_HEADS, HEAD_DIM, HEAD_DIM))
        caug0 = jnp.concatenate([state_mlstm_C[l].astype(F32), n0], axis=-1)
        m0 = jnp.broadcast_to(state_mlstm_m[l].astype(F32).T[:, :, None, None], (N_HEADS, nseq, dseq, HEAD_DIM))
        m0 = m0.reshape(N_HEADS, nseq * dseq, HEAD_DIM)
        hist0 = jnp.pad(cache_conv[l].astype(F32), ((0, 0), (HIST_PAD - HIST, 0), (0, 0)))
        mix, caug, m, hist = _mixer_sample(q, kt, v, og, u, gt, caug0, m0, hist0, p, nseq, dseq)
        ys = _ffn_out(mix, x1, p, TOKEN_TILE)
        outs_s.append(_split_state(caug, m, hist))
    stack = lambda outs, k: jnp.stack([o[k] for o in outs])
    return (yp.reshape(batch, seq, D_MODEL), ys.reshape(nseq, dseq, D_MODEL),
            stack(outs_p, 0), stack(outs_p, 1), stack(outs_p, 2), stack(outs_p, 3),
            stack(outs_s, 0), stack(outs_s, 1), stack(outs_s, 2), stack(outs_s, 3))
```

```python
import functools

import jax
import jax.numpy as jnp
from jax import lax
from jax.experimental import pallas as pl
from jax.experimental.pallas import tpu as pltpu

D_MODEL = 1024
D_FF = 2816
N_HEADS = 4
HEAD_DIM = 128
M_WIDTH = N_HEADS * HEAD_DIM
CONV_CH = 512
CONV_WIDTH = 31
HIST = CONV_WIDTH - 1
HIST_PAD = 32
SUBLANES = 8
EPS = 1e-6
CHUNK = 128
AUG = 2 * HEAD_DIM
CONV_ROWS = 128
LANES = 128
LN_ROWS = 32
NCH = 256

F32 = jnp.float32
BF16 = jnp.bfloat16
HIGHEST = lax.Precision.HIGHEST
NT_DIMS = (((1,), (1,)), ((), ()))

TOKEN_TILE = 256
PROMPT_TILE = 256
FFN_ROWS = 256
VMEM_LIMIT_TOKENWISE = 56 * 1024 * 1024
VMEM_LIMIT_MIXER = 40 * 1024 * 1024


def _rms(x, g):
    return x * lax.rsqrt(jnp.mean(x * x, axis=-1, keepdims=True) + EPS) * g


def _dot(a, b):
    return jnp.dot(a, b, preferred_element_type=F32)


def _swiglu(h, wg_ref, wu_ref, wd_ref):
    a = _dot(h, wg_ref[...])
    b = _dot(h, wu_ref[...])
    s = (a * jax.nn.sigmoid(a) * b).astype(BF16)
    return _dot(s, wd_ref[...])


class _Side:
    def __init__(self, make_pieces):
        self.zero = None
        self.pieces = make_pieces(self)

    def run(self, after, n=1):
        bits = pltpu.bitcast(after[after.shape[0] - SUBLANES:, 0:LANES], jnp.int32)
        self.zero = lax.shift_right_logical(lax.shift_right_logical(bits, 16), 16)[0, 0]
        if n is None:
            for _ in self.pieces:
                pass
        else:
            for _ in range(n):
                next(self.pieces, None)


def _swiglu_chunked(h_s, s_s, wg_ref, wu_ref, wd_ref, side, chunk, after_gating, down_pieces=1):
    for c0 in range(0, D_FF, chunk):
        cols = slice(c0, min(c0 + chunk, D_FF))
        a = _dot(h_s[...], wg_ref[:, cols])
        b = _dot(h_s[...], wu_ref[:, cols])
        s = a * jax.nn.sigmoid(a) * b
        s_s[:, cols] = s.astype(BF16)
        for t0 in range(0, cols.stop - cols.start, NCH):
            side.run((s if after_gating else a)[:, t0:t0 + LANES])
    d = []
    for c0 in range(0, D_MODEL, chunk):
        d.append(_dot(s_s[...], wd_ref[:, c0:c0 + chunk]))
        for t0 in range(0, chunk, NCH):
            side.run(d[-1][:, t0:t0 + LANES], down_pieces)
    return jnp.concatenate(d, axis=1)


def _gate_rows(pre, gbias_ref):
    g = pre + gbias_ref[...]
    row = lax.broadcasted_iota(jnp.int32, g.shape, 0)
    return jnp.where(row < N_HEADS, g, jax.nn.log_sigmoid(g))


def _ffn_in_values(x, g1_ref, wg_ref, wu_ref, wd_ref, gpost_ref, gmix_ref, wqvo_ref, wktg_ref, wconv_ref, gbias_ref):
    h = _rms(x, g1_ref[...]).astype(BF16)
    d = _swiglu(h, wg_ref, wu_ref, wd_ref)
    x1 = x + 0.5 * _rms(d, gpost_ref[...])
    h2 = _rms(x1, gmix_ref[...]).astype(BF16)
    qvo = _dot(h2, wqvo_ref[...])
    q = qvo[:, :M_WIDTH].astype(BF16)
    v = qvo[:, M_WIDTH:2 * M_WIDTH]
    og = jax.nn.sigmoid(qvo[:, 2 * M_WIDTH:])
    ktg = lax.dot_general(wktg_ref[...], h2, NT_DIMS, preferred_element_type=F32)
    kt = ktg[:M_WIDTH] * (HEAD_DIM ** -0.5)
    cc = _dot(h2, wconv_ref[...])
    u = cc[:, :CONV_CH] * jax.nn.sigmoid(cc[:, CONV_CH:])
    return x1, q, kt.astype(BF16), v, og, u, _gate_rows(ktg[M_WIDTH:], gbias_ref)


def _ffn_out_values(ym, c, x1, wout_ref, gmp_ref, g2_ref, wg_ref, wu_ref, wd_ref, gpost_ref, gfin_ref):
    o = _dot(ym.astype(BF16), wout_ref[0:M_WIDTH, :]) + _dot(c.astype(BF16), wout_ref[M_WIDTH:, :])
    x2 = x1 + _rms(o, gmp_ref[...])
    h = _rms(x2, g2_ref[...]).astype(BF16)
    d = _swiglu(h, wg_ref, wu_ref, wd_ref)
    x3 = x2 + 0.5 * _rms(d, gpost_ref[...])
    return _rms(x3, gfin_ref[...])


def _ffn_out_phases(ym, c, x1, y_ref, rows, wout_ref, gmp_ref, g2_ref, wg_ref, wu_ref, wd_ref, gpost_ref, gfin_ref):
    o = _dot(ym.astype(BF16), wout_ref[0:M_WIDTH, :]) + _dot(c.astype(BF16), wout_ref[M_WIDTH:, :])
    yield o
    x2 = x1 + _rms(o, gmp_ref[...])
    h = _rms(x2, g2_ref[...]).astype(BF16)
    a = _dot(h, wg_ref[...])
    yield a
    b = _dot(h, wu_ref[...])
    yield b
    s = (a * jax.nn.sigmoid(a) * b).astype(BF16)
    d = _dot(s, wd_ref[...])
    yield d
    x3 = x2 + 0.5 * _rms(d, gpost_ref[...])
    y = _rms(x3, gfin_ref[...])
    y_ref[rows, :] = y
    yield y


def _conv_taps(ubuf, row0, nrows, lanes, cw_ref, cb_ref):
    acc = jnp.broadcast_to(cb_ref[:, lanes], (nrows, lanes.stop - lanes.start))
    first = HIST_PAD - HIST
    for res in range(SUBLANES):
        taps = [j for j in range(CONV_WIDTH) if (first + j) % SUBLANES == res]
        lo = (first + taps[0]) // SUBLANES * SUBLANES
        hi = (first + taps[-1]) // SUBLANES * SUBLANES
        win = ubuf[row0 + lo:row0 + hi + nrows + (SUBLANES if res else 0), lanes]
        if res:
            win = pltpu.roll(win, win.shape[0] - res, 0)
        for j in taps:
            off = first + j - res - lo
            acc = acc + win[off:off + nrows, :] * cw_ref[j:j + 1, lanes]
    return acc


def _conv_taps_interleaved(uslab, row0, nrows, lanes, cw_ref, cb_ref):
    first = HIST_PAD - HIST
    half = nrows // 2
    acc = [jnp.broadcast_to(cb_ref[:, lanes], (half, LANES))] * 2
    for j in range(CONV_WIDTH):
        for parity in range(2):
            x = uslab[pl.ds(row0 + first + j + parity, half, stride=2), :]
            acc[parity] = acc[parity] + x * cw_ref[j:j + 1, lanes]
    return acc


def _ln_swish(acc, lg_ref, lb_ref):
    mu = jnp.mean(acc, axis=-1, keepdims=True)
    xc = acc - mu
    var = jnp.mean(xc * xc, axis=-1, keepdims=True)
    y = xc * lax.rsqrt(var + EPS) * lg_ref[...] + lb_ref[...]
    return y * jax.nn.sigmoid(y)


def _mlstm_gates(b_r, a_r, caus, qh, kth, caug):
    L = b_r.shape[1]
    arow = jnp.broadcast_to(a_r, (L, L))
    bcol = jnp.broadcast_to(b_r, (L, L)).T
    acol = arow.T
    dmat = jnp.where(caus, bcol + arow, -jnp.inf)
    return dict(bcol=bcol, acol=acol, dmat=dmat, rowmax=jnp.max(dmat, axis=1, keepdims=True),
                s_raw=_dot(qh, kth), qc=_dot(qh, caug.astype(BF16)), caug=caug, kth=kth)


def _mlstm_scores(st, mprev, vh):
    m_t = jnp.maximum(st["bcol"] + mprev, st["rowmax"])
    s = st["s_raw"] * jnp.exp(st["dmat"] - m_t)
    vaug = jnp.concatenate([vh, jnp.ones_like(vh)], axis=1)
    st.update(m_t=m_t, mprev=mprev, vaug=vaug, sv=_dot(s.astype(BF16), vaug.astype(BF16)),
              w_inter=jnp.exp(st["bcol"] + mprev - m_t))
    return m_t[m_t.shape[0] - 1:, :]


def _mlstm_output(st):
    L = st["m_t"].shape[0]
    m_t, qc, sv, w_inter = st["m_t"], st["qc"], st["sv"], st["w_inter"]
    num = w_inter * qc[:, :HEAD_DIM] + sv[:, :HEAD_DIM]
    den = w_inter * qc[:, HEAD_DIM:] + sv[:, HEAD_DIM:]
    hh = num / jnp.maximum(jnp.abs(den), jnp.exp(-m_t))
    m_new = m_t[L - 1:L, :]
    b_last = st["bcol"][L - 1:L, :]
    g_state = jnp.exp(b_last + st["mprev"] - m_new)
    g_rows = jnp.exp(b_last + st["acol"] - m_new)
    gv = (jnp.concatenate([g_rows, g_rows], axis=1) * st["vaug"]).astype(BF16)
    caug_new = jnp.concatenate([g_state, g_state], axis=1) * st["caug"] + _dot(st["kth"], gv)
    return hh, caug_new


def _ffn_in_mlstm_body(x_ref, g1_ref, wg_ref, wu_ref, wd_ref, gpost_ref, gmix_ref, wqvo_ref, wktg_ref, wconv_ref,
                       gbias_ref, x1_ref, u_ref, ym_ref, caug_out, m_out,
                       q_st, kt_st, v_st, og_st, g_st, caug_s, m_s, h_s, s_s, *, tm, nt):
    g = pl.program_id(0)
    wslot = lax.rem(g, 2)
    rslot = 1 - wslot
    L = CHUNK

    @pl.when(g == 0)
    def _init():
        for st in (q_st, kt_st, v_st, og_st, g_st, caug_s, m_s):
            st[...] = jnp.zeros_like(st)

    def mlstm_pieces(side):
        fresh = lax.rem(g - 1, nt) == 0
        r = lax.broadcasted_iota(jnp.int32, (L, L), 0)
        c = lax.broadcasted_iota(jnp.int32, (L, L), 1)
        caus = c <= r
        triu = (r <= c).astype(F32)
        nck = tm // L
        items = [(ck, h) for ck in range(nck) for h in range(N_HEADS)]
        gates = {}
        stages = {}

        def stage1(ck, h):
            rows, hs, slot = slice(ck * L, (ck + 1) * L), slice(h * HEAD_DIM, (h + 1) * HEAD_DIM), rslot + side.zero
            if ck not in gates:
                gt = g_st[slot, :, rows]
                gates[ck] = gt, jnp.dot(gt, triu, precision=HIGHEST, preferred_element_type=F32)
            gt, b_rows = gates[ck]
            b_r = b_rows[N_HEADS + h:N_HEADS + h + 1, :]
            caug = caug_s[h + side.zero]
            if ck == 0:
                caug = jnp.where(fresh, 0.0, caug)
            stages[ck, h] = _mlstm_gates(b_r, gt[h:h + 1, :] - b_r, caus, q_st[slot, rows, hs], kt_st[slot, hs, rows],
                                         caug)

        def stage2(ck, h):
            rows, hs, slot = slice(ck * L, (ck + 1) * L), slice(h * HEAD_DIM, (h + 1) * HEAD_DIM), rslot + side.zero
            mprev = m_s[h + side.zero]
            if ck == 0:
                mprev = jnp.where(fresh, 0.0, mprev)
            m_new = _mlstm_scores(stages[ck, h], mprev, v_st[slot, rows, hs])
            m_s[h] = m_new
            if ck == nck - 1:
                m_out[0, h] = m_new

        def stage3(ck, h):
            rows, hs, slot = slice(ck * L, (ck + 1) * L), slice(h * HEAD_DIM, (h + 1) * HEAD_DIM), rslot + side.zero
            hh, caug = _mlstm_output(stages.pop((ck, h)))
            ym_ref[rows, hs] = og_st[slot, rows, hs] * hh
            caug_s[h] = caug
            if ck == nck - 1:
                caug_out[0, h] = caug

        for t in range(len(items) + 2):
            for lag, stage in enumerate((stage1, stage2, stage3)):
                if 0 <= t - lag < len(items):
                    stage(*items[t - lag])
            yield

    side = _Side(mlstm_pieces)

    h_s[...] = _rms(x_ref[...], g1_ref[...]).astype(BF16)
    d = _swiglu_chunked(h_s, s_s, wg_ref, wu_ref, wd_ref, side, 2 * NCH, False)
    x1 = x_ref[...] + 0.5 * _rms(d, gpost_ref[...])
    x1_ref[...] = x1
    h_s[...] = _rms(x1, gmix_ref[...]).astype(BF16)
    for c0 in range(0, 3 * M_WIDTH, NCH):
        r = _dot(h_s[...], wqvo_ref[:, c0:c0 + NCH])
        cols = slice(c0 % M_WIDTH, c0 % M_WIDTH + NCH)
        if c0 < M_WIDTH:
            q_st[wslot, :, cols] = r.astype(BF16)
        elif c0 < 2 * M_WIDTH:
            v_st[wslot, :, cols] = r
        else:
            og_st[wslot, :, cols] = jax.nn.sigmoid(r)
        side.run(r)
    for c0 in range(0, M_WIDTH, NCH):
        last = c0 + NCH == M_WIDTH
        kt = lax.dot_general(wktg_ref[c0:c0 + NCH + (2 * N_HEADS if last else 0), :], h_s[...], NT_DIMS,
                             preferred_element_type=F32)
        kt_st[wslot, c0:c0 + NCH, :] = (kt[:NCH] * (HEAD_DIM ** -0.5)).astype(BF16)
        if last:
            g_st[wslot] = _gate_rows(kt[NCH:], gbias_ref)
        side.run(kt)
    for c0 in range(0, CONV_CH, NCH):
        cv = _dot(h_s[...], wconv_ref[:, c0:c0 + NCH])
        cg = _dot(h_s[...], wconv_ref[:, CONV_CH + c0:CONV_CH + c0 + NCH])
        u = cv * jax.nn.sigmoid(cg)
        u_ref[:, c0:c0 + NCH] = u
        side.run(cv, None if c0 + NCH == CONV_CH else 1)


def _conv_ffn_out_body(u_ref, ym_ref, x1_ref, cw_ref, cb_ref, lg_ref, lb_ref, wout_ref, gmp_ref, g2_ref,
                       wg_ref, wu_ref, wd_ref, gpost_ref, gfin_ref, y_ref, hist_out, ubuf, cbuf, *, tm, nt):
    nslab = CONV_CH // LANES

    @pl.when(lax.rem(pl.program_id(0), nt) == 0)
    def _new_sequence():
        ubuf[:, 0:HIST_PAD, :] = jnp.zeros((nslab, HIST_PAD, LANES), F32)

    for lb in range(nslab):
        ubuf[lb, HIST_PAD:HIST_PAD + tm, :] = u_ref[:, lb * LANES:(lb + 1) * LANES]
    for k in range(tm // CONV_ROWS):
        for lb in range(nslab):
            acc = _conv_taps_interleaved(ubuf.at[lb], k * CONV_ROWS, CONV_ROWS, slice(lb * LANES, (lb + 1) * LANES),
                                         cw_ref, cb_ref)
            for parity in range(2):
                cbuf[lb, pl.ds(k * CONV_ROWS + parity, CONV_ROWS // 2, stride=2), :] = acc[parity]
    tail = jnp.concatenate([ubuf[lb, tm:tm + HIST_PAD, :] for lb in range(nslab)], axis=1)
    hist_out[0] = tail
    for lb in range(nslab):
        ubuf[lb, 0:HIST_PAD, :] = tail[:, lb * LANES:(lb + 1) * LANES]

    halves = []
    for r0 in range(0, tm, FFN_ROWS):
        c = []
        for k in range(FFN_ROWS // LN_ROWS):
            rows = slice(r0 + k * LN_ROWS, r0 + (k + 1) * LN_ROWS)
            c.append(_ln_swish(jnp.concatenate([cbuf[lb, rows, :] for lb in range(nslab)], axis=1), lg_ref, lb_ref))
        rows = slice(r0, r0 + FFN_ROWS)
        halves.append(_ffn_out_phases(ym_ref[rows, :], jnp.concatenate(c, axis=0), x1_ref[rows, :], y_ref, rows,
                                      wout_ref, gmp_ref, g2_ref, wg_ref, wu_ref, wd_ref, gpost_ref, gfin_ref))
    for _ in zip(*halves):
        pass


def _const_spec(shape):
    nd = len(shape)
    return pl.BlockSpec(shape, lambda *_: (0,) * nd, pipeline_mode=pl.Buffered(1))


def _ffn_in_consts(p, tm):
    gbias = jnp.broadcast_to(p["gbias"][:, None], (2 * N_HEADS, tm))
    return [p["ffn1_pre_g"], p["ffn1_wg"], p["ffn1_wu"], p["ffn1_wd"], p["ffn1_post_g"], p["mix_pre_g"],
            p["w_qvo"], p["w_ktg"], p["w_conv"], gbias]


def _ffn_out_consts(p):
    return [p["w_out"], p["mix_post_g"], p["ffn2_pre_g"], p["ffn2_wg"], p["ffn2_wu"], p["ffn2_wd"],
            p["ffn2_post_g"], p["final_g"]]


def _ffn_in_mlstm(x2d, p, batch, seq, tm):
    m = batch * seq
    nt = seq // tm
    ntiles = m // tm
    cur = lambda w: pl.BlockSpec((tm, w), lambda g: (jnp.minimum(g, ntiles - 1), 0))
    prev = lambda w: pl.BlockSpec((tm, w), lambda g: (jnp.maximum(g - 1, 0), 0))
    prev_seq = lambda g: jnp.maximum(g - 1, 0) // nt
    consts = _ffn_in_consts(p, tm)
    return pl.pallas_call(
        functools.partial(_ffn_in_mlstm_body, tm=tm, nt=nt),
        grid=(ntiles + 1,),
        in_specs=[cur(D_MODEL)] + [_const_spec(c.shape) for c in consts],
        out_specs=[cur(D_MODEL), cur(CONV_CH), prev(M_WIDTH),
                   pl.BlockSpec((1, N_HEADS, HEAD_DIM, AUG), lambda g: (prev_seq(g), 0, 0, 0)),
                   pl.BlockSpec((1, N_HEADS, 1, HEAD_DIM), lambda g: (prev_seq(g), 0, 0, 0))],
        out_shape=[jax.ShapeDtypeStruct((m, D_MODEL), F32), jax.ShapeDtypeStruct((m, CONV_CH), F32),
                   jax.ShapeDtypeStruct((m, M_WIDTH), F32),
                   jax.ShapeDtypeStruct((batch, N_HEADS, HEAD_DIM, AUG), F32),
                   jax.ShapeDtypeStruct((batch, N_HEADS, 1, HEAD_DIM), F32)],
        scratch_shapes=[pltpu.VMEM((2, tm, M_WIDTH), BF16), pltpu.VMEM((2, M_WIDTH, tm), BF16),
                        pltpu.VMEM((2, tm, M_WIDTH), F32), pltpu.VMEM((2, tm, M_WIDTH), F32),
                        pltpu.VMEM((2, 2 * N_HEADS, tm), F32),
                        pltpu.VMEM((N_HEADS, HEAD_DIM, AUG), F32), pltpu.VMEM((N_HEADS, 1, HEAD_DIM), F32),
                        pltpu.VMEM((tm, D_MODEL), BF16), pltpu.VMEM((tm, D_FF), BF16)],
        compiler_params=pltpu.CompilerParams(dimension_semantics=("arbitrary",),
                                             vmem_limit_bytes=VMEM_LIMIT_TOKENWISE),
        name="ffn_in_mlstm",
    )(x2d, *consts)


def _conv_ffn_out(u, ym, x1, p, batch, seq, tm):
    m = batch * seq
    nt = seq // tm
    row = lambda w: pl.BlockSpec((tm, w), lambda g: (g, 0))
    consts = [p["conv_w"], p["conv_b"], p["conv_ln_g"], p["conv_ln_b"]] + _ffn_out_consts(p)
    return pl.pallas_call(
        functools.partial(_conv_ffn_out_body, tm=tm, nt=nt),
        grid=(m // tm,),
        in_specs=[row(CONV_CH), row(M_WIDTH), row(D_MODEL)] + [_const_spec(c.shape) for c in consts],
        out_specs=[row(D_MODEL), pl.BlockSpec((1, HIST_PAD, CONV_CH), lambda g: (g // nt, 0, 0))],
        out_shape=[jax.ShapeDtypeStruct((m, D_MODEL), F32), jax.ShapeDtypeStruct((batch, HIST_PAD, CONV_CH), F32)],
        scratch_shapes=[pltpu.VMEM((CONV_CH // LANES, HIST_PAD + tm, LANES), F32),
                        pltpu.VMEM((CONV_CH // LANES, tm, LANES), F32)],
        compiler_params=pltpu.CompilerParams(dimension_semantics=("arbitrary",),
                                             vmem_limit_bytes=VMEM_LIMIT_TOKENWISE),
        name="conv_ffn_out",
    )(u, ym, x1, *consts)


def _ffn_in_body(x_ref, g1_ref, wg_ref, wu_ref, wd_ref, gpost_ref, gmix_ref, wqvo_ref, wktg_ref, wconv_ref,
                 gbias_ref, x1_ref, q_ref, kt_ref, v_ref, og_ref, u_ref, gt_ref):
    outs = _ffn_in_values(x_ref[...], g1_ref, wg_ref, wu_ref, wd_ref, gpost_ref, gmix_ref, wqvo_ref, wktg_ref,
                          wconv_ref, gbias_ref)
    for ref, val in zip((x1_ref, q_ref, kt_ref, v_ref, og_ref, u_ref, gt_ref), outs):
        ref[...] = val


def _ffn_out_body(mix_ref, x1_ref, wout_ref, gmp_ref, g2_ref, wg_ref, wu_ref, wd_ref, gpost_ref, gfin_ref, y_ref):
    y_ref[...] = _ffn_out_values(mix_ref[:, :M_WIDTH], mix_ref[:, M_WIDTH:], x1_ref[...], wout_ref, gmp_ref, g2_ref,
                                 wg_ref, wu_ref, wd_ref, gpost_ref, gfin_ref)


def _mixer_sample_body(q_ref, kt_ref, v_ref, og_ref, u_ref, gt_ref, caug0_ref, m0_ref, hist0_ref,
                       cw_ref, cb_ref, lg_ref, lb_ref, mix_ref, caug_out, m_out, hist_out, ubuf, *, nseq, L):
    R = nseq * L
    r = lax.broadcasted_iota(jnp.int32, (R, R), 0)
    c = lax.broadcasted_iota(jnp.int32, (R, R), 1)
    same = lax.div(r, L) == lax.div(c, L)
    caus = same & (c <= r)
    segtriu = jnp.where(same & (r <= c), 1.0, 0.0).astype(F32)
    segones = jnp.where(same, 1.0, 0.0).astype(F32)
    rowseq = lax.div(lax.broadcasted_iota(jnp.int32, (R, AUG), 0), L)
    ones = jnp.ones((R, HEAD_DIM), F32)
    gt = gt_ref[...]
    b_rows = jnp.dot(gt, segtriu, precision=HIGHEST, preferred_element_type=F32)
    tot_rows = jnp.dot(gt, segones, precision=HIGHEST, preferred_element_type=F32)
    for h in range(N_HEADS):
        hs = slice(h * HEAD_DIM, (h + 1) * HEAD_DIM)
        b_r = b_rows[N_HEADS + h:N_HEADS + h + 1, :]
        a_r = gt[h:h + 1, :] - b_r
        t_r = tot_rows[N_HEADS + h:N_HEADS + h + 1, :]
        arow = jnp.broadcast_to(a_r, (R, R))
        bcol = jnp.broadcast_to(b_r, (R, R)).T
        acol = arow.T
        tcol = jnp.broadcast_to(t_r, (R, R)).T
        mprev = m0_ref[h]
        mprev = jnp.concatenate([mprev, mprev], axis=1)
        dmat = jnp.where(caus, bcol + arow, -jnp.inf)
        m_t = jnp.maximum(bcol + mprev, jnp.max(dmat, axis=1, keepdims=True))
        dend = jnp.where(same, tcol + arow, -jnp.inf)
        m_new = jnp.maximum(tcol + mprev, jnp.max(dend, axis=1, keepdims=True))
        qh = q_ref[:, hs]
        kth = kt_ref[hs, :]
        s = _dot(qh, kth) * jnp.exp(dmat - m_t)
        vaug = jnp.concatenate([v_ref[:, hs], ones], axis=1)
        w_inter = jnp.exp(bcol + mprev - m_t)
        qc = jnp.zeros((R, AUG), F32)
        for i in range(nseq):
            qc = jnp.where(rowseq == i, _dot(qh, caug0_ref[i, h].astype(BF16)), qc)
        sv = _dot(s.astype(BF16), vaug.astype(BF16))
        num = w_inter[:, :HEAD_DIM] * qc[:, :HEAD_DIM] + sv[:, :HEAD_DIM]
        den = w_inter[:, :HEAD_DIM] * qc[:, HEAD_DIM:] + sv[:, HEAD_DIM:]
        hh = num / jnp.maximum(jnp.abs(den), jnp.exp(-m_t[:, :HEAD_DIM]))
        mix_ref[:, hs] = og_ref[:, hs] * hh
        g_state = jnp.exp(tcol + mprev - m_new)
        g_rows = jnp.exp(tcol + acol - m_new)
        gv = g_rows * vaug
        for i in range(nseq):
            gvi = jnp.where(rowseq == i, gv, 0.0).astype(BF16)
            caug_out[i, h] = g_state[i * L:i * L + 1, :] * caug0_ref[i, h] + _dot(kth, gvi)
            m_out[i, h] = m_new[i * L:i * L + 1, :HEAD_DIM]

    for i in range(nseq):
        ubuf[0:HIST_PAD, :] = hist0_ref[i]
        ubuf[HIST_PAD:HIST_PAD + L, :] = u_ref[i * L:(i + 1) * L, :]
        acc = _conv_taps(ubuf, 0, L, slice(0, CONV_CH), cw_ref, cb_ref)
        mix_ref[i * L:(i + 1) * L, M_WIDTH:] = _ln_swish(acc, lg_ref, lb_ref)
        hist_out[i] = ubuf[L:L + HIST_PAD, :]


def _ffn_in(x2d, p, tm):
    m = x2d.shape[0]
    row = lambda w: pl.BlockSpec((tm, w), lambda i: (i, 0))
    col = lambda h: pl.BlockSpec((h, tm), lambda i: (0, i))
    consts = _ffn_in_consts(p, tm)
    return pl.pallas_call(
        _ffn_in_body,
        grid=(m // tm,),
        in_specs=[row(D_MODEL)] + [_const_spec(c.shape) for c in consts],
        out_specs=[row(D_MODEL), row(M_WIDTH), col(M_WIDTH), row(M_WIDTH), row(M_WIDTH), row(CONV_CH),
                   col(2 * N_HEADS)],
        out_shape=[jax.ShapeDtypeStruct((m, D_MODEL), F32), jax.ShapeDtypeStruct((m, M_WIDTH), BF16),
                   jax.ShapeDtypeStruct((M_WIDTH, m), BF16), jax.ShapeDtypeStruct((m, M_WIDTH), F32),
                   jax.ShapeDtypeStruct((m, M_WIDTH), F32), jax.ShapeDtypeStruct((m, CONV_CH), F32),
                   jax.ShapeDtypeStruct((2 * N_HEADS, m), F32)],
        compiler_params=pltpu.CompilerParams(dimension_semantics=("arbitrary",),
                                             vmem_limit_bytes=VMEM_LIMIT_TOKENWISE),
        name="ffn_in",
    )(x2d, *consts)


def _ffn_out(mix, x1, p, tm):
    m = mix.shape[0]
    row = pl.BlockSpec((tm, D_MODEL), lambda i: (i, 0))
    consts = _ffn_out_consts(p)
    return pl.pallas_call(
        _ffn_out_body,
        grid=(m // tm,),
        in_specs=[row, row] + [_const_spec(c.shape) for c in consts],
        out_specs=row,
        out_shape=jax.ShapeDtypeStruct((m, D_MODEL), F32),
        compiler_params=pltpu.CompilerParams(dimension_semantics=("arbitrary",),
                                             vmem_limit_bytes=VMEM_LIMIT_TOKENWISE),
        name="ffn_out",
    )(mix, x1, *consts)


def _mixer_sample(q, kt, v, og, u, gt, caug0, m0, hist0, p, nseq, L):
    rows = nseq * L
    args = [q, kt, v, og, u, gt, caug0, m0, hist0, p["conv_w"], p["conv_b"], p["conv_ln_g"], p["conv_ln_b"]]
    full = lambda a: pl.BlockSpec(a.shape, lambda i, nd=a.ndim: (0,) * nd)
    out_shape = [jax.ShapeDtypeStruct((rows, D_MODEL), F32),
                 jax.ShapeDtypeStruct((nseq, N_HEADS, HEAD_DIM, AUG), F32),
                 jax.ShapeDtypeStruct((nseq, N_HEADS, 1, HEAD_DIM), F32),
                 jax.ShapeDtypeStruct((nseq, HIST_PAD, CONV_CH), F32)]
    return pl.pallas_call(
        functools.partial(_mixer_sample_body, nseq=nseq, L=L),
        grid=(1,),
        in_specs=[full(a) for a in args],
        out_specs=[full(s) for s in out_shape],
        out_shape=out_shape,
        scratch_shapes=[pltpu.VMEM((HIST_PAD + L, CONV_CH), F32)],
        compiler_params=pltpu.CompilerParams(dimension_semantics=("arbitrary",),
                                             vmem_limit_bytes=VMEM_LIMIT_MIXER),
        name="mixer_sample",
    )(*args)


def _layer_params(l, ffn1_pre_g, ffn1_wg, ffn1_wu, ffn1_wd, ffn1_post_g, mix_pre_g, w_in, b_igate, b_fgate,
                  conv_w, conv_b, conv_ln_g, conv_ln_b, w_out, mix_post_g, ffn2_pre_g, ffn2_wg, ffn2_wu, ffn2_wd,
                  ffn2_post_g, final_g):
    vec = lambda a: a[l].astype(F32).reshape(1, -1)
    w = w_in[l]
    cuts = [0, M_WIDTH, 2 * M_WIDTH, 3 * M_WIDTH, 4 * M_WIDTH, 4 * M_WIDTH + N_HEADS, 4 * M_WIDTH + 2 * N_HEADS,
            4 * M_WIDTH + 2 * N_HEADS + CONV_CH, 4 * M_WIDTH + 2 * N_HEADS + 2 * CONV_CH]
    wq, wk, wv, wo, wi, wf, wcv, wcg = [w[:, a:b] for a, b in zip(cuts[:-1], cuts[1:])]
    return {
        "ffn1_pre_g": vec(ffn1_pre_g), "ffn1_post_g": vec(ffn1_post_g), "mix_pre_g": vec(mix_pre_g),
        "mix_post_g": vec(mix_post_g), "ffn2_pre_g": vec(ffn2_pre_g), "ffn2_post_g": vec(ffn2_post_g),
        "final_g": vec(final_g),
        "ffn1_wg": ffn1_wg[l].astype(BF16), "ffn1_wu": ffn1_wu[l].astype(BF16), "ffn1_wd": ffn1_wd[l].astype(BF16),
        "ffn2_wg": ffn2_wg[l].astype(BF16), "ffn2_wu": ffn2_wu[l].astype(BF16), "ffn2_wd": ffn2_wd[l].astype(BF16),
        "w_qvo": jnp.concatenate([wq, wv, wo], axis=1).astype(BF16),
        "w_ktg": jnp.concatenate([wk, wi, wf], axis=1).T.astype(BF16),
        "w_conv": jnp.concatenate([wcv, wcg], axis=1).astype(BF16),
        "gbias": jnp.concatenate([b_igate[l], b_fgate[l]]).astype(F32),
        "w_out": w_out[l].astype(BF16),
        "conv_w": conv_w[l].astype(F32), "conv_b": vec(conv_b), "conv_ln_g": vec(conv_ln_g),
        "conv_ln_b": vec(conv_ln_b),
    }


def _split_state(caug, m, hist):
    return caug[..., :HEAD_DIM], caug[..., HEAD_DIM], m[:, :, 0, 0], hist[:, HIST_PAD - HIST:, :]


def kernel(x_prompt, x_sample, state_mlstm_C, state_mlstm_n, state_mlstm_m, cache_conv, ffn1_pre_g, ffn1_wg,
           ffn1_wu, ffn1_wd, ffn1_post_g, mix_pre_g, w_in, b_igate, b_fgate, conv_w, conv_b, conv_ln_g, conv_ln_b,
           w_out, mix_post_g, ffn2_pre_g, ffn2_wg, ffn2_wu, ffn2_wd, ffn2_post_g, final_g):
    batch, seq, _ = x_prompt.shape
    nseq, dseq, _ = x_sample.shape
    depth = w_in.shape[0]
    assert seq % PROMPT_TILE == 0 and PROMPT_TILE % CHUNK == 0 and PROMPT_TILE % CONV_ROWS == 0
    assert (nseq * dseq) % TOKEN_TILE == 0 and dseq <= HIST_PAD
    yp = x_prompt.reshape(batch * seq, D_MODEL)
    ys = x_sample.reshape(nseq * dseq, D_MODEL)
    outs_p, outs_s = [], []
    for l in range(depth):
        p = _layer_params(l, ffn1_pre_g, ffn1_wg, ffn1_wu, ffn1_wd, ffn1_post_g, mix_pre_g, w_in, b_igate, b_fgate,
                          conv_w, conv_b, conv_ln_g, conv_ln_b, w_out, mix_post_g, ffn2_pre_g, ffn2_wg, ffn2_wu,
                          ffn2_wd, ffn2_post_g, final_g)
        x1, u, ym, caug, m = _ffn_in_mlstm(yp, p, batch, seq, PROMPT_TILE)
        yp, hist = _conv_ffn_out(u, ym, x1, p, batch, seq, 2 * PROMPT_TILE)
        outs_p.append(_split_state(caug, m, hist))
        x1, q, kt, v, og, u, gt = _ffn_in(ys, p, TOKEN_TILE)
        n0 = jnp.broadcast_to(state_mlstm_n[l].astype(F32)[..., None], (nseq, N_HEADS, HEAD_DIM, HEAD_DIM))
        caug0 = jnp.concatenate([state_mlstm_C[l].astype(F32), n0], axis=-1)
        m0 = jnp.broadcast_to(state_mlstm_m[l].astype(F32).T[:, :, None, None], (N_HEADS, nseq, dseq, HEAD_DIM))
        m0 = m0.reshape(N_HEADS, nseq * dseq, HEAD_DIM)
        hist0 = jnp.pad(cache_conv[l].astype(F32), ((0, 0), (HIST_PAD - HIST, 0), (0, 0)))
        mix, caug, m, hist = _mixer_sample(q, kt, v, og, u, gt, caug0, m0, hist0, p, nseq, dseq)
        ys = _ffn_out(mix, x1, p, TOKEN_TILE)
        outs_s.append(_split_state(caug, m, hist))
    stack = lambda outs, k: jnp.stack([o[k] for o in outs])
    return (yp.reshape(batch, seq, D_MODEL), ys.reshape(nseq, dseq, D_MODEL),
            stack(outs_p, 0), stack(outs_p, 1), stack(outs_p, 2), stack(outs_p, 3),
            stack(outs_s, 0), stack(outs_s, 1), stack(outs_s, 2), stack(outs_s, 3))
```

```python
import functools

import jax
import jax.numpy as jnp
from jax import lax
from jax.experimental import pallas as pl
from jax.experimental.pallas import tpu as pltpu

D_MODEL = 1024
D_FF = 2816
N_HEADS = 4
HEAD_DIM = 128
M_WIDTH = N_HEADS * HEAD_DIM
CONV_CH = 512
CONV_WIDTH = 31
HIST = CONV_WIDTH - 1
HIST_PAD = 32
SUBLANES = 8
EPS = 1e-6
CHUNK = 128
AUG = 2 * HEAD_DIM
CONV_ROWS = 128
LANES = 128
LN_ROWS = 32
NCH = 256

F32 = jnp.float32
BF16 = jnp.bfloat16
HIGHEST = lax.Precision.HIGHEST
NT_DIMS = (((1,), (1,)), ((), ()))

TOKEN_TILE = 256
PROMPT_TILE = 256
FFN_ROWS = 256
PIN_MLSTM = False
MLSTM_PIECES_PER_MATMUL = 3
VMEM_LIMIT_TOKENWISE = 56 * 1024 * 1024
VMEM_LIMIT_MIXER = 40 * 1024 * 1024


def _rms(x, g):
    return x * lax.rsqrt(jnp.mean(x * x, axis=-1, keepdims=True) + EPS) * g


def _dot(a, b):
    return jnp.dot(a, b, preferred_element_type=F32)


def _swiglu(h, wg_ref, wu_ref, wd_ref):
    a = _dot(h, wg_ref[...])
    b = _dot(h, wu_ref[...])
    s = (a * jax.nn.sigmoid(a) * b).astype(BF16)
    return _dot(s, wd_ref[...])


class _Side:
    def __init__(self, make_pieces, pin=True):
        self.zero = 0
        self.pin = pin
        self.pieces = make_pieces(self)

    def run(self, after, n=1):
        if self.pin:
            bits = pltpu.bitcast(after[after.shape[0] - SUBLANES:, 0:LANES], jnp.int32)
            self.zero = lax.shift_right_logical(lax.shift_right_logical(bits, 16), 16)[0, 0]
        if n is None:
            for _ in self.pieces:
                pass
        else:
            for _ in range(n):
                next(self.pieces, None)


def _swiglu_chunked(h_s, s_s, wg_ref, wu_ref, wd_ref, side, chunk, after_gating, down_pieces=1):
    for c0 in range(0, D_FF, chunk):
        cols = slice(c0, min(c0 + chunk, D_FF))
        a = _dot(h_s[...], wg_ref[:, cols])
        b = _dot(h_s[...], wu_ref[:, cols])
        s = a * jax.nn.sigmoid(a) * b
        s_s[:, cols] = s.astype(BF16)
        for t0 in range(0, cols.stop - cols.start, NCH):
            side.run((s if after_gating else a)[:, t0:t0 + LANES])
    d = []
    for c0 in range(0, D_MODEL, chunk):
        d.append(_dot(s_s[...], wd_ref[:, c0:c0 + chunk]))
        for t0 in range(0, chunk, NCH):
            side.run(d[-1][:, t0:t0 + LANES], down_pieces)
    return jnp.concatenate(d, axis=1)


def _gate_rows(pre, gbias_ref):
    g = pre + gbias_ref[...]
    row = lax.broadcasted_iota(jnp.int32, g.shape, 0)
    return jnp.where(row < N_HEADS, g, jax.nn.log_sigmoid(g))


def _ffn_in_values(x, g1_ref, wg_ref, wu_ref, wd_ref, gpost_ref, gmix_ref, wqvo_ref, wktg_ref, wconv_ref, gbias_ref):
    h = _rms(x, g1_ref[...]).astype(BF16)
    d = _swiglu(h, wg_ref, wu_ref, wd_ref)
    x1 = x + 0.5 * _rms(d, gpost_ref[...])
    h2 = _rms(x1, gmix_ref[...]).astype(BF16)
    qvo = _dot(h2, wqvo_ref[...])
    q = qvo[:, :M_WIDTH].astype(BF16)
    v = qvo[:, M_WIDTH:2 * M_WIDTH]
    og = jax.nn.sigmoid(qvo[:, 2 * M_WIDTH:])
    ktg = lax.dot_general(wktg_ref[...], h2, NT_DIMS, preferred_element_type=F32)
    kt = ktg[:M_WIDTH] * (HEAD_DIM ** -0.5)
    cc = _dot(h2, wconv_ref[...])
    u = cc[:, :CONV_CH] * jax.nn.sigmoid(cc[:, CONV_CH:])
    return x1, q, kt.astype(BF16), v, og, u, _gate_rows(ktg[M_WIDTH:], gbias_ref)


def _ffn_out_values(ym, c, x1, wout_ref, gmp_ref, g2_ref, wg_ref, wu_ref, wd_ref, gpost_ref, gfin_ref):
    o = _dot(ym.astype(BF16), wout_ref[0:M_WIDTH, :]) + _dot(c.astype(BF16), wout_ref[M_WIDTH:, :])
    x2 = x1 + _rms(o, gmp_ref[...])
    h = _rms(x2, g2_ref[...]).astype(BF16)
    d = _swiglu(h, wg_ref, wu_ref, wd_ref)
    x3 = x2 + 0.5 * _rms(d, gpost_ref[...])
    return _rms(x3, gfin_ref[...])


def _ffn_out_phases(ym, c, x1, y_ref, rows, wout_ref, gmp_ref, g2_ref, wg_ref, wu_ref, wd_ref, gpost_ref, gfin_ref):
    o = _dot(ym.astype(BF16), wout_ref[0:M_WIDTH, :]) + _dot(c.astype(BF16), wout_ref[M_WIDTH:, :])
    yield o
    x2 = x1 + _rms(o, gmp_ref[...])
    h = _rms(x2, g2_ref[...]).astype(BF16)
    a = _dot(h, wg_ref[...])
    yield a
    b = _dot(h, wu_ref[...])
    yield b
    s = (a * jax.nn.sigmoid(a) * b).astype(BF16)
    d = _dot(s, wd_ref[...])
    yield d
    x3 = x2 + 0.5 * _rms(d, gpost_ref[...])
    y = _rms(x3, gfin_ref[...])
    y_ref[rows, :] = y
    yield y


def _conv_taps(ubuf, row0, nrows, lanes, cw_ref, cb_ref):
    acc = jnp.broadcast_to(cb_ref[:, lanes], (nrows, lanes.stop - lanes.start))
    first = HIST_PAD - HIST
    for res in range(SUBLANES):
        taps = [j for j in range(CONV_WIDTH) if (first + j) % SUBLANES == res]
        lo = (first + taps[0]) // SUBLANES * SUBLANES
        hi = (first + taps[-1]) // SUBLANES * SUBLANES
        win = ubuf[row0 + lo:row0 + hi + nrows + (SUBLANES if res else 0), lanes]
        if res:
            win = pltpu.roll(win, win.shape[0] - res, 0)
        for j in taps:
            off = first + j - res - lo
            acc = acc + win[off:off + nrows, :] * cw_ref[j:j + 1, lanes]
    return acc


def _conv_taps_interleaved(uslab, row0, nrows, lanes, cw_ref, cb_ref):
    first = HIST_PAD - HIST
    half = nrows // 2
    acc = [jnp.broadcast_to(cb_ref[:, lanes], (half, LANES))] * 2
    for j in range(CONV_WIDTH):
        for parity in range(2):
            x = uslab[pl.ds(row0 + first + j + parity, half, stride=2), :]
            acc[parity] = acc[parity] + x * cw_ref[j:j + 1, lanes]
    return acc


def _ln_swish(acc, lg_ref, lb_ref):
    mu = jnp.mean(acc, axis=-1, keepdims=True)
    xc = acc - mu
    var = jnp.mean(xc * xc, axis=-1, keepdims=True)
    y = xc * lax.rsqrt(var + EPS) * lg_ref[...] + lb_ref[...]
    return y * jax.nn.sigmoid(y)


def _mlstm_gates(b_r, a_r, caus, qh, kth, caug):
    L = b_r.shape[1]
    arow = jnp.broadcast_to(a_r, (L, L))
    bcol = jnp.broadcast_to(b_r, (L, L)).T
    acol = arow.T
    dmat = jnp.where(caus, bcol + arow, -jnp.inf)
    return dict(bcol=bcol, acol=acol, dmat=dmat, rowmax=jnp.max(dmat, axis=1, keepdims=True),
                s_raw=_dot(qh, kth), qc=_dot(qh, caug.astype(BF16)), caug=caug, kth=kth)


def _mlstm_scores(st, mprev, vh):
    m_t = jnp.maximum(st["bcol"] + mprev, st["rowmax"])
    s = st["s_raw"] * jnp.exp(st["dmat"] - m_t)
    vaug = jnp.concatenate([vh, jnp.ones_like(vh)], axis=1)
    st.update(m_t=m_t, mprev=mprev, vaug=vaug, sv=_dot(s.astype(BF16), vaug.astype(BF16)),
              w_inter=jnp.exp(st["bcol"] + mprev - m_t))
    return m_t[m_t.shape[0] - 1:, :]


def _mlstm_output(st):
    L = st["m_t"].shape[0]
    m_t, qc, sv, w_inter = st["m_t"], st["qc"], st["sv"], st["w_inter"]
    num = w_inter * qc[:, :HEAD_DIM] + sv[:, :HEAD_DIM]
    den = w_inter * qc[:, HEAD_DIM:] + sv[:, HEAD_DIM:]
    hh = num / jnp.maximum(jnp.abs(den), jnp.exp(-m_t))
    m_new = m_t[L - 1:L, :]
    b_last = st["bcol"][L - 1:L, :]
    g_state = jnp.exp(b_last + st["mprev"] - m_new)
    g_rows = jnp.exp(b_last + st["acol"] - m_new)
    gv = (jnp.concatenate([g_rows, g_rows], axis=1) * st["vaug"]).astype(BF16)
    caug_new = jnp.concatenate([g_state, g_state], axis=1) * st["caug"] + _dot(st["kth"], gv)
    return hh, caug_new


def _ffn_in_mlstm_body(x_ref, g1_ref, wg_ref, wu_ref, wd_ref, gpost_ref, gmix_ref, wqvo_ref, wktg_ref, wconv_ref,
                       gbias_ref, x1_ref, u_ref, ym_ref, caug_out, m_out,
                       q_st, kt_st, v_st, og_st, g_st, caug_s, m_s, *, tm, nt):
    g = pl.program_id(0)
    wslot = lax.rem(g, 2)
    rslot = 1 - wslot
    L = CHUNK

    @pl.when(g == 0)
    def _init():
        for st in (q_st, kt_st, v_st, og_st, g_st, caug_s, m_s):
            st[...] = jnp.zeros_like(st)

    def mlstm_pieces(side):
        fresh = lax.rem(g - 1, nt) == 0
        r = lax.broadcasted_iota(jnp.int32, (L, L), 0)
        c = lax.broadcasted_iota(jnp.int32, (L, L), 1)
        caus = c <= r
        triu = (r <= c).astype(F32)
        nck = tm // L
        items = [(ck, h) for ck in range(nck) for h in range(N_HEADS)]
        gates = {}
        stages = {}

        def stage1(ck, h):
            rows, hs, slot = slice(ck * L, (ck + 1) * L), slice(h * HEAD_DIM, (h + 1) * HEAD_DIM), rslot + side.zero
            if ck not in gates:
                gt = g_st[slot, :, rows]
                gates[ck] = gt, jnp.dot(gt, triu, precision=HIGHEST, preferred_element_type=F32)
            gt, b_rows = gates[ck]
            b_r = b_rows[N_HEADS + h:N_HEADS + h + 1, :]
            caug = caug_s[h + side.zero]
            if ck == 0:
                caug = jnp.where(fresh, 0.0, caug)
            stages[ck, h] = _mlstm_gates(b_r, gt[h:h + 1, :] - b_r, caus, q_st[slot, rows, hs], kt_st[slot, hs, rows],
                                         caug)

        def stage2(ck, h):
            rows, hs, slot = slice(ck * L, (ck + 1) * L), slice(h * HEAD_DIM, (h + 1) * HEAD_DIM), rslot + side.zero
            mprev = m_s[h + side.zero]
            if ck == 0:
                mprev = jnp.where(fresh, 0.0, mprev)
            m_new = _mlstm_scores(stages[ck, h], mprev, v_st[slot, rows, hs])
            m_s[h] = m_new
            if ck == nck - 1:
                m_out[0, h] = m_new

        def stage3(ck, h):
            rows, hs, slot = slice(ck * L, (ck + 1) * L), slice(h * HEAD_DIM, (h + 1) * HEAD_DIM), rslot + side.zero
            hh, caug = _mlstm_output(stages.pop((ck, h)))
            ym_ref[rows, hs] = og_st[slot, rows, hs] * hh
            caug_s[h] = caug
            if ck == nck - 1:
                caug_out[0, h] = caug

        for t in range(len(items) + 2):
            for lag, stage in enumerate((stage1, stage2, stage3)):
                if 0 <= t - lag < len(items):
                    stage(*items[t - lag])
            yield

    side = _Side(mlstm_pieces, pin=PIN_MLSTM)

    blocks = [_ffn_in_block(x_ref, slice(r0, r0 + FFN_ROWS), wslot, g1_ref, wg_ref, wu_ref, wd_ref, gpost_ref,
                            gmix_ref, wqvo_ref, wktg_ref, wconv_ref, gbias_ref, x1_ref, u_ref, q_st, kt_st, v_st,
                            og_st, g_st) for r0 in range(0, tm, FFN_ROWS)]
    last = None
    while blocks:
        for block in list(blocks):
            token = next(block, None)
            if token is None:
                blocks.remove(block)
            else:
                side.run(token, MLSTM_PIECES_PER_MATMUL)
                last = token
    side.run(last, None)


def _ffn_in_block(x_ref, rows, wslot, g1_ref, wg_ref, wu_ref, wd_ref, gpost_ref, gmix_ref, wqvo_ref, wktg_ref, wconv_ref,
                  gbias_ref, x1_ref, u_ref, q_st, kt_st, v_st, og_st, g_st):
    x = x_ref[rows, :]
    h = _rms(x, g1_ref[...]).astype(BF16)
    a = _dot(h, wg_ref[...])
    yield a
    b = _dot(h, wu_ref[...])
    yield b
    s = (a * jax.nn.sigmoid(a) * b).astype(BF16)
    d = _dot(s, wd_ref[...])
    yield d
    x1 = x + 0.5 * _rms(d, gpost_ref[...])
    x1_ref[rows, :] = x1
    h2 = _rms(x1, gmix_ref[...]).astype(BF16)
    qvo = _dot(h2, wqvo_ref[...])
    q_st[wslot, rows, :] = qvo[:, :M_WIDTH].astype(BF16)
    v_st[wslot, rows, :] = qvo[:, M_WIDTH:2 * M_WIDTH]
    og_st[wslot, rows, :] = jax.nn.sigmoid(qvo[:, 2 * M_WIDTH:])
    yield qvo
    ktg = lax.dot_general(wktg_ref[...], h2, NT_DIMS, preferred_element_type=F32)
    kt_st[wslot, :, rows] = (ktg[:M_WIDTH] * (HEAD_DIM ** -0.5)).astype(BF16)
    g_st[wslot, :, rows] = _gate_rows(ktg[M_WIDTH:], gbias_ref)
    yield ktg
    cc = _dot(h2, wconv_ref[...])
    u_ref[rows, :] = cc[:, :CONV_CH] * jax.nn.sigmoid(cc[:, CONV_CH:])
    yield cc


def _conv_ffn_out_body(u_ref, ym_ref, x1_ref, cw_ref, cb_ref, lg_ref, lb_ref, wout_ref, gmp_ref, g2_ref,
                       wg_ref, wu_ref, wd_ref, gpost_ref, gfin_ref, y_ref, hist_out, ubuf, cbuf, *, tm, nt):
    nslab = CONV_CH // LANES

    @pl.when(lax.rem(pl.program_id(0), nt) == 0)
    def _new_sequence():
        ubuf[:, 0:HIST_PAD, :] = jnp.zeros((nslab, HIST_PAD, LANES), F32)

    for lb in range(nslab):
        ubuf[lb, HIST_PAD:HIST_PAD + tm, :] = u_ref[:, lb * LANES:(lb + 1) * LANES]
    for k in range(tm // CONV_ROWS):
        for lb in range(nslab):
            acc = _conv_taps_interleaved(ubuf.at[lb], k * CONV_ROWS, CONV_ROWS, slice(lb * LANES, (lb + 1) * LANES),
                                         cw_ref, cb_ref)
            for parity in range(2):
                cbuf[lb, pl.ds(k * CONV_ROWS + parity, CONV_ROWS // 2, stride=2), :] = acc[parity]
    tail = jnp.concatenate([ubuf[lb, tm:tm + HIST_PAD, :] for lb in range(nslab)], axis=1)
    hist_out[0] = tail
    for lb in range(nslab):
        ubuf[lb, 0:HIST_PAD, :] = tail[:, lb * LANES:(lb + 1) * LANES]

    halves = []
    for r0 in range(0, tm, FFN_ROWS):
        c = []
        for k in range(FFN_ROWS // LN_ROWS):
            rows = slice(r0 + k * LN_ROWS, r0 + (k + 1) * LN_ROWS)
            c.append(_ln_swish(jnp.concatenate([cbuf[lb, rows, :] for lb in range(nslab)], axis=1), lg_ref, lb_ref))
        rows = slice(r0, r0 + FFN_ROWS)
        halves.append(_ffn_out_phases(ym_ref[rows, :], jnp.concatenate(c, axis=0), x1_ref[rows, :], y_ref, rows,
                                      wout_ref, gmp_ref, g2_ref, wg_ref, wu_ref, wd_ref, gpost_ref, gfin_ref))
    for _ in zip(*halves):
        pass


def _const_spec(shape):
    nd = len(shape)
    return pl.BlockSpec(shape, lambda *_: (0,) * nd, pipeline_mode=pl.Buffered(1))


def _ffn_in_consts(p, tm):
    gbias = jnp.broadcast_to(p["gbias"][:, None], (2 * N_HEADS, tm))
    return [p["ffn1_pre_g"], p["ffn1_wg"], p["ffn1_wu"], p["ffn1_wd"], p["ffn1_post_g"], p["mix_pre_g"],
            p["w_qvo"], p["w_ktg"], p["w_conv"], gbias]


def _ffn_out_consts(p):
    return [p["w_out"], p["mix_post_g"], p["ffn2_pre_g"], p["ffn2_wg"], p["ffn2_wu"], p["ffn2_wd"],
            p["ffn2_post_g"], p["final_g"]]


def _ffn_in_mlstm(x2d, p, batch, seq, tm):
    m = batch * seq
    nt = seq // tm
    ntiles = m // tm
    cur = lambda w: pl.BlockSpec((tm, w), lambda g: (jnp.minimum(g, ntiles - 1), 0))
    prev = lambda w: pl.BlockSpec((tm, w), lambda g: (jnp.maximum(g - 1, 0), 0))
    prev_seq = lambda g: jnp.maximum(g - 1, 0) // nt
    consts = _ffn_in_consts(p, FFN_ROWS)
    return pl.pallas_call(
        functools.partial(_ffn_in_mlstm_body, tm=tm, nt=nt),
        grid=(ntiles + 1,),
        in_specs=[cur(D_MODEL)] + [_const_spec(c.shape) for c in consts],
        out_specs=[cur(D_MODEL), cur(CONV_CH), prev(M_WIDTH),
                   pl.BlockSpec((1, N_HEADS, HEAD_DIM, AUG), lambda g: (prev_seq(g), 0, 0, 0)),
                   pl.BlockSpec((1, N_HEADS, 1, HEAD_DIM), lambda g: (prev_seq(g), 0, 0, 0))],
        out_shape=[jax.ShapeDtypeStruct((m, D_MODEL), F32), jax.ShapeDtypeStruct((m, CONV_CH), F32),
                   jax.ShapeDtypeStruct((m, M_WIDTH), F32),
                   jax.ShapeDtypeStruct((batch, N_HEADS, HEAD_DIM, AUG), F32),
                   jax.ShapeDtypeStruct((batch, N_HEADS, 1, HEAD_DIM), F32)],
        scratch_shapes=[pltpu.VMEM((2, tm, M_WIDTH), BF16), pltpu.VMEM((2, M_WIDTH, tm), BF16),
                        pltpu.VMEM((2, tm, M_WIDTH), F32), pltpu.VMEM((2, tm, M_WIDTH), F32),
                        pltpu.VMEM((2, 2 * N_HEADS, tm), F32),
                        pltpu.VMEM((N_HEADS, HEAD_DIM, AUG), F32), pltpu.VMEM((N_HEADS, 1, HEAD_DIM), F32)],
        compiler_params=pltpu.CompilerParams(dimension_semantics=("arbitrary",),
                                             vmem_limit_bytes=VMEM_LIMIT_TOKENWISE),
        name="ffn_in_mlstm",
    )(x2d, *consts)


def _conv_ffn_out(u, ym, x1, p, batch, seq, tm):
    m = batch * seq
    nt = seq // tm
    row = lambda w: pl.BlockSpec((tm, w), lambda g: (g, 0))
    consts = [p["conv_w"], p["conv_b"], p["conv_ln_g"], p["conv_ln_b"]] + _ffn_out_consts(p)
    return pl.pallas_call(
        functools.partial(_conv_ffn_out_body, tm=tm, nt=nt),
        grid=(m // tm,),
        in_specs=[row(CONV_CH), row(M_WIDTH), row(D_MODEL)] + [_const_spec(c.shape) for c in consts],
        out_specs=[row(D_MODEL), pl.BlockSpec((1, HIST_PAD, CONV_CH), lambda g: (g // nt, 0, 0))],
        out_shape=[jax.ShapeDtypeStruct((m, D_MODEL), F32), jax.ShapeDtypeStruct((batch, HIST_PAD, CONV_CH), F32)],
        scratch_shapes=[pltpu.VMEM((CONV_CH // LANES, HIST_PAD + tm, LANES), F32),
                        pltpu.VMEM((CONV_CH // LANES, tm, LANES), F32)],
        compiler_params=pltpu.CompilerParams(dimension_semantics=("arbitrary",),
                                             vmem_limit_bytes=VMEM_LIMIT_TOKENWISE),
        name="conv_ffn_out",
    )(u, ym, x1, *consts)


def _ffn_in_body(x_ref, g1_ref, wg_ref, wu_ref, wd_ref, gpost_ref, gmix_ref, wqvo_ref, wktg_ref, wconv_ref,
                 gbias_ref, x1_ref, q_ref, kt_ref, v_ref, og_ref, u_ref, gt_ref):
    outs = _ffn_in_values(x_ref[...], g1_ref, wg_ref, wu_ref, wd_ref, gpost_ref, gmix_ref, wqvo_ref, wktg_ref,
                          wconv_ref, gbias_ref)
    for ref, val in zip((x1_ref, q_ref, kt_ref, v_ref, og_ref, u_ref, gt_ref), outs):
        ref[...] = val


def _ffn_out_body(mix_ref, x1_ref, wout_ref, gmp_ref, g2_ref, wg_ref, wu_ref, wd_ref, gpost_ref, gfin_ref, y_ref):
    y_ref[...] = _ffn_out_values(mix_ref[:, :M_WIDTH], mix_ref[:, M_WIDTH:], x1_ref[...], wout_ref, gmp_ref, g2_ref,
                                 wg_ref, wu_ref, wd_ref, gpost_ref, gfin_ref)


def _mixer_sample_body(q_ref, kt_ref, v_ref, og_ref, u_ref, gt_ref, caug0_ref, m0_ref, hist0_ref,
                       cw_ref, cb_ref, lg_ref, lb_ref, mix_ref, caug_out, m_out, hist_out, ubuf, *, nseq, L):
    R = nseq * L
    r = lax.broadcasted_iota(jnp.int32, (R, R), 0)
    c = lax.broadcasted_iota(jnp.int32, (R, R), 1)
    same = lax.div(r, L) == lax.div(c, L)
    caus = same & (c <= r)
    segtriu = jnp.where(same & (r <= c), 1.0, 0.0).astype(F32)
    segones = jnp.where(same, 1.0, 0.0).astype(F32)
    rowseq = lax.div(lax.broadcasted_iota(jnp.int32, (R, AUG), 0), L)
    ones = jnp.ones((R, HEAD_DIM), F32)
    gt = gt_ref[...]
    b_rows = jnp.dot(gt, segtriu, precision=HIGHEST, preferred_element_type=F32)
    tot_rows = jnp.dot(gt, segones, precision=HIGHEST, preferred_element_type=F32)
    for h in range(N_HEADS):
        hs = slice(h * HEAD_DIM, (h + 1) * HEAD_DIM)
        b_r = b_rows[N_HEADS + h:N_HEADS + h + 1, :]
        a_r = gt[h:h + 1, :] - b_r
        t_r = tot_rows[N_HEADS + h:N_HEADS + h + 1, :]
        arow = jnp.broadcast_to(a_r, (R, R))
        bcol = jnp.broadcast_to(b_r, (R, R)).T
        acol = arow.T
        tcol = jnp.broadcast_to(t_r, (R, R)).T
        mprev = m0_ref[h]
        mprev = jnp.concatenate([mprev, mprev], axis=1)
        dmat = jnp.where(caus, bcol + arow, -jnp.inf)
        m_t = jnp.maximum(bcol + mprev, jnp.max(dmat, axis=1, keepdims=True))
        dend = jnp.where(same, tcol + arow, -jnp.inf)
        m_new = jnp.maximum(tcol + mprev, jnp.max(dend, axis=1, keepdims=True))
        qh = q_ref[:, hs]
        kth = kt_ref[hs, :]
        s = _dot(qh, kth) * jnp.exp(dmat - m_t)
        vaug = jnp.concatenate([v_ref[:, hs], ones], axis=1)
        w_inter = jnp.exp(bcol + mprev - m_t)
        qc = jnp.zeros((R, AUG), F32)
        for i in range(nseq):
            qc = jnp.where(rowseq == i, _dot(qh, caug0_ref[i, h].astype(BF16)), qc)
        sv = _dot(s.astype(BF16), vaug.astype(BF16))
        num = w_inter[:, :HEAD_DIM] * qc[:, :HEAD_DIM] + sv[:, :HEAD_DIM]
        den = w_inter[:, :HEAD_DIM] * qc[:, HEAD_DIM:] + sv[:, HEAD_DIM:]
        hh = num / jnp.maximum(jnp.abs(den), jnp.exp(-m_t[:, :HEAD_DIM]))
        mix_ref[:, hs] = og_ref[:, hs] * hh
        g_state = jnp.exp(tcol + mprev - m_new)
        g_rows = jnp.exp(tcol + acol - m_new)
        gv = g_rows * vaug
        for i in range(nseq):
            gvi = jnp.where(rowseq == i, gv, 0.0).astype(BF16)
            caug_out[i, h] = g_state[i * L:i * L + 1, :] * caug0_ref[i, h] + _dot(kth, gvi)
            m_out[i, h] = m_new[i * L:i * L + 1, :HEAD_DIM]

    for i in range(nseq):
        ubuf[0:HIST_PAD, :] = hist0_ref[i]
        ubuf[HIST_PAD:HIST_PAD + L, :] = u_ref[i * L:(i + 1) * L, :]
        acc = _conv_taps(ubuf, 0, L, slice(0, CONV_CH), cw_ref, cb_ref)
        mix_ref[i * L:(i + 1) * L, M_WIDTH:] = _ln_swish(acc, lg_ref, lb_ref)
        hist_out[i] = ubuf[L:L + HIST_PAD, :]


def _ffn_in(x2d, p, tm):
    m = x2d.shape[0]
    row = lambda w: pl.BlockSpec((tm, w), lambda i: (i, 0))
    col = lambda h: pl.BlockSpec((h, tm), lambda i: (0, i))
    consts = _ffn_in_consts(p, tm)
    return pl.pallas_call(
        _ffn_in_body,
        grid=(m // tm,),
        in_specs=[row(D_MODEL)] + [_const_spec(c.shape) for c in consts],
        out_specs=[row(D_MODEL), row(M_WIDTH), col(M_WIDTH), row(M_WIDTH), row(M_WIDTH), row(CONV_CH),
                   col(2 * N_HEADS)],
        out_shape=[jax.ShapeDtypeStruct((m, D_MODEL), F32), jax.ShapeDtypeStruct((m, M_WIDTH), BF16),
                   jax.ShapeDtypeStruct((M_WIDTH, m), BF16), jax.ShapeDtypeStruct((m, M_WIDTH), F32),
                   jax.ShapeDtypeStruct((m, M_WIDTH), F32), jax.ShapeDtypeStruct((m, CONV_CH), F32),
                   jax.ShapeDtypeStruct((2 * N_HEADS, m), F32)],
        compiler_params=pltpu.CompilerParams(dimension_semantics=("arbitrary",),
                                             vmem_limit_bytes=VMEM_LIMIT_TOKENWISE),
        name="ffn_in",
    )(x2d, *consts)


def _ffn_out(mix, x1, p, tm):
    m = mix.shape[0]
    row = pl.BlockSpec((tm, D_MODEL), lambda i: (i, 0))
    consts = _ffn_out_consts(p)
    return pl.pallas_call(
        _ffn_out_body,
        grid=(m // tm,),
        in_specs=[row, row] + [_const_spec(c.shape) for c in consts],
        out_specs=row,
        out_shape=jax.ShapeDtypeStruct((m, D_MODEL), F32),
        compiler_params=pltpu.CompilerParams(dimension_semantics=("arbitrary",),
                                             vmem_limit_bytes=VMEM_LIMIT_TOKENWISE),
        name="ffn_out",
    )(mix, x1, *consts)


def _mixer_sample(q, kt, v, og, u, gt, caug0, m0, hist0, p, nseq, L):
    rows = nseq * L
    args = [q, kt, v, og, u, gt, caug0, m0, hist0, p["conv_w"], p["conv_b"], p["conv_ln_g"], p["conv_ln_b"]]
    full = lambda a: pl.BlockSpec(a.shape, lambda i, nd=a.ndim: (0,) * nd)
    out_shape = [jax.ShapeDtypeStruct((rows, D_MODEL), F32),
                 jax.ShapeDtypeStruct((nseq, N_HEADS, HEAD_DIM, AUG), F32),
                 jax.ShapeDtypeStruct((nseq, N_HEADS, 1, HEAD_DIM), F32),
                 jax.ShapeDtypeStruct((nseq, HIST_PAD, CONV_CH), F32)]
    return pl.pallas_call(
        functools.partial(_mixer_sample_body, nseq=nseq, L=L),
        grid=(1,),
        in_specs=[full(a) for a in args],
        out_specs=[full(s) for s in out_shape],
        out_shape=out_shape,
        scratch_shapes=[pltpu.VMEM((HIST_PAD + L, CONV_CH), F32)],
        compiler_params=pltpu.CompilerParams(dimension_semantics=("arbitrary",),
                                             vmem_limit_bytes=VMEM_LIMIT_MIXER),
        name="mixer_sample",
    )(*args)


def _layer_params(l, ffn1_pre_g, ffn1_wg, ffn1_wu, ffn1_wd, ffn1_post_g, mix_pre_g, w_in, b_igate, b_fgate,
                  conv_w, conv_b, conv_ln_g, conv_ln_b, w_out, mix_post_g, ffn2_pre_g, ffn2_wg, ffn2_wu, ffn2_wd,
                  ffn2_post_g, final_g):
    vec = lambda a: a[l].astype(F32).reshape(1, -1)
    w = w_in[l]
    cuts = [0, M_WIDTH, 2 * M_WIDTH, 3 * M_WIDTH, 4 * M_WIDTH, 4 * M_WIDTH + N_HEADS, 4 * M_WIDTH + 2 * N_HEADS,
            4 * M_WIDTH + 2 * N_HEADS + CONV_CH, 4 * M_WIDTH + 2 * N_HEADS + 2 * CONV_CH]
    wq, wk, wv, wo, wi, wf, wcv, wcg = [w[:, a:b] for a, b in zip(cuts[:-1], cuts[1:])]
    return {
        "ffn1_pre_g": vec(ffn1_pre_g), "ffn1_post_g": vec(ffn1_post_g), "mix_pre_g": vec(mix_pre_g),
        "mix_post_g": vec(mix_post_g), "ffn2_pre_g": vec(ffn2_pre_g), "ffn2_post_g": vec(ffn2_post_g),
        "final_g": vec(final_g),
        "ffn1_wg": ffn1_wg[l].astype(BF16), "ffn1_wu": ffn1_wu[l].astype(BF16), "ffn1_wd": ffn1_wd[l].astype(BF16),
        "ffn2_wg": ffn2_wg[l].astype(BF16), "ffn2_wu": ffn2_wu[l].astype(BF16), "ffn2_wd": ffn2_wd[l].astype(BF16),
        "w_qvo": jnp.concatenate([wq, wv, wo], axis=1).astype(BF16),
        "w_ktg": jnp.concatenate([wk, wi, wf], axis=1).T.astype(BF16),
        "w_conv": jnp.concatenate([wcv, wcg], axis=1).astype(BF16),
        "gbias": jnp.concatenate([b_igate[l], b_fgate[l]]).astype(F32),
        "w_out": w_out[l].astype(BF16),
        "conv_w": conv_w[l].astype(F32), "conv_b": vec(conv_b), "conv_ln_g": vec(conv_ln_g),
        "conv_ln_b": vec(conv_ln_b),
    }


def _split_state(caug, m, hist):
    return caug[..., :HEAD_DIM], caug[..., HEAD_DIM], m[:, :, 0, 0], hist[:, HIST_PAD - HIST:, :]


def kernel(x_prompt, x_sample, state_mlstm_C, state_mlstm_n, state_mlstm_m, cache_conv, ffn1_pre_g, ffn1_wg,
           ffn1_wu, ffn1_wd, ffn1_post_g, mix_pre_g, w_in, b_igate, b_fgate, conv_w, conv_b, conv_ln_g, conv_ln_b,
           w_out, mix_post_g, ffn2_pre_g, ffn2_wg, ffn2_wu, ffn2_wd, ffn2_post_g, final_g):
    batch, seq, _ = x_prompt.shape
    nseq, dseq, _ = x_sample.shape
    depth = w_in.shape[0]
    assert seq % PROMPT_TILE == 0 and PROMPT_TILE % CHUNK == 0 and PROMPT_TILE % CONV_ROWS == 0
    assert (nseq * dseq) % TOKEN_TILE == 0 and dseq <= HIST_PAD
    yp = x_prompt.reshape(batch * seq, D_MODEL)
    ys = x_sample.reshape(nseq * dseq, D_MODEL)
    outs_p, outs_s = [], []
    for l in range(depth):
        p = _layer_params(l, ffn1_pre_g, ffn1_wg, ffn1_wu, ffn1_wd, ffn1_post_g, mix_pre_g, w_in, b_igate, b_fgate,
                          conv_w, conv_b, conv_ln_g, conv_ln_b, w_out, mix_post_g, ffn2_pre_g, ffn2_wg, ffn2_wu,
                          ffn2_wd, ffn2_post_g, final_g)
        x1, u, ym, caug, m = _ffn_in_mlstm(yp, p, batch, seq, 2 * PROMPT_TILE)
        yp, hist = _conv_ffn_out(u, ym, x1, p, batch, seq, 2 * PROMPT_TILE)
        outs_p.append(_split_state(caug, m, hist))
        x1, q, kt, v, og, u, gt = _ffn_in(ys, p, TOKEN_TILE)
        n0 = jnp.broadcast_to(state_mlstm_n[l].astype(F32)[..., None], (nseq, N_HEADS, HEAD_DIM, HEAD_DIM))
        caug0 = jnp.concatenate([state_mlstm_C[l].astype(F32), n0], axis=-1)
        m0 = jnp.broadcast_to(state_mlstm_m[l].astype(F32).T[:, :, None, None], (N_HEADS, nseq, dseq, HEAD_DIM))
        m0 = m0.reshape(N_HEADS, nseq * dseq, HEAD_DIM)
        hist0 = jnp.pad(cache_conv[l].astype(F32), ((0, 0), (HIST_PAD - HIST, 0), (0, 0)))
        mix, caug, m, hist = _mixer_sample(q, kt, v, og, u, gt, caug0, m0, hist0, p, nseq, dseq)
        ys = _ffn_out(mix, x1, p, TOKEN_TILE)
        outs_s.append(_split_state(caug, m, hist))
    stack = lambda outs, k: jnp.stack([o[k] for o in outs])
    return (yp.reshape(batch, seq, D_MODEL), ys.reshape(nseq, dseq, D_MODEL),
            stack(outs_p, 0), stack(outs_p, 1), stack(outs_p, 2), stack(outs_p, 3),
            stack(outs_s, 0), stack(outs_s, 1), stack(outs_s, 2), stack(outs_s, 3))
```

```python
import functools

import jax
import jax.numpy as jnp
from jax import lax
from jax.experimental import pallas as pl
from jax.experimental.pallas import tpu as pltpu

D_MODEL = 1024
D_FF = 2816
N_HEADS = 4
HEAD_DIM = 128
M_WIDTH = N_HEADS * HEAD_DIM
CONV_CH = 512
CONV_WIDTH = 31
HIST = CONV_WIDTH - 1
HIST_PAD = 32
SUBLANES = 8
EPS = 1e-6
CHUNK = 128
AUG = 2 * HEAD_DIM
CONV_ROWS = 128
LANES = 128
LN_ROWS = 32
NCH = 256

F32 = jnp.float32
BF16 = jnp.bfloat16
HIGHEST = lax.Precision.HIGHEST
NT_DIMS = (((1,), (1,)), ((), ()))

TOKEN_TILE = 256
PROMPT_TILE = 256
FFN_ROWS = 256
PIN_MLSTM = False
VMEM_LIMIT_TOKENWISE = 56 * 1024 * 1024
VMEM_LIMIT_MIXER = 40 * 1024 * 1024


def _rms(x, g):
    return x * lax.rsqrt(jnp.mean(x * x, axis=-1, keepdims=True) + EPS) * g


def _dot(a, b):
    return jnp.dot(a, b, preferred_element_type=F32)


def _swiglu(h, wg_ref, wu_ref, wd_ref):
    a = _dot(h, wg_ref[...])
    b = _dot(h, wu_ref[...])
    s = (a * jax.nn.sigmoid(a) * b).astype(BF16)
    return _dot(s, wd_ref[...])


class _Side:
    def __init__(self, make_pieces, pin=True):
        self.zero = 0
        self.pin = pin
        self.pieces = make_pieces(self)

    def run(self, after, n=1):
        if self.pin:
            bits = pltpu.bitcast(after[after.shape[0] - SUBLANES:, 0:LANES], jnp.int32)
            self.zero = lax.shift_right_logical(lax.shift_right_logical(bits, 16), 16)[0, 0]
        if n is None:
            for _ in self.pieces:
                pass
        else:
            for _ in range(n):
                next(self.pieces, None)


def _swiglu_chunked(h_s, s_s, wg_ref, wu_ref, wd_ref, side, chunk, after_gating, down_pieces=1):
    for c0 in range(0, D_FF, chunk):
        cols = slice(c0, min(c0 + chunk, D_FF))
        a = _dot(h_s[...], wg_ref[:, cols])
        b = _dot(h_s[...], wu_ref[:, cols])
        s = a * jax.nn.sigmoid(a) * b
        s_s[:, cols] = s.astype(BF16)
        for t0 in range(0, cols.stop - cols.start, NCH):
            side.run((s if after_gating else a)[:, t0:t0 + LANES])
    d = []
    for c0 in range(0, D_MODEL, chunk):
        d.append(_dot(s_s[...], wd_ref[:, c0:c0 + chunk]))
        for t0 in range(0, chunk, NCH):
            side.run(d[-1][:, t0:t0 + LANES], down_pieces)
    return jnp.concatenate(d, axis=1)


def _gate_rows(pre, gbias_ref):
    g = pre + gbias_ref[...]
    row = lax.broadcasted_iota(jnp.int32, g.shape, 0)
    return jnp.where(row < N_HEADS, g, jax.nn.log_sigmoid(g))


def _ffn_in_values(x, g1_ref, wg_ref, wu_ref, wd_ref, gpost_ref, gmix_ref, wqvo_ref, wktg_ref, wconv_ref, gbias_ref):
    h = _rms(x, g1_ref[...]).astype(BF16)
    d = _swiglu(h, wg_ref, wu_ref, wd_ref)
    x1 = x + 0.5 * _rms(d, gpost_ref[...])
    h2 = _rms(x1, gmix_ref[...]).astype(BF16)
    qvo = _dot(h2, wqvo_ref[...])
    q = qvo[:, :M_WIDTH].astype(BF16)
    v = qvo[:, M_WIDTH:2 * M_WIDTH]
    og = jax.nn.sigmoid(qvo[:, 2 * M_WIDTH:])
    ktg = lax.dot_general(wktg_ref[...], h2, NT_DIMS, preferred_element_type=F32)
    kt = ktg[:M_WIDTH] * (HEAD_DIM ** -0.5)
    cc = _dot(h2, wconv_ref[...])
    u = cc[:, :CONV_CH] * jax.nn.sigmoid(cc[:, CONV_CH:])
    return x1, q, kt.astype(BF16), v, og, u, _gate_rows(ktg[M_WIDTH:], gbias_ref)


def _ffn_out_values(ym, c, x1, wout_ref, gmp_ref, g2_ref, wg_ref, wu_ref, wd_ref, gpost_ref, gfin_ref):
    o = _dot(ym.astype(BF16), wout_ref[0:M_WIDTH, :]) + _dot(c.astype(BF16), wout_ref[M_WIDTH:, :])
    x2 = x1 + _rms(o, gmp_ref[...])
    h = _rms(x2, g2_ref[...]).astype(BF16)
    d = _swiglu(h, wg_ref, wu_ref, wd_ref)
    x3 = x2 + 0.5 * _rms(d, gpost_ref[...])
    return _rms(x3, gfin_ref[...])


def _ffn_out_phases(ym, c, x1, y_ref, rows, wout_ref, gmp_ref, g2_ref, wg_ref, wu_ref, wd_ref, gpost_ref, gfin_ref):
    o = _dot(ym.astype(BF16), wout_ref[0:M_WIDTH, :]) + _dot(c.astype(BF16), wout_ref[M_WIDTH:, :])
    yield o
    x2 = x1 + _rms(o, gmp_ref[...])
    h = _rms(x2, g2_ref[...]).astype(BF16)
    a = _dot(h, wg_ref[...])
    yield a
    b = _dot(h, wu_ref[...])
    yield b
    s = (a * jax.nn.sigmoid(a) * b).astype(BF16)
    d = _dot(s, wd_ref[...])
    yield d
    x3 = x2 + 0.5 * _rms(d, gpost_ref[...])
    y = _rms(x3, gfin_ref[...])
    y_ref[rows, :] = y
    yield y


def _conv_taps(ubuf, row0, nrows, lanes, cw_ref, cb_ref):
    acc = jnp.broadcast_to(cb_ref[:, lanes], (nrows, lanes.stop - lanes.start))
    first = HIST_PAD - HIST
    for res in range(SUBLANES):
        taps = [j for j in range(CONV_WIDTH) if (first + j) % SUBLANES == res]
        lo = (first + taps[0]) // SUBLANES * SUBLANES
        hi = (first + taps[-1]) // SUBLANES * SUBLANES
        win = ubuf[row0 + lo:row0 + hi + nrows + (SUBLANES if res else 0), lanes]
        if res:
            win = pltpu.roll(win, win.shape[0] - res, 0)
        for j in taps:
            off = first + j - res - lo
            acc = acc + win[off:off + nrows, :] * cw_ref[j:j + 1, lanes]
    return acc


def _conv_taps_interleaved(uslab, row0, nrows, lanes, cw_ref, cb_ref):
    first = HIST_PAD - HIST
    half = nrows // 2
    acc = [jnp.broadcast_to(cb_ref[:, lanes], (half, LANES))] * 2
    for j in range(CONV_WIDTH):
        for parity in range(2):
            x = uslab[pl.ds(row0 + first + j + parity, half, stride=2), :]
            acc[parity] = acc[parity] + x * cw_ref[j:j + 1, lanes]
    return acc


def _ln_swish(acc, lg_ref, lb_ref):
    mu = jnp.mean(acc, axis=-1, keepdims=True)
    xc = acc - mu
    var = jnp.mean(xc * xc, axis=-1, keepdims=True)
    y = xc * lax.rsqrt(var + EPS) * lg_ref[...] + lb_ref[...]
    return y * jax.nn.sigmoid(y)


def _mlstm_gates(b_r, a_r, caus, qh, kth):
    L = b_r.shape[1]
    arow = jnp.broadcast_to(a_r, (L, L))
    bcol = jnp.broadcast_to(b_r, (L, L)).T
    acol = arow.T
    dmat = jnp.where(caus, bcol + arow, -jnp.inf)
    return dict(bcol=bcol, acol=acol, dmat=dmat, rowmax=jnp.max(dmat, axis=1, keepdims=True), s_raw=_dot(qh, kth))


def _mlstm_scores(st, mprev, vh):
    L = st["bcol"].shape[0]
    m_t = jnp.maximum(st["bcol"] + mprev, st["rowmax"])
    s = st["s_raw"] * jnp.exp(st["dmat"] - m_t)
    vaug = jnp.concatenate([vh, jnp.ones_like(vh)], axis=1)
    m_new = m_t[L - 1:L, :]
    b_last = st["bcol"][L - 1:L, :]
    g_rows = jnp.exp(b_last + st["acol"] - m_new)
    new = dict(m_t=m_t, sv=_dot(s.astype(BF16), vaug.astype(BF16)), w_inter=jnp.exp(st["bcol"] + mprev - m_t),
               g_state=jnp.exp(b_last + mprev - m_new),
               gv=(jnp.concatenate([g_rows, g_rows], axis=1) * vaug).astype(BF16))
    st.clear()
    st.update(new)
    return m_new


def _mlstm_output(st, qh, kth, caug):
    m_t, sv, w_inter, g_state = st["m_t"], st["sv"], st["w_inter"], st["g_state"]
    qc = _dot(qh, caug.astype(BF16))
    caug_new = jnp.concatenate([g_state, g_state], axis=1) * caug + _dot(kth, st["gv"])
    num = w_inter * qc[:, :HEAD_DIM] + sv[:, :HEAD_DIM]
    den = w_inter * qc[:, HEAD_DIM:] + sv[:, HEAD_DIM:]
    hh = num / jnp.maximum(jnp.abs(den), jnp.exp(-m_t))
    return hh, caug_new


def _ffn_in_mlstm_body(x_ref, g1_ref, wg_ref, wu_ref, wd_ref, gpost_ref, gmix_ref, wqvo_ref, wktg_ref, wconv_ref,
                       gbias_ref, x1_ref, u_ref, ym_ref, caug_out, m_out,
                       q_st, kt_st, v_st, og_st, g_st, caug_s, m_s, *, tm, nt):
    g = pl.program_id(0)
    wslot = lax.rem(g, 2)
    rslot = 1 - wslot
    L = CHUNK

    @pl.when(g == 0)
    def _init():
        for st in (q_st, kt_st, v_st, og_st, g_st, caug_s, m_s):
            st[...] = jnp.zeros_like(st)

    def mlstm_pieces(side):
        fresh = lax.rem(g - 1, nt) == 0
        r = lax.broadcasted_iota(jnp.int32, (L, L), 0)
        c = lax.broadcasted_iota(jnp.int32, (L, L), 1)
        caus = c <= r
        triu = (r <= c).astype(BF16)
        nck = tm // L
        stages = {}

        def stage1(ck):
            rows, slot = slice(ck * L, (ck + 1) * L), rslot + side.zero
            gt = g_st[slot, :, rows]
            hi = gt.astype(BF16)
            mid = (gt - hi.astype(F32)).astype(BF16)
            lo = (gt - hi.astype(F32) - mid.astype(F32)).astype(BF16)
            parts = _dot(jnp.concatenate([hi, mid, lo], axis=0), triu)
            b_rows = parts[0:2 * N_HEADS] + parts[2 * N_HEADS:4 * N_HEADS] + parts[4 * N_HEADS:]
            for h in range(N_HEADS):
                hs = slice(h * HEAD_DIM, (h + 1) * HEAD_DIM)
                b_r = b_rows[N_HEADS + h:N_HEADS + h + 1, :]
                stages[ck, h] = _mlstm_gates(b_r, gt[h:h + 1, :] - b_r, caus, q_st[slot, rows, hs],
                                             kt_st[slot, hs, rows])

        def stage2(ck):
            rows, slot = slice(ck * L, (ck + 1) * L), rslot + side.zero
            for h in range(N_HEADS):
                mprev = m_s[h + side.zero]
                if ck == 0:
                    mprev = jnp.where(fresh, 0.0, mprev)
                m_new = _mlstm_scores(stages[ck, h], mprev, v_st[slot, rows, slice(h * HEAD_DIM, (h + 1) * HEAD_DIM)])
                m_s[h] = m_new
                if ck == nck - 1:
                    m_out[0, h] = m_new

        def stage3(ck):
            rows, slot = slice(ck * L, (ck + 1) * L), rslot + side.zero
            for h in range(N_HEADS):
                hs = slice(h * HEAD_DIM, (h + 1) * HEAD_DIM)
                caug = caug_s[h + side.zero]
                if ck == 0:
                    caug = jnp.where(fresh, 0.0, caug)
                hh, caug = _mlstm_output(stages.pop((ck, h)), q_st[slot, rows, hs], kt_st[slot, hs, rows], caug)
                ym_ref[rows, hs] = og_st[slot, rows, hs] * hh
                caug_s[h] = caug
                if ck == nck - 1:
                    caug_out[0, h] = caug

        for t in range(nck + 2):
            for lag, stage in enumerate((stage1, stage2, stage3)):
                if 0 <= t - lag < nck:
                    stage(t - lag)
            yield

    side = _Side(mlstm_pieces, pin=PIN_MLSTM)

    blocks = [_ffn_in_block(x_ref, slice(r0, r0 + FFN_ROWS), wslot, g1_ref, wg_ref, wu_ref, wd_ref, gpost_ref,
                            gmix_ref, wqvo_ref, wktg_ref, wconv_ref, gbias_ref, x1_ref, u_ref, q_st, kt_st, v_st,
                            og_st, g_st) for r0 in range(0, tm, FFN_ROWS)]
    side.run(None, 1)
    last = None
    while blocks:
        for block in list(blocks):
            token = next(block, None)
            if token is None:
                blocks.remove(block)
            else:
                side.run(token, 1)
                last = token
    side.run(last, None)


def _ffn_in_block(x_ref, rows, wslot, g1_ref, wg_ref, wu_ref, wd_ref, gpost_ref, gmix_ref, wqvo_ref, wktg_ref, wconv_ref,
                  gbias_ref, x1_ref, u_ref, q_st, kt_st, v_st, og_st, g_st):
    x = x_ref[rows, :]
    h = _rms(x, g1_ref[...]).astype(BF16)
    a = _dot(h, wg_ref[...])
    yield a
    b = _dot(h, wu_ref[...])
    yield b
    s = (a * jax.nn.sigmoid(a) * b).astype(BF16)
    d = _dot(s, wd_ref[...])
    yield d
    x1 = x + 0.5 * _rms(d, gpost_ref[...])
    x1_ref[rows, :] = x1
    h2 = _rms(x1, gmix_ref[...]).astype(BF16)
    qvo = _dot(h2, wqvo_ref[...])
    q_st[wslot, rows, :] = qvo[:, :M_WIDTH].astype(BF16)
    v_st[wslot, rows, :] = qvo[:, M_WIDTH:2 * M_WIDTH]
    og_st[wslot, rows, :] = jax.nn.sigmoid(qvo[:, 2 * M_WIDTH:])
    yield qvo
    ktg = lax.dot_general(wktg_ref[...], h2, NT_DIMS, preferred_element_type=F32)
    kt_st[wslot, :, rows] = (ktg[:M_WIDTH] * (HEAD_DIM ** -0.5)).astype(BF16)
    g_st[wslot, :, rows] = _gate_rows(ktg[M_WIDTH:], gbias_ref)
    yield ktg
    cc = _dot(h2, wconv_ref[...])
    u_ref[rows, :] = cc[:, :CONV_CH] * jax.nn.sigmoid(cc[:, CONV_CH:])
    yield cc


def _conv_ffn_out_body(u_ref, ym_ref, x1_ref, cw_ref, cb_ref, lg_ref, lb_ref, wout_ref, gmp_ref, g2_ref,
                       wg_ref, wu_ref, wd_ref, gpost_ref, gfin_ref, y_ref, hist_out, ubuf, cbuf, *, tm, nt):
    nslab = CONV_CH // LANES

    @pl.when(lax.rem(pl.program_id(0), nt) == 0)
    def _new_sequence():
        ubuf[:, 0:HIST_PAD, :] = jnp.zeros((nslab, HIST_PAD, LANES), F32)

    for lb in range(nslab):
        ubuf[lb, HIST_PAD:HIST_PAD + tm, :] = u_ref[:, lb * LANES:(lb + 1) * LANES]
    for k in range(tm // CONV_ROWS):
        for lb in range(nslab):
            acc = _conv_taps_interleaved(ubuf.at[lb], k * CONV_ROWS, CONV_ROWS, slice(lb * LANES, (lb + 1) * LANES),
                                         cw_ref, cb_ref)
            for parity in range(2):
                cbuf[lb, pl.ds(k * CONV_ROWS + parity, CONV_ROWS // 2, stride=2), :] = acc[parity]
    tail = jnp.concatenate([ubuf[lb, tm:tm + HIST_PAD, :] for lb in range(nslab)], axis=1)
    hist_out[0] = tail
    for lb in range(nslab):
        ubuf[lb, 0:HIST_PAD, :] = tail[:, lb * LANES:(lb + 1) * LANES]

    halves = []
    for r0 in range(0, tm, FFN_ROWS):
        c = []
        for k in range(FFN_ROWS // LN_ROWS):
            rows = slice(r0 + k * LN_ROWS, r0 + (k + 1) * LN_ROWS)
            c.append(_ln_swish(jnp.concatenate([cbuf[lb, rows, :] for lb in range(nslab)], axis=1), lg_ref, lb_ref))
        rows = slice(r0, r0 + FFN_ROWS)
        halves.append(_ffn_out_phases(ym_ref[rows, :], jnp.concatenate(c, axis=0), x1_ref[rows, :], y_ref, rows,
                                      wout_ref, gmp_ref, g2_ref, wg_ref, wu_ref, wd_ref, gpost_ref, gfin_ref))
    for _ in zip(*halves):
        pass


def _const_spec(shape):
    nd = len(shape)
    return pl.BlockSpec(shape, lambda *_: (0,) * nd, pipeline_mode=pl.Buffered(1))


def _ffn_in_consts(p, tm):
    gbias = jnp.broadcast_to(p["gbias"][:, None], (2 * N_HEADS, tm))
    return [p["ffn1_pre_g"], p["ffn1_wg"], p["ffn1_wu"], p["ffn1_wd"], p["ffn1_post_g"], p["mix_pre_g"],
            p["w_qvo"], p["w_ktg"], p["w_conv"], gbias]


def _ffn_out_consts(p):
    return [p["w_out"], p["mix_post_g"], p["ffn2_pre_g"], p["ffn2_wg"], p["ffn2_wu"], p["ffn2_wd"],
            p["ffn2_post_g"], p["final_g"]]


def _ffn_in_mlstm(x2d, p, batch, seq, tm):
    m = batch * seq
    nt = seq // tm
    ntiles = m // tm
    cur = lambda w: pl.BlockSpec((tm, w), lambda g: (jnp.minimum(g, ntiles - 1), 0))
    prev = lambda w: pl.BlockSpec((tm, w), lambda g: (jnp.maximum(g - 1, 0), 0))
    prev_seq = lambda g: jnp.maximum(g - 1, 0) // nt
    consts = _ffn_in_consts(p, FFN_ROWS)
    return pl.pallas_call(
        functools.partial(_ffn_in_mlstm_body, tm=tm, nt=nt),
        grid=(ntiles + 1,),
        in_specs=[cur(D_MODEL)] + [_const_spec(c.shape) for c in consts],
        out_specs=[cur(D_MODEL), cur(CONV_CH), prev(M_WIDTH),
                   pl.BlockSpec((1, N_HEADS, HEAD_DIM, AUG), lambda g: (prev_seq(g), 0, 0, 0)),
                   pl.BlockSpec((1, N_HEADS, 1, HEAD_DIM), lambda g: (prev_seq(g), 0, 0, 0))],
        out_shape=[jax.ShapeDtypeStruct((m, D_MODEL), F32), jax.ShapeDtypeStruct((m, CONV_CH), F32),
                   jax.ShapeDtypeStruct((m, M_WIDTH), F32),
                   jax.ShapeDtypeStruct((batch, N_HEADS, HEAD_DIM, AUG), F32),
                   jax.ShapeDtypeStruct((batch, N_HEADS, 1, HEAD_DIM), F32)],
        scratch_shapes=[pltpu.VMEM((2, tm, M_WIDTH), BF16), pltpu.VMEM((2, M_WIDTH, tm), BF16),
                        pltpu.VMEM((2, tm, M_WIDTH), F32), pltpu.VMEM((2, tm, M_WIDTH), F32),
                        pltpu.VMEM((2, 2 * N_HEADS, tm), F32),
                        pltpu.VMEM((N_HEADS, HEAD_DIM, AUG), F32), pltpu.VMEM((N_HEADS, 1, HEAD_DIM), F32)],
        compiler_params=pltpu.CompilerParams(dimension_semantics=("arbitrary",),
                                             vmem_limit_bytes=VMEM_LIMIT_TOKENWISE),
        name="ffn_in_mlstm",
    )(x2d, *consts)


def _conv_ffn_out(u, ym, x1, p, batch, seq, tm):
    m = batch * seq
    nt = seq // tm
    row = lambda w: pl.BlockSpec((tm, w), lambda g: (g, 0))
    consts = [p["conv_w"], p["conv_b"], p["conv_ln_g"], p["conv_ln_b"]] + _ffn_out_consts(p)
    return pl.pallas_call(
        functools.partial(_conv_ffn_out_body, tm=tm, nt=nt),
        grid=(m // tm,),
        in_specs=[row(CONV_CH), row(M_WIDTH), row(D_MODEL)] + [_const_spec(c.shape) for c in consts],
        out_specs=[row(D_MODEL), pl.BlockSpec((1, HIST_PAD, CONV_CH), lambda g: (g // nt, 0, 0))],
        out_shape=[jax.ShapeDtypeStruct((m, D_MODEL), F32), jax.ShapeDtypeStruct((batch, HIST_PAD, CONV_CH), F32)],
        scratch_shapes=[pltpu.VMEM((CONV_CH // LANES, HIST_PAD + tm, LANES), F32),
                        pltpu.VMEM((CONV_CH // LANES, tm, LANES), F32)],
        compiler_params=pltpu.CompilerParams(dimension_semantics=("arbitrary",),
                                             vmem_limit_bytes=VMEM_LIMIT_TOKENWISE),
        name="conv_ffn_out",
    )(u, ym, x1, *consts)


def _ffn_in_body(x_ref, g1_ref, wg_ref, wu_ref, wd_ref, gpost_ref, gmix_ref, wqvo_ref, wktg_ref, wconv_ref,
                 gbias_ref, x1_ref, q_ref, kt_ref, v_ref, og_ref, u_ref, gt_ref):
    outs = _ffn_in_values(x_ref[...], g1_ref, wg_ref, wu_ref, wd_ref, gpost_ref, gmix_ref, wqvo_ref, wktg_ref,
                          wconv_ref, gbias_ref)
    for ref, val in zip((x1_ref, q_ref, kt_ref, v_ref, og_ref, u_ref, gt_ref), outs):
        ref[...] = val


def _ffn_out_body(mix_ref, x1_ref, wout_ref, gmp_ref, g2_ref, wg_ref, wu_ref, wd_ref, gpost_ref, gfin_ref, y_ref):
    y_ref[...] = _ffn_out_values(mix_ref[:, :M_WIDTH], mix_ref[:, M_WIDTH:], x1_ref[...], wout_ref, gmp_ref, g2_ref,
                                 wg_ref, wu_ref, wd_ref, gpost_ref, gfin_ref)


def _mixer_sample_body(q_ref, kt_ref, v_ref, og_ref, u_ref, gt_ref, caug0_ref, m0_ref, hist0_ref,
                       cw_ref, cb_ref, lg_ref, lb_ref, mix_ref, caug_out, m_out, hist_out, ubuf, *, nseq, L):
    R = nseq * L
    r = lax.broadcasted_iota(jnp.int32, (R, R), 0)
    c = lax.broadcasted_iota(jnp.int32, (R, R), 1)
    same = lax.div(r, L) == lax.div(c, L)
    caus = same & (c <= r)
    segtriu = jnp.where(same & (r <= c), 1.0, 0.0).astype(F32)
    segones = jnp.where(same, 1.0, 0.0).astype(F32)
    rowseq = lax.div(lax.broadcasted_iota(jnp.int32, (R, AUG), 0), L)
    ones = jnp.ones((R, HEAD_DIM), F32)
    gt = gt_ref[...]
    b_rows = jnp.dot(gt, segtriu, precision=HIGHEST, preferred_element_type=F32)
    tot_rows = jnp.dot(gt, segones, precision=HIGHEST, preferred_element_type=F32)
    for h in range(N_HEADS):
        hs = slice(h * HEAD_DIM, (h + 1) * HEAD_DIM)
        b_r = b_rows[N_HEADS + h:N_HEADS + h + 1, :]
        a_r = gt[h:h + 1, :] - b_r
        t_r = tot_rows[N_HEADS + h:N_HEADS + h + 1, :]
        arow = jnp.broadcast_to(a_r, (R, R))
        bcol = jnp.broadcast_to(b_r, (R, R)).T
        acol = arow.T
        tcol = jnp.broadcast_to(t_r, (R, R)).T
        mprev = m0_ref[h]
        mprev = jnp.concatenate([mprev, mprev], axis=1)
        dmat = jnp.where(caus, bcol + arow, -jnp.inf)
        m_t = jnp.maximum(bcol + mprev, jnp.max(dmat, axis=1, keepdims=True))
        dend = jnp.where(same, tcol + arow, -jnp.inf)
        m_new = jnp.maximum(tcol + mprev, jnp.max(dend, axis=1, keepdims=True))
        qh = q_ref[:, hs]
        kth = kt_ref[hs, :]
        s = _dot(qh, kth) * jnp.exp(dmat - m_t)
        vaug = jnp.concatenate([v_ref[:, hs], ones], axis=1)
        w_inter = jnp.exp(bcol + mprev - m_t)
        qc = jnp.zeros((R, AUG), F32)
        for i in range(nseq):
            qc = jnp.where(rowseq == i, _dot(qh, caug0_ref[i, h].astype(BF16)), qc)
        sv = _dot(s.astype(BF16), vaug.astype(BF16))
        num = w_inter[:, :HEAD_DIM] * qc[:, :HEAD_DIM] + sv[:, :HEAD_DIM]
        den = w_inter[:, :HEAD_DIM] * qc[:, HEAD_DIM:] + sv[:, HEAD_DIM:]
        hh = num / jnp.maximum(jnp.abs(den), jnp.exp(-m_t[:, :HEAD_DIM]))
        mix_ref[:, hs] = og_ref[:, hs] * hh
        g_state = jnp.exp(tcol + mprev - m_new)
        g_rows = jnp.exp(tcol + acol - m_new)
        gv = g_rows * vaug
        for i in range(nseq):
            gvi = jnp.where(rowseq == i, gv, 0.0).astype(BF16)
            caug_out[i, h] = g_state[i * L:i * L + 1, :] * caug0_ref[i, h] + _dot(kth, gvi)
            m_out[i, h] = m_new[i * L:i * L + 1, :HEAD_DIM]

    for i in range(nseq):
        ubuf[0:HIST_PAD, :] = hist0_ref[i]
        ubuf[HIST_PAD:HIST_PAD + L, :] = u_ref[i * L:(i + 1) * L, :]
        acc = _conv_taps(ubuf, 0, L, slice(0, CONV_CH), cw_ref, cb_ref)
        mix_ref[i * L:(i + 1) * L, M_WIDTH:] = _ln_swish(acc, lg_ref, lb_ref)
        hist_out[i] = ubuf[L:L + HIST_PAD, :]


def _ffn_in(x2d, p, tm):
    m = x2d.shape[0]
    row = lambda w: pl.BlockSpec((tm, w), lambda i: (i, 0))
    col = lambda h: pl.BlockSpec((h, tm), lambda i: (0, i))
    consts = _ffn_in_consts(p, tm)
    return pl.pallas_call(
        _ffn_in_body,
        grid=(m // tm,),
        in_specs=[row(D_MODEL)] + [_const_spec(c.shape) for c in consts],
        out_specs=[row(D_MODEL), row(M_WIDTH), col(M_WIDTH), row(M_WIDTH), row(M_WIDTH), row(CONV_CH),
                   col(2 * N_HEADS)],
        out_shape=[jax.ShapeDtypeStruct((m, D_MODEL), F32), jax.ShapeDtypeStruct((m, M_WIDTH), BF16),
                   jax.ShapeDtypeStruct((M_WIDTH, m), BF16), jax.ShapeDtypeStruct((m, M_WIDTH), F32),
                   jax.ShapeDtypeStruct((m, M_WIDTH), F32), jax.ShapeDtypeStruct((m, CONV_CH), F32),
                   jax.ShapeDtypeStruct((2 * N_HEADS, m), F32)],
        compiler_params=pltpu.CompilerParams(dimension_semantics=("arbitrary",),
                                             vmem_limit_bytes=VMEM_LIMIT_TOKENWISE),
        name="ffn_in",
    )(x2d, *consts)


def _ffn_out(mix, x1, p, tm):
    m = mix.shape[0]
    row = pl.BlockSpec((tm, D_MODEL), lambda i: (i, 0))
    consts = _ffn_out_consts(p)
    return pl.pallas_call(
        _ffn_out_body,
        grid=(m // tm,),
        in_specs=[row, row] + [_const_spec(c.shape) for c in consts],
        out_specs=row,
        out_shape=jax.ShapeDtypeStruct((m, D_MODEL), F32),
        compiler_params=pltpu.CompilerParams(dimension_semantics=("arbitrary",),
                                             vmem_limit_bytes=VMEM_LIMIT_TOKENWISE),
        name="ffn_out",
    )(mix, x1, *consts)


def _mixer_sample(q, kt, v, og, u, gt, caug0, m0, hist0, p, nseq, L):
    rows = nseq * L
    args = [q, kt, v, og, u, gt, caug0, m0, hist0, p["conv_w"], p["conv_b"], p["conv_ln_g"], p["conv_ln_b"]]
    full = lambda a: pl.BlockSpec(a.shape, lambda i, nd=a.ndim: (0,) * nd)
    out_shape = [jax.ShapeDtypeStruct((rows, D_MODEL), F32),
                 jax.ShapeDtypeStruct((nseq, N_HEADS, HEAD_DIM, AUG), F32),
                 jax.ShapeDtypeStruct((nseq, N_HEADS, 1, HEAD_DIM), F32),
                 jax.ShapeDtypeStruct((nseq, HIST_PAD, CONV_CH), F32)]
    return pl.pallas_call(
        functools.partial(_mixer_sample_body, nseq=nseq, L=L),
        grid=(1,),
        in_specs=[full(a) for a in args],
        out_specs=[full(s) for s in out_shape],
        out_shape=out_shape,
        scratch_shapes=[pltpu.VMEM((HIST_PAD + L, CONV_CH), F32)],
        compiler_params=pltpu.CompilerParams(dimension_semantics=("arbitrary",),
                                             vmem_limit_bytes=VMEM_LIMIT_MIXER),
        name="mixer_sample",
    )(*args)


def _layer_params(l, ffn1_pre_g, ffn1_wg, ffn1_wu, ffn1_wd, ffn1_post_g, mix_pre_g, w_in, b_igate, b_fgate,
                  conv_w, conv_b, conv_ln_g, conv_ln_b, w_out, mix_post_g, ffn2_pre_g, ffn2_wg, ffn2_wu, ffn2_wd,
                  ffn2_post_g, final_g):
    vec = lambda a: a[l].astype(F32).reshape(1, -1)
    w = w_in[l]
    cuts = [0, M_WIDTH, 2 * M_WIDTH, 3 * M_WIDTH, 4 * M_WIDTH, 4 * M_WIDTH + N_HEADS, 4 * M_WIDTH + 2 * N_HEADS,
            4 * M_WIDTH + 2 * N_HEADS + CONV_CH, 4 * M_WIDTH + 2 * N_HEADS + 2 * CONV_CH]
    wq, wk, wv, wo, wi, wf, wcv, wcg = [w[:, a:b] for a, b in zip(cuts[:-1], cuts[1:])]
    return {
        "ffn1_pre_g": vec(ffn1_pre_g), "ffn1_post_g": vec(ffn1_post_g), "mix_pre_g": vec(mix_pre_g),
        "mix_post_g": vec(mix_post_g), "ffn2_pre_g": vec(ffn2_pre_g), "ffn2_post_g": vec(ffn2_post_g),
        "final_g": vec(final_g),
        "ffn1_wg": ffn1_wg[l].astype(BF16), "ffn1_wu": ffn1_wu[l].astype(BF16), "ffn1_wd": ffn1_wd[l].astype(BF16),
        "ffn2_wg": ffn2_wg[l].astype(BF16), "ffn2_wu": ffn2_wu[l].astype(BF16), "ffn2_wd": ffn2_wd[l].astype(BF16),
        "w_qvo": jnp.concatenate([wq, wv, wo], axis=1).astype(BF16),
        "w_ktg": jnp.concatenate([wk, wi, wf], axis=1).T.astype(BF16),
        "w_conv": jnp.concatenate([wcv, wcg], axis=1).astype(BF16),
        "gbias": jnp.concatenate([b_igate[l], b_fgate[l]]).astype(F32),
        "w_out": w_out[l].astype(BF16),
        "conv_w": conv_w[l].astype(F32), "conv_b": vec(conv_b), "conv_ln_g": vec(conv_ln_g),
        "conv_ln_b": vec(conv_ln_b),
    }


def _split_state(caug, m, hist):
    return caug[..., :HEAD_DIM], caug[..., HEAD_DIM], m[:, :, 0, 0], hist[:, HIST_PAD - HIST:, :]


def kernel(x_prompt, x_sample, state_mlstm_C, state_mlstm_n, state_mlstm_m, cache_conv, ffn1_pre_g, ffn1_wg,
           ffn1_wu, ffn1_wd, ffn1_post_g, mix_pre_g, w_in, b_igate, b_fgate, conv_w, conv_b, conv_ln_g, conv_ln_b,
           w_out, mix_post_g, ffn2_pre_g, ffn2_wg, ffn2_wu, ffn2_wd, ffn2_post_g, final_g):
    batch, seq, _ = x_prompt.shape
    nseq, dseq, _ = x_sample.shape
    depth = w_in.shape[0]
    assert seq % PROMPT_TILE == 0 and PROMPT_TILE % CHUNK == 0 and PROMPT_TILE % CONV_ROWS == 0
    assert (nseq * dseq) % TOKEN_TILE == 0 and dseq <= HIST_PAD
    yp = x_prompt.reshape(batch * seq, D_MODEL)
    ys = x_sample.reshape(nseq * dseq, D_MODEL)
    outs_p, outs_s = [], []
    for l in range(depth):
        p = _layer_params(l, ffn1_pre_g, ffn1_wg, ffn1_wu, ffn1_wd, ffn1_post_g, mix_pre_g, w_in, b_igate, b_fgate,
                          conv_w, conv_b, conv_ln_g, conv_ln_b, w_out, mix_post_g, ffn2_pre_g, ffn2_wg, ffn2_wu,
                          ffn2_wd, ffn2_post_g, final_g)
        x1, u, ym, caug, m = _ffn_in_mlstm(yp, p, batch, seq, 2 * PROMPT_TILE)
        yp, hist = _conv_ffn_out(u, ym, x1, p, batch, seq, 2 * PROMPT_TILE)
        outs_p.append(_split_state(caug, m, hist))
        x1, q, kt, v, og, u, gt = _ffn_in(ys, p, TOKEN_TILE)
        n0 = jnp.broadcast_to(state_mlstm_n[l].astype(F32)[..., None], (nseq, N_HEADS, HEAD_DIM, HEAD_DIM))
        caug0 = jnp.concatenate([state_mlstm_C[l].astype(F32), n0], axis=-1)
        m0 = jnp.broadcast_to(state_mlstm_m[l].astype(F32).T[:, :, None, None], (N_HEADS, nseq, dseq, HEAD_DIM))
        m0 = m0.reshape(N_HEADS, nseq * dseq, HEAD_DIM)
        hist0 = jnp.pad(cache_conv[l].astype(F32), ((0, 0), (HIST_PAD - HIST, 0), (0, 0)))
        mix, caug, m, hist = _mixer_sample(q, kt, v, og, u, gt, caug0, m0, hist0, p, nseq, dseq)
        ys = _ffn_out(mix, x1, p, TOKEN_TILE)
        outs_s.append(_split_state(caug, m, hist))
    stack = lambda outs, k: jnp.stack([o[k] for o in outs])
    return (yp.reshape(batch, seq, D_MODEL), ys.reshape(nseq, dseq, D_MODEL),
            stack(outs_p, 0), stack(outs_p, 1), stack(outs_p, 2), stack(outs_p, 3),
            stack(outs_s, 0), stack(outs_s, 1), stack(outs_s, 2), stack(outs_s, 3))
```

```python
import functools

import jax
import jax.numpy as jnp
from jax import lax
from jax.experimental import pallas as pl
from jax.experimental.pallas import tpu as pltpu

D_MODEL = 1024
D_FF = 2816
N_HEADS = 4
HEAD_DIM = 128
M_WIDTH = N_HEADS * HEAD_DIM
CONV_CH = 512
CONV_WIDTH = 31
HIST = CONV_WIDTH - 1
HIST_PAD = 32
SUBLANES = 8
EPS = 1e-6
CHUNK = 128
AUG = 2 * HEAD_DIM
CONV_ROWS = 128
LANES = 128
LN_ROWS = 32
NCH = 256

F32 = jnp.float32
BF16 = jnp.bfloat16
HIGHEST = lax.Precision.HIGHEST
NT_DIMS = (((1,), (1,)), ((), ()))

TOKEN_TILE = 256
PROMPT_TILE = 256
FFN_ROWS = 256
PIN_MLSTM = False
VMEM_LIMIT_TOKENWISE = 56 * 1024 * 1024
VMEM_LIMIT_MIXER = 40 * 1024 * 1024


def _rms(x, g):
    return x * lax.rsqrt(jnp.mean(x * x, axis=-1, keepdims=True) + EPS) * g


def _dot(a, b):
    return jnp.dot(a, b, preferred_element_type=F32)


def _swiglu(h, wg_ref, wu_ref, wd_ref):
    a = _dot(h, wg_ref[...])
    b = _dot(h, wu_ref[...])
    s = (a * jax.nn.sigmoid(a) * b).astype(BF16)
    return _dot(s, wd_ref[...])


class _Side:
    def __init__(self, make_pieces, pin=True):
        self.zero = 0
        self.pin = pin
        self.pieces = make_pieces(self)

    def run(self, after, n=1):
        if self.pin:
            bits = pltpu.bitcast(after[after.shape[0] - SUBLANES:, 0:LANES], jnp.int32)
            self.zero = lax.shift_right_logical(lax.shift_right_logical(bits, 16), 16)[0, 0]
        if n is None:
            for _ in self.pieces:
                pass
        else:
            for _ in range(n):
                next(self.pieces, None)


def _swiglu_chunked(h_s, s_s, wg_ref, wu_ref, wd_ref, side, chunk, after_gating, down_pieces=1):
    for c0 in range(0, D_FF, chunk):
        cols = slice(c0, min(c0 + chunk, D_FF))
        a = _dot(h_s[...], wg_ref[:, cols])
        b = _dot(h_s[...], wu_ref[:, cols])
        s = a * jax.nn.sigmoid(a) * b
        s_s[:, cols] = s.astype(BF16)
        for t0 in range(0, cols.stop - cols.start, NCH):
            side.run((s if after_gating else a)[:, t0:t0 + LANES])
    d = []
    for c0 in range(0, D_MODEL, chunk):
        d.append(_dot(s_s[...], wd_ref[:, c0:c0 + chunk]))
        for t0 in range(0, chunk, NCH):
            side.run(d[-1][:, t0:t0 + LANES], down_pieces)
    return jnp.concatenate(d, axis=1)


def _gate_rows(pre, gbias_ref):
    g = pre + gbias_ref[...]
    row = lax.broadcasted_iota(jnp.int32, g.shape, 0)
    return jnp.where(row < N_HEADS, g, jax.nn.log_sigmoid(g))


def _ffn_in_values(x, g1_ref, wg_ref, wu_ref, wd_ref, gpost_ref, gmix_ref, wqvo_ref, wktg_ref, wconv_ref, gbias_ref):
    h = _rms(x, g1_ref[...]).astype(BF16)
    d = _swiglu(h, wg_ref, wu_ref, wd_ref)
    x1 = x + 0.5 * _rms(d, gpost_ref[...])
    h2 = _rms(x1, gmix_ref[...]).astype(BF16)
    qvo = _dot(h2, wqvo_ref[...])
    q = qvo[:, :M_WIDTH].astype(BF16)
    v = qvo[:, M_WIDTH:2 * M_WIDTH]
    og = jax.nn.sigmoid(qvo[:, 2 * M_WIDTH:])
    ktg = lax.dot_general(wktg_ref[...], h2, NT_DIMS, preferred_element_type=F32)
    kt = ktg[:M_WIDTH] * (HEAD_DIM ** -0.5)
    cc = _dot(h2, wconv_ref[...])
    u = cc[:, :CONV_CH] * jax.nn.sigmoid(cc[:, CONV_CH:])
    return x1, q, kt.astype(BF16), v, og, u, _gate_rows(ktg[M_WIDTH:], gbias_ref)


def _ffn_out_values(ym, c, x1, wout_ref, gmp_ref, g2_ref, wg_ref, wu_ref, wd_ref, gpost_ref, gfin_ref):
    o = _dot(ym.astype(BF16), wout_ref[0:M_WIDTH, :]) + _dot(c.astype(BF16), wout_ref[M_WIDTH:, :])
    x2 = x1 + _rms(o, gmp_ref[...])
    h = _rms(x2, g2_ref[...]).astype(BF16)
    d = _swiglu(h, wg_ref, wu_ref, wd_ref)
    x3 = x2 + 0.5 * _rms(d, gpost_ref[...])
    return _rms(x3, gfin_ref[...])


def _ffn_out_phases(ym, c, x1, y_ref, rows, wout_ref, gmp_ref, g2_ref, wg_ref, wu_ref, wd_ref, gpost_ref, gfin_ref):
    o = _dot(ym.astype(BF16), wout_ref[0:M_WIDTH, :]) + _dot(c.astype(BF16), wout_ref[M_WIDTH:, :])
    yield o
    x2 = x1 + _rms(o, gmp_ref[...])
    h = _rms(x2, g2_ref[...]).astype(BF16)
    a = _dot(h, wg_ref[...])
    yield a
    b = _dot(h, wu_ref[...])
    yield b
    s = (a * jax.nn.sigmoid(a) * b).astype(BF16)
    d = _dot(s, wd_ref[...])
    yield d
    x3 = x2 + 0.5 * _rms(d, gpost_ref[...])
    y = _rms(x3, gfin_ref[...])
    y_ref[rows, :] = y
    yield y


def _conv_taps(ubuf, row0, nrows, lanes, cw_ref, cb_ref):
    acc = jnp.broadcast_to(cb_ref[:, lanes], (nrows, lanes.stop - lanes.start))
    first = HIST_PAD - HIST
    for res in range(SUBLANES):
        taps = [j for j in range(CONV_WIDTH) if (first + j) % SUBLANES == res]
        lo = (first + taps[0]) // SUBLANES * SUBLANES
        hi = (first + taps[-1]) // SUBLANES * SUBLANES
        win = ubuf[row0 + lo:row0 + hi + nrows + (SUBLANES if res else 0), lanes]
        if res:
            win = pltpu.roll(win, win.shape[0] - res, 0)
        for j in taps:
            off = first + j - res - lo
            acc = acc + win[off:off + nrows, :] * cw_ref[j:j + 1, lanes]
    return acc


def _conv_taps_interleaved(uslab, row0, nrows, lanes, cw_ref, cb_ref):
    first = HIST_PAD - HIST
    half = nrows // 2
    acc = [jnp.broadcast_to(cb_ref[:, lanes], (half, LANES))] * 2
    for j in range(CONV_WIDTH):
        for parity in range(2):
            x = uslab[pl.ds(row0 + first + j + parity, half, stride=2), :]
            acc[parity] = acc[parity] + x * cw_ref[j:j + 1, lanes]
    return acc


def _ln_swish(acc, lg_ref, lb_ref):
    mu = jnp.mean(acc, axis=-1, keepdims=True)
    xc = acc - mu
    var = jnp.mean(xc * xc, axis=-1, keepdims=True)
    y = xc * lax.rsqrt(var + EPS) * lg_ref[...] + lb_ref[...]
    return y * jax.nn.sigmoid(y)


def _mlstm_gates(b_r, a_r, caus, qh, kth):
    L = b_r.shape[1]
    arow = jnp.broadcast_to(a_r, (L, L))
    bcol = jnp.broadcast_to(b_r, (L, L)).T
    acol = arow.T
    dmat = jnp.where(caus, bcol + arow, -jnp.inf)
    return dict(bcol=bcol, acol=acol, dmat=dmat, rowmax=jnp.max(dmat, axis=1, keepdims=True), s_raw=_dot(qh, kth))


def _mlstm_scores(st, mprev, vh):
    L = st["bcol"].shape[0]
    m_t = jnp.maximum(st["bcol"] + mprev, st["rowmax"])
    s = st["s_raw"] * jnp.exp(st["dmat"] - m_t)
    vaug = jnp.concatenate([vh, jnp.ones_like(vh)], axis=1)
    m_new = m_t[L - 1:L, :]
    b_last = st["bcol"][L - 1:L, :]
    g_rows = jnp.exp(b_last + st["acol"] - m_new)
    new = dict(m_t=m_t, sv=_dot(s.astype(BF16), vaug.astype(BF16)), w_inter=jnp.exp(st["bcol"] + mprev - m_t),
               g_state=jnp.exp(b_last + mprev - m_new),
               gv=(jnp.concatenate([g_rows, g_rows], axis=1) * vaug).astype(BF16))
    st.clear()
    st.update(new)
    return m_new


def _mlstm_output(st, qh, kth, caug):
    m_t, sv, w_inter, g_state = st["m_t"], st["sv"], st["w_inter"], st["g_state"]
    qc = _dot(qh, caug.astype(BF16))
    caug_new = jnp.concatenate([g_state, g_state], axis=1) * caug + _dot(kth, st["gv"])
    num = w_inter * qc[:, :HEAD_DIM] + sv[:, :HEAD_DIM]
    den = w_inter * qc[:, HEAD_DIM:] + sv[:, HEAD_DIM:]
    hh = num / jnp.maximum(jnp.abs(den), jnp.exp(-m_t))
    return hh, caug_new


def _store_state(c_out, n_out, i, h, caug):
    c_out[i, h] = caug[:, :HEAD_DIM]
    n_out[i, h] = caug[:, HEAD_DIM:].T[0:1, :]


def _ffn_in_mlstm_body(x_ref, g1_ref, wg_ref, wu_ref, wd_ref, gpost_ref, gmix_ref, wqvo_ref, wktg_ref, wconv_ref,
                       gbias_ref, x1_ref, u_ref, ym_ref, c_out, n_out, m_out,
                       q_st, kt_st, v_st, og_st, g_st, caug_s, m_s, *, tm, nt):
    g = pl.program_id(0)
    wslot = lax.rem(g, 2)
    rslot = 1 - wslot
    L = CHUNK

    @pl.when(g == 0)
    def _init():
        for st in (q_st, kt_st, v_st, og_st, g_st, caug_s, m_s):
            st[...] = jnp.zeros_like(st)

    def mlstm_pieces(side):
        fresh = lax.rem(g - 1, nt) == 0
        r = lax.broadcasted_iota(jnp.int32, (L, L), 0)
        c = lax.broadcasted_iota(jnp.int32, (L, L), 1)
        caus = c <= r
        triu = (r <= c).astype(BF16)
        nck = tm // L
        stages = {}

        def stage1(ck):
            rows, slot = slice(ck * L, (ck + 1) * L), rslot + side.zero
            gt = g_st[slot, :, rows]
            hi = gt.astype(BF16)
            mid = (gt - hi.astype(F32)).astype(BF16)
            lo = (gt - hi.astype(F32) - mid.astype(F32)).astype(BF16)
            parts = _dot(jnp.concatenate([hi, mid, lo], axis=0), triu)
            b_rows = parts[0:2 * N_HEADS] + parts[2 * N_HEADS:4 * N_HEADS] + parts[4 * N_HEADS:]
            for h in range(N_HEADS):
                hs = slice(h * HEAD_DIM, (h + 1) * HEAD_DIM)
                b_r = b_rows[N_HEADS + h:N_HEADS + h + 1, :]
                stages[ck, h] = _mlstm_gates(b_r, gt[h:h + 1, :] - b_r, caus, q_st[slot, rows, hs],
                                             kt_st[slot, hs, rows])

        def stage2(ck):
            rows, slot = slice(ck * L, (ck + 1) * L), rslot + side.zero
            for h in range(N_HEADS):
                mprev = m_s[h + side.zero]
                if ck == 0:
                    mprev = jnp.where(fresh, 0.0, mprev)
                m_new = _mlstm_scores(stages[ck, h], mprev, v_st[slot, rows, slice(h * HEAD_DIM, (h + 1) * HEAD_DIM)])
                m_s[h] = m_new
                if ck == nck - 1:
                    m_out[0, h] = m_new

        def stage3(ck):
            rows, slot = slice(ck * L, (ck + 1) * L), rslot + side.zero
            for h in range(N_HEADS):
                hs = slice(h * HEAD_DIM, (h + 1) * HEAD_DIM)
                caug = caug_s[h + side.zero]
                if ck == 0:
                    caug = jnp.where(fresh, 0.0, caug)
                hh, caug = _mlstm_output(stages.pop((ck, h)), q_st[slot, rows, hs], kt_st[slot, hs, rows], caug)
                ym_ref[rows, hs] = (og_st[slot, rows, hs] * hh).astype(BF16)
                caug_s[h] = caug
                if ck == nck - 1:
                    _store_state(c_out, n_out, 0, h, caug)

        for t in range(nck + 2):
            for lag, stage in enumerate((stage1, stage2, stage3)):
                if 0 <= t - lag < nck:
                    stage(t - lag)
            yield

    side = _Side(mlstm_pieces, pin=PIN_MLSTM)

    blocks = [_ffn_in_block(x_ref, slice(r0, r0 + FFN_ROWS), wslot, g1_ref, wg_ref, wu_ref, wd_ref, gpost_ref,
                            gmix_ref, wqvo_ref, wktg_ref, wconv_ref, gbias_ref, x1_ref, u_ref, q_st, kt_st, v_st,
                            og_st, g_st) for r0 in range(0, tm, FFN_ROWS)]
    side.run(None, 1)
    last = None
    while blocks:
        for block in list(blocks):
            token = next(block, None)
            if token is None:
                blocks.remove(block)
            else:
                side.run(token, 1)
                last = token
    side.run(last, None)


def _ffn_in_block(x_ref, rows, wslot, g1_ref, wg_ref, wu_ref, wd_ref, gpost_ref, gmix_ref, wqvo_ref, wktg_ref, wconv_ref,
                  gbias_ref, x1_ref, u_ref, q_st, kt_st, v_st, og_st, g_st):
    x = x_ref[rows, :]
    h = _rms(x, g1_ref[...]).astype(BF16)
    a = _dot(h, wg_ref[...])
    yield a
    b = _dot(h, wu_ref[...])
    yield b
    s = (a * jax.nn.sigmoid(a) * b).astype(BF16)
    d = _dot(s, wd_ref[...])
    yield d
    x1 = x + 0.5 * _rms(d, gpost_ref[...])
    x1_ref[rows, :] = x1
    h2 = _rms(x1, gmix_ref[...]).astype(BF16)
    qvo = _dot(h2, wqvo_ref[...])
    q_st[wslot, rows, :] = qvo[:, :M_WIDTH].astype(BF16)
    v_st[wslot, rows, :] = qvo[:, M_WIDTH:2 * M_WIDTH]
    og_st[wslot, rows, :] = jax.nn.sigmoid(qvo[:, 2 * M_WIDTH:])
    yield qvo
    ktg = lax.dot_general(wktg_ref[...], h2, NT_DIMS, preferred_element_type=F32)
    kt_st[wslot, :, rows] = (ktg[:M_WIDTH] * (HEAD_DIM ** -0.5)).astype(BF16)
    g_st[wslot, :, rows] = _gate_rows(ktg[M_WIDTH:], gbias_ref)
    yield ktg
    cc = _dot(h2, wconv_ref[...])
    u_ref[rows, :] = cc[:, :CONV_CH] * jax.nn.sigmoid(cc[:, CONV_CH:])
    yield cc


def _conv_ffn_out_body(u_ref, ym_ref, x1_ref, cw_ref, cb_ref, lg_ref, lb_ref, wout_ref, gmp_ref, g2_ref,
                       wg_ref, wu_ref, wd_ref, gpost_ref, gfin_ref, y_ref, hist_out, ubuf, cbuf, *, tm, nt):
    nslab = CONV_CH // LANES

    @pl.when(lax.rem(pl.program_id(0), nt) == 0)
    def _new_sequence():
        ubuf[:, 0:HIST_PAD, :] = jnp.zeros((nslab, HIST_PAD, LANES), F32)

    for lb in range(nslab):
        ubuf[lb, HIST_PAD:HIST_PAD + tm, :] = u_ref[:, lb * LANES:(lb + 1) * LANES]
    for k in range(tm // CONV_ROWS):
        for lb in range(nslab):
            acc = _conv_taps_interleaved(ubuf.at[lb], k * CONV_ROWS, CONV_ROWS, slice(lb * LANES, (lb + 1) * LANES),
                                         cw_ref, cb_ref)
            for parity in range(2):
                cbuf[lb, pl.ds(k * CONV_ROWS + parity, CONV_ROWS // 2, stride=2), :] = acc[parity]
    tail = jnp.concatenate([ubuf[lb, tm:tm + HIST_PAD, :] for lb in range(nslab)], axis=1)
    hist_out[0] = tail
    for lb in range(nslab):
        ubuf[lb, 0:HIST_PAD, :] = tail[:, lb * LANES:(lb + 1) * LANES]

    halves = []
    for r0 in range(0, tm, FFN_ROWS):
        c = []
        for k in range(FFN_ROWS // LN_ROWS):
            rows = slice(r0 + k * LN_ROWS, r0 + (k + 1) * LN_ROWS)
            c.append(_ln_swish(jnp.concatenate([cbuf[lb, rows, :] for lb in range(nslab)], axis=1), lg_ref, lb_ref))
        rows = slice(r0, r0 + FFN_ROWS)
        halves.append(_ffn_out_phases(ym_ref[rows, :], jnp.concatenate(c, axis=0), x1_ref[rows, :], y_ref, rows,
                                      wout_ref, gmp_ref, g2_ref, wg_ref, wu_ref, wd_ref, gpost_ref, gfin_ref))
    for _ in zip(*halves):
        pass


def _const_spec(shape):
    nd = len(shape)
    return pl.BlockSpec(shape, lambda *_: (0,) * nd, pipeline_mode=pl.Buffered(1))


def _ffn_in_consts(p, tm):
    gbias = jnp.broadcast_to(p["gbias"][:, None], (2 * N_HEADS, tm))
    return [p["ffn1_pre_g"], p["ffn1_wg"], p["ffn1_wu"], p["ffn1_wd"], p["ffn1_post_g"], p["mix_pre_g"],
            p["w_qvo"], p["w_ktg"], p["w_conv"], gbias]


def _ffn_out_consts(p):
    return [p["w_out"], p["mix_post_g"], p["ffn2_pre_g"], p["ffn2_wg"], p["ffn2_wu"], p["ffn2_wd"],
            p["ffn2_post_g"], p["final_g"]]


def _ffn_in_mlstm(x2d, p, batch, seq, tm):
    m = batch * seq
    nt = seq // tm
    ntiles = m // tm
    cur = lambda w: pl.BlockSpec((tm, w), lambda g: (jnp.minimum(g, ntiles - 1), 0))
    prev = lambda w: pl.BlockSpec((tm, w), lambda g: (jnp.maximum(g - 1, 0), 0))
    prev_seq = lambda g: jnp.maximum(g - 1, 0) // nt
    consts = _ffn_in_consts(p, FFN_ROWS)
    return pl.pallas_call(
        functools.partial(_ffn_in_mlstm_body, tm=tm, nt=nt),
        grid=(ntiles + 1,),
        in_specs=[cur(D_MODEL)] + [_const_spec(c.shape) for c in consts],
        out_specs=[cur(D_MODEL), cur(CONV_CH), prev(M_WIDTH),
                   pl.BlockSpec((1, N_HEADS, HEAD_DIM, HEAD_DIM), lambda g: (prev_seq(g), 0, 0, 0)),
                   pl.BlockSpec((1, N_HEADS, 1, HEAD_DIM), lambda g: (prev_seq(g), 0, 0, 0)),
                   pl.BlockSpec((1, N_HEADS, 1, HEAD_DIM), lambda g: (prev_seq(g), 0, 0, 0))],
        out_shape=[jax.ShapeDtypeStruct((m, D_MODEL), F32), jax.ShapeDtypeStruct((m, CONV_CH), F32),
                   jax.ShapeDtypeStruct((m, M_WIDTH), BF16),
                   jax.ShapeDtypeStruct((batch, N_HEADS, HEAD_DIM, HEAD_DIM), F32),
                   jax.ShapeDtypeStruct((batch, N_HEADS, 1, HEAD_DIM), F32),
                   jax.ShapeDtypeStruct((batch, N_HEADS, 1, HEAD_DIM), F32)],
        scratch_shapes=[pltpu.VMEM((2, tm, M_WIDTH), BF16), pltpu.VMEM((2, M_WIDTH, tm), BF16),
                        pltpu.VMEM((2, tm, M_WIDTH), F32), pltpu.VMEM((2, tm, M_WIDTH), F32),
                        pltpu.VMEM((2, 2 * N_HEADS, tm), F32),
                        pltpu.VMEM((N_HEADS, HEAD_DIM, AUG), F32), pltpu.VMEM((N_HEADS, 1, HEAD_DIM), F32)],
        compiler_params=pltpu.CompilerParams(dimension_semantics=("arbitrary",),
                                             vmem_limit_bytes=VMEM_LIMIT_TOKENWISE),
        name="ffn_in_mlstm",
    )(x2d, *consts)


def _conv_ffn_out(u, ym, x1, p, batch, seq, tm):
    m = batch * seq
    nt = seq // tm
    row = lambda w: pl.BlockSpec((tm, w), lambda g: (g, 0))
    consts = [p["conv_w"], p["conv_b"], p["conv_ln_g"], p["conv_ln_b"]] + _ffn_out_consts(p)
    return pl.pallas_call(
        functools.partial(_conv_ffn_out_body, tm=tm, nt=nt),
        grid=(m // tm,),
        in_specs=[row(CONV_CH), row(M_WIDTH), row(D_MODEL)] + [_const_spec(c.shape) for c in consts],
        out_specs=[row(D_MODEL), pl.BlockSpec((1, HIST_PAD, CONV_CH), lambda g: (g // nt, 0, 0))],
        out_shape=[jax.ShapeDtypeStruct((m, D_MODEL), F32), jax.ShapeDtypeStruct((batch, HIST_PAD, CONV_CH), F32)],
        scratch_shapes=[pltpu.VMEM((CONV_CH // LANES, HIST_PAD + tm, LANES), F32),
                        pltpu.VMEM((CONV_CH // LANES, tm, LANES), F32)],
        compiler_params=pltpu.CompilerParams(dimension_semantics=("arbitrary",),
                                             vmem_limit_bytes=VMEM_LIMIT_TOKENWISE),
        name="conv_ffn_out",
    )(u, ym, x1, *consts)


def _ffn_in_body(x_ref, g1_ref, wg_ref, wu_ref, wd_ref, gpost_ref, gmix_ref, wqvo_ref, wktg_ref, wconv_ref,
                 gbias_ref, x1_ref, q_ref, kt_ref, v_ref, og_ref, u_ref, gt_ref):
    outs = _ffn_in_values(x_ref[...], g1_ref, wg_ref, wu_ref, wd_ref, gpost_ref, gmix_ref, wqvo_ref, wktg_ref,
                          wconv_ref, gbias_ref)
    for ref, val in zip((x1_ref, q_ref, kt_ref, v_ref, og_ref, u_ref, gt_ref), outs):
        ref[...] = val


def _ffn_out_body(mix_ref, x1_ref, wout_ref, gmp_ref, g2_ref, wg_ref, wu_ref, wd_ref, gpost_ref, gfin_ref, y_ref):
    y_ref[...] = _ffn_out_values(mix_ref[:, :M_WIDTH], mix_ref[:, M_WIDTH:], x1_ref[...], wout_ref, gmp_ref, g2_ref,
                                 wg_ref, wu_ref, wd_ref, gpost_ref, gfin_ref)


def _mixer_sample_body(q_ref, kt_ref, v_ref, og_ref, u_ref, gt_ref, c0_ref, n0_ref, m0_ref, hist0_ref,
                       cw_ref, cb_ref, lg_ref, lb_ref, mix_ref, c_out, n_out, m_out, hist_out, ubuf, *, nseq, L):
    R = nseq * L
    r = lax.broadcasted_iota(jnp.int32, (R, R), 0)
    c = lax.broadcasted_iota(jnp.int32, (R, R), 1)
    same = lax.div(r, L) == lax.div(c, L)
    caus = same & (c <= r)
    segtriu = jnp.where(same & (r <= c), 1.0, 0.0).astype(F32)
    segones = jnp.where(same, 1.0, 0.0).astype(F32)
    rowseq = lax.div(lax.broadcasted_iota(jnp.int32, (R, AUG), 0), L)
    ones = jnp.ones((R, HEAD_DIM), F32)
    gt = gt_ref[...]
    b_rows = jnp.dot(gt, segtriu, precision=HIGHEST, preferred_element_type=F32)
    tot_rows = jnp.dot(gt, segones, precision=HIGHEST, preferred_element_type=F32)
    for h in range(N_HEADS):
        hs = slice(h * HEAD_DIM, (h + 1) * HEAD_DIM)
        b_r = b_rows[N_HEADS + h:N_HEADS + h + 1, :]
        a_r = gt[h:h + 1, :] - b_r
        t_r = tot_rows[N_HEADS + h:N_HEADS + h + 1, :]
        arow = jnp.broadcast_to(a_r, (R, R))
        bcol = jnp.broadcast_to(b_r, (R, R)).T
        acol = arow.T
        tcol = jnp.broadcast_to(t_r, (R, R)).T
        mprev = m0_ref[h]
        mprev = jnp.concatenate([mprev, mprev], axis=1)
        dmat = jnp.where(caus, bcol + arow, -jnp.inf)
        m_t = jnp.maximum(bcol + mprev, jnp.max(dmat, axis=1, keepdims=True))
        dend = jnp.where(same, tcol + arow, -jnp.inf)
        m_new = jnp.maximum(tcol + mprev, jnp.max(dend, axis=1, keepdims=True))
        qh = q_ref[:, hs]
        kth = kt_ref[hs, :]
        s = _dot(qh, kth) * jnp.exp(dmat - m_t)
        vaug = jnp.concatenate([v_ref[:, hs], ones], axis=1)
        w_inter = jnp.exp(bcol + mprev - m_t)
        caug0 = [jnp.concatenate([c0_ref[i, h], n0_ref[i, h]], axis=1) for i in range(nseq)]
        qc = jnp.zeros((R, AUG), F32)
        for i in range(nseq):
            qc = jnp.where(rowseq == i, _dot(qh, caug0[i].astype(BF16)), qc)
        sv = _dot(s.astype(BF16), vaug.astype(BF16))
        num = w_inter[:, :HEAD_DIM] * qc[:, :HEAD_DIM] + sv[:, :HEAD_DIM]
        den = w_inter[:, :HEAD_DIM] * qc[:, HEAD_DIM:] + sv[:, HEAD_DIM:]
        hh = num / jnp.maximum(jnp.abs(den), jnp.exp(-m_t[:, :HEAD_DIM]))
        mix_ref[:, hs] = og_ref[:, hs] * hh
        g_state = jnp.exp(tcol + mprev - m_new)
        g_rows = jnp.exp(tcol + acol - m_new)
        gv = g_rows * vaug
        for i in range(nseq):
            gvi = jnp.where(rowseq == i, gv, 0.0).astype(BF16)
            caug = g_state[i * L:i * L + 1, :] * caug0[i] + _dot(kth, gvi)
            c_out[i, h] = caug[:, :HEAD_DIM]
            n_out[i, h] = caug[:, HEAD_DIM:]
            m_out[i, h] = m_new[i * L:i * L + 1, :HEAD_DIM]

    for i in range(nseq):
        ubuf[0:HIST_PAD, :] = hist0_ref[i]
        ubuf[HIST_PAD:HIST_PAD + L, :] = u_ref[i * L:(i + 1) * L, :]
        acc = _conv_taps(ubuf, 0, L, slice(0, CONV_CH), cw_ref, cb_ref)
        mix_ref[i * L:(i + 1) * L, M_WIDTH:] = _ln_swish(acc, lg_ref, lb_ref)
        hist_out[i] = ubuf[L:L + HIST_PAD, :]


def _ffn_in(x2d, p, tm):
    m = x2d.shape[0]
    row = lambda w: pl.BlockSpec((tm, w), lambda i: (i, 0))
    col = lambda h: pl.BlockSpec((h, tm), lambda i: (0, i))
    consts = _ffn_in_consts(p, tm)
    return pl.pallas_call(
        _ffn_in_body,
        grid=(m // tm,),
        in_specs=[row(D_MODEL)] + [_const_spec(c.shape) for c in consts],
        out_specs=[row(D_MODEL), row(M_WIDTH), col(M_WIDTH), row(M_WIDTH), row(M_WIDTH), row(CONV_CH),
                   col(2 * N_HEADS)],
        out_shape=[jax.ShapeDtypeStruct((m, D_MODEL), F32), jax.ShapeDtypeStruct((m, M_WIDTH), BF16),
                   jax.ShapeDtypeStruct((M_WIDTH, m), BF16), jax.ShapeDtypeStruct((m, M_WIDTH), F32),
                   jax.ShapeDtypeStruct((m, M_WIDTH), F32), jax.ShapeDtypeStruct((m, CONV_CH), F32),
                   jax.ShapeDtypeStruct((2 * N_HEADS, m), F32)],
        compiler_params=pltpu.CompilerParams(dimension_semantics=("arbitrary",),
                                             vmem_limit_bytes=VMEM_LIMIT_TOKENWISE),
        name="ffn_in",
    )(x2d, *consts)


def _ffn_out(mix, x1, p, tm):
    m = mix.shape[0]
    row = pl.BlockSpec((tm, D_MODEL), lambda i: (i, 0))
    consts = _ffn_out_consts(p)
    return pl.pallas_call(
        _ffn_out_body,
        grid=(m // tm,),
        in_specs=[row, row] + [_const_spec(c.shape) for c in consts],
        out_specs=row,
        out_shape=jax.ShapeDtypeStruct((m, D_MODEL), F32),
        compiler_params=pltpu.CompilerParams(dimension_semantics=("arbitrary",),
                                             vmem_limit_bytes=VMEM_LIMIT_TOKENWISE),
        name="ffn_out",
    )(mix, x1, *consts)


def _mixer_sample(q, kt, v, og, u, gt, c0, n0, m0, hist0, p, nseq, L):
    rows = nseq * L
    args = [q, kt, v, og, u, gt, c0, n0, m0, hist0, p["conv_w"], p["conv_b"], p["conv_ln_g"], p["conv_ln_b"]]
    full = lambda a: pl.BlockSpec(a.shape, lambda i, nd=a.ndim: (0,) * nd)
    out_shape = [jax.ShapeDtypeStruct((rows, D_MODEL), F32),
                 jax.ShapeDtypeStruct((nseq, N_HEADS, HEAD_DIM, HEAD_DIM), F32),
                 jax.ShapeDtypeStruct((nseq, N_HEADS, HEAD_DIM, HEAD_DIM), F32),
                 jax.ShapeDtypeStruct((nseq, N_HEADS, 1, HEAD_DIM), F32),
                 jax.ShapeDtypeStruct((nseq, HIST_PAD, CONV_CH), F32)]
    return pl.pallas_call(
        functools.partial(_mixer_sample_body, nseq=nseq, L=L),
        grid=(1,),
        in_specs=[full(a) for a in args],
        out_specs=[full(s) for s in out_shape],
        out_shape=out_shape,
        scratch_shapes=[pltpu.VMEM((HIST_PAD + L, CONV_CH), F32)],
        compiler_params=pltpu.CompilerParams(dimension_semantics=("arbitrary",),
                                             vmem_limit_bytes=VMEM_LIMIT_MIXER),
        name="mixer_sample",
    )(*args)


def _layer_params(l, ffn1_pre_g, ffn1_wg, ffn1_wu, ffn1_wd, ffn1_post_g, mix_pre_g, w_in, b_igate, b_fgate,
                  conv_w, conv_b, conv_ln_g, conv_ln_b, w_out, mix_post_g, ffn2_pre_g, ffn2_wg, ffn2_wu, ffn2_wd,
                  ffn2_post_g, final_g):
    vec = lambda a: a[l].astype(F32).reshape(1, -1)
    w = w_in[l]
    cuts = [0, M_WIDTH, 2 * M_WIDTH, 3 * M_WIDTH, 4 * M_WIDTH, 4 * M_WIDTH + N_HEADS, 4 * M_WIDTH + 2 * N_HEADS,
            4 * M_WIDTH + 2 * N_HEADS + CONV_CH, 4 * M_WIDTH + 2 * N_HEADS + 2 * CONV_CH]
    wq, wk, wv, wo, wi, wf, wcv, wcg = [w[:, a:b] for a, b in zip(cuts[:-1], cuts[1:])]
    return {
        "ffn1_pre_g": vec(ffn1_pre_g), "ffn1_post_g": vec(ffn1_post_g), "mix_pre_g": vec(mix_pre_g),
        "mix_post_g": vec(mix_post_g), "ffn2_pre_g": vec(ffn2_pre_g), "ffn2_post_g": vec(ffn2_post_g),
        "final_g": vec(final_g),
        "ffn1_wg": ffn1_wg[l].astype(BF16), "ffn1_wu": ffn1_wu[l].astype(BF16), "ffn1_wd": ffn1_wd[l].astype(BF16),
        "ffn2_wg": ffn2_wg[l].astype(BF16), "ffn2_wu": ffn2_wu[l].astype(BF16), "ffn2_wd": ffn2_wd[l].astype(BF16),
        "w_qvo": jnp.concatenate([wq, wv, wo], axis=1).astype(BF16),
        "w_ktg": jnp.concatenate([wk, wi, wf], axis=1).T.astype(BF16),
        "w_conv": jnp.concatenate([wcv, wcg], axis=1).astype(BF16),
        "gbias": jnp.concatenate([b_igate[l], b_fgate[l]]).astype(F32),
        "w_out": w_out[l].astype(BF16),
        "conv_w": conv_w[l].astype(F32), "conv_b": vec(conv_b), "conv_ln_g": vec(conv_ln_g),
        "conv_ln_b": vec(conv_ln_b),
    }


def _split_state(c, n, m, hist):
    return c, n[:, :, 0, :], m[:, :, 0, 0], hist[:, HIST_PAD - HIST:, :]


def kernel(x_prompt, x_sample, state_mlstm_C, state_mlstm_n, state_mlstm_m, cache_conv, ffn1_pre_g, ffn1_wg,
           ffn1_wu, ffn1_wd, ffn1_post_g, mix_pre_g, w_in, b_igate, b_fgate, conv_w, conv_b, conv_ln_g, conv_ln_b,
           w_out, mix_post_g, ffn2_pre_g, ffn2_wg, ffn2_wu, ffn2_wd, ffn2_post_g, final_g):
    batch, seq, _ = x_prompt.shape
    nseq, dseq, _ = x_sample.shape
    depth = w_in.shape[0]
    assert seq % PROMPT_TILE == 0 and PROMPT_TILE % CHUNK == 0 and PROMPT_TILE % CONV_ROWS == 0
    assert (nseq * dseq) % TOKEN_TILE == 0 and dseq <= HIST_PAD
    yp = x_prompt.reshape(batch * seq, D_MODEL)
    ys = x_sample.reshape(nseq * dseq, D_MODEL)
    outs_p, outs_s = [], []
    for l in range(depth):
        p = _layer_params(l, ffn1_pre_g, ffn1_wg, ffn1_wu, ffn1_wd, ffn1_post_g, mix_pre_g, w_in, b_igate, b_fgate,
                          conv_w, conv_b, conv_ln_g, conv_ln_b, w_out, mix_post_g, ffn2_pre_g, ffn2_wg, ffn2_wu,
                          ffn2_wd, ffn2_post_g, final_g)
        x1, u, ym, c, n, m = _ffn_in_mlstm(yp, p, batch, seq, 2 * PROMPT_TILE)
        yp, hist = _conv_ffn_out(u, ym, x1, p, batch, seq, 2 * PROMPT_TILE)
        outs_p.append(_split_state(c, n, m, hist))
        x1, q, kt, v, og, u, gt = _ffn_in(ys, p, TOKEN_TILE)
        n0 = jnp.broadcast_to(state_mlstm_n[l].astype(F32)[..., None], (nseq, N_HEADS, HEAD_DIM, HEAD_DIM))
        m0 = jnp.broadcast_to(state_mlstm_m[l].astype(F32).T[:, :, None, None], (N_HEADS, nseq, dseq, HEAD_DIM))
        m0 = m0.reshape(N_HEADS, nseq * dseq, HEAD_DIM)
        hist0 = jnp.pad(cache_conv[l].astype(F32), ((0, 0), (HIST_PAD - HIST, 0), (0, 0)))
        mix, c, n, m, hist = _mixer_sample(q, kt, v, og, u, gt, state_mlstm_C[l].astype(F32), n0, m0, hist0, p, nseq,
                                          dseq)
        ys = _ffn_out(mix, x1, p, TOKEN_TILE)
        outs_s.append(_split_state(c, n[:, :, None, :, 0], m, hist))
    stack = lambda outs, k: jnp.stack([o[k] for o in outs])
    return (yp.reshape(batch, seq, D_MODEL), ys.reshape(nseq, dseq, D_MODEL),
            stack(outs_p, 0), stack(outs_p, 1), stack(outs_p, 2), stack(outs_p, 3),
            stack(outs_s, 0), stack(outs_s, 1), stack(outs_s, 2), stack(outs_s, 3))
```

```python
import functools

import jax
import jax.numpy as jnp
from jax import lax
from jax.experimental import pallas as pl
from jax.experimental.pallas import tpu as pltpu

D_MODEL = 1024
D_FF = 2816
N_HEADS = 4
HEAD_DIM = 128
M_WIDTH = N_HEADS * HEAD_DIM
CONV_CH = 512
CONV_WIDTH = 31
HIST = CONV_WIDTH - 1
HIST_PAD = 32
SUBLANES = 8
EPS = 1e-6
CHUNK = 128
AUG = 2 * HEAD_DIM
CONV_ROWS = 128
LANES = 128
LN_ROWS = 32
NCH = 256

F32 = jnp.float32
BF16 = jnp.bfloat16
HIGHEST = lax.Precision.HIGHEST
NT_DIMS = (((1,), (1,)), ((), ()))

TOKEN_TILE = 256
PROMPT_TILE = 256
FFN_ROWS = 256
PIN_MLSTM = False
VMEM_LIMIT_TOKENWISE = 56 * 1024 * 1024
VMEM_LIMIT_MIXER = 40 * 1024 * 1024


def _rms(x, g):
    return x * lax.rsqrt(jnp.mean(x * x, axis=-1, keepdims=True) + EPS) * g


def _dot(a, b):
    return jnp.dot(a, b, preferred_element_type=F32)


def _swiglu(h, wg_ref, wu_ref, wd_ref):
    a = _dot(h, wg_ref[...])
    b = _dot(h, wu_ref[...])
    s = (a * jax.nn.sigmoid(a) * b).astype(BF16)
    return _dot(s, wd_ref[...])


class _Side:
    def __init__(self, make_pieces, pin=True):
        self.zero = 0
        self.pin = pin
        self.pieces = make_pieces(self)

    def run(self, after, n=1):
        if self.pin:
            bits = pltpu.bitcast(after[after.shape[0] - SUBLANES:, 0:LANES], jnp.int32)
            self.zero = lax.shift_right_logical(lax.shift_right_logical(bits, 16), 16)[0, 0]
        if n is None:
            for _ in self.pieces:
                pass
        else:
            for _ in range(n):
                next(self.pieces, None)


def _swiglu_chunked(h_s, s_s, wg_ref, wu_ref, wd_ref, side, chunk, after_gating, down_pieces=1):
    for c0 in range(0, D_FF, chunk):
        cols = slice(c0, min(c0 + chunk, D_FF))
        a = _dot(h_s[...], wg_ref[:, cols])
        b = _dot(h_s[...], wu_ref[:, cols])
        s = a * jax.nn.sigmoid(a) * b
        s_s[:, cols] = s.astype(BF16)
        for t0 in range(0, cols.stop - cols.start, NCH):
            side.run((s if after_gating else a)[:, t0:t0 + LANES])
    d = []
    for c0 in range(0, D_MODEL, chunk):
        d.append(_dot(s_s[...], wd_ref[:, c0:c0 + chunk]))
        for t0 in range(0, chunk, NCH):
            side.run(d[-1][:, t0:t0 + LANES], down_pieces)
    return jnp.concatenate(d, axis=1)


def _gate_rows(pre, gbias_ref):
    g = pre + gbias_ref[...]
    row = lax.broadcasted_iota(jnp.int32, g.shape, 0)
    return jnp.where(row < N_HEADS, g, jax.nn.log_sigmoid(g))


def _ffn_in_values(x, g1_ref, wg_ref, wu_ref, wd_ref, gpost_ref, gmix_ref, wqvo_ref, wktg_ref, wconv_ref, gbias_ref):
    h = _rms(x, g1_ref[...]).astype(BF16)
    d = _swiglu(h, wg_ref, wu_ref, wd_ref)
    x1 = x + 0.5 * _rms(d, gpost_ref[...])
    h2 = _rms(x1, gmix_ref[...]).astype(BF16)
    qvo = _dot(h2, wqvo_ref[...])
    q = qvo[:, :M_WIDTH].astype(BF16)
    v = qvo[:, M_WIDTH:2 * M_WIDTH]
    og = jax.nn.sigmoid(qvo[:, 2 * M_WIDTH:])
    ktg = lax.dot_general(wktg_ref[...], h2, NT_DIMS, preferred_element_type=F32)
    kt = ktg[:M_WIDTH] * (HEAD_DIM ** -0.5)
    cc = _dot(h2, wconv_ref[...])
    u = cc[:, :CONV_CH] * jax.nn.sigmoid(cc[:, CONV_CH:])
    return x1, q, kt.astype(BF16), v, og, u, _gate_rows(ktg[M_WIDTH:], gbias_ref)


def _ffn_out_values(ym, c, x1, wout_ref, gmp_ref, g2_ref, wg_ref, wu_ref, wd_ref, gpost_ref, gfin_ref):
    o = _dot(ym.astype(BF16), wout_ref[0:M_WIDTH, :]) + _dot(c.astype(BF16), wout_ref[M_WIDTH:, :])
    x2 = x1 + _rms(o, gmp_ref[...])
    h = _rms(x2, g2_ref[...]).astype(BF16)
    d = _swiglu(h, wg_ref, wu_ref, wd_ref)
    x3 = x2 + 0.5 * _rms(d, gpost_ref[...])
    return _rms(x3, gfin_ref[...])


def _ffn_out_phases(ym, c, x1, y_ref, rows, wout_ref, gmp_ref, g2_ref, wg_ref, wu_ref, wd_ref, gpost_ref, gfin_ref):
    o = _dot(ym.astype(BF16), wout_ref[0:M_WIDTH, :]) + _dot(c.astype(BF16), wout_ref[M_WIDTH:, :])
    yield o
    x2 = x1 + _rms(o, gmp_ref[...])
    h = _rms(x2, g2_ref[...]).astype(BF16)
    a = _dot(h, wg_ref[...])
    yield a
    b = _dot(h, wu_ref[...])
    yield b
    s = (a * jax.nn.sigmoid(a) * b).astype(BF16)
    d = _dot(s, wd_ref[...])
    yield d
    x3 = x2 + 0.5 * _rms(d, gpost_ref[...])
    y = _rms(x3, gfin_ref[...])
    y_ref[rows, :] = y
    yield y


def _conv_taps(ubuf, row0, nrows, lanes, cw_ref, cb_ref):
    acc = jnp.broadcast_to(cb_ref[:, lanes], (nrows, lanes.stop - lanes.start))
    first = HIST_PAD - HIST
    for res in range(SUBLANES):
        taps = [j for j in range(CONV_WIDTH) if (first + j) % SUBLANES == res]
        lo = (first + taps[0]) // SUBLANES * SUBLANES
        hi = (first + taps[-1]) // SUBLANES * SUBLANES
        win = ubuf[row0 + lo:row0 + hi + nrows + (SUBLANES if res else 0), lanes]
        if res:
            win = pltpu.roll(win, win.shape[0] - res, 0)
        for j in taps:
            off = first + j - res - lo
            acc = acc + win[off:off + nrows, :] * cw_ref[j:j + 1, lanes]
    return acc


def _conv_taps_interleaved(uslab, row0, nrows, lanes, cw_ref, cb_ref):
    first = HIST_PAD - HIST
    half = nrows // 2
    acc = [jnp.broadcast_to(cb_ref[:, lanes], (half, LANES))] * 2
    for j in range(CONV_WIDTH):
        for parity in range(2):
            x = uslab[pl.ds(row0 + first + j + parity, half, stride=2), :]
            acc[parity] = acc[parity] + x * cw_ref[j:j + 1, lanes]
    return acc


def _ln_swish(acc, lg_ref, lb_ref):
    mu = jnp.mean(acc, axis=-1, keepdims=True)
    xc = acc - mu
    var = jnp.mean(xc * xc, axis=-1, keepdims=True)
    y = xc * lax.rsqrt(var + EPS) * lg_ref[...] + lb_ref[...]
    return y * jax.nn.sigmoid(y)


def _mlstm_gates(b_r, a_r, caus, qh, kth):
    L = b_r.shape[1]
    arow = jnp.broadcast_to(a_r, (L, L))
    bcol = jnp.broadcast_to(b_r, (L, L)).T
    acol = arow.T
    dmat = jnp.where(caus, bcol + arow, -jnp.inf)
    return dict(bcol=bcol, acol=acol, dmat=dmat, rowmax=jnp.max(dmat, axis=1, keepdims=True), s_raw=_dot(qh, kth))


def _mlstm_scores(st, mprev, vh):
    L = st["bcol"].shape[0]
    m_t = jnp.maximum(st["bcol"] + mprev, st["rowmax"])
    s = st["s_raw"] * jnp.exp(st["dmat"] - m_t)
    vaug = jnp.concatenate([vh, jnp.ones_like(vh)], axis=1)
    m_new = m_t[L - 1:L, :]
    b_last = st["bcol"][L - 1:L, :]
    g_rows = jnp.exp(b_last + st["acol"] - m_new)
    new = dict(m_t=m_t, sv=_dot(s.astype(BF16), vaug.astype(BF16)), w_inter=jnp.exp(st["bcol"] + mprev - m_t),
               g_state=jnp.exp(b_last + mprev - m_new),
               gv=(jnp.concatenate([g_rows, g_rows], axis=1) * vaug).astype(BF16))
    st.clear()
    st.update(new)
    return m_new


def _mlstm_output(st, qh, kth, caug):
    m_t, sv, w_inter, g_state = st["m_t"], st["sv"], st["w_inter"], st["g_state"]
    qc = _dot(qh, caug.astype(BF16))
    caug_new = jnp.concatenate([g_state, g_state], axis=1) * caug + _dot(kth, st["gv"])
    num = w_inter * qc[:, :HEAD_DIM] + sv[:, :HEAD_DIM]
    den = w_inter * qc[:, HEAD_DIM:] + sv[:, HEAD_DIM:]
    hh = num / jnp.maximum(jnp.abs(den), jnp.exp(-m_t))
    return hh, caug_new


def _store_state(c_out, n_out, i, h, caug):
    c_out[i, h] = caug[:, :HEAD_DIM]
    n_out[i, h] = caug[:, HEAD_DIM:].T[0:1, :]


def _ffn_in_mlstm_body(x_ref, g1_ref, wg_ref, wu_ref, wd_ref, gpost_ref, gmix_ref, wqvo_ref, wktg_ref, wconv_ref,
                       gbias_ref, x1_ref, u_ref, ym_ref, c_out, n_out, m_out,
                       q_st, kt_st, v_st, og_st, g_st, caug_s, m_s, *, tm, nt):
    g = pl.program_id(0)
    wslot = lax.rem(g, 2)
    rslot = 1 - wslot
    L = CHUNK

    @pl.when(g == 0)
    def _init():
        for st in (q_st, kt_st, v_st, og_st, g_st, caug_s, m_s):
            st[...] = jnp.zeros_like(st)

    def mlstm_pieces(side):
        fresh = lax.rem(g - 1, nt) == 0
        r = lax.broadcasted_iota(jnp.int32, (L, L), 0)
        c = lax.broadcasted_iota(jnp.int32, (L, L), 1)
        caus = c <= r
        triu = (r <= c).astype(BF16)
        nck = tm // L
        stages = {}

        def stage1(ck):
            rows, slot = slice(ck * L, (ck + 1) * L), rslot + side.zero
            gt = g_st[slot, :, rows]
            hi = gt.astype(BF16)
            mid = (gt - hi.astype(F32)).astype(BF16)
            lo = (gt - hi.astype(F32) - mid.astype(F32)).astype(BF16)
            parts = _dot(jnp.concatenate([hi, mid, lo], axis=0), triu)
            b_rows = parts[0:2 * N_HEADS] + parts[2 * N_HEADS:4 * N_HEADS] + parts[4 * N_HEADS:]
            for h in range(N_HEADS):
                hs = slice(h * HEAD_DIM, (h + 1) * HEAD_DIM)
                b_r = b_rows[N_HEADS + h:N_HEADS + h + 1, :]
                stages[ck, h] = _mlstm_gates(b_r, gt[h:h + 1, :] - b_r, caus, q_st[slot, rows, hs],
                                             kt_st[slot, hs, rows])

        def stage2(ck):
            rows, slot = slice(ck * L, (ck + 1) * L), rslot + side.zero
            for h in range(N_HEADS):
                mprev = m_s[h + side.zero]
                if ck == 0:
                    mprev = jnp.where(fresh, 0.0, mprev)
                m_new = _mlstm_scores(stages[ck, h], mprev, v_st[slot, rows, slice(h * HEAD_DIM, (h + 1) * HEAD_DIM)])
                m_s[h] = m_new
                if ck == nck - 1:
                    m_out[0, h] = m_new

        def stage3(ck):
            rows, slot = slice(ck * L, (ck + 1) * L), rslot + side.zero
            for h in range(N_HEADS):
                hs = slice(h * HEAD_DIM, (h + 1) * HEAD_DIM)
                caug = caug_s[h + side.zero]
                if ck == 0:
                    caug = jnp.where(fresh, 0.0, caug)
                hh, caug = _mlstm_output(stages.pop((ck, h)), q_st[slot, rows, hs], kt_st[slot, hs, rows], caug)
                ym_ref[rows, hs] = (og_st[slot, rows, hs] * hh).astype(BF16)
                caug_s[h] = caug
                if ck == nck - 1:
                    _store_state(c_out, n_out, 0, h, caug)

        for t in range(nck + 2):
            for lag, stage in enumerate((stage1, stage2, stage3)):
                if 0 <= t - lag < nck:
                    stage(t - lag)
            yield

    side = _Side(mlstm_pieces, pin=PIN_MLSTM)

    blocks = [_ffn_in_block(x_ref, slice(r0, r0 + FFN_ROWS), wslot, g1_ref, wg_ref, wu_ref, wd_ref, gpost_ref,
                            gmix_ref, wqvo_ref, wktg_ref, wconv_ref, gbias_ref, x1_ref, u_ref, q_st, kt_st, v_st,
                            og_st, g_st) for r0 in range(0, tm, FFN_ROWS)]
    side.run(None, 1)
    last = None
    while blocks:
        for block in list(blocks):
            token = next(block, None)
            if token is None:
                blocks.remove(block)
            else:
                side.run(token, 1)
                last = token
    side.run(last, None)


def _ffn_in_block(x_ref, rows, wslot, g1_ref, wg_ref, wu_ref, wd_ref, gpost_ref, gmix_ref, wqvo_ref, wktg_ref, wconv_ref,
                  gbias_ref, x1_ref, u_ref, q_st, kt_st, v_st, og_st, g_st):
    x = x_ref[rows, :]
    h = _rms(x, g1_ref[...]).astype(BF16)
    a = _dot(h, wg_ref[...])
    yield a
    b = _dot(h, wu_ref[...])
    yield b
    s = (a * jax.nn.sigmoid(a) * b).astype(BF16)
    d = _dot(s, wd_ref[...])
    yield d
    x1 = x + 0.5 * _rms(d, gpost_ref[...])
    x1_ref[rows, :] = x1
    h2 = _rms(x1, gmix_ref[...]).astype(BF16)
    qvo = _dot(h2, wqvo_ref[...])
    q_st[wslot, rows, :] = qvo[:, :M_WIDTH].astype(BF16)
    v_st[wslot, rows, :] = qvo[:, M_WIDTH:2 * M_WIDTH]
    og_st[wslot, rows, :] = jax.nn.sigmoid(qvo[:, 2 * M_WIDTH:])
    yield qvo
    ktg = lax.dot_general(wktg_ref[...], h2, NT_DIMS, preferred_element_type=F32)
    kt_st[wslot, :, rows] = (ktg[:M_WIDTH] * (HEAD_DIM ** -0.5)).astype(BF16)
    g_st[wslot, :, rows] = _gate_rows(ktg[M_WIDTH:], gbias_ref)
    yield ktg
    cc = _dot(h2, wconv_ref[...])
    u_ref[rows, :] = cc[:, :CONV_CH] * jax.nn.sigmoid(cc[:, CONV_CH:])
    yield cc


def _conv_ffn_out_body(u_ref, ym_ref, x1_ref, cw_ref, cb_ref, lg_ref, lb_ref, wout_ref, gmp_ref, g2_ref,
                       wg_ref, wu_ref, wd_ref, gpost_ref, gfin_ref, y_ref, hist_out, ubuf, cbuf, *, tm, nt):
    nslab = CONV_CH // LANES

    @pl.when(lax.rem(pl.program_id(0), nt) == 0)
    def _new_sequence():
        ubuf[:, 0:HIST_PAD, :] = jnp.zeros((nslab, HIST_PAD, LANES), F32)

    for lb in range(nslab):
        ubuf[lb, HIST_PAD:HIST_PAD + tm, :] = u_ref[:, lb * LANES:(lb + 1) * LANES]
    for k in range(tm // CONV_ROWS):
        for lb in range(nslab):
            acc = _conv_taps_interleaved(ubuf.at[lb], k * CONV_ROWS, CONV_ROWS, slice(lb * LANES, (lb + 1) * LANES),
                                         cw_ref, cb_ref)
            for parity in range(2):
                cbuf[lb, pl.ds(k * CONV_ROWS + parity, CONV_ROWS // 2, stride=2), :] = acc[parity]
    tail = jnp.concatenate([ubuf[lb, tm:tm + HIST_PAD, :] for lb in range(nslab)], axis=1)
    hist_out[0] = tail
    for lb in range(nslab):
        ubuf[lb, 0:HIST_PAD, :] = tail[:, lb * LANES:(lb + 1) * LANES]

    halves = []
    for r0 in range(0, tm, FFN_ROWS):
        c = []
        for k in range(FFN_ROWS // LN_ROWS):
            rows = slice(r0 + k * LN_ROWS, r0 + (k + 1) * LN_ROWS)
            c.append(_ln_swish(jnp.concatenate([cbuf[lb, rows, :] for lb in range(nslab)], axis=1), lg_ref, lb_ref))
        rows = slice(r0, r0 + FFN_ROWS)
        halves.append(_ffn_out_phases(ym_ref[rows, :], jnp.concatenate(c, axis=0), x1_ref[rows, :], y_ref, rows,
                                      wout_ref, gmp_ref, g2_ref, wg_ref, wu_ref, wd_ref, gpost_ref, gfin_ref))
    for _ in zip(*halves):
        pass


def _const_spec(shape):
    nd = len(shape)
    return pl.BlockSpec(shape, lambda *_: (0,) * nd, pipeline_mode=pl.Buffered(1))


def _ffn_in_consts(p, tm):
    gbias = jnp.broadcast_to(p["gbias"][:, None], (2 * N_HEADS, tm))
    return [p["ffn1_pre_g"], p["ffn1_wg"], p["ffn1_wu"], p["ffn1_wd"], p["ffn1_post_g"], p["mix_pre_g"],
            p["w_qvo"], p["w_ktg"], p["w_conv"], gbias]


def _ffn_out_consts(p):
    return [p["w_out"], p["mix_post_g"], p["ffn2_pre_g"], p["ffn2_wg"], p["ffn2_wu"], p["ffn2_wd"],
            p["ffn2_post_g"], p["final_g"]]


def _ffn_in_mlstm(x2d, p, batch, seq, tm):
    m = batch * seq
    nt = seq // tm
    ntiles = m // tm
    cur = lambda w: pl.BlockSpec((tm, w), lambda g: (jnp.minimum(g, ntiles - 1), 0))
    prev = lambda w: pl.BlockSpec((tm, w), lambda g: (jnp.maximum(g - 1, 0), 0))
    prev_seq = lambda g: jnp.maximum(g - 1, 0) // nt
    consts = _ffn_in_consts(p, FFN_ROWS)
    return pl.pallas_call(
        functools.partial(_ffn_in_mlstm_body, tm=tm, nt=nt),
        grid=(ntiles + 1,),
        in_specs=[cur(D_MODEL)] + [_const_spec(c.shape) for c in consts],
        out_specs=[cur(D_MODEL), cur(CONV_CH), prev(M_WIDTH),
                   pl.BlockSpec((1, N_HEADS, HEAD_DIM, HEAD_DIM), lambda g: (prev_seq(g), 0, 0, 0)),
                   pl.BlockSpec((1, N_HEADS, 1, HEAD_DIM), lambda g: (prev_seq(g), 0, 0, 0)),
                   pl.BlockSpec((1, N_HEADS, 1, HEAD_DIM), lambda g: (prev_seq(g), 0, 0, 0))],
        out_shape=[jax.ShapeDtypeStruct((m, D_MODEL), F32), jax.ShapeDtypeStruct((m, CONV_CH), F32),
                   jax.ShapeDtypeStruct((m, M_WIDTH), BF16),
                   jax.ShapeDtypeStruct((batch, N_HEADS, HEAD_DIM, HEAD_DIM), F32),
                   jax.ShapeDtypeStruct((batch, N_HEADS, 1, HEAD_DIM), F32),
                   jax.ShapeDtypeStruct((batch, N_HEADS, 1, HEAD_DIM), F32)],
        scratch_shapes=[pltpu.VMEM((2, tm, M_WIDTH), BF16), pltpu.VMEM((2, M_WIDTH, tm), BF16),
                        pltpu.VMEM((2, tm, M_WIDTH), F32), pltpu.VMEM((2, tm, M_WIDTH), F32),
                        pltpu.VMEM((2, 2 * N_HEADS, tm), F32),
                        pltpu.VMEM((N_HEADS, HEAD_DIM, AUG), F32), pltpu.VMEM((N_HEADS, 1, HEAD_DIM), F32)],
        compiler_params=pltpu.CompilerParams(dimension_semantics=("arbitrary",),
                                             vmem_limit_bytes=VMEM_LIMIT_TOKENWISE),
        name="ffn_in_mlstm",
    )(x2d, *consts)


def _conv_ffn_out(u, ym, x1, p, batch, seq, tm):
    m = batch * seq
    nt = seq // tm
    row = lambda w: pl.BlockSpec((tm, w), lambda g: (g, 0))
    consts = [p["conv_w"], p["conv_b"], p["conv_ln_g"], p["conv_ln_b"]] + _ffn_out_consts(p)
    return pl.pallas_call(
        functools.partial(_conv_ffn_out_body, tm=tm, nt=nt),
        grid=(m // tm,),
        in_specs=[row(CONV_CH), row(M_WIDTH), row(D_MODEL)] + [_const_spec(c.shape) for c in consts],
        out_specs=[row(D_MODEL), pl.BlockSpec((1, HIST_PAD, CONV_CH), lambda g: (g // nt, 0, 0))],
        out_shape=[jax.ShapeDtypeStruct((m, D_MODEL), F32), jax.ShapeDtypeStruct((batch, HIST_PAD, CONV_CH), F32)],
        scratch_shapes=[pltpu.VMEM((CONV_CH // LANES, HIST_PAD + tm, LANES), F32),
                        pltpu.VMEM((CONV_CH // LANES, tm, LANES), F32)],
        compiler_params=pltpu.CompilerParams(dimension_semantics=("arbitrary",),
                                             vmem_limit_bytes=VMEM_LIMIT_TOKENWISE),
        name="conv_ffn_out",
    )(u, ym, x1, *consts)


def _ffn_in_body(x_ref, g1_ref, wg_ref, wu_ref, wd_ref, gpost_ref, gmix_ref, wqvo_ref, wktg_ref, wconv_ref,
                 gbias_ref, x1_ref, q_ref, kt_ref, v_ref, og_ref, u_ref, gt_ref):
    outs = _ffn_in_values(x_ref[...], g1_ref, wg_ref, wu_ref, wd_ref, gpost_ref, gmix_ref, wqvo_ref, wktg_ref,
                          wconv_ref, gbias_ref)
    for ref, val in zip((x1_ref, q_ref, kt_ref, v_ref, og_ref, u_ref, gt_ref), outs):
        ref[...] = val


def _ffn_out_body(mix_ref, x1_ref, wout_ref, gmp_ref, g2_ref, wg_ref, wu_ref, wd_ref, gpost_ref, gfin_ref, y_ref):
    y_ref[...] = _ffn_out_values(mix_ref[:, :M_WIDTH], mix_ref[:, M_WIDTH:], x1_ref[...], wout_ref, gmp_ref, g2_ref,
                                 wg_ref, wu_ref, wd_ref, gpost_ref, gfin_ref)


def _mixer_sample_body(q_ref, kt_ref, v_ref, og_ref, u_ref, gt_ref, caug0_ref, m0_ref, hist0_ref,
                       cw_ref, cb_ref, lg_ref, lb_ref, mix_ref, caug_out, m_out, hist_out, ubuf, *, nseq, L):
    R = nseq * L
    r = lax.broadcasted_iota(jnp.int32, (R, R), 0)
    c = lax.broadcasted_iota(jnp.int32, (R, R), 1)
    same = lax.div(r, L) == lax.div(c, L)
    caus = same & (c <= r)
    segtriu = jnp.where(same & (r <= c), 1.0, 0.0).astype(F32)
    segones = jnp.where(same, 1.0, 0.0).astype(F32)
    rowseq = lax.div(lax.broadcasted_iota(jnp.int32, (R, AUG), 0), L)
    ones = jnp.ones((R, HEAD_DIM), F32)
    gt = gt_ref[...]
    b_rows = jnp.dot(gt, segtriu, precision=HIGHEST, preferred_element_type=F32)
    tot_rows = jnp.dot(gt, segones, precision=HIGHEST, preferred_element_type=F32)
    for h in range(N_HEADS):
        hs = slice(h * HEAD_DIM, (h + 1) * HEAD_DIM)
        b_r = b_rows[N_HEADS + h:N_HEADS + h + 1, :]
        a_r = gt[h:h + 1, :] - b_r
        t_r = tot_rows[N_HEADS + h:N_HEADS + h + 1, :]
        arow = jnp.broadcast_to(a_r, (R, R))
        bcol = jnp.broadcast_to(b_r, (R, R)).T
        acol = arow.T
        tcol = jnp.broadcast_to(t_r, (R, R)).T
        mprev = m0_ref[h]
        mprev = jnp.concatenate([mprev, mprev], axis=1)
        dmat = jnp.where(caus, bcol + arow, -jnp.inf)
        m_t = jnp.maximum(bcol + mprev, jnp.max(dmat, axis=1, keepdims=True))
        dend = jnp.where(same, tcol + arow, -jnp.inf)
        m_new = jnp.maximum(tcol + mprev, jnp.max(dend, axis=1, keepdims=True))
        qh = q_ref[:, hs]
        kth = kt_ref[hs, :]
        s = _dot(qh, kth) * jnp.exp(dmat - m_t)
        vaug = jnp.concatenate([v_ref[:, hs], ones], axis=1)
        w_inter = jnp.exp(bcol + mprev - m_t)
        caug0 = [caug0_ref[i, h] for i in range(nseq)]
        qc = jnp.zeros((R, AUG), F32)
        for i in range(nseq):
            qc = jnp.where(rowseq == i, _dot(qh, caug0[i].astype(BF16)), qc)
        sv = _dot(s.astype(BF16), vaug.astype(BF16))
        num = w_inter[:, :HEAD_DIM] * qc[:, :HEAD_DIM] + sv[:, :HEAD_DIM]
        den = w_inter[:, :HEAD_DIM] * qc[:, HEAD_DIM:] + sv[:, HEAD_DIM:]
        hh = num / jnp.maximum(jnp.abs(den), jnp.exp(-m_t[:, :HEAD_DIM]))
        mix_ref[:, hs] = og_ref[:, hs] * hh
        g_state = jnp.exp(tcol + mprev - m_new)
        g_rows = jnp.exp(tcol + acol - m_new)
        gv = g_rows * vaug
        for i in range(nseq):
            gvi = jnp.where(rowseq == i, gv, 0.0).astype(BF16)
            caug_out[i, h] = g_state[i * L:i * L + 1, :] * caug0[i] + _dot(kth, gvi)
            m_out[i, h] = m_new[i * L:i * L + 1, :HEAD_DIM]

    for i in range(nseq):
        ubuf[0:HIST_PAD, :] = hist0_ref[i]
        ubuf[HIST_PAD:HIST_PAD + L, :] = u_ref[i * L:(i + 1) * L, :]
        acc = _conv_taps(ubuf, 0, L, slice(0, CONV_CH), cw_ref, cb_ref)
        mix_ref[i * L:(i + 1) * L, M_WIDTH:] = _ln_swish(acc, lg_ref, lb_ref)
        hist_out[i] = ubuf[L:L + HIST_PAD, :]


def _ffn_in(x2d, p, tm):
    m = x2d.shape[0]
    row = lambda w: pl.BlockSpec((tm, w), lambda i: (i, 0))
    col = lambda h: pl.BlockSpec((h, tm), lambda i: (0, i))
    consts = _ffn_in_consts(p, tm)
    return pl.pallas_call(
        _ffn_in_body,
        grid=(m // tm,),
        in_specs=[row(D_MODEL)] + [_const_spec(c.shape) for c in consts],
        out_specs=[row(D_MODEL), row(M_WIDTH), col(M_WIDTH), row(M_WIDTH), row(M_WIDTH), row(CONV_CH),
                   col(2 * N_HEADS)],
        out_shape=[jax.ShapeDtypeStruct((m, D_MODEL), F32), jax.ShapeDtypeStruct((m, M_WIDTH), BF16),
                   jax.ShapeDtypeStruct((M_WIDTH, m), BF16), jax.ShapeDtypeStruct((m, M_WIDTH), F32),
                   jax.ShapeDtypeStruct((m, M_WIDTH), F32), jax.ShapeDtypeStruct((m, CONV_CH), F32),
                   jax.ShapeDtypeStruct((2 * N_HEADS, m), F32)],
        compiler_params=pltpu.CompilerParams(dimension_semantics=("arbitrary",),
                                             vmem_limit_bytes=VMEM_LIMIT_TOKENWISE),
        name="ffn_in",
    )(x2d, *consts)


def _ffn_out(mix, x1, p, tm):
    m = mix.shape[0]
    row = pl.BlockSpec((tm, D_MODEL), lambda i: (i, 0))
    consts = _ffn_out_consts(p)
    return pl.pallas_call(
        _ffn_out_body,
        grid=(m // tm,),
        in_specs=[row, row] + [_const_spec(c.shape) for c in consts],
        out_specs=row,
        out_shape=jax.ShapeDtypeStruct((m, D_MODEL), F32),
        compiler_params=pltpu.CompilerParams(dimension_semantics=("arbitrary",),
                                             vmem_limit_bytes=VMEM_LIMIT_TOKENWISE),
        name="ffn_out",
    )(mix, x1, *consts)


def _mixer_sample(q, kt, v, og, u, gt, caug0, m0, hist0, p, nseq, L):
    rows = nseq * L
    args = [q, kt, v, og, u, gt, caug0, m0, hist0, p["conv_w"], p["conv_b"], p["conv_ln_g"], p["conv_ln_b"]]
    full = lambda a: pl.BlockSpec(a.shape, lambda i, nd=a.ndim: (0,) * nd)
    out_shape = [jax.ShapeDtypeStruct((rows, D_MODEL), F32),
                 jax.ShapeDtypeStruct((nseq, N_HEADS, HEAD_DIM, AUG), F32),
                 jax.ShapeDtypeStruct((nseq, N_HEADS, 1, HEAD_DIM), F32),
                 jax.ShapeDtypeStruct((nseq, HIST_PAD, CONV_CH), F32)]
    return pl.pallas_call(
        functools.partial(_mixer_sample_body, nseq=nseq, L=L),
        grid=(1,),
        in_specs=[full(a) for a in args],
        out_specs=[full(s) for s in out_shape],
        out_shape=out_shape,
        scratch_shapes=[pltpu.VMEM((HIST_PAD + L, CONV_CH), F32)],
        compiler_params=pltpu.CompilerParams(dimension_semantics=("arbitrary",),
                                             vmem_limit_bytes=VMEM_LIMIT_MIXER),
        name="mixer_sample",
    )(*args)


def _layer_params(l, ffn1_pre_g, ffn1_wg, ffn1_wu, ffn1_wd, ffn1_post_g, mix_pre_g, w_in, b_igate, b_fgate,
                  conv_w, conv_b, conv_ln_g, conv_ln_b, w_out, mix_post_g, ffn2_pre_g, ffn2_wg, ffn2_wu, ffn2_wd,
                  ffn2_post_g, final_g):
    vec = lambda a: a[l].astype(F32).reshape(1, -1)
    w = w_in[l]
    cuts = [0, M_WIDTH, 2 * M_WIDTH, 3 * M_WIDTH, 4 * M_WIDTH, 4 * M_WIDTH + N_HEADS, 4 * M_WIDTH + 2 * N_HEADS,
            4 * M_WIDTH + 2 * N_HEADS + CONV_CH, 4 * M_WIDTH + 2 * N_HEADS + 2 * CONV_CH]
    wq, wk, wv, wo, wi, wf, wcv, wcg = [w[:, a:b] for a, b in zip(cuts[:-1], cuts[1:])]
    return {
        "ffn1_pre_g": vec(ffn1_pre_g), "ffn1_post_g": vec(ffn1_post_g), "mix_pre_g": vec(mix_pre_g),
        "mix_post_g": vec(mix_post_g), "ffn2_pre_g": vec(ffn2_pre_g), "ffn2_post_g": vec(ffn2_post_g),
        "final_g": vec(final_g),
        "ffn1_wg": ffn1_wg[l].astype(BF16), "ffn1_wu": ffn1_wu[l].astype(BF16), "ffn1_wd": ffn1_wd[l].astype(BF16),
        "ffn2_wg": ffn2_wg[l].astype(BF16), "ffn2_wu": ffn2_wu[l].astype(BF16), "ffn2_wd": ffn2_wd[l].astype(BF16),
        "w_qvo": jnp.concatenate([wq, wv, wo], axis=1).astype(BF16),
        "w_ktg": jnp.concatenate([wk, wi, wf], axis=1).T.astype(BF16),
        "w_conv": jnp.concatenate([wcv, wcg], axis=1).astype(BF16),
        "gbias": jnp.concatenate([b_igate[l], b_fgate[l]]).astype(F32),
        "w_out": w_out[l].astype(BF16),
        "conv_w": conv_w[l].astype(F32), "conv_b": vec(conv_b), "conv_ln_g": vec(conv_ln_g),
        "conv_ln_b": vec(conv_ln_b),
    }


def _split_state(c, n, m, hist):
    return c, n[:, :, 0, :], m[:, :, 0, 0], hist[:, HIST_PAD - HIST:, :]


def kernel(x_prompt, x_sample, state_mlstm_C, state_mlstm_n, state_mlstm_m, cache_conv, ffn1_pre_g, ffn1_wg,
           ffn1_wu, ffn1_wd, ffn1_post_g, mix_pre_g, w_in, b_igate, b_fgate, conv_w, conv_b, conv_ln_g, conv_ln_b,
           w_out, mix_post_g, ffn2_pre_g, ffn2_wg, ffn2_wu, ffn2_wd, ffn2_post_g, final_g):
    batch, seq, _ = x_prompt.shape
    nseq, dseq, _ = x_sample.shape
    depth = w_in.shape[0]
    assert seq % PROMPT_TILE == 0 and PROMPT_TILE % CHUNK == 0 and PROMPT_TILE % CONV_ROWS == 0
    assert (nseq * dseq) % TOKEN_TILE == 0 and dseq <= HIST_PAD
    yp = x_prompt.reshape(batch * seq, D_MODEL)
    ys = x_sample.reshape(nseq * dseq, D_MODEL)
    outs_p, outs_s = [], []
    for l in range(depth):
        p = _layer_params(l, ffn1_pre_g, ffn1_wg, ffn1_wu, ffn1_wd, ffn1_post_g, mix_pre_g, w_in, b_igate, b_fgate,
                          conv_w, conv_b, conv_ln_g, conv_ln_b, w_out, mix_post_g, ffn2_pre_g, ffn2_wg, ffn2_wu,
                          ffn2_wd, ffn2_post_g, final_g)
        x1, u, ym, c, n, m = _ffn_in_mlstm(yp, p, batch, seq, 2 * PROMPT_TILE)
        yp, hist = _conv_ffn_out(u, ym, x1, p, batch, seq, 2 * PROMPT_TILE)
        outs_p.append(_split_state(c, n, m, hist))
        x1, q, kt, v, og, u, gt = _ffn_in(ys, p, TOKEN_TILE)
        n0 = jnp.broadcast_to(state_mlstm_n[l].astype(F32)[..., None], (nseq, N_HEADS, HEAD_DIM, HEAD_DIM))
        caug0 = jnp.concatenate([state_mlstm_C[l].astype(F32), n0], axis=-1)
        m0 = jnp.broadcast_to(state_mlstm_m[l].astype(F32).T[:, :, None, None], (N_HEADS, nseq, dseq, HEAD_DIM))
        m0 = m0.reshape(N_HEADS, nseq * dseq, HEAD_DIM)
        hist0 = jnp.pad(cache_conv[l].astype(F32), ((0, 0), (HIST_PAD - HIST, 0), (0, 0)))
        mix, caug, m, hist = _mixer_sample(q, kt, v, og, u, gt, caug0, m0, hist0, p, nseq, dseq)
        ys = _ffn_out(mix, x1, p, TOKEN_TILE)
        outs_s.append(_split_state(caug[..., :HEAD_DIM], caug[..., None, :, HEAD_DIM], m, hist))
    stack = lambda outs, k: jnp.stack([o[k] for o in outs])
    return (yp.reshape(batch, seq, D_MODEL), ys.reshape(nseq, dseq, D_MODEL),
            stack(outs_p, 0), stack(outs_p, 1), stack(outs_p, 2), stack(outs_p, 3),
            stack(outs_s, 0), stack(outs_s, 1), stack(outs_s, 2), stack(outs_s, 3))
```

```python
import functools

import jax
import jax.numpy as jnp
from jax import lax
from jax.experimental import pallas as pl
from jax.experimental.pallas import tpu as pltpu

D_MODEL = 1024
D_FF = 2816
N_HEADS = 4
HEAD_DIM = 128
M_WIDTH = N_HEADS * HEAD_DIM
CONV_CH = 512
CONV_WIDTH = 31
HIST = CONV_WIDTH - 1
HIST_PAD = 32
SUBLANES = 8
EPS = 1e-6
CHUNK = 128
AUG = 2 * HEAD_DIM
CONV_ROWS = 128
LANES = 128
LN_ROWS = 32
NCH = 256

F32 = jnp.float32
BF16 = jnp.bfloat16
HIGHEST = lax.Precision.HIGHEST
NT_DIMS = (((1,), (1,)), ((), ()))

TOKEN_TILE = 256
PROMPT_TILE = 256
FFN_ROWS = 256
PIN_MLSTM = False
VMEM_LIMIT_TOKENWISE = 56 * 1024 * 1024
VMEM_LIMIT_MIXER = 40 * 1024 * 1024


def _rms(x, g):
    return x * lax.rsqrt(jnp.mean(x * x, axis=-1, keepdims=True) + EPS) * g


def _dot(a, b):
    return jnp.dot(a, b, preferred_element_type=F32)


def _swiglu(h, wg_ref, wu_ref, wd_ref):
    a = _dot(h, wg_ref[...])
    b = _dot(h, wu_ref[...])
    s = (a * jax.nn.sigmoid(a) * b).astype(BF16)
    return _dot(s, wd_ref[...])


class _Side:
    def __init__(self, make_pieces, pin=True):
        self.zero = 0
        self.pin = pin
        self.pieces = make_pieces(self)

    def run(self, after, n=1):
        if self.pin:
            bits = pltpu.bitcast(after[after.shape[0] - SUBLANES:, 0:LANES], jnp.int32)
            self.zero = lax.shift_right_logical(lax.shift_right_logical(bits, 16), 16)[0, 0]
        if n is None:
            for _ in self.pieces:
                pass
        else:
            for _ in range(n):
                next(self.pieces, None)


def _swiglu_chunked(h_s, s_s, wg_ref, wu_ref, wd_ref, side, chunk, after_gating, down_pieces=1):
    for c0 in range(0, D_FF, chunk):
        cols = slice(c0, min(c0 + chunk, D_FF))
        a = _dot(h_s[...], wg_ref[:, cols])
        b = _dot(h_s[...], wu_ref[:, cols])
        s = a * jax.nn.sigmoid(a) * b
        s_s[:, cols] = s.astype(BF16)
        for t0 in range(0, cols.stop - cols.start, NCH):
            side.run((s if after_gating else a)[:, t0:t0 + LANES])
    d = []
    for c0 in range(0, D_MODEL, chunk):
        d.append(_dot(s_s[...], wd_ref[:, c0:c0 + chunk]))
        for t0 in range(0, chunk, NCH):
            side.run(d[-1][:, t0:t0 + LANES], down_pieces)
    return jnp.concatenate(d, axis=1)


def _gate_rows(pre, gbias_ref):
    g = pre + gbias_ref[...]
    row = lax.broadcasted_iota(jnp.int32, g.shape, 0)
    return jnp.where(row < N_HEADS, g, jax.nn.log_sigmoid(g))


def _ffn_in_values(x, g1_ref, wg_ref, wu_ref, wd_ref, gpost_ref, gmix_ref, wqvo_ref, wktg_ref, wconv_ref, gbias_ref):
    h = _rms(x, g1_ref[...]).astype(BF16)
    d = _swiglu(h, wg_ref, wu_ref, wd_ref)
    x1 = x + 0.5 * _rms(d, gpost_ref[...])
    h2 = _rms(x1, gmix_ref[...]).astype(BF16)
    qvo = _dot(h2, wqvo_ref[...])
    q = qvo[:, :M_WIDTH].astype(BF16)
    v = qvo[:, M_WIDTH:2 * M_WIDTH]
    og = jax.nn.sigmoid(qvo[:, 2 * M_WIDTH:])
    ktg = lax.dot_general(wktg_ref[...], h2, NT_DIMS, preferred_element_type=F32)
    kt = ktg[:M_WIDTH] * (HEAD_DIM ** -0.5)
    cc = _dot(h2, wconv_ref[...])
    u = cc[:, :CONV_CH] * jax.nn.sigmoid(cc[:, CONV_CH:])
    return x1, q, kt.astype(BF16), v, og, u, _gate_rows(ktg[M_WIDTH:], gbias_ref)


def _ffn_out_values(ym, c, x1, wout_ref, gmp_ref, g2_ref, wg_ref, wu_ref, wd_ref, gpost_ref, gfin_ref):
    o = _dot(ym.astype(BF16), wout_ref[0:M_WIDTH, :]) + _dot(c.astype(BF16), wout_ref[M_WIDTH:, :])
    x2 = x1 + _rms(o, gmp_ref[...])
    h = _rms(x2, g2_ref[...]).astype(BF16)
    d = _swiglu(h, wg_ref, wu_ref, wd_ref)
    x3 = x2 + 0.5 * _rms(d, gpost_ref[...])
    return _rms(x3, gfin_ref[...])


def _ffn_out_phases(ym, c, x1, y_ref, rows, wout_ref, gmp_ref, g2_ref, wg_ref, wu_ref, wd_ref, gpost_ref, gfin_ref):
    o = _dot(ym.astype(BF16), wout_ref[0:M_WIDTH, :]) + _dot(c.astype(BF16), wout_ref[M_WIDTH:, :])
    yield o
    x2 = x1 + _rms(o, gmp_ref[...])
    h = _rms(x2, g2_ref[...]).astype(BF16)
    a = _dot(h, wg_ref[...])
    yield a
    b = _dot(h, wu_ref[...])
    yield b
    s = (a * jax.nn.sigmoid(a) * b).astype(BF16)
    d = _dot(s, wd_ref[...])
    yield d
    x3 = x2 + 0.5 * _rms(d, gpost_ref[...])
    y = _rms(x3, gfin_ref[...])
    y_ref[rows, :] = y
    yield y


def _conv_taps(ubuf, row0, nrows, lanes, cw_ref, cb_ref):
    acc = jnp.broadcast_to(cb_ref[:, lanes], (nrows, lanes.stop - lanes.start))
    first = HIST_PAD - HIST
    for res in range(SUBLANES):
        taps = [j for j in range(CONV_WIDTH) if (first + j) % SUBLANES == res]
        lo = (first + taps[0]) // SUBLANES * SUBLANES
        hi = (first + taps[-1]) // SUBLANES * SUBLANES
        win = ubuf[row0 + lo:row0 + hi + nrows + (SUBLANES if res else 0), lanes]
        if res:
            win = pltpu.roll(win, win.shape[0] - res, 0)
        for j in taps:
            off = first + j - res - lo
            acc = acc + win[off:off + nrows, :] * cw_ref[j:j + 1, lanes]
    return acc


def _conv_taps_interleaved(uslab, row0, nrows, lanes, cw_ref, cb_ref):
    first = HIST_PAD - HIST
    half = nrows // 2
    acc = [jnp.broadcast_to(cb_ref[:, lanes], (half, LANES))] * 2
    for j in range(CONV_WIDTH):
        for parity in range(2):
            x = uslab[pl.ds(row0 + first + j + parity, half, stride=2), :]
            acc[parity] = acc[parity] + x * cw_ref[j:j + 1, lanes]
    return acc


def _ln_swish(acc, lg_ref, lb_ref):
    mu = jnp.mean(acc, axis=-1, keepdims=True)
    xc = acc - mu
    var = jnp.mean(xc * xc, axis=-1, keepdims=True)
    y = xc * lax.rsqrt(var + EPS) * lg_ref[...] + lb_ref[...]
    return y * jax.nn.sigmoid(y)


def _mlstm_gates(b_r, a_r, caus, qh, kth):
    L = b_r.shape[1]
    arow = jnp.broadcast_to(a_r, (L, L))
    bcol = jnp.broadcast_to(b_r, (L, L)).T
    acol = arow.T
    dmat = jnp.where(caus, bcol + arow, -jnp.inf)
    return dict(bcol=bcol, acol=acol, dmat=dmat, rowmax=jnp.max(dmat, axis=1, keepdims=True), s_raw=_dot(qh, kth))


def _mlstm_scores(st, mprev, vh):
    L = st["bcol"].shape[0]
    m_t = jnp.maximum(st["bcol"] + mprev, st["rowmax"])
    s = st["s_raw"] * jnp.exp(st["dmat"] - m_t)
    vaug = jnp.concatenate([vh, jnp.ones_like(vh)], axis=1)
    m_new = m_t[L - 1:L, :]
    b_last = st["bcol"][L - 1:L, :]
    g_rows = jnp.exp(b_last + st["acol"] - m_new)
    new = dict(m_t=m_t, sv=_dot(s.astype(BF16), vaug.astype(BF16)), w_inter=jnp.exp(st["bcol"] + mprev - m_t),
               g_state=jnp.exp(b_last + mprev - m_new),
               gv=(jnp.concatenate([g_rows, g_rows], axis=1) * vaug).astype(BF16))
    st.clear()
    st.update(new)
    return m_new


def _mlstm_output(st, qh, kth, caug):
    m_t, sv, w_inter, g_state = st["m_t"], st["sv"], st["w_inter"], st["g_state"]
    qc = _dot(qh, caug.astype(BF16))
    caug_new = jnp.concatenate([g_state, g_state], axis=1) * caug + _dot(kth, st["gv"])
    num = w_inter * qc[:, :HEAD_DIM] + sv[:, :HEAD_DIM]
    den = w_inter * qc[:, HEAD_DIM:] + sv[:, HEAD_DIM:]
    hh = num / jnp.maximum(jnp.abs(den), jnp.exp(-m_t))
    return hh, caug_new


def _store_state(c_out, n_out, i, h, caug):
    c_out[i, h] = caug[:, :HEAD_DIM]
    n_out[i, h] = caug[:, HEAD_DIM:].T[0:1, :]


def _ffn_in_mlstm_body(x_ref, g1_ref, wg_ref, wu_ref, wd_ref, gpost_ref, gmix_ref, wqvo_ref, wktg_ref, wconv_ref,
                       gbias_ref, x1_ref, u_ref, ym_ref, c_out, n_out, m_out,
                       q_st, kt_st, v_st, og_st, g_st, caug_s, m_s, *, tm, nt):
    g = pl.program_id(0)
    wslot = lax.rem(g, 2)
    rslot = 1 - wslot
    L = CHUNK

    @pl.when(g == 0)
    def _init():
        for st in (q_st, kt_st, v_st, og_st, g_st, caug_s, m_s):
            st[...] = jnp.zeros_like(st)

    def mlstm_pieces(side):
        fresh = lax.rem(g - 1, nt) == 0
        r = lax.broadcasted_iota(jnp.int32, (L, L), 0)
        c = lax.broadcasted_iota(jnp.int32, (L, L), 1)
        caus = c <= r
        triu = (r <= c).astype(BF16)
        nck = tm // L
        stages = {}

        def stage1(ck):
            rows, slot = slice(ck * L, (ck + 1) * L), rslot + side.zero
            gt = g_st[slot, :, rows]
            hi = gt.astype(BF16)
            mid = (gt - hi.astype(F32)).astype(BF16)
            lo = (gt - hi.astype(F32) - mid.astype(F32)).astype(BF16)
            parts = _dot(jnp.concatenate([hi, mid, lo], axis=0), triu)
            b_rows = parts[0:2 * N_HEADS] + parts[2 * N_HEADS:4 * N_HEADS] + parts[4 * N_HEADS:]
            for h in range(N_HEADS):
                hs = slice(h * HEAD_DIM, (h + 1) * HEAD_DIM)
                b_r = b_rows[N_HEADS + h:N_HEADS + h + 1, :]
                stages[ck, h] = _mlstm_gates(b_r, gt[h:h + 1, :] - b_r, caus, q_st[slot, rows, hs],
                                             kt_st[slot, hs, rows])

        def stage2(ck):
            rows, slot = slice(ck * L, (ck + 1) * L), rslot + side.zero
            for h in range(N_HEADS):
                mprev = m_s[h + side.zero]
                if ck == 0:
                    mprev = jnp.where(fresh, 0.0, mprev)
                m_new = _mlstm_scores(stages[ck, h], mprev, v_st[slot, rows, slice(h * HEAD_DIM, (h + 1) * HEAD_DIM)])
                m_s[h] = m_new
                if ck == nck - 1:
                    m_out[0, h] = m_new

        def stage3(ck):
            rows, slot = slice(ck * L, (ck + 1) * L), rslot + side.zero
            for h in range(N_HEADS):
                hs = slice(h * HEAD_DIM, (h + 1) * HEAD_DIM)
                caug = caug_s[h + side.zero]
                if ck == 0:
                    caug = jnp.where(fresh, 0.0, caug)
                hh, caug = _mlstm_output(stages.pop((ck, h)), q_st[slot, rows, hs], kt_st[slot, hs, rows], caug)
                ym_ref[rows, hs] = (og_st[slot, rows, hs] * hh).astype(BF16)
                caug_s[h] = caug
                if ck == nck - 1:
                    _store_state(c_out, n_out, 0, h, caug)

        for t in range(nck + 2):
            for lag, stage in enumerate((stage1, stage2, stage3)):
                if 0 <= t - lag < nck:
                    stage(t - lag)
            yield

    side = _Side(mlstm_pieces, pin=PIN_MLSTM)

    blocks = [_ffn_in_block(x_ref, slice(r0, r0 + FFN_ROWS), wslot, g1_ref, wg_ref, wu_ref, wd_ref, gpost_ref,
                            gmix_ref, wqvo_ref, wktg_ref, wconv_ref, gbias_ref, x1_ref, u_ref, q_st, kt_st, v_st,
                            og_st, g_st) for r0 in range(0, tm, FFN_ROWS)]
    side.run(None, 1)
    last = None
    while blocks:
        for block in list(blocks):
            token = next(block, None)
            if token is None:
                blocks.remove(block)
            else:
                side.run(token, 1)
                last = token
    side.run(last, None)


def _ffn_in_block(x_ref, rows, wslot, g1_ref, wg_ref, wu_ref, wd_ref, gpost_ref, gmix_ref, wqvo_ref, wktg_ref, wconv_ref,
                  gbias_ref, x1_ref, u_ref, q_st, kt_st, v_st, og_st, g_st):
    x = x_ref[rows, :]
    h = _rms(x, g1_ref[...]).astype(BF16)
    a = _dot(h, wg_ref[...])
    yield a
    b = _dot(h, wu_ref[...])
    yield b
    s = (a * jax.nn.sigmoid(a) * b).astype(BF16)
    d = _dot(s, wd_ref[...])
    yield d
    x1 = x + 0.5 * _rms(d, gpost_ref[...])
    x1_ref[rows, :] = x1
    h2 = _rms(x1, gmix_ref[...]).astype(BF16)
    qvo = _dot(h2, wqvo_ref[...])
    q_st[wslot, rows, :] = qvo[:, :M_WIDTH].astype(BF16)
    v_st[wslot, rows, :] = qvo[:, M_WIDTH:2 * M_WIDTH]
    og_st[wslot, rows, :] = jax.nn.sigmoid(qvo[:, 2 * M_WIDTH:])
    yield qvo
    ktg = lax.dot_general(wktg_ref[...], h2, NT_DIMS, preferred_element_type=F32)
    kt_st[wslot, :, rows] = (ktg[:M_WIDTH] * (HEAD_DIM ** -0.5)).astype(BF16)
    g_st[wslot, :, rows] = _gate_rows(ktg[M_WIDTH:], gbias_ref)
    yield ktg
    cc = _dot(h2, wconv_ref[...])
    u_ref[rows, :] = cc[:, :CONV_CH] * jax.nn.sigmoid(cc[:, CONV_CH:])
    yield cc


def _conv_ffn_out_body(u_ref, ym_ref, x1_ref, cw_ref, cb_ref, lg_ref, lb_ref, wout_ref, gmp_ref, g2_ref,
                       wg_ref, wu_ref, wd_ref, gpost_ref, gfin_ref, y_ref, hist_out, ubuf, cbuf, *, tm, nt):
    nslab = CONV_CH // LANES

    @pl.when(lax.rem(pl.program_id(0), nt) == 0)
    def _new_sequence():
        ubuf[:, 0:HIST_PAD, :] = jnp.zeros((nslab, HIST_PAD, LANES), F32)

    for lb in range(nslab):
        ubuf[lb, HIST_PAD:HIST_PAD + tm, :] = u_ref[:, lb * LANES:(lb + 1) * LANES]
    for k in range(tm // CONV_ROWS):
        for lb in range(nslab):
            acc = _conv_taps_interleaved(ubuf.at[lb], k * CONV_ROWS, CONV_ROWS, slice(lb * LANES, (lb + 1) * LANES),
                                         cw_ref, cb_ref)
            for parity in range(2):
                cbuf[lb, pl.ds(k * CONV_ROWS + parity, CONV_ROWS // 2, stride=2), :] = acc[parity]
    tail = jnp.concatenate([ubuf[lb, tm:tm + HIST_PAD, :] for lb in range(nslab)], axis=1)
    hist_out[0] = tail
    for lb in range(nslab):
        ubuf[lb, 0:HIST_PAD, :] = tail[:, lb * LANES:(lb + 1) * LANES]

    halves = []
    for r0 in range(0, tm, FFN_ROWS):
        c = []
        for k in range(FFN_ROWS // LN_ROWS):
            rows = slice(r0 + k * LN_ROWS, r0 + (k + 1) * LN_ROWS)
            c.append(_ln_swish(jnp.concatenate([cbuf[lb, rows, :] for lb in range(nslab)], axis=1), lg_ref, lb_ref))
        rows = slice(r0, r0 + FFN_ROWS)
        halves.append(_ffn_out_phases(ym_ref[rows, :], jnp.concatenate(c, axis=0), x1_ref[rows, :], y_ref, rows,
                                      wout_ref, gmp_ref, g2_ref, wg_ref, wu_ref, wd_ref, gpost_ref, gfin_ref))
    for _ in zip(*halves):
        pass


def _const_spec(shape):
    nd = len(shape)
    return pl.BlockSpec(shape, lambda *_: (0,) * nd, pipeline_mode=pl.Buffered(1))


def _ffn_in_consts(p, tm):
    gbias = jnp.broadcast_to(p["gbias"][:, None], (2 * N_HEADS, tm))
    return [p["ffn1_pre_g"], p["ffn1_wg"], p["ffn1_wu"], p["ffn1_wd"], p["ffn1_post_g"], p["mix_pre_g"],
            p["w_qvo"], p["w_ktg"], p["w_conv"], gbias]


def _ffn_out_consts(p):
    return [p["w_out"], p["mix_post_g"], p["ffn2_pre_g"], p["ffn2_wg"], p["ffn2_wu"], p["ffn2_wd"],
            p["ffn2_post_g"], p["final_g"]]


def _ffn_in_mlstm(x2d, p, batch, seq, tm):
    m = batch * seq
    nt = seq // tm
    ntiles = m // tm
    cur = lambda w: pl.BlockSpec((tm, w), lambda g: (jnp.minimum(g, ntiles - 1), 0))
    prev = lambda w: pl.BlockSpec((tm, w), lambda g: (jnp.maximum(g - 1, 0), 0))
    prev_seq = lambda g: jnp.maximum(g - 1, 0) // nt
    consts = _ffn_in_consts(p, FFN_ROWS)
    return pl.pallas_call(
        functools.partial(_ffn_in_mlstm_body, tm=tm, nt=nt),
        grid=(ntiles + 1,),
        in_specs=[cur(D_MODEL)] + [_const_spec(c.shape) for c in consts],
        out_specs=[cur(D_MODEL), cur(CONV_CH), prev(M_WIDTH),
                   pl.BlockSpec((1, N_HEADS, HEAD_DIM, HEAD_DIM), lambda g: (prev_seq(g), 0, 0, 0)),
                   pl.BlockSpec((1, N_HEADS, 1, HEAD_DIM), lambda g: (prev_seq(g), 0, 0, 0)),
                   pl.BlockSpec((1, N_HEADS, 1, HEAD_DIM), lambda g: (prev_seq(g), 0, 0, 0))],
        out_shape=[jax.ShapeDtypeStruct((m, D_MODEL), F32), jax.ShapeDtypeStruct((m, CONV_CH), F32),
                   jax.ShapeDtypeStruct((m, M_WIDTH), BF16),
                   jax.ShapeDtypeStruct((batch, N_HEADS, HEAD_DIM, HEAD_DIM), F32),
                   jax.ShapeDtypeStruct((batch, N_HEADS, 1, HEAD_DIM), F32),
                   jax.ShapeDtypeStruct((batch, N_HEADS, 1, HEAD_DIM), F32)],
        scratch_shapes=[pltpu.VMEM((2, tm, M_WIDTH), BF16), pltpu.VMEM((2, M_WIDTH, tm), BF16),
                        pltpu.VMEM((2, tm, M_WIDTH), F32), pltpu.VMEM((2, tm, M_WIDTH), F32),
                        pltpu.VMEM((2, 2 * N_HEADS, tm), F32),
                        pltpu.VMEM((N_HEADS, HEAD_DIM, AUG), F32), pltpu.VMEM((N_HEADS, 1, HEAD_DIM), F32)],
        compiler_params=pltpu.CompilerParams(dimension_semantics=("arbitrary",),
                                             vmem_limit_bytes=VMEM_LIMIT_TOKENWISE),
        name="ffn_in_mlstm",
    )(x2d, *consts)


def _conv_ffn_out(u, ym, x1, p, batch, seq, tm):
    m = batch * seq
    nt = seq // tm
    row = lambda w: pl.BlockSpec((tm, w), lambda g: (g, 0))
    consts = [p["conv_w"], p["conv_b"], p["conv_ln_g"], p["conv_ln_b"]] + _ffn_out_consts(p)
    return pl.pallas_call(
        functools.partial(_conv_ffn_out_body, tm=tm, nt=nt),
        grid=(m // tm,),
        in_specs=[row(CONV_CH), row(M_WIDTH), row(D_MODEL)] + [_const_spec(c.shape) for c in consts],
        out_specs=[row(D_MODEL), pl.BlockSpec((1, HIST_PAD, CONV_CH), lambda g: (g // nt, 0, 0))],
        out_shape=[jax.ShapeDtypeStruct((m, D_MODEL), F32), jax.ShapeDtypeStruct((batch, HIST_PAD, CONV_CH), F32)],
        scratch_shapes=[pltpu.VMEM((CONV_CH // LANES, HIST_PAD + tm, LANES), F32),
                        pltpu.VMEM((CONV_CH // LANES, tm, LANES), F32)],
        compiler_params=pltpu.CompilerParams(dimension_semantics=("arbitrary",),
                                             vmem_limit_bytes=VMEM_LIMIT_TOKENWISE),
        name="conv_ffn_out",
    )(u, ym, x1, *consts)


def _ffn_in_body(x_ref, g1_ref, wg_ref, wu_ref, wd_ref, gpost_ref, gmix_ref, wqvo_ref, wktg_ref, wconv_ref,
                 gbias_ref, x1_ref, q_ref, kt_ref, v_ref, og_ref, u_ref, gt_ref):
    outs = _ffn_in_values(x_ref[...], g1_ref, wg_ref, wu_ref, wd_ref, gpost_ref, gmix_ref, wqvo_ref, wktg_ref,
                          wconv_ref, gbias_ref)
    for ref, val in zip((x1_ref, q_ref, kt_ref, v_ref, og_ref, u_ref, gt_ref), outs):
        ref[...] = val


def _ffn_out_body(mix_ref, x1_ref, wout_ref, gmp_ref, g2_ref, wg_ref, wu_ref, wd_ref, gpost_ref, gfin_ref, y_ref):
    y_ref[...] = _ffn_out_values(mix_ref[:, :M_WIDTH], mix_ref[:, M_WIDTH:], x1_ref[...], wout_ref, gmp_ref, g2_ref,
                                 wg_ref, wu_ref, wd_ref, gpost_ref, gfin_ref)


def _mixer_sample_body(q_ref, kt_ref, v_ref, og_ref, u_ref, gt_ref, c0_ref, n0_ref, m0_ref, hist0_ref,
                       cw_ref, cb_ref, lg_ref, lb_ref, mix_ref, c_out, n_out, m_out, hist_out, ubuf, *, nseq, L):
    R = nseq * L
    r = lax.broadcasted_iota(jnp.int32, (R, R), 0)
    c = lax.broadcasted_iota(jnp.int32, (R, R), 1)
    same = lax.div(r, L) == lax.div(c, L)
    caus = same & (c <= r)
    segtriu = jnp.where(same & (r <= c), 1.0, 0.0).astype(F32)
    segones = jnp.where(same, 1.0, 0.0).astype(F32)
    rowseq = lax.div(lax.broadcasted_iota(jnp.int32, (R, AUG), 0), L)
    ones = jnp.ones((R, HEAD_DIM), F32)
    gt = gt_ref[...]
    b_rows = jnp.dot(gt, segtriu, precision=HIGHEST, preferred_element_type=F32)
    tot_rows = jnp.dot(gt, segones, precision=HIGHEST, preferred_element_type=F32)
    for h in range(N_HEADS):
        hs = slice(h * HEAD_DIM, (h + 1) * HEAD_DIM)
        b_r = b_rows[N_HEADS + h:N_HEADS + h + 1, :]
        a_r = gt[h:h + 1, :] - b_r
        t_r = tot_rows[N_HEADS + h:N_HEADS + h + 1, :]
        arow = jnp.broadcast_to(a_r, (R, R))
        bcol = jnp.broadcast_to(b_r, (R, R)).T
        acol = arow.T
        tcol = jnp.broadcast_to(t_r, (R, R)).T
        mprev = m0_ref[h]
        mprev = jnp.concatenate([mprev, mprev], axis=1)
        dmat = jnp.where(caus, bcol + arow, -jnp.inf)
        m_t = jnp.maximum(bcol + mprev, jnp.max(dmat, axis=1, keepdims=True))
        dend = jnp.where(same, tcol + arow, -jnp.inf)
        m_new = jnp.maximum(tcol + mprev, jnp.max(dend, axis=1, keepdims=True))
        qh = q_ref[:, hs]
        kth = kt_ref[hs, :]
        s = _dot(qh, kth) * jnp.exp(dmat - m_t)
        vaug = jnp.concatenate([v_ref[:, hs], ones], axis=1)
        w_inter = jnp.exp(bcol + mprev - m_t)
        caug0 = [jnp.concatenate([c0_ref[0, i, h], jnp.broadcast_to(n0_ref[0, i, h:h + 1, :], (HEAD_DIM, HEAD_DIM)).T],
                                 axis=1) for i in range(nseq)]
        qc = jnp.zeros((R, AUG), F32)
        for i in range(nseq):
            qc = jnp.where(rowseq == i, _dot(qh, caug0[i].astype(BF16)), qc)
        sv = _dot(s.astype(BF16), vaug.astype(BF16))
        num = w_inter[:, :HEAD_DIM] * qc[:, :HEAD_DIM] + sv[:, :HEAD_DIM]
        den = w_inter[:, :HEAD_DIM] * qc[:, HEAD_DIM:] + sv[:, HEAD_DIM:]
        hh = num / jnp.maximum(jnp.abs(den), jnp.exp(-m_t[:, :HEAD_DIM]))
        mix_ref[:, hs] = og_ref[:, hs] * hh
        g_state = jnp.exp(tcol + mprev - m_new)
        g_rows = jnp.exp(tcol + acol - m_new)
        gv = g_rows * vaug
        for i in range(nseq):
            gvi = jnp.where(rowseq == i, gv, 0.0).astype(BF16)
            caug = g_state[i * L:i * L + 1, :] * caug0[i] + _dot(kth, gvi)
            c_out[0, i, h] = caug[:, :HEAD_DIM]
            n_out[0, i, h:h + 1, :] = caug[:, HEAD_DIM:].T[0:1, :]
            m_out[i, h] = m_new[i * L:i * L + 1, :HEAD_DIM]

    for i in range(nseq):
        ubuf[0:HIST_PAD, :] = hist0_ref[i]
        ubuf[HIST_PAD:HIST_PAD + L, :] = u_ref[i * L:(i + 1) * L, :]
        acc = _conv_taps(ubuf, 0, L, slice(0, CONV_CH), cw_ref, cb_ref)
        mix_ref[i * L:(i + 1) * L, M_WIDTH:] = _ln_swish(acc, lg_ref, lb_ref)
        hist_out[i] = ubuf[L:L + HIST_PAD, :]


def _ffn_in(x2d, p, tm):
    m = x2d.shape[0]
    row = lambda w: pl.BlockSpec((tm, w), lambda i: (i, 0))
    col = lambda h: pl.BlockSpec((h, tm), lambda i: (0, i))
    consts = _ffn_in_consts(p, tm)
    return pl.pallas_call(
        _ffn_in_body,
        grid=(m // tm,),
        in_specs=[row(D_MODEL)] + [_const_spec(c.shape) for c in consts],
        out_specs=[row(D_MODEL), row(M_WIDTH), col(M_WIDTH), row(M_WIDTH), row(M_WIDTH), row(CONV_CH),
                   col(2 * N_HEADS)],
        out_shape=[jax.ShapeDtypeStruct((m, D_MODEL), F32), jax.ShapeDtypeStruct((m, M_WIDTH), BF16),
                   jax.ShapeDtypeStruct((M_WIDTH, m), BF16), jax.ShapeDtypeStruct((m, M_WIDTH), F32),
                   jax.ShapeDtypeStruct((m, M_WIDTH), F32), jax.ShapeDtypeStruct((m, CONV_CH), F32),
                   jax.ShapeDtypeStruct((2 * N_HEADS, m), F32)],
        compiler_params=pltpu.CompilerParams(dimension_semantics=("arbitrary",),
                                             vmem_limit_bytes=VMEM_LIMIT_TOKENWISE),
        name="ffn_in",
    )(x2d, *consts)


def _ffn_out(mix, x1, p, tm):
    m = mix.shape[0]
    row = pl.BlockSpec((tm, D_MODEL), lambda i: (i, 0))
    consts = _ffn_out_consts(p)
    return pl.pallas_call(
        _ffn_out_body,
        grid=(m // tm,),
        in_specs=[row, row] + [_const_spec(c.shape) for c in consts],
        out_specs=row,
        out_shape=jax.ShapeDtypeStruct((m, D_MODEL), F32),
        compiler_params=pltpu.CompilerParams(dimension_semantics=("arbitrary",),
                                             vmem_limit_bytes=VMEM_LIMIT_TOKENWISE),
        name="ffn_out",
    )(mix, x1, *consts)


def _mixer_sample(q, kt, v, og, u, gt, c_all, n_all, layer, m0, hist0, p, nseq, L):
    rows = nseq * L
    args = [q, kt, v, og, u, gt, c_all, n_all, m0, hist0, p["conv_w"], p["conv_b"], p["conv_ln_g"], p["conv_ln_b"]]
    full = lambda a: pl.BlockSpec(a.shape, lambda i, nd=a.ndim: (0,) * nd)
    of_layer = lambda a: pl.BlockSpec((1,) + a.shape[1:], lambda i, nd=a.ndim: (layer,) + (0,) * (nd - 1))
    out_shape = [jax.ShapeDtypeStruct((rows, D_MODEL), F32),
                 jax.ShapeDtypeStruct((1, nseq, N_HEADS, HEAD_DIM, HEAD_DIM), F32),
                 jax.ShapeDtypeStruct((1, nseq, N_HEADS, HEAD_DIM), F32),
                 jax.ShapeDtypeStruct((nseq, N_HEADS, 1, HEAD_DIM), F32),
                 jax.ShapeDtypeStruct((nseq, HIST_PAD, CONV_CH), F32)]
    return pl.pallas_call(
        functools.partial(_mixer_sample_body, nseq=nseq, L=L),
        grid=(1,),
        in_specs=[of_layer(a) if a is c_all or a is n_all else full(a) for a in args],
        out_specs=[full(s) for s in out_shape],
        out_shape=out_shape,
        scratch_shapes=[pltpu.VMEM((HIST_PAD + L, CONV_CH), F32)],
        compiler_params=pltpu.CompilerParams(dimension_semantics=("arbitrary",),
                                             vmem_limit_bytes=VMEM_LIMIT_MIXER),
        name="mixer_sample",
    )(*args)


def _layer_params(l, ffn1_pre_g, ffn1_wg, ffn1_wu, ffn1_wd, ffn1_post_g, mix_pre_g, w_in, b_igate, b_fgate,
                  conv_w, conv_b, conv_ln_g, conv_ln_b, w_out, mix_post_g, ffn2_pre_g, ffn2_wg, ffn2_wu, ffn2_wd,
                  ffn2_post_g, final_g):
    vec = lambda a: a[l].astype(F32).reshape(1, -1)
    w = w_in[l]
    cuts = [0, M_WIDTH, 2 * M_WIDTH, 3 * M_WIDTH, 4 * M_WIDTH, 4 * M_WIDTH + N_HEADS, 4 * M_WIDTH + 2 * N_HEADS,
            4 * M_WIDTH + 2 * N_HEADS + CONV_CH, 4 * M_WIDTH + 2 * N_HEADS + 2 * CONV_CH]
    wq, wk, wv, wo, wi, wf, wcv, wcg = [w[:, a:b] for a, b in zip(cuts[:-1], cuts[1:])]
    return {
        "ffn1_pre_g": vec(ffn1_pre_g), "ffn1_post_g": vec(ffn1_post_g), "mix_pre_g": vec(mix_pre_g),
        "mix_post_g": vec(mix_post_g), "ffn2_pre_g": vec(ffn2_pre_g), "ffn2_post_g": vec(ffn2_post_g),
        "final_g": vec(final_g),
        "ffn1_wg": ffn1_wg[l].astype(BF16), "ffn1_wu": ffn1_wu[l].astype(BF16), "ffn1_wd": ffn1_wd[l].astype(BF16),
        "ffn2_wg": ffn2_wg[l].astype(BF16), "ffn2_wu": ffn2_wu[l].astype(BF16), "ffn2_wd": ffn2_wd[l].astype(BF16),
        "w_qvo": jnp.concatenate([wq, wv, wo], axis=1).astype(BF16),
        "w_ktg": jnp.concatenate([wk, wi, wf], axis=1).T.astype(BF16),
        "w_conv": jnp.concatenate([wcv, wcg], axis=1).astype(BF16),
        "gbias": jnp.concatenate([b_igate[l], b_fgate[l]]).astype(F32),
        "w_out": w_out[l].astype(BF16),
        "conv_w": conv_w[l].astype(F32), "conv_b": vec(conv_b), "conv_ln_g": vec(conv_ln_g),
        "conv_ln_b": vec(conv_ln_b),
    }


def _split_state(c, n, m, hist):
    return c, n[:, :, 0, :], m[:, :, 0, 0], hist[:, HIST_PAD - HIST:, :]


def kernel(x_prompt, x_sample, state_mlstm_C, state_mlstm_n, state_mlstm_m, cache_conv, ffn1_pre_g, ffn1_wg,
           ffn1_wu, ffn1_wd, ffn1_post_g, mix_pre_g, w_in, b_igate, b_fgate, conv_w, conv_b, conv_ln_g, conv_ln_b,
           w_out, mix_post_g, ffn2_pre_g, ffn2_wg, ffn2_wu, ffn2_wd, ffn2_post_g, final_g):
    batch, seq, _ = x_prompt.shape
    nseq, dseq, _ = x_sample.shape
    depth = w_in.shape[0]
    assert seq % PROMPT_TILE == 0 and PROMPT_TILE % CHUNK == 0 and PROMPT_TILE % CONV_ROWS == 0
    assert (nseq * dseq) % TOKEN_TILE == 0 and dseq <= HIST_PAD
    yp = x_prompt.reshape(batch * seq, D_MODEL)
    ys = x_sample.reshape(nseq * dseq, D_MODEL)
    outs_p, outs_s = [], []
    for l in range(depth):
        p = _layer_params(l, ffn1_pre_g, ffn1_wg, ffn1_wu, ffn1_wd, ffn1_post_g, mix_pre_g, w_in, b_igate, b_fgate,
                          conv_w, conv_b, conv_ln_g, conv_ln_b, w_out, mix_post_g, ffn2_pre_g, ffn2_wg, ffn2_wu,
                          ffn2_wd, ffn2_post_g, final_g)
        x1, u, ym, c, n, m = _ffn_in_mlstm(yp, p, batch, seq, 2 * PROMPT_TILE)
        yp, hist = _conv_ffn_out(u, ym, x1, p, batch, seq, 2 * PROMPT_TILE)
        outs_p.append(_split_state(c, n, m, hist))
        x1, q, kt, v, og, u, gt = _ffn_in(ys, p, TOKEN_TILE)
        m0 = jnp.broadcast_to(state_mlstm_m[l].astype(F32).T[:, :, None, None], (N_HEADS, nseq, dseq, HEAD_DIM))
        m0 = m0.reshape(N_HEADS, nseq * dseq, HEAD_DIM)
        hist0 = jnp.pad(cache_conv[l].astype(F32), ((0, 0), (HIST_PAD - HIST, 0), (0, 0)))
        mix, c, n, m, hist = _mixer_sample(q, kt, v, og, u, gt, state_mlstm_C.astype(F32), state_mlstm_n.astype(F32),
                                          l, m0, hist0, p, nseq, dseq)
        ys = _ffn_out(mix, x1, p, TOKEN_TILE)
        outs_s.append((c, n, m[:, :, 0, 0], hist[:, HIST_PAD - HIST:, :]))
    stack = lambda outs, k: jnp.stack([o[k] for o in outs])
    cat = lambda outs, k: outs[0][k] if len(outs) == 1 else jnp.concatenate([o[k] for o in outs])
    return (yp.reshape(batch, seq, D_MODEL), ys.reshape(nseq, dseq, D_MODEL),
            stack(outs_p, 0), stack(outs_p, 1), stack(outs_p, 2), stack(outs_p, 3),
            cat(outs_s, 0), cat(outs_s, 1), stack(outs_s, 2), stack(outs_s, 3))
```

```python
import functools

import jax
import jax.numpy as jnp
from jax import lax
from jax.experimental import pallas as pl
from jax.experimental.pallas import tpu as pltpu

D_MODEL = 1024
D_FF = 2816
N_HEADS = 4
HEAD_DIM = 128
M_WIDTH = N_HEADS * HEAD_DIM
CONV_CH = 512
CONV_WIDTH = 31
HIST = CONV_WIDTH - 1
HIST_PAD = 32
SUBLANES = 8
EPS = 1e-6
CHUNK = 128
AUG = 2 * HEAD_DIM
CONV_ROWS = 128
LANES = 128
LN_ROWS = 32
NCH = 256

F32 = jnp.float32
BF16 = jnp.bfloat16
HIGHEST = lax.Precision.HIGHEST
NT_DIMS = (((1,), (1,)), ((), ()))

TOKEN_TILE = 256
PROMPT_TILE = 256
FFN_ROWS = 256
PIN_MLSTM = False
VMEM_LIMIT_TOKENWISE = 56 * 1024 * 1024
VMEM_LIMIT_MIXER = 40 * 1024 * 1024


def _rms(x, g):
    return x * lax.rsqrt(jnp.mean(x * x, axis=-1, keepdims=True) + EPS) * g


def _dot(a, b):
    return jnp.dot(a, b, preferred_element_type=F32)


def _swiglu(h, wg_ref, wu_ref, wd_ref):
    a = _dot(h, wg_ref[...])
    b = _dot(h, wu_ref[...])
    s = (a * jax.nn.sigmoid(a) * b).astype(BF16)
    return _dot(s, wd_ref[...])


class _Side:
    def __init__(self, make_pieces, pin=True):
        self.zero = 0
        self.pin = pin
        self.pieces = make_pieces(self)

    def run(self, after, n=1):
        if self.pin:
            bits = pltpu.bitcast(after[after.shape[0] - SUBLANES:, 0:LANES], jnp.int32)
            self.zero = lax.shift_right_logical(lax.shift_right_logical(bits, 16), 16)[0, 0]
        if n is None:
            for _ in self.pieces:
                pass
        else:
            for _ in range(n):
                next(self.pieces, None)


def _swiglu_chunked(h_s, s_s, wg_ref, wu_ref, wd_ref, side, chunk, after_gating, down_pieces=1):
    for c0 in range(0, D_FF, chunk):
        cols = slice(c0, min(c0 + chunk, D_FF))
        a = _dot(h_s[...], wg_ref[:, cols])
        b = _dot(h_s[...], wu_ref[:, cols])
        s = a * jax.nn.sigmoid(a) * b
        s_s[:, cols] = s.astype(BF16)
        for t0 in range(0, cols.stop - cols.start, NCH):
            side.run((s if after_gating else a)[:, t0:t0 + LANES])
    d = []
    for c0 in range(0, D_MODEL, chunk):
        d.append(_dot(s_s[...], wd_ref[:, c0:c0 + chunk]))
        for t0 in range(0, chunk, NCH):
            side.run(d[-1][:, t0:t0 + LANES], down_pieces)
    return jnp.concatenate(d, axis=1)


def _gate_rows(pre, gbias_ref):
    g = pre + gbias_ref[...]
    row = lax.broadcasted_iota(jnp.int32, g.shape, 0)
    return jnp.where(row < N_HEADS, g, jax.nn.log_sigmoid(g))


def _ffn_in_values(x, g1_ref, wg_ref, wu_ref, wd_ref, gpost_ref, gmix_ref, wqvo_ref, wktg_ref, wconv_ref, gbias_ref):
    h = _rms(x, g1_ref[...]).astype(BF16)
    d = _swiglu(h, wg_ref, wu_ref, wd_ref)
    x1 = x + 0.5 * _rms(d, gpost_ref[...])
    h2 = _rms(x1, gmix_ref[...]).astype(BF16)
    qvo = _dot(h2, wqvo_ref[...])
    q = qvo[:, :M_WIDTH]
    v = qvo[:, M_WIDTH:2 * M_WIDTH]
    og = jax.nn.sigmoid(qvo[:, 2 * M_WIDTH:])
    ktg = lax.dot_general(wktg_ref[...], h2, NT_DIMS, preferred_element_type=F32)
    kt = ktg[:M_WIDTH] * (HEAD_DIM ** -0.5)
    cc = _dot(h2, wconv_ref[...])
    u = cc[:, :CONV_CH] * jax.nn.sigmoid(cc[:, CONV_CH:])
    return x1, q, kt, v, og, u, _gate_rows(ktg[M_WIDTH:], gbias_ref)


def _ffn_out_values(ym, c, x1, wout_ref, gmp_ref, g2_ref, wg_ref, wu_ref, wd_ref, gpost_ref, gfin_ref):
    o = _dot(ym.astype(BF16), wout_ref[0:M_WIDTH, :]) + _dot(c.astype(BF16), wout_ref[M_WIDTH:, :])
    x2 = x1 + _rms(o, gmp_ref[...])
    h = _rms(x2, g2_ref[...]).astype(BF16)
    d = _swiglu(h, wg_ref, wu_ref, wd_ref)
    x3 = x2 + 0.5 * _rms(d, gpost_ref[...])
    return _rms(x3, gfin_ref[...])


def _ffn_out_phases(ym, c, x1, y_ref, rows, wout_ref, gmp_ref, g2_ref, wg_ref, wu_ref, wd_ref, gpost_ref, gfin_ref):
    o = _dot(ym.astype(BF16), wout_ref[0:M_WIDTH, :]) + _dot(c.astype(BF16), wout_ref[M_WIDTH:, :])
    yield o
    x2 = x1 + _rms(o, gmp_ref[...])
    h = _rms(x2, g2_ref[...]).astype(BF16)
    a = _dot(h, wg_ref[...])
    yield a
    b = _dot(h, wu_ref[...])
    yield b
    s = (a * jax.nn.sigmoid(a) * b).astype(BF16)
    d = _dot(s, wd_ref[...])
    yield d
    x3 = x2 + 0.5 * _rms(d, gpost_ref[...])
    y = _rms(x3, gfin_ref[...])
    y_ref[rows, :] = y
    yield y


def _conv_taps(ubuf, row0, nrows, lanes, cw_ref, cb_ref):
    acc = jnp.broadcast_to(cb_ref[:, lanes], (nrows, lanes.stop - lanes.start))
    first = HIST_PAD - HIST
    for res in range(SUBLANES):
        taps = [j for j in range(CONV_WIDTH) if (first + j) % SUBLANES == res]
        lo = (first + taps[0]) // SUBLANES * SUBLANES
        hi = (first + taps[-1]) // SUBLANES * SUBLANES
        win = ubuf[row0 + lo:row0 + hi + nrows + (SUBLANES if res else 0), lanes]
        if res:
            win = pltpu.roll(win, win.shape[0] - res, 0)
        for j in taps:
            off = first + j - res - lo
            acc = acc + win[off:off + nrows, :] * cw_ref[j:j + 1, lanes]
    return acc


def _conv_taps_interleaved(uslab, row0, nrows, lanes, cw_ref, cb_ref):
    first = HIST_PAD - HIST
    half = nrows // 2
    acc = [jnp.broadcast_to(cb_ref[:, lanes], (half, LANES))] * 2
    for j in range(CONV_WIDTH):
        for parity in range(2):
            x = uslab[pl.ds(row0 + first + j + parity, half, stride=2), :]
            acc[parity] = acc[parity] + x * cw_ref[j:j + 1, lanes]
    return acc


def _ln_swish(acc, lg_ref, lb_ref):
    mu = jnp.mean(acc, axis=-1, keepdims=True)
    xc = acc - mu
    var = jnp.mean(xc * xc, axis=-1, keepdims=True)
    y = xc * lax.rsqrt(var + EPS) * lg_ref[...] + lb_ref[...]
    return y * jax.nn.sigmoid(y)


def _mlstm_gates(b_r, a_r, caus, qh, kth):
    L = b_r.shape[1]
    arow = jnp.broadcast_to(a_r, (L, L))
    bcol = jnp.broadcast_to(b_r, (L, L)).T
    acol = arow.T
    dmat = jnp.where(caus, bcol + arow, -jnp.inf)
    return dict(bcol=bcol, acol=acol, dmat=dmat, rowmax=jnp.max(dmat, axis=1, keepdims=True), s_raw=_dot(qh, kth))


def _mlstm_scores(st, mprev, vh):
    L = st["bcol"].shape[0]
    m_t = jnp.maximum(st["bcol"] + mprev, st["rowmax"])
    s = st["s_raw"] * jnp.exp(st["dmat"] - m_t)
    vaug = jnp.concatenate([vh, jnp.ones_like(vh)], axis=1)
    m_new = m_t[L - 1:L, :]
    b_last = st["bcol"][L - 1:L, :]
    g_rows = jnp.exp(b_last + st["acol"] - m_new)
    new = dict(m_t=m_t, sv=_dot(s.astype(BF16), vaug.astype(BF16)), w_inter=jnp.exp(st["bcol"] + mprev - m_t),
               g_state=jnp.exp(b_last + mprev - m_new),
               gv=(jnp.concatenate([g_rows, g_rows], axis=1) * vaug).astype(BF16))
    st.clear()
    st.update(new)
    return m_new


def _mlstm_output(st, qh, kth, caug):
    m_t, sv, w_inter, g_state = st["m_t"], st["sv"], st["w_inter"], st["g_state"]
    qc = _dot(qh, caug.astype(BF16))
    caug_new = jnp.concatenate([g_state, g_state], axis=1) * caug + _dot(kth, st["gv"])
    num = w_inter * qc[:, :HEAD_DIM] + sv[:, :HEAD_DIM]
    den = w_inter * qc[:, HEAD_DIM:] + sv[:, HEAD_DIM:]
    hh = num / jnp.maximum(jnp.abs(den), jnp.exp(-m_t))
    return hh, caug_new


def _store_state(c_out, n_out, i, h, caug):
    c_out[i, h] = caug[:, :HEAD_DIM]
    n_out[i, h] = caug[:, HEAD_DIM:].T[0:1, :]


def _ffn_in_mlstm_body(x_ref, g1_ref, wg_ref, wu_ref, wd_ref, gpost_ref, gmix_ref, wqvo_ref, wktg_ref, wconv_ref,
                       gbias_ref, x1_ref, u_ref, ym_ref, c_out, n_out, m_out,
                       q_st, kt_st, v_st, og_st, g_st, caug_s, m_s, *, tm, nt):
    g = pl.program_id(0)
    wslot = lax.rem(g, 2)
    rslot = 1 - wslot
    L = CHUNK

    @pl.when(g == 0)
    def _init():
        for st in (q_st, kt_st, v_st, og_st, g_st, caug_s, m_s):
            st[...] = jnp.zeros_like(st)

    def mlstm_pieces(side):
        fresh = lax.rem(g - 1, nt) == 0
        r = lax.broadcasted_iota(jnp.int32, (L, L), 0)
        c = lax.broadcasted_iota(jnp.int32, (L, L), 1)
        caus = c <= r
        triu = (r <= c).astype(BF16)
        nck = tm // L
        stages = {}

        def stage1(ck):
            rows, slot = slice(ck * L, (ck + 1) * L), rslot + side.zero
            gt = g_st[slot, :, rows]
            hi = gt.astype(BF16)
            mid = (gt - hi.astype(F32)).astype(BF16)
            lo = (gt - hi.astype(F32) - mid.astype(F32)).astype(BF16)
            parts = _dot(jnp.concatenate([hi, mid, lo], axis=0), triu)
            b_rows = parts[0:2 * N_HEADS] + parts[2 * N_HEADS:4 * N_HEADS] + parts[4 * N_HEADS:]
            for h in range(N_HEADS):
                hs = slice(h * HEAD_DIM, (h + 1) * HEAD_DIM)
                b_r = b_rows[N_HEADS + h:N_HEADS + h + 1, :]
                stages[ck, h] = _mlstm_gates(b_r, gt[h:h + 1, :] - b_r, caus, q_st[slot, rows, hs],
                                             kt_st[slot, hs, rows])

        def stage2(ck):
            rows, slot = slice(ck * L, (ck + 1) * L), rslot + side.zero
            for h in range(N_HEADS):
                mprev = m_s[h + side.zero]
                if ck == 0:
                    mprev = jnp.where(fresh, 0.0, mprev)
                m_new = _mlstm_scores(stages[ck, h], mprev, v_st[slot, rows, slice(h * HEAD_DIM, (h + 1) * HEAD_DIM)])
                m_s[h] = m_new
                if ck == nck - 1:
                    m_out[0, h] = m_new

        def stage3(ck):
            rows, slot = slice(ck * L, (ck + 1) * L), rslot + side.zero
            for h in range(N_HEADS):
                hs = slice(h * HEAD_DIM, (h + 1) * HEAD_DIM)
                caug = caug_s[h + side.zero]
                if ck == 0:
                    caug = jnp.where(fresh, 0.0, caug)
                hh, caug = _mlstm_output(stages.pop((ck, h)), q_st[slot, rows, hs], kt_st[slot, hs, rows], caug)
                ym_ref[rows, hs] = og_st[slot, rows, hs] * hh
                caug_s[h] = caug
                if ck == nck - 1:
                    _store_state(c_out, n_out, 0, h, caug)

        for t in range(nck + 2):
            for lag, stage in enumerate((stage1, stage2, stage3)):
                if 0 <= t - lag < nck:
                    stage(t - lag)
            yield

    side = _Side(mlstm_pieces, pin=PIN_MLSTM)

    blocks = [_ffn_in_block(x_ref, slice(r0, r0 + FFN_ROWS), wslot, g1_ref, wg_ref, wu_ref, wd_ref, gpost_ref,
                            gmix_ref, wqvo_ref, wktg_ref, wconv_ref, gbias_ref, x1_ref, u_ref, q_st, kt_st, v_st,
                            og_st, g_st) for r0 in range(0, tm, FFN_ROWS)]
    side.run(None, 1)
    last = None
    while blocks:
        for block in list(blocks):
            token = next(block, None)
            if token is None:
                blocks.remove(block)
            else:
                side.run(token, 1)
                last = token
    side.run(last, None)


def _ffn_in_block(x_ref, rows, wslot, g1_ref, wg_ref, wu_ref, wd_ref, gpost_ref, gmix_ref, wqvo_ref, wktg_ref, wconv_ref,
                  gbias_ref, x1_ref, u_ref, q_st, kt_st, v_st, og_st, g_st):
    x = x_ref[rows, :]
    h = _rms(x, g1_ref[...]).astype(BF16)
    a = _dot(h, wg_ref[...])
    yield a
    b = _dot(h, wu_ref[...])
    yield b
    s = (a * jax.nn.sigmoid(a) * b).astype(BF16)
    d = _dot(s, wd_ref[...])
    yield d
    x1 = x + 0.5 * _rms(d, gpost_ref[...])
    x1_ref[rows, :] = x1
    h2 = _rms(x1, gmix_ref[...]).astype(BF16)
    qvo = _dot(h2, wqvo_ref[...])
    q_st[wslot, rows, :] = qvo[:, :M_WIDTH].astype(BF16)
    v_st[wslot, rows, :] = qvo[:, M_WIDTH:2 * M_WIDTH]
    og_st[wslot, rows, :] = jax.nn.sigmoid(qvo[:, 2 * M_WIDTH:])
    yield qvo
    ktg = lax.dot_general(wktg_ref[...], h2, NT_DIMS, preferred_element_type=F32)
    kt_st[wslot, :, rows] = (ktg[:M_WIDTH] * (HEAD_DIM ** -0.5)).astype(BF16)
    g_st[wslot, :, rows] = _gate_rows(ktg[M_WIDTH:], gbias_ref)
    yield ktg
    cc = _dot(h2, wconv_ref[...])
    u_ref[rows, :] = cc[:, :CONV_CH] * jax.nn.sigmoid(cc[:, CONV_CH:])
    yield cc


def _conv_ffn_out_body(u_ref, ym_ref, x1_ref, cw_ref, cb_ref, lg_ref, lb_ref, wout_ref, gmp_ref, g2_ref,
                       wg_ref, wu_ref, wd_ref, gpost_ref, gfin_ref, y_ref, hist_out, ubuf, cbuf, *, tm, nt):
    nslab = CONV_CH // LANES

    @pl.when(lax.rem(pl.program_id(0), nt) == 0)
    def _new_sequence():
        ubuf[:, 0:HIST_PAD, :] = jnp.zeros((nslab, HIST_PAD, LANES), F32)

    for lb in range(nslab):
        ubuf[lb, HIST_PAD:HIST_PAD + tm, :] = u_ref[:, lb * LANES:(lb + 1) * LANES]
    for k in range(tm // CONV_ROWS):
        for lb in range(nslab):
            acc = _conv_taps_interleaved(ubuf.at[lb], k * CONV_ROWS, CONV_ROWS, slice(lb * LANES, (lb + 1) * LANES),
                                         cw_ref, cb_ref)
            for parity in range(2):
                cbuf[lb, pl.ds(k * CONV_ROWS + parity, CONV_ROWS // 2, stride=2), :] = acc[parity]
    tail = jnp.concatenate([ubuf[lb, tm:tm + HIST_PAD, :] for lb in range(nslab)], axis=1)
    hist_out[0] = tail
    for lb in range(nslab):
        ubuf[lb, 0:HIST_PAD, :] = tail[:, lb * LANES:(lb + 1) * LANES]

    halves = []
    for r0 in range(0, tm, FFN_ROWS):
        c = []
        for k in range(FFN_ROWS // LN_ROWS):
            rows = slice(r0 + k * LN_ROWS, r0 + (k + 1) * LN_ROWS)
            c.append(_ln_swish(jnp.concatenate([cbuf[lb, rows, :] for lb in range(nslab)], axis=1), lg_ref, lb_ref))
        rows = slice(r0, r0 + FFN_ROWS)
        halves.append(_ffn_out_phases(ym_ref[rows, :], jnp.concatenate(c, axis=0), x1_ref[rows, :], y_ref, rows,
                                      wout_ref, gmp_ref, g2_ref, wg_ref, wu_ref, wd_ref, gpost_ref, gfin_ref))
    for _ in zip(*halves):
        pass


def _const_spec(shape):
    nd = len(shape)
    return pl.BlockSpec(shape, lambda *_: (0,) * nd, pipeline_mode=pl.Buffered(1))


def _ffn_in_consts(p, tm):
    gbias = jnp.broadcast_to(p["gbias"][:, None], (2 * N_HEADS, tm))
    return [p["ffn1_pre_g"], p["ffn1_wg"], p["ffn1_wu"], p["ffn1_wd"], p["ffn1_post_g"], p["mix_pre_g"],
            p["w_qvo"], p["w_ktg"], p["w_conv"], gbias]


def _ffn_out_consts(p):
    return [p["w_out"], p["mix_post_g"], p["ffn2_pre_g"], p["ffn2_wg"], p["ffn2_wu"], p["ffn2_wd"],
            p["ffn2_post_g"], p["final_g"]]


def _ffn_in_mlstm(x2d, p, batch, seq, tm):
    m = batch * seq
    nt = seq // tm
    ntiles = m // tm
    cur = lambda w: pl.BlockSpec((tm, w), lambda g: (jnp.minimum(g, ntiles - 1), 0))
    prev = lambda w: pl.BlockSpec((tm, w), lambda g: (jnp.maximum(g - 1, 0), 0))
    prev_seq = lambda g: jnp.maximum(g - 1, 0) // nt
    consts = _ffn_in_consts(p, FFN_ROWS)
    return pl.pallas_call(
        functools.partial(_ffn_in_mlstm_body, tm=tm, nt=nt),
        grid=(ntiles + 1,),
        in_specs=[cur(D_MODEL)] + [_const_spec(c.shape) for c in consts],
        out_specs=[cur(D_MODEL), cur(CONV_CH), prev(M_WIDTH),
                   pl.BlockSpec((1, N_HEADS, HEAD_DIM, HEAD_DIM), lambda g: (prev_seq(g), 0, 0, 0)),
                   pl.BlockSpec((1, N_HEADS, 1, HEAD_DIM), lambda g: (prev_seq(g), 0, 0, 0)),
                   pl.BlockSpec((1, N_HEADS, 1, HEAD_DIM), lambda g: (prev_seq(g), 0, 0, 0))],
        out_shape=[jax.ShapeDtypeStruct((m, D_MODEL), F32), jax.ShapeDtypeStruct((m, CONV_CH), F32),
                   jax.ShapeDtypeStruct((m, M_WIDTH), F32),
                   jax.ShapeDtypeStruct((batch, N_HEADS, HEAD_DIM, HEAD_DIM), F32),
                   jax.ShapeDtypeStruct((batch, N_HEADS, 1, HEAD_DIM), F32),
                   jax.ShapeDtypeStruct((batch, N_HEADS, 1, HEAD_DIM), F32)],
        scratch_shapes=[pltpu.VMEM((2, tm, M_WIDTH), BF16), pltpu.VMEM((2, M_WIDTH, tm), BF16),
                        pltpu.VMEM((2, tm, M_WIDTH), F32), pltpu.VMEM((2, tm, M_WIDTH), F32),
                        pltpu.VMEM((2, 2 * N_HEADS, tm), F32),
                        pltpu.VMEM((N_HEADS, HEAD_DIM, AUG), F32), pltpu.VMEM((N_HEADS, 1, HEAD_DIM), F32)],
        compiler_params=pltpu.CompilerParams(dimension_semantics=("arbitrary",),
                                             vmem_limit_bytes=VMEM_LIMIT_TOKENWISE),
        name="ffn_in_mlstm",
    )(x2d, *consts)


def _conv_ffn_out(u, ym, x1, p, batch, seq, tm):
    m = batch * seq
    nt = seq // tm
    row = lambda w: pl.BlockSpec((tm, w), lambda g: (g, 0))
    consts = [p["conv_w"], p["conv_b"], p["conv_ln_g"], p["conv_ln_b"]] + _ffn_out_consts(p)
    return pl.pallas_call(
        functools.partial(_conv_ffn_out_body, tm=tm, nt=nt),
        grid=(m // tm,),
        in_specs=[row(CONV_CH), row(M_WIDTH), row(D_MODEL)] + [_const_spec(c.shape) for c in consts],
        out_specs=[row(D_MODEL), pl.BlockSpec((1, HIST_PAD, CONV_CH), lambda g: (g // nt, 0, 0))],
        out_shape=[jax.ShapeDtypeStruct((m, D_MODEL), F32), jax.ShapeDtypeStruct((batch, HIST_PAD, CONV_CH), F32)],
        scratch_shapes=[pltpu.VMEM((CONV_CH // LANES, HIST_PAD + tm, LANES), F32),
                        pltpu.VMEM((CONV_CH // LANES, tm, LANES), F32)],
        compiler_params=pltpu.CompilerParams(dimension_semantics=("arbitrary",),
                                             vmem_limit_bytes=VMEM_LIMIT_TOKENWISE),
        name="conv_ffn_out",
    )(u, ym, x1, *consts)


def _ffn_in_body(x_ref, g1_ref, wg_ref, wu_ref, wd_ref, gpost_ref, gmix_ref, wqvo_ref, wktg_ref, wconv_ref,
                 gbias_ref, x1_ref, q_ref, kt_ref, v_ref, og_ref, u_ref, gt_ref):
    outs = _ffn_in_values(x_ref[...], g1_ref, wg_ref, wu_ref, wd_ref, gpost_ref, gmix_ref, wqvo_ref, wktg_ref,
                          wconv_ref, gbias_ref)
    for ref, val in zip((x1_ref, q_ref, kt_ref, v_ref, og_ref, u_ref, gt_ref), outs):
        ref[...] = val


def _ffn_out_body(mix_ref, x1_ref, wout_ref, gmp_ref, g2_ref, wg_ref, wu_ref, wd_ref, gpost_ref, gfin_ref, y_ref):
    y_ref[...] = _ffn_out_values(mix_ref[:, :M_WIDTH], mix_ref[:, M_WIDTH:], x1_ref[...], wout_ref, gmp_ref, g2_ref,
                                 wg_ref, wu_ref, wd_ref, gpost_ref, gfin_ref)


def _mixer_sample_body(q_ref, kt_ref, v_ref, og_ref, u_ref, gt_ref, c0_ref, n0_ref, m0_ref, hist0_ref,
                       cw_ref, cb_ref, lg_ref, lb_ref, mix_ref, c_out, n_out, m_out, hist_out, ubuf, *, nseq, L):
    R = nseq * L
    r = lax.broadcasted_iota(jnp.int32, (R, R), 0)
    c = lax.broadcasted_iota(jnp.int32, (R, R), 1)
    same = lax.div(r, L) == lax.div(c, L)
    caus = same & (c <= r)
    segtriu = jnp.where(same & (r <= c), 1.0, 0.0).astype(F32)
    segones = jnp.where(same, 1.0, 0.0).astype(F32)
    rowseq = lax.div(lax.broadcasted_iota(jnp.int32, (R, AUG), 0), L)
    ones = jnp.ones((R, HEAD_DIM), F32)
    gt = gt_ref[...]
    b_rows = jnp.dot(gt, segtriu, precision=HIGHEST, preferred_element_type=F32)
    tot_rows = jnp.dot(gt, segones, precision=HIGHEST, preferred_element_type=F32)
    for h in range(N_HEADS):
        hs = slice(h * HEAD_DIM, (h + 1) * HEAD_DIM)
        b_r = b_rows[N_HEADS + h:N_HEADS + h + 1, :]
        a_r = gt[h:h + 1, :] - b_r
        t_r = tot_rows[N_HEADS + h:N_HEADS + h + 1, :]
        arow = jnp.broadcast_to(a_r, (R, R))
        bcol = jnp.broadcast_to(b_r, (R, R)).T
        acol = arow.T
        tcol = jnp.broadcast_to(t_r, (R, R)).T
        mprev = m0_ref[h]
        mprev = jnp.concatenate([mprev, mprev], axis=1)
        dmat = jnp.where(caus, bcol + arow, -jnp.inf)
        m_t = jnp.maximum(bcol + mprev, jnp.max(dmat, axis=1, keepdims=True))
        dend = jnp.where(same, tcol + arow, -jnp.inf)
        m_new = jnp.maximum(tcol + mprev, jnp.max(dend, axis=1, keepdims=True))
        qh = q_ref[:, hs].astype(BF16)
        kth = kt_ref[hs, :].astype(BF16)
        s = _dot(qh, kth) * jnp.exp(dmat - m_t)
        vaug = jnp.concatenate([v_ref[:, hs], ones], axis=1)
        w_inter = jnp.exp(bcol + mprev - m_t)
        caug0 = [jnp.concatenate([c0_ref[0, i, h], jnp.broadcast_to(n0_ref[0, i, h:h + 1, :], (HEAD_DIM, HEAD_DIM)).T],
                                 axis=1) for i in range(nseq)]
        qc = jnp.zeros((R, AUG), F32)
        for i in range(nseq):
            qc = jnp.where(rowseq == i, _dot(qh, caug0[i].astype(BF16)), qc)
        sv = _dot(s.astype(BF16), vaug.astype(BF16))
        num = w_inter[:, :HEAD_DIM] * qc[:, :HEAD_DIM] + sv[:, :HEAD_DIM]
        den = w_inter[:, :HEAD_DIM] * qc[:, HEAD_DIM:] + sv[:, HEAD_DIM:]
        hh = num / jnp.maximum(jnp.abs(den), jnp.exp(-m_t[:, :HEAD_DIM]))
        mix_ref[:, hs] = og_ref[:, hs] * hh
        g_state = jnp.exp(tcol + mprev - m_new)
        g_rows = jnp.exp(tcol + acol - m_new)
        gv = g_rows * vaug
        for i in range(nseq):
            gvi = jnp.where(rowseq == i, gv, 0.0).astype(BF16)
            caug = g_state[i * L:i * L + 1, :] * caug0[i] + _dot(kth, gvi)
            c_out[0, i, h] = caug[:, :HEAD_DIM]
            n_out[0, i, h:h + 1, :] = caug[:, HEAD_DIM:].T[0:1, :]
            m_out[i, h] = m_new[i * L:i * L + 1, :HEAD_DIM]

    for i in range(nseq):
        ubuf[0:HIST_PAD, :] = hist0_ref[i]
        ubuf[HIST_PAD:HIST_PAD + L, :] = u_ref[i * L:(i + 1) * L, :]
        acc = _conv_taps(ubuf, 0, L, slice(0, CONV_CH), cw_ref, cb_ref)
        mix_ref[i * L:(i + 1) * L, M_WIDTH:] = _ln_swish(acc, lg_ref, lb_ref)
        hist_out[i] = ubuf[L:L + HIST_PAD, :]


def _ffn_in(x2d, p, tm):
    m = x2d.shape[0]
    row = lambda w: pl.BlockSpec((tm, w), lambda i: (i, 0))
    col = lambda h: pl.BlockSpec((h, tm), lambda i: (0, i))
    consts = _ffn_in_consts(p, tm)
    return pl.pallas_call(
        _ffn_in_body,
        grid=(m // tm,),
        in_specs=[row(D_MODEL)] + [_const_spec(c.shape) for c in consts],
        out_specs=[row(D_MODEL), row(M_WIDTH), col(M_WIDTH), row(M_WIDTH), row(M_WIDTH), row(CONV_CH),
                   col(2 * N_HEADS)],
        out_shape=[jax.ShapeDtypeStruct((m, D_MODEL), F32), jax.ShapeDtypeStruct((m, M_WIDTH), F32),
                   jax.ShapeDtypeStruct((M_WIDTH, m), F32), jax.ShapeDtypeStruct((m, M_WIDTH), F32),
                   jax.ShapeDtypeStruct((m, M_WIDTH), F32), jax.ShapeDtypeStruct((m, CONV_CH), F32),
                   jax.ShapeDtypeStruct((2 * N_HEADS, m), F32)],
        compiler_params=pltpu.CompilerParams(dimension_semantics=("arbitrary",),
                                             vmem_limit_bytes=VMEM_LIMIT_TOKENWISE),
        name="ffn_in",
    )(x2d, *consts)


def _ffn_out(mix, x1, p, tm):
    m = mix.shape[0]
    row = pl.BlockSpec((tm, D_MODEL), lambda i: (i, 0))
    consts = _ffn_out_consts(p)
    return pl.pallas_call(
        _ffn_out_body,
        grid=(m // tm,),
        in_specs=[row, row] + [_const_spec(c.shape) for c in consts],
        out_specs=row,
        out_shape=jax.ShapeDtypeStruct((m, D_MODEL), F32),
        compiler_params=pltpu.CompilerParams(dimension_semantics=("arbitrary",),
                                             vmem_limit_bytes=VMEM_LIMIT_TOKENWISE),
        name="ffn_out",
    )(mix, x1, *consts)


def _mixer_sample(q, kt, v, og, u, gt, c_all, n_all, layer, m0, hist0, p, nseq, L):
    rows = nseq * L
    args = [q, kt, v, og, u, gt, c_all, n_all, m0, hist0, p["conv_w"], p["conv_b"], p["conv_ln_g"], p["conv_ln_b"]]
    full = lambda a: pl.BlockSpec(a.shape, lambda i, nd=a.ndim: (0,) * nd)
    of_layer = lambda a: pl.BlockSpec((1,) + a.shape[1:], lambda i, nd=a.ndim: (layer,) + (0,) * (nd - 1))
    out_shape = [jax.ShapeDtypeStruct((rows, D_MODEL), F32),
                 jax.ShapeDtypeStruct((1, nseq, N_HEADS, HEAD_DIM, HEAD_DIM), F32),
                 jax.ShapeDtypeStruct((1, nseq, N_HEADS, HEAD_DIM), F32),
                 jax.ShapeDtypeStruct((nseq, N_HEADS, 1, HEAD_DIM), F32),
                 jax.ShapeDtypeStruct((nseq, HIST_PAD, CONV_CH), F32)]
    return pl.pallas_call(
        functools.partial(_mixer_sample_body, nseq=nseq, L=L),
        grid=(1,),
        in_specs=[of_layer(a) if a is c_all or a is n_all else full(a) for a in args],
        out_specs=[full(s) for s in out_shape],
        out_shape=out_shape,
        scratch_shapes=[pltpu.VMEM((HIST_PAD + L, CONV_CH), F32)],
        compiler_params=pltpu.CompilerParams(dimension_semantics=("arbitrary",),
                                             vmem_limit_bytes=VMEM_LIMIT_MIXER),
        name="mixer_sample",
    )(*args)


def _layer_params(l, ffn1_pre_g, ffn1_wg, ffn1_wu, ffn1_wd, ffn1_post_g, mix_pre_g, w_in, b_igate, b_fgate,
                  conv_w, conv_b, conv_ln_g, conv_ln_b, w_out, mix_post_g, ffn2_pre_g, ffn2_wg, ffn2_wu, ffn2_wd,
                  ffn2_post_g, final_g):
    vec = lambda a: a[l].astype(F32).reshape(1, -1)
    w = w_in[l]
    cuts = [0, M_WIDTH, 2 * M_WIDTH, 3 * M_WIDTH, 4 * M_WIDTH, 4 * M_WIDTH + N_HEADS, 4 * M_WIDTH + 2 * N_HEADS,
            4 * M_WIDTH + 2 * N_HEADS + CONV_CH, 4 * M_WIDTH + 2 * N_HEADS + 2 * CONV_CH]
    wq, wk, wv, wo, wi, wf, wcv, wcg = [w[:, a:b] for a, b in zip(cuts[:-1], cuts[1:])]
    return {
        "ffn1_pre_g": vec(ffn1_pre_g), "ffn1_post_g": vec(ffn1_post_g), "mix_pre_g": vec(mix_pre_g),
        "mix_post_g": vec(mix_post_g), "ffn2_pre_g": vec(ffn2_pre_g), "ffn2_post_g": vec(ffn2_post_g),
        "final_g": vec(final_g),
        "ffn1_wg": ffn1_wg[l].astype(BF16), "ffn1_wu": ffn1_wu[l].astype(BF16), "ffn1_wd": ffn1_wd[l].astype(BF16),
        "ffn2_wg": ffn2_wg[l].astype(BF16), "ffn2_wu": ffn2_wu[l].astype(BF16), "ffn2_wd": ffn2_wd[l].astype(BF16),
        "w_qvo": jnp.concatenate([wq, wv, wo], axis=1).astype(BF16),
        "w_ktg": jnp.concatenate([wk, wi, wf], axis=1).T.astype(BF16),
        "w_conv": jnp.concatenate([wcv, wcg], axis=1).astype(BF16),
        "gbias": jnp.concatenate([b_igate[l], b_fgate[l]]).astype(F32),
        "w_out": w_out[l].astype(BF16),
        "conv_w": conv_w[l].astype(F32), "conv_b": vec(conv_b), "conv_ln_g": vec(conv_ln_g),
        "conv_ln_b": vec(conv_ln_b),
    }


def _split_state(c, n, m, hist):
    return c, n[:, :, 0, :], m[:, :, 0, 0], hist[:, HIST_PAD - HIST:, :]


def kernel(x_prompt, x_sample, state_mlstm_C, state_mlstm_n, state_mlstm_m, cache_conv, ffn1_pre_g, ffn1_wg,
           ffn1_wu, ffn1_wd, ffn1_post_g, mix_pre_g, w_in, b_igate, b_fgate, conv_w, conv_b, conv_ln_g, conv_ln_b,
           w_out, mix_post_g, ffn2_pre_g, ffn2_wg, ffn2_wu, ffn2_wd, ffn2_post_g, final_g):
    batch, seq, _ = x_prompt.shape
    nseq, dseq, _ = x_sample.shape
    depth = w_in.shape[0]
    assert seq % PROMPT_TILE == 0 and PROMPT_TILE % CHUNK == 0 and PROMPT_TILE % CONV_ROWS == 0
    assert (nseq * dseq) % TOKEN_TILE == 0 and dseq <= HIST_PAD
    yp = x_prompt.reshape(batch * seq, D_MODEL)
    ys = x_sample.reshape(nseq * dseq, D_MODEL)
    outs_p, outs_s = [], []
    for l in range(depth):
        p = _layer_params(l, ffn1_pre_g, ffn1_wg, ffn1_wu, ffn1_wd, ffn1_post_g, mix_pre_g, w_in, b_igate, b_fgate,
                          conv_w, conv_b, conv_ln_g, conv_ln_b, w_out, mix_post_g, ffn2_pre_g, ffn2_wg, ffn2_wu,
                          ffn2_wd, ffn2_post_g, final_g)
        x1, u, ym, c, n, m = _ffn_in_mlstm(yp, p, batch, seq, 2 * PROMPT_TILE)
        yp, hist = _conv_ffn_out(u, ym, x1, p, batch, seq, 2 * PROMPT_TILE)
        outs_p.append(_split_state(c, n, m, hist))
        x1, q, kt, v, og, u, gt = _ffn_in(ys, p, TOKEN_TILE)
        m0 = jnp.broadcast_to(state_mlstm_m[l].astype(F32).T[:, :, None, None], (N_HEADS, nseq, dseq, HEAD_DIM))
        m0 = m0.reshape(N_HEADS, nseq * dseq, HEAD_DIM)
        hist0 = jnp.pad(cache_conv[l].astype(F32), ((0, 0), (HIST_PAD - HIST, 0), (0, 0)))
        mix, c, n, m, hist = _mixer_sample(q, kt, v, og, u, gt, state_mlstm_C.astype(F32), state_mlstm_n.astype(F32),
                                          l, m0, hist0, p, nseq, dseq)
        ys = _ffn_out(mix, x1, p, TOKEN_TILE)
        outs_s.append((c, n, m[:, :, 0, 0], hist[:, HIST_PAD - HIST:, :]))
    stack = lambda outs, k: jnp.stack([o[k] for o in outs])
    cat = lambda outs, k: outs[0][k] if len(outs) == 1 else jnp.concatenate([o[k] for o in outs])
    return (yp.reshape(batch, seq, D_MODEL), ys.reshape(nseq, dseq, D_MODEL),
            stack(outs_p, 0), stack(outs_p, 1), stack(outs_p, 2), stack(outs_p, 3),
            cat(outs_s, 0), cat(outs_s, 1), stack(outs_s, 2), stack(outs_s, 3))
```

```python
import functools

import jax
import jax.numpy as jnp
from jax import lax
from jax.experimental import pallas as pl
from jax.experimental.pallas import tpu as pltpu

D_MODEL = 1024
D_FF = 2816
N_HEADS = 4
HEAD_DIM = 128
M_WIDTH = N_HEADS * HEAD_DIM
CONV_CH = 512
CONV_WIDTH = 31
HIST = CONV_WIDTH - 1
HIST_PAD = 32
SUBLANES = 8
EPS = 1e-6
CHUNK = 128
AUG = 2 * HEAD_DIM
CONV_ROWS = 128
LANES = 128
LN_ROWS = 32
NCH = 256

F32 = jnp.float32
BF16 = jnp.bfloat16
HIGHEST = lax.Precision.HIGHEST
NT_DIMS = (((1,), (1,)), ((), ()))

TOKEN_TILE = 256
PROMPT_TILE = 256
FFN_ROWS = 256
PIN_MLSTM = False
VMEM_LIMIT_TOKENWISE = 56 * 1024 * 1024
VMEM_LIMIT_MIXER = 40 * 1024 * 1024


def _rms(x, g):
    return x * lax.rsqrt(jnp.mean(x * x, axis=-1, keepdims=True) + EPS) * g


def _dot(a, b):
    return jnp.dot(a, b, preferred_element_type=F32)


def _swiglu(h, wg_ref, wu_ref, wd_ref):
    a = _dot(h, wg_ref[...])
    b = _dot(h, wu_ref[...])
    s = (a * jax.nn.sigmoid(a) * b).astype(BF16)
    return _dot(s, wd_ref[...])


class _Side:
    def __init__(self, make_pieces, pin=True):
        self.zero = 0
        self.pin = pin
        self.pieces = make_pieces(self)

    def run(self, after, n=1):
        if self.pin:
            bits = pltpu.bitcast(after[after.shape[0] - SUBLANES:, 0:LANES], jnp.int32)
            self.zero = lax.shift_right_logical(lax.shift_right_logical(bits, 16), 16)[0, 0]
        if n is None:
            for _ in self.pieces:
                pass
        else:
            for _ in range(n):
                next(self.pieces, None)


def _swiglu_chunked(h_s, s_s, wg_ref, wu_ref, wd_ref, side, chunk, after_gating, down_pieces=1):
    for c0 in range(0, D_FF, chunk):
        cols = slice(c0, min(c0 + chunk, D_FF))
        a = _dot(h_s[...], wg_ref[:, cols])
        b = _dot(h_s[...], wu_ref[:, cols])
        s = a * jax.nn.sigmoid(a) * b
        s_s[:, cols] = s.astype(BF16)
        for t0 in range(0, cols.stop - cols.start, NCH):
            side.run((s if after_gating else a)[:, t0:t0 + LANES])
    d = []
    for c0 in range(0, D_MODEL, chunk):
        d.append(_dot(s_s[...], wd_ref[:, c0:c0 + chunk]))
        for t0 in range(0, chunk, NCH):
            side.run(d[-1][:, t0:t0 + LANES], down_pieces)
    return jnp.concatenate(d, axis=1)


def _gate_cols(pre, gbias_ref):
    g = pre + gbias_ref[...]
    lane = lax.broadcasted_iota(jnp.int32, g.shape, 1)
    return jnp.where(lane < N_HEADS, g, jax.nn.log_sigmoid(g))


def _gate_rows(gcol):
    return gcol.T[0:2 * N_HEADS, :]


def _ffn_in_values(x, g1_ref, wg_ref, wu_ref, wd_ref, gpost_ref, gmix_ref, wqkvo_ref, wgate_ref, wconv_ref, gbias_ref):
    h = _rms(x, g1_ref[...]).astype(BF16)
    d = _swiglu(h, wg_ref, wu_ref, wd_ref)
    x1 = x + 0.5 * _rms(d, gpost_ref[...])
    h2 = _rms(x1, gmix_ref[...]).astype(BF16)
    qkvo = _dot(h2, wqkvo_ref[...])
    q = qkvo[:, :M_WIDTH]
    k = qkvo[:, M_WIDTH:2 * M_WIDTH] * (HEAD_DIM ** -0.5)
    v = qkvo[:, 2 * M_WIDTH:3 * M_WIDTH]
    og = jax.nn.sigmoid(qkvo[:, 3 * M_WIDTH:])
    cc = _dot(h2, wconv_ref[...])
    u = cc[:, :CONV_CH] * jax.nn.sigmoid(cc[:, CONV_CH:])
    return x1, q, k, v, og, u, _gate_cols(_dot(h2, wgate_ref[...]), gbias_ref)


def _ffn_out_values(ym, c, x1, wout_ref, gmp_ref, g2_ref, wg_ref, wu_ref, wd_ref, gpost_ref, gfin_ref):
    o = _dot(ym.astype(BF16), wout_ref[0:M_WIDTH, :]) + _dot(c.astype(BF16), wout_ref[M_WIDTH:, :])
    x2 = x1 + _rms(o, gmp_ref[...])
    h = _rms(x2, g2_ref[...]).astype(BF16)
    d = _swiglu(h, wg_ref, wu_ref, wd_ref)
    x3 = x2 + 0.5 * _rms(d, gpost_ref[...])
    return _rms(x3, gfin_ref[...])


def _ffn_out_phases(ym, c, x1, y_ref, rows, wout_ref, gmp_ref, g2_ref, wg_ref, wu_ref, wd_ref, gpost_ref, gfin_ref):
    o = _dot(ym.astype(BF16), wout_ref[0:M_WIDTH, :]) + _dot(c.astype(BF16), wout_ref[M_WIDTH:, :])
    yield o
    x2 = x1 + _rms(o, gmp_ref[...])
    h = _rms(x2, g2_ref[...]).astype(BF16)
    a = _dot(h, wg_ref[...])
    yield a
    b = _dot(h, wu_ref[...])
    yield b
    s = (a * jax.nn.sigmoid(a) * b).astype(BF16)
    d = _dot(s, wd_ref[...])
    yield d
    x3 = x2 + 0.5 * _rms(d, gpost_ref[...])
    y = _rms(x3, gfin_ref[...])
    y_ref[rows, :] = y
    yield y


def _conv_taps(ubuf, row0, nrows, lanes, cw_ref, cb_ref):
    acc = jnp.broadcast_to(cb_ref[:, lanes], (nrows, lanes.stop - lanes.start))
    first = HIST_PAD - HIST
    for res in range(SUBLANES):
        taps = [j for j in range(CONV_WIDTH) if (first + j) % SUBLANES == res]
        lo = (first + taps[0]) // SUBLANES * SUBLANES
        hi = (first + taps[-1]) // SUBLANES * SUBLANES
        win = ubuf[row0 + lo:row0 + hi + nrows + (SUBLANES if res else 0), lanes]
        if res:
            win = pltpu.roll(win, win.shape[0] - res, 0)
        for j in taps:
            off = first + j - res - lo
            acc = acc + win[off:off + nrows, :] * cw_ref[j:j + 1, lanes]
    return acc


def _conv_taps_interleaved(uslab, row0, nrows, lanes, cw_ref, cb_ref):
    first = HIST_PAD - HIST
    half = nrows // 2
    acc = [jnp.broadcast_to(cb_ref[:, lanes], (half, LANES))] * 2
    for j in range(CONV_WIDTH):
        for parity in range(2):
            x = uslab[pl.ds(row0 + first + j + parity, half, stride=2), :]
            acc[parity] = acc[parity] + x * cw_ref[j:j + 1, lanes]
    return acc


def _ln_swish(acc, lg_ref, lb_ref):
    mu = jnp.mean(acc, axis=-1, keepdims=True)
    xc = acc - mu
    var = jnp.mean(xc * xc, axis=-1, keepdims=True)
    y = xc * lax.rsqrt(var + EPS) * lg_ref[...] + lb_ref[...]
    return y * jax.nn.sigmoid(y)


def _mlstm_gates(b_r, a_r, caus, qh, kth):
    L = b_r.shape[1]
    arow = jnp.broadcast_to(a_r, (L, L))
    bcol = jnp.broadcast_to(b_r, (L, L)).T
    acol = arow.T
    dmat = jnp.where(caus, bcol + arow, -jnp.inf)
    return dict(bcol=bcol, acol=acol, dmat=dmat, rowmax=jnp.max(dmat, axis=1, keepdims=True), s_raw=_dot(qh, kth))


def _mlstm_scores(st, mprev, vh):
    L = st["bcol"].shape[0]
    m_t = jnp.maximum(st["bcol"] + mprev, st["rowmax"])
    s = st["s_raw"] * jnp.exp(st["dmat"] - m_t)
    vaug = jnp.concatenate([vh, jnp.ones_like(vh)], axis=1)
    m_new = m_t[L - 1:L, :]
    b_last = st["bcol"][L - 1:L, :]
    g_rows = jnp.exp(b_last + st["acol"] - m_new)
    new = dict(m_t=m_t, sv=_dot(s.astype(BF16), vaug.astype(BF16)), w_inter=jnp.exp(st["bcol"] + mprev - m_t),
               g_state=jnp.exp(b_last + mprev - m_new),
               gv=(jnp.concatenate([g_rows, g_rows], axis=1) * vaug).astype(BF16))
    st.clear()
    st.update(new)
    return m_new


def _mlstm_output(st, qh, kth, caug):
    m_t, sv, w_inter, g_state = st["m_t"], st["sv"], st["w_inter"], st["g_state"]
    qc = _dot(qh, caug.astype(BF16))
    caug_new = jnp.concatenate([g_state, g_state], axis=1) * caug + _dot(kth, st["gv"])
    num = w_inter * qc[:, :HEAD_DIM] + sv[:, :HEAD_DIM]
    den = w_inter * qc[:, HEAD_DIM:] + sv[:, HEAD_DIM:]
    hh = num / jnp.maximum(jnp.abs(den), jnp.exp(-m_t))
    return hh, caug_new


def _store_state(c_out, n_out, i, h, caug):
    c_out[i, h] = caug[:, :HEAD_DIM]
    n_out[i, h] = caug[:, HEAD_DIM:].T[0:1, :]


def _ffn_in_mlstm_body(x_ref, g1_ref, wg_ref, wu_ref, wd_ref, gpost_ref, gmix_ref, wqkvo_ref, wgate_ref, wconv_ref,
                       gbias_ref, x1_ref, u_ref, ym_ref, c_out, n_out, m_out,
                       q_st, kt_st, v_st, og_st, g_st, caug_s, m_s, *, tm, nt):
    g = pl.program_id(0)
    wslot = lax.rem(g, 2)
    rslot = 1 - wslot
    L = CHUNK

    @pl.when(g == 0)
    def _init():
        for st in (q_st, kt_st, v_st, og_st, g_st, caug_s, m_s):
            st[...] = jnp.zeros_like(st)

    def mlstm_pieces(side):
        fresh = lax.rem(g - 1, nt) == 0
        r = lax.broadcasted_iota(jnp.int32, (L, L), 0)
        c = lax.broadcasted_iota(jnp.int32, (L, L), 1)
        caus = c <= r
        triu = (r <= c).astype(BF16)
        nck = tm // L
        stages = {}

        def stage1(ck):
            rows, slot = slice(ck * L, (ck + 1) * L), rslot + side.zero
            gt = g_st[slot, :, rows]
            hi = gt.astype(BF16)
            mid = (gt - hi.astype(F32)).astype(BF16)
            lo = (gt - hi.astype(F32) - mid.astype(F32)).astype(BF16)
            parts = _dot(jnp.concatenate([hi, mid, lo], axis=0), triu)
            b_rows = parts[0:2 * N_HEADS] + parts[2 * N_HEADS:4 * N_HEADS] + parts[4 * N_HEADS:]
            for h in range(N_HEADS):
                hs = slice(h * HEAD_DIM, (h + 1) * HEAD_DIM)
                b_r = b_rows[N_HEADS + h:N_HEADS + h + 1, :]
                stages[ck, h] = _mlstm_gates(b_r, gt[h:h + 1, :] - b_r, caus, q_st[slot, rows, hs],
                                             kt_st[slot, hs, rows])

        def stage2(ck):
            rows, slot = slice(ck * L, (ck + 1) * L), rslot + side.zero
            for h in range(N_HEADS):
                mprev = m_s[h + side.zero]
                if ck == 0:
                    mprev = jnp.where(fresh, 0.0, mprev)
                m_new = _mlstm_scores(stages[ck, h], mprev, v_st[slot, rows, slice(h * HEAD_DIM, (h + 1) * HEAD_DIM)])
                m_s[h] = m_new
                if ck == nck - 1:
                    m_out[0, h] = m_new

        def stage3(ck):
            rows, slot = slice(ck * L, (ck + 1) * L), rslot + side.zero
            for h in range(N_HEADS):
                hs = slice(h * HEAD_DIM, (h + 1) * HEAD_DIM)
                caug = caug_s[h + side.zero]
                if ck == 0:
                    caug = jnp.where(fresh, 0.0, caug)
                hh, caug = _mlstm_output(stages.pop((ck, h)), q_st[slot, rows, hs], kt_st[slot, hs, rows], caug)
                ym_ref[rows, hs] = og_st[slot, rows, hs] * hh
                caug_s[h] = caug
                if ck == nck - 1:
                    _store_state(c_out, n_out, 0, h, caug)

        for t in range(nck + 2):
            for lag, stage in enumerate((stage1, stage2, stage3)):
                if 0 <= t - lag < nck:
                    stage(t - lag)
            yield

    side = _Side(mlstm_pieces, pin=PIN_MLSTM)

    blocks = [_ffn_in_block(x_ref, slice(r0, r0 + FFN_ROWS), wslot, g1_ref, wg_ref, wu_ref, wd_ref, gpost_ref,
                            gmix_ref, wqkvo_ref, wgate_ref, wconv_ref, gbias_ref, x1_ref, u_ref, q_st, kt_st, v_st,
                            og_st, g_st) for r0 in range(0, tm, FFN_ROWS)]
    side.run(None, 1)
    last = None
    while blocks:
        for block in list(blocks):
            token = next(block, None)
            if token is None:
                blocks.remove(block)
            else:
                side.run(token, 1)
                last = token
    side.run(last, None)


def _ffn_in_block(x_ref, rows, wslot, g1_ref, wg_ref, wu_ref, wd_ref, gpost_ref, gmix_ref, wqkvo_ref, wgate_ref,
                  wconv_ref, gbias_ref, x1_ref, u_ref, q_st, kt_st, v_st, og_st, g_st):
    x = x_ref[rows, :]
    h = _rms(x, g1_ref[...]).astype(BF16)
    a = _dot(h, wg_ref[...])
    yield a
    b = _dot(h, wu_ref[...])
    yield b
    s = (a * jax.nn.sigmoid(a) * b).astype(BF16)
    d = _dot(s, wd_ref[...])
    yield d
    x1 = x + 0.5 * _rms(d, gpost_ref[...])
    x1_ref[rows, :] = x1
    h2 = _rms(x1, gmix_ref[...]).astype(BF16)
    qkvo = _dot(h2, wqkvo_ref[...])
    q_st[wslot, rows, :] = qkvo[:, :M_WIDTH].astype(BF16)
    kt_st[wslot, :, rows] = (qkvo[:, M_WIDTH:2 * M_WIDTH] * (HEAD_DIM ** -0.5)).T.astype(BF16)
    v_st[wslot, rows, :] = qkvo[:, 2 * M_WIDTH:3 * M_WIDTH]
    og_st[wslot, rows, :] = jax.nn.sigmoid(qkvo[:, 3 * M_WIDTH:])
    yield qkvo
    gcol = _gate_cols(_dot(h2, wgate_ref[...]), gbias_ref)
    g_st[wslot, :, rows] = _gate_rows(gcol)
    yield gcol
    cc = _dot(h2, wconv_ref[...])
    u_ref[rows, :] = cc[:, :CONV_CH] * jax.nn.sigmoid(cc[:, CONV_CH:])
    yield cc


def _conv_ffn_out_body(u_ref, ym_ref, x1_ref, cw_ref, cb_ref, lg_ref, lb_ref, wout_ref, gmp_ref, g2_ref,
                       wg_ref, wu_ref, wd_ref, gpost_ref, gfin_ref, y_ref, hist_out, ubuf, cbuf, *, tm, nt):
    nslab = CONV_CH // LANES

    @pl.when(lax.rem(pl.program_id(0), nt) == 0)
    def _new_sequence():
        ubuf[:, 0:HIST_PAD, :] = jnp.zeros((nslab, HIST_PAD, LANES), F32)

    for lb in range(nslab):
        ubuf[lb, HIST_PAD:HIST_PAD + tm, :] = u_ref[:, lb * LANES:(lb + 1) * LANES]
    for k in range(tm // CONV_ROWS):
        for lb in range(nslab):
            acc = _conv_taps_interleaved(ubuf.at[lb], k * CONV_ROWS, CONV_ROWS, slice(lb * LANES, (lb + 1) * LANES),
                                         cw_ref, cb_ref)
            for parity in range(2):
                cbuf[lb, pl.ds(k * CONV_ROWS + parity, CONV_ROWS // 2, stride=2), :] = acc[parity]
    tail = jnp.concatenate([ubuf[lb, tm:tm + HIST_PAD, :] for lb in range(nslab)], axis=1)
    hist_out[0] = tail
    for lb in range(nslab):
        ubuf[lb, 0:HIST_PAD, :] = tail[:, lb * LANES:(lb + 1) * LANES]

    halves = []
    for r0 in range(0, tm, FFN_ROWS):
        c = []
        for k in range(FFN_ROWS // LN_ROWS):
            rows = slice(r0 + k * LN_ROWS, r0 + (k + 1) * LN_ROWS)
            c.append(_ln_swish(jnp.concatenate([cbuf[lb, rows, :] for lb in range(nslab)], axis=1), lg_ref, lb_ref))
        rows = slice(r0, r0 + FFN_ROWS)
        halves.append(_ffn_out_phases(ym_ref[rows, :], jnp.concatenate(c, axis=0), x1_ref[rows, :], y_ref, rows,
                                      wout_ref, gmp_ref, g2_ref, wg_ref, wu_ref, wd_ref, gpost_ref, gfin_ref))
    for _ in zip(*halves):
        pass


def _const_spec(shape):
    nd = len(shape)
    return pl.BlockSpec(shape, lambda *_: (0,) * nd, pipeline_mode=pl.Buffered(1))


def _ffn_in_consts(p):
    return [p["ffn1_pre_g"], p["ffn1_wg"], p["ffn1_wu"], p["ffn1_wd"], p["ffn1_post_g"], p["mix_pre_g"],
            p["w_qkvo"], p["w_gate"], p["w_conv"], p["gbias"]]


def _ffn_out_consts(p):
    return [p["w_out"], p["mix_post_g"], p["ffn2_pre_g"], p["ffn2_wg"], p["ffn2_wu"], p["ffn2_wd"],
            p["ffn2_post_g"], p["final_g"]]


def _ffn_in_mlstm(x2d, p, batch, seq, tm):
    m = batch * seq
    nt = seq // tm
    ntiles = m // tm
    cur = lambda w: pl.BlockSpec((tm, w), lambda g: (jnp.minimum(g, ntiles - 1), 0))
    prev = lambda w: pl.BlockSpec((tm, w), lambda g: (jnp.maximum(g - 1, 0), 0))
    prev_seq = lambda g: jnp.maximum(g - 1, 0) // nt
    consts = _ffn_in_consts(p)
    return pl.pallas_call(
        functools.partial(_ffn_in_mlstm_body, tm=tm, nt=nt),
        grid=(ntiles + 1,),
        in_specs=[cur(D_MODEL)] + [_const_spec(c.shape) for c in consts],
        out_specs=[cur(D_MODEL), cur(CONV_CH), prev(M_WIDTH),
                   pl.BlockSpec((1, N_HEADS, HEAD_DIM, HEAD_DIM), lambda g: (prev_seq(g), 0, 0, 0)),
                   pl.BlockSpec((1, N_HEADS, 1, HEAD_DIM), lambda g: (prev_seq(g), 0, 0, 0)),
                   pl.BlockSpec((1, N_HEADS, 1, HEAD_DIM), lambda g: (prev_seq(g), 0, 0, 0))],
        out_shape=[jax.ShapeDtypeStruct((m, D_MODEL), F32), jax.ShapeDtypeStruct((m, CONV_CH), F32),
                   jax.ShapeDtypeStruct((m, M_WIDTH), F32),
                   jax.ShapeDtypeStruct((batch, N_HEADS, HEAD_DIM, HEAD_DIM), F32),
                   jax.ShapeDtypeStruct((batch, N_HEADS, 1, HEAD_DIM), F32),
                   jax.ShapeDtypeStruct((batch, N_HEADS, 1, HEAD_DIM), F32)],
        scratch_shapes=[pltpu.VMEM((2, tm, M_WIDTH), BF16), pltpu.VMEM((2, M_WIDTH, tm), BF16),
                        pltpu.VMEM((2, tm, M_WIDTH), F32), pltpu.VMEM((2, tm, M_WIDTH), F32),
                        pltpu.VMEM((2, 2 * N_HEADS, tm), F32),
                        pltpu.VMEM((N_HEADS, HEAD_DIM, AUG), F32), pltpu.VMEM((N_HEADS, 1, HEAD_DIM), F32)],
        compiler_params=pltpu.CompilerParams(dimension_semantics=("arbitrary",),
                                             vmem_limit_bytes=VMEM_LIMIT_TOKENWISE),
        name="ffn_in_mlstm",
    )(x2d, *consts)


def _conv_ffn_out(u, ym, x1, p, batch, seq, tm):
    m = batch * seq
    nt = seq // tm
    row = lambda w: pl.BlockSpec((tm, w), lambda g: (g, 0))
    consts = [p["conv_w"], p["conv_b"], p["conv_ln_g"], p["conv_ln_b"]] + _ffn_out_consts(p)
    return pl.pallas_call(
        functools.partial(_conv_ffn_out_body, tm=tm, nt=nt),
        grid=(m // tm,),
        in_specs=[row(CONV_CH), row(M_WIDTH), row(D_MODEL)] + [_const_spec(c.shape) for c in consts],
        out_specs=[row(D_MODEL), pl.BlockSpec((1, HIST_PAD, CONV_CH), lambda g: (g // nt, 0, 0))],
        out_shape=[jax.ShapeDtypeStruct((m, D_MODEL), F32), jax.ShapeDtypeStruct((batch, HIST_PAD, CONV_CH), F32)],
        scratch_shapes=[pltpu.VMEM((CONV_CH // LANES, HIST_PAD + tm, LANES), F32),
                        pltpu.VMEM((CONV_CH // LANES, tm, LANES), F32)],
        compiler_params=pltpu.CompilerParams(dimension_semantics=("arbitrary",),
                                             vmem_limit_bytes=VMEM_LIMIT_TOKENWISE),
        name="conv_ffn_out",
    )(u, ym, x1, *consts)


def _ffn_in_body(x_ref, g1_ref, wg_ref, wu_ref, wd_ref, gpost_ref, gmix_ref, wqkvo_ref, wgate_ref, wconv_ref,
                 gbias_ref, x1_ref, q_ref, kt_ref, v_ref, og_ref, u_ref, gt_ref):
    x1, q, k, v, og, u, gcol = _ffn_in_values(x_ref[...], g1_ref, wg_ref, wu_ref, wd_ref, gpost_ref, gmix_ref,
                                              wqkvo_ref, wgate_ref, wconv_ref, gbias_ref)
    for ref, val in zip((x1_ref, q_ref, kt_ref, v_ref, og_ref, u_ref, gt_ref), (x1, q, k.T, v, og, u, _gate_rows(gcol))):
        ref[...] = val


def _ffn_out_body(mix_ref, x1_ref, wout_ref, gmp_ref, g2_ref, wg_ref, wu_ref, wd_ref, gpost_ref, gfin_ref, y_ref):
    y_ref[...] = _ffn_out_values(mix_ref[:, :M_WIDTH], mix_ref[:, M_WIDTH:], x1_ref[...], wout_ref, gmp_ref, g2_ref,
                                 wg_ref, wu_ref, wd_ref, gpost_ref, gfin_ref)


def _mixer_sample_body(q_ref, kt_ref, v_ref, og_ref, u_ref, gt_ref, c0_ref, n0_ref, m0_ref, hist0_ref,
                       cw_ref, cb_ref, lg_ref, lb_ref, mix_ref, c_out, n_out, m_out, hist_out, ubuf, *, nseq, L):
    R = nseq * L
    r = lax.broadcasted_iota(jnp.int32, (R, R), 0)
    c = lax.broadcasted_iota(jnp.int32, (R, R), 1)
    same = lax.div(r, L) == lax.div(c, L)
    caus = same & (c <= r)
    segtriu = jnp.where(same & (r <= c), 1.0, 0.0).astype(F32)
    segones = jnp.where(same, 1.0, 0.0).astype(F32)
    rowseq = lax.div(lax.broadcasted_iota(jnp.int32, (R, AUG), 0), L)
    ones = jnp.ones((R, HEAD_DIM), F32)
    gt = gt_ref[...]
    b_rows = jnp.dot(gt, segtriu, precision=HIGHEST, preferred_element_type=F32)
    tot_rows = jnp.dot(gt, segones, precision=HIGHEST, preferred_element_type=F32)
    for h in range(N_HEADS):
        hs = slice(h * HEAD_DIM, (h + 1) * HEAD_DIM)
        b_r = b_rows[N_HEADS + h:N_HEADS + h + 1, :]
        a_r = gt[h:h + 1, :] - b_r
        t_r = tot_rows[N_HEADS + h:N_HEADS + h + 1, :]
        arow = jnp.broadcast_to(a_r, (R, R))
        bcol = jnp.broadcast_to(b_r, (R, R)).T
        acol = arow.T
        tcol = jnp.broadcast_to(t_r, (R, R)).T
        mprev = m0_ref[h]
        mprev = jnp.concatenate([mprev, mprev], axis=1)
        dmat = jnp.where(caus, bcol + arow, -jnp.inf)
        m_t = jnp.maximum(bcol + mprev, jnp.max(dmat, axis=1, keepdims=True))
        dend = jnp.where(same, tcol + arow, -jnp.inf)
        m_new = jnp.maximum(tcol + mprev, jnp.max(dend, axis=1, keepdims=True))
        qh = q_ref[:, hs].astype(BF16)
        kth = kt_ref[hs, :].astype(BF16)
        s = _dot(qh, kth) * jnp.exp(dmat - m_t)
        vaug = jnp.concatenate([v_ref[:, hs], ones], axis=1)
        w_inter = jnp.exp(bcol + mprev - m_t)
        caug0 = [jnp.concatenate([c0_ref[0, i, h], jnp.broadcast_to(n0_ref[0, i, h:h + 1, :], (HEAD_DIM, HEAD_DIM)).T],
                                 axis=1) for i in range(nseq)]
        qc = jnp.zeros((R, AUG), F32)
        for i in range(nseq):
            qc = jnp.where(rowseq == i, _dot(qh, caug0[i].astype(BF16)), qc)
        sv = _dot(s.astype(BF16), vaug.astype(BF16))
        num = w_inter[:, :HEAD_DIM] * qc[:, :HEAD_DIM] + sv[:, :HEAD_DIM]
        den = w_inter[:, :HEAD_DIM] * qc[:, HEAD_DIM:] + sv[:, HEAD_DIM:]
        hh = num / jnp.maximum(jnp.abs(den), jnp.exp(-m_t[:, :HEAD_DIM]))
        mix_ref[:, hs] = og_ref[:, hs] * hh
        g_state = jnp.exp(tcol + mprev - m_new)
        g_rows = jnp.exp(tcol + acol - m_new)
        gv = g_rows * vaug
        for i in range(nseq):
            gvi = jnp.where(rowseq == i, gv, 0.0).astype(BF16)
            caug = g_state[i * L:i * L + 1, :] * caug0[i] + _dot(kth, gvi)
            c_out[0, i, h] = caug[:, :HEAD_DIM]
            n_out[0, i, h:h + 1, :] = caug[:, HEAD_DIM:].T[0:1, :]
            m_out[i, h] = m_new[i * L:i * L + 1, :HEAD_DIM]

    for i in range(nseq):
        ubuf[0:HIST_PAD, :] = hist0_ref[i]
        ubuf[HIST_PAD:HIST_PAD + L, :] = u_ref[i * L:(i + 1) * L, :]
        acc = _conv_taps(ubuf, 0, L, slice(0, CONV_CH), cw_ref, cb_ref)
        mix_ref[i * L:(i + 1) * L, M_WIDTH:] = _ln_swish(acc, lg_ref, lb_ref)
        hist_out[i] = ubuf[L:L + HIST_PAD, :]


def _ffn_in(x2d, p, tm):
    m = x2d.shape[0]
    row = lambda w: pl.BlockSpec((tm, w), lambda i: (i, 0))
    col = lambda h: pl.BlockSpec((h, tm), lambda i: (0, i))
    consts = _ffn_in_consts(p)
    return pl.pallas_call(
        _ffn_in_body,
        grid=(m // tm,),
        in_specs=[row(D_MODEL)] + [_const_spec(c.shape) for c in consts],
        out_specs=[row(D_MODEL), row(M_WIDTH), col(M_WIDTH), row(M_WIDTH), row(M_WIDTH), row(CONV_CH),
                   col(2 * N_HEADS)],
        out_shape=[jax.ShapeDtypeStruct((m, D_MODEL), F32), jax.ShapeDtypeStruct((m, M_WIDTH), F32),
                   jax.ShapeDtypeStruct((M_WIDTH, m), F32), jax.ShapeDtypeStruct((m, M_WIDTH), F32),
                   jax.ShapeDtypeStruct((m, M_WIDTH), F32), jax.ShapeDtypeStruct((m, CONV_CH), F32),
                   jax.ShapeDtypeStruct((2 * N_HEADS, m), F32)],
        compiler_params=pltpu.CompilerParams(dimension_semantics=("arbitrary",),
                                             vmem_limit_bytes=VMEM_LIMIT_TOKENWISE),
        name="ffn_in",
    )(x2d, *consts)


def _ffn_out(mix, x1, p, tm):
    m = mix.shape[0]
    row = pl.BlockSpec((tm, D_MODEL), lambda i: (i, 0))
    consts = _ffn_out_consts(p)
    return pl.pallas_call(
        _ffn_out_body,
        grid=(m // tm,),
        in_specs=[row, row] + [_const_spec(c.shape) for c in consts],
        out_specs=row,
        out_shape=jax.ShapeDtypeStruct((m, D_MODEL), F32),
        compiler_params=pltpu.CompilerParams(dimension_semantics=("arbitrary",),
                                             vmem_limit_bytes=VMEM_LIMIT_TOKENWISE),
        name="ffn_out",
    )(mix, x1, *consts)


def _mixer_sample(q, kt, v, og, u, gt, c_all, n_all, layer, m0, hist0, p, nseq, L):
    rows = nseq * L
    args = [q, kt, v, og, u, gt, c_all, n_all, m0, hist0, p["conv_w"], p["conv_b"], p["conv_ln_g"], p["conv_ln_b"]]
    full = lambda a: pl.BlockSpec(a.shape, lambda i, nd=a.ndim: (0,) * nd)
    of_layer = lambda a: pl.BlockSpec((1,) + a.shape[1:], lambda i, nd=a.ndim: (layer,) + (0,) * (nd - 1))
    out_shape = [jax.ShapeDtypeStruct((rows, D_MODEL), F32),
                 jax.ShapeDtypeStruct((1, nseq, N_HEADS, HEAD_DIM, HEAD_DIM), F32),
                 jax.ShapeDtypeStruct((1, nseq, N_HEADS, HEAD_DIM), F32),
                 jax.ShapeDtypeStruct((nseq, N_HEADS, 1, HEAD_DIM), F32),
                 jax.ShapeDtypeStruct((nseq, HIST_PAD, CONV_CH), F32)]
    return pl.pallas_call(
        functools.partial(_mixer_sample_body, nseq=nseq, L=L),
        grid=(1,),
        in_specs=[of_layer(a) if a is c_all or a is n_all else full(a) for a in args],
        out_specs=[full(s) for s in out_shape],
        out_shape=out_shape,
        scratch_shapes=[pltpu.VMEM((HIST_PAD + L, CONV_CH), F32)],
        compiler_params=pltpu.CompilerParams(dimension_semantics=("arbitrary",),
                                             vmem_limit_bytes=VMEM_LIMIT_MIXER),
        name="mixer_sample",
    )(*args)


def _layer_params(l, ffn1_pre_g, ffn1_wg, ffn1_wu, ffn1_wd, ffn1_post_g, mix_pre_g, w_in, b_igate, b_fgate,
                  conv_w, conv_b, conv_ln_g, conv_ln_b, w_out, mix_post_g, ffn2_pre_g, ffn2_wg, ffn2_wu, ffn2_wd,
                  ffn2_post_g, final_g):
    vec = lambda a: a[l].astype(F32).reshape(1, -1)
    w = w_in[l]
    return {
        "ffn1_pre_g": vec(ffn1_pre_g), "ffn1_post_g": vec(ffn1_post_g), "mix_pre_g": vec(mix_pre_g),
        "mix_post_g": vec(mix_post_g), "ffn2_pre_g": vec(ffn2_pre_g), "ffn2_post_g": vec(ffn2_post_g),
        "final_g": vec(final_g),
        "ffn1_wg": ffn1_wg[l].astype(BF16), "ffn1_wu": ffn1_wu[l].astype(BF16), "ffn1_wd": ffn1_wd[l].astype(BF16),
        "ffn2_wg": ffn2_wg[l].astype(BF16), "ffn2_wu": ffn2_wu[l].astype(BF16), "ffn2_wd": ffn2_wd[l].astype(BF16),
        "w_qkvo": w[:, :4 * M_WIDTH].astype(BF16),
        "w_gate": jnp.pad(w[:, 4 * M_WIDTH:4 * M_WIDTH + 2 * N_HEADS], ((0, 0), (0, LANES - 2 * N_HEADS))).astype(BF16),
        "w_conv": w[:, 4 * M_WIDTH + 2 * N_HEADS:].astype(BF16),
        "gbias": jnp.pad(jnp.concatenate([b_igate[l], b_fgate[l]]).astype(F32), (0, LANES - 2 * N_HEADS))[None, :],
        "w_out": w_out[l].astype(BF16),
        "conv_w": conv_w[l].astype(F32), "conv_b": vec(conv_b), "conv_ln_g": vec(conv_ln_g),
        "conv_ln_b": vec(conv_ln_b),
    }


def _split_state(c, n, m, hist):
    return c, n[:, :, 0, :], m[:, :, 0, 0], hist[:, HIST_PAD - HIST:, :]


def kernel(x_prompt, x_sample, state_mlstm_C, state_mlstm_n, state_mlstm_m, cache_conv, ffn1_pre_g, ffn1_wg,
           ffn1_wu, ffn1_wd, ffn1_post_g, mix_pre_g, w_in, b_igate, b_fgate, conv_w, conv_b, conv_ln_g, conv_ln_b,
           w_out, mix_post_g, ffn2_pre_g, ffn2_wg, ffn2_wu, ffn2_wd, ffn2_post_g, final_g):
    batch, seq, _ = x_prompt.shape
    nseq, dseq, _ = x_sample.shape
    depth = w_in.shape[0]
    assert seq % PROMPT_TILE == 0 and PROMPT_TILE % CHUNK == 0 and PROMPT_TILE % CONV_ROWS == 0
    assert (nseq * dseq) % TOKEN_TILE == 0 and dseq <= HIST_PAD
    yp = x_prompt.reshape(batch * seq, D_MODEL)
    ys = x_sample.reshape(nseq * dseq, D_MODEL)
    outs_p, outs_s = [], []
    for l in range(depth):
        p = _layer_params(l, ffn1_pre_g, ffn1_wg, ffn1_wu, ffn1_wd, ffn1_post_g, mix_pre_g, w_in, b_igate, b_fgate,
                          conv_w, conv_b, conv_ln_g, conv_ln_b, w_out, mix_post_g, ffn2_pre_g, ffn2_wg, ffn2_wu,
                          ffn2_wd, ffn2_post_g, final_g)
        x1, u, ym, c, n, m = _ffn_in_mlstm(yp, p, batch, seq, 2 * PROMPT_TILE)
        yp, hist = _conv_ffn_out(u, ym, x1, p, batch, seq, 2 * PROMPT_TILE)
        outs_p.append(_split_state(c, n, m, hist))
        x1, q, kt, v, og, u, gt = _ffn_in(ys, p, TOKEN_TILE)
        m0 = jnp.broadcast_to(state_mlstm_m[l].astype(F32).T[:, :, None, None], (N_HEADS, nseq, dseq, HEAD_DIM))
        m0 = m0.reshape(N_HEADS, nseq * dseq, HEAD_DIM)
        hist0 = jnp.pad(cache_conv[l].astype(F32), ((0, 0), (HIST_PAD - HIST, 0), (0, 0)))
        mix, c, n, m, hist = _mixer_sample(q, kt, v, og, u, gt, state_mlstm_C.astype(F32), state_mlstm_n.astype(F32),
                                          l, m0, hist0, p, nseq, dseq)
        ys = _ffn_out(mix, x1, p, TOKEN_TILE)
        outs_s.append((c, n, m[:, :, 0, 0], hist[:, HIST_PAD - HIST:, :]))
    stack = lambda outs, k: jnp.stack([o[k] for o in outs])
    cat = lambda outs, k: outs[0][k] if len(outs) == 1 else jnp.concatenate([o[k] for o in outs])
    return (yp.reshape(batch, seq, D_MODEL), ys.reshape(nseq, dseq, D_MODEL),
            stack(outs_p, 0), stack(outs_p, 1), stack(outs_p, 2), stack(outs_p, 3),
            cat(outs_s, 0), cat(outs_s, 1), stack(outs_s, 2), stack(outs_s, 3))
```

```python
import functools

import jax
import jax.numpy as jnp
from jax import lax
from jax.experimental import pallas as pl
from jax.experimental.pallas import tpu as pltpu

D_MODEL = 1024
D_FF = 2816
N_HEADS = 4
HEAD_DIM = 128
M_WIDTH = N_HEADS * HEAD_DIM
CONV_CH = 512
CONV_WIDTH = 31
HIST = CONV_WIDTH - 1
HIST_PAD = 32
SUBLANES = 8
LANES = 128
EPS = 1e-6
CHUNK = 128
AUG = 2 * HEAD_DIM
CONV_ROWS = 128
LN_ROWS = 32

F32 = jnp.float32
BF16 = jnp.bfloat16
HIGHEST = lax.Precision.HIGHEST
NT_DIMS = (((1,), (1,)), ((), ()))

SAMPLE_TILE = 256
PROMPT_TILE = 512
FFN_ROWS = 256
VMEM_LIMIT_TOKENWISE = 56 * 1024 * 1024
VMEM_LIMIT_MIXER = 40 * 1024 * 1024


def _rms(x, g):
    return x * lax.rsqrt(jnp.mean(x * x, axis=-1, keepdims=True) + EPS) * g


def _dot(a, b):
    return jnp.dot(a, b, preferred_element_type=F32)


def _swiglu(h, wg_ref, wu_ref, wd_ref):
    a = _dot(h, wg_ref[...])
    b = _dot(h, wu_ref[...])
    s = (a * jax.nn.sigmoid(a) * b).astype(BF16)
    return _dot(s, wd_ref[...])


def _gate_rows(pre, gbias_ref):
    g = pre + gbias_ref[...]
    row = lax.broadcasted_iota(jnp.int32, g.shape, 0)
    return jnp.where(row < N_HEADS, g, jax.nn.log_sigmoid(g))


def _ffn_in_values(x, g1_ref, wg_ref, wu_ref, wd_ref, gpost_ref, gmix_ref, wqvo_ref, wktg_ref, wconv_ref, gbias_ref):
    h = _rms(x, g1_ref[...]).astype(BF16)
    d = _swiglu(h, wg_ref, wu_ref, wd_ref)
    x1 = x + 0.5 * _rms(d, gpost_ref[...])
    h2 = _rms(x1, gmix_ref[...]).astype(BF16)
    qvo = _dot(h2, wqvo_ref[...])
    q = qvo[:, :M_WIDTH]
    v = qvo[:, M_WIDTH:2 * M_WIDTH]
    og = jax.nn.sigmoid(qvo[:, 2 * M_WIDTH:])
    ktg = lax.dot_general(wktg_ref[...], h2, NT_DIMS, preferred_element_type=F32)
    kt = ktg[:M_WIDTH] * (HEAD_DIM ** -0.5)
    cc = _dot(h2, wconv_ref[...])
    u = cc[:, :CONV_CH] * jax.nn.sigmoid(cc[:, CONV_CH:])
    return x1, q, kt, v, og, u, _gate_rows(ktg[M_WIDTH:], gbias_ref)


def _ffn_out_values(ym, c, x1, wout_ref, gmp_ref, g2_ref, wg_ref, wu_ref, wd_ref, gpost_ref, gfin_ref):
    o = _dot(ym.astype(BF16), wout_ref[0:M_WIDTH, :]) + _dot(c.astype(BF16), wout_ref[M_WIDTH:, :])
    x2 = x1 + _rms(o, gmp_ref[...])
    h = _rms(x2, g2_ref[...]).astype(BF16)
    d = _swiglu(h, wg_ref, wu_ref, wd_ref)
    x3 = x2 + 0.5 * _rms(d, gpost_ref[...])
    return _rms(x3, gfin_ref[...])


def _ffn_out_phases(ym, c, x1, y_ref, rows, wout_ref, gmp_ref, g2_ref, wg_ref, wu_ref, wd_ref, gpost_ref, gfin_ref):
    o = _dot(ym.astype(BF16), wout_ref[0:M_WIDTH, :]) + _dot(c.astype(BF16), wout_ref[M_WIDTH:, :])
    yield o
    x2 = x1 + _rms(o, gmp_ref[...])
    h = _rms(x2, g2_ref[...]).astype(BF16)
    a = _dot(h, wg_ref[...])
    yield a
    b = _dot(h, wu_ref[...])
    yield b
    s = (a * jax.nn.sigmoid(a) * b).astype(BF16)
    d = _dot(s, wd_ref[...])
    yield d
    x3 = x2 + 0.5 * _rms(d, gpost_ref[...])
    y = _rms(x3, gfin_ref[...])
    y_ref[rows, :] = y
    yield y


def _conv_taps(ubuf, row0, nrows, lanes, cw_ref, cb_ref):
    acc = jnp.broadcast_to(cb_ref[:, lanes], (nrows, lanes.stop - lanes.start))
    first = HIST_PAD - HIST
    for res in range(SUBLANES):
        taps = [j for j in range(CONV_WIDTH) if (first + j) % SUBLANES == res]
        lo = (first + taps[0]) // SUBLANES * SUBLANES
        hi = (first + taps[-1]) // SUBLANES * SUBLANES
        win = ubuf[row0 + lo:row0 + hi + nrows + (SUBLANES if res else 0), lanes]
        if res:
            win = pltpu.roll(win, win.shape[0] - res, 0)
        for j in taps:
            off = first + j - res - lo
            acc = acc + win[off:off + nrows, :] * cw_ref[j:j + 1, lanes]
    return acc


def _conv_taps_interleaved(uslab, row0, nrows, lanes, cw_ref, cb_ref):
    first = HIST_PAD - HIST
    half = nrows // 2
    acc = [jnp.broadcast_to(cb_ref[:, lanes], (half, LANES))] * 2
    for j in range(CONV_WIDTH):
        for parity in range(2):
            x = uslab[pl.ds(row0 + first + j + parity, half, stride=2), :]
            acc[parity] = acc[parity] + x * cw_ref[j:j + 1, lanes]
    return acc


def _ln_swish(acc, lg_ref, lb_ref):
    mu = jnp.mean(acc, axis=-1, keepdims=True)
    xc = acc - mu
    var = jnp.mean(xc * xc, axis=-1, keepdims=True)
    y = xc * lax.rsqrt(var + EPS) * lg_ref[...] + lb_ref[...]
    return y * jax.nn.sigmoid(y)


def _mlstm_gates(b_r, a_r, caus, qh, kth):
    L = b_r.shape[1]
    arow = jnp.broadcast_to(a_r, (L, L))
    bcol = jnp.broadcast_to(b_r, (L, L)).T
    acol = arow.T
    dmat = jnp.where(caus, bcol + arow, -jnp.inf)
    return dict(bcol=bcol, acol=acol, dmat=dmat, rowmax=jnp.max(dmat, axis=1, keepdims=True), s_raw=_dot(qh, kth))


def _mlstm_scores(st, mprev, vh):
    L = st["bcol"].shape[0]
    m_t = jnp.maximum(st["bcol"] + mprev, st["rowmax"])
    s = st["s_raw"] * jnp.exp(st["dmat"] - m_t)
    vaug = jnp.concatenate([vh, jnp.ones_like(vh)], axis=1)
    m_new = m_t[L - 1:L, :]
    b_last = st["bcol"][L - 1:L, :]
    g_rows = jnp.exp(b_last + st["acol"] - m_new)
    new = dict(m_t=m_t, sv=_dot(s.astype(BF16), vaug.astype(BF16)), w_inter=jnp.exp(st["bcol"] + mprev - m_t),
               g_state=jnp.exp(b_last + mprev - m_new),
               gv=(jnp.concatenate([g_rows, g_rows], axis=1) * vaug).astype(BF16))
    st.clear()
    st.update(new)
    return m_new


def _mlstm_output(st, qh, kth, caug):
    m_t, sv, w_inter, g_state = st["m_t"], st["sv"], st["w_inter"], st["g_state"]
    qc = _dot(qh, caug.astype(BF16))
    caug_new = jnp.concatenate([g_state, g_state], axis=1) * caug + _dot(kth, st["gv"])
    num = w_inter * qc[:, :HEAD_DIM] + sv[:, :HEAD_DIM]
    den = w_inter * qc[:, HEAD_DIM:] + sv[:, HEAD_DIM:]
    hh = num / jnp.maximum(jnp.abs(den), jnp.exp(-m_t))
    return hh, caug_new


def _store_state(c_out, n_out, i, h, caug):
    c_out[i, h] = caug[:, :HEAD_DIM]
    n_out[i, h] = caug[:, HEAD_DIM:].T[0:1, :]


def _mlstm_pieces(rslot, fresh, nck, q_st, kt_st, v_st, og_st, g_st, caug_s, m_s, ym_ref, c_out, n_out, m_out):
    L = CHUNK
    r = lax.broadcasted_iota(jnp.int32, (L, L), 0)
    c = lax.broadcasted_iota(jnp.int32, (L, L), 1)
    caus = c <= r
    triu = (r <= c).astype(BF16)
    stages = {}

    def stage1(ck):
        rows = slice(ck * L, (ck + 1) * L)
        gt = g_st[rslot, :, rows]
        hi = gt.astype(BF16)
        mid = (gt - hi.astype(F32)).astype(BF16)
        lo = (gt - hi.astype(F32) - mid.astype(F32)).astype(BF16)
        parts = _dot(jnp.concatenate([hi, mid, lo], axis=0), triu)
        b_rows = parts[0:2 * N_HEADS] + parts[2 * N_HEADS:4 * N_HEADS] + parts[4 * N_HEADS:]
        for h in range(N_HEADS):
            hs = slice(h * HEAD_DIM, (h + 1) * HEAD_DIM)
            b_r = b_rows[N_HEADS + h:N_HEADS + h + 1, :]
            stages[ck, h] = _mlstm_gates(b_r, gt[h:h + 1, :] - b_r, caus, q_st[rslot, rows, hs], kt_st[rslot, hs, rows])

    def stage2(ck):
        rows = slice(ck * L, (ck + 1) * L)
        for h in range(N_HEADS):
            mprev = m_s[h]
            if ck == 0:
                mprev = jnp.where(fresh, 0.0, mprev)
            m_new = _mlstm_scores(stages[ck, h], mprev, v_st[rslot, rows, slice(h * HEAD_DIM, (h + 1) * HEAD_DIM)])
            m_s[h] = m_new
            if ck == nck - 1:
                m_out[0, h] = m_new

    def stage3(ck):
        rows = slice(ck * L, (ck + 1) * L)
        for h in range(N_HEADS):
            hs = slice(h * HEAD_DIM, (h + 1) * HEAD_DIM)
            caug = caug_s[h]
            if ck == 0:
                caug = jnp.where(fresh, 0.0, caug)
            hh, caug = _mlstm_output(stages.pop((ck, h)), q_st[rslot, rows, hs], kt_st[rslot, hs, rows], caug)
            ym_ref[rows, hs] = og_st[rslot, rows, hs] * hh
            caug_s[h] = caug
            if ck == nck - 1:
                _store_state(c_out, n_out, 0, h, caug)

    for t in range(nck + 2):
        for lag, stage in enumerate((stage1, stage2, stage3)):
            if 0 <= t - lag < nck:
                stage(t - lag)
        yield


def _ffn_in_block(x_ref, rows, wslot, g1_ref, wg_ref, wu_ref, wd_ref, gpost_ref, gmix_ref, wqvo_ref, wktg_ref, wconv_ref,
                  gbias_ref, x1_ref, u_ref, q_st, kt_st, v_st, og_st, g_st):
    x = x_ref[rows, :]
    h = _rms(x, g1_ref[...]).astype(BF16)
    a = _dot(h, wg_ref[...])
    yield
    b = _dot(h, wu_ref[...])
    yield
    s = (a * jax.nn.sigmoid(a) * b).astype(BF16)
    d = _dot(s, wd_ref[...])
    yield
    x1 = x + 0.5 * _rms(d, gpost_ref[...])
    x1_ref[rows, :] = x1
    h2 = _rms(x1, gmix_ref[...]).astype(BF16)
    qvo = _dot(h2, wqvo_ref[...])
    q_st[wslot, rows, :] = qvo[:, :M_WIDTH].astype(BF16)
    v_st[wslot, rows, :] = qvo[:, M_WIDTH:2 * M_WIDTH]
    og_st[wslot, rows, :] = jax.nn.sigmoid(qvo[:, 2 * M_WIDTH:])
    yield
    ktg = lax.dot_general(wktg_ref[...], h2, NT_DIMS, preferred_element_type=F32)
    kt_st[wslot, :, rows] = (ktg[:M_WIDTH] * (HEAD_DIM ** -0.5)).astype(BF16)
    g_st[wslot, :, rows] = _gate_rows(ktg[M_WIDTH:], gbias_ref)
    yield
    cc = _dot(h2, wconv_ref[...])
    u_ref[rows, :] = cc[:, :CONV_CH] * jax.nn.sigmoid(cc[:, CONV_CH:])
    yield


def _ffn_in_mlstm_body(x_ref, g1_ref, wg_ref, wu_ref, wd_ref, gpost_ref, gmix_ref, wqvo_ref, wktg_ref, wconv_ref,
                       gbias_ref, x1_ref, u_ref, ym_ref, c_out, n_out, m_out,
                       q_st, kt_st, v_st, og_st, g_st, caug_s, m_s, *, tm, nt):
    g = pl.program_id(0)
    wslot = lax.rem(g, 2)
    rslot = 1 - wslot

    @pl.when(g == 0)
    def _init():
        for st in (q_st, kt_st, v_st, og_st, g_st, caug_s, m_s):
            st[...] = jnp.zeros_like(st)

    pieces = _mlstm_pieces(rslot, lax.rem(g - 1, nt) == 0, tm // CHUNK, q_st, kt_st, v_st, og_st, g_st, caug_s, m_s,
                           ym_ref, c_out, n_out, m_out)
    blocks = [_ffn_in_block(x_ref, slice(r0, r0 + FFN_ROWS), wslot, g1_ref, wg_ref, wu_ref, wd_ref, gpost_ref,
                            gmix_ref, wqvo_ref, wktg_ref, wconv_ref, gbias_ref, x1_ref, u_ref, q_st, kt_st, v_st,
                            og_st, g_st) for r0 in range(0, tm, FFN_ROWS)]
    next(pieces)
    while blocks:
        for block in list(blocks):
            if next(block, blocks) is blocks:
                blocks.remove(block)
            else:
                next(pieces, None)
    for _ in pieces:
        pass


def _conv_ffn_out_body(u_ref, ym_ref, x1_ref, cw_ref, cb_ref, lg_ref, lb_ref, wout_ref, gmp_ref, g2_ref,
                       wg_ref, wu_ref, wd_ref, gpost_ref, gfin_ref, y_ref, hist_out, ubuf, cbuf, *, tm, nt):
    nslab = CONV_CH // LANES

    @pl.when(lax.rem(pl.program_id(0), nt) == 0)
    def _new_sequence():
        ubuf[:, 0:HIST_PAD, :] = jnp.zeros((nslab, HIST_PAD, LANES), F32)

    for lb in range(nslab):
        ubuf[lb, HIST_PAD:HIST_PAD + tm, :] = u_ref[:, lb * LANES:(lb + 1) * LANES]
    for k in range(tm // CONV_ROWS):
        for lb in range(nslab):
            acc = _conv_taps_interleaved(ubuf.at[lb], k * CONV_ROWS, CONV_ROWS, slice(lb * LANES, (lb + 1) * LANES),
                                         cw_ref, cb_ref)
            for parity in range(2):
                cbuf[lb, pl.ds(k * CONV_ROWS + parity, CONV_ROWS // 2, stride=2), :] = acc[parity]
    tail = jnp.concatenate([ubuf[lb, tm:tm + HIST_PAD, :] for lb in range(nslab)], axis=1)
    hist_out[0] = tail
    for lb in range(nslab):
        ubuf[lb, 0:HIST_PAD, :] = tail[:, lb * LANES:(lb + 1) * LANES]

    halves = []
    for r0 in range(0, tm, FFN_ROWS):
        c = []
        for k in range(FFN_ROWS // LN_ROWS):
            rows = slice(r0 + k * LN_ROWS, r0 + (k + 1) * LN_ROWS)
            c.append(_ln_swish(jnp.concatenate([cbuf[lb, rows, :] for lb in range(nslab)], axis=1), lg_ref, lb_ref))
        rows = slice(r0, r0 + FFN_ROWS)
        halves.append(_ffn_out_phases(ym_ref[rows, :], jnp.concatenate(c, axis=0), x1_ref[rows, :], y_ref, rows,
                                      wout_ref, gmp_ref, g2_ref, wg_ref, wu_ref, wd_ref, gpost_ref, gfin_ref))
    for _ in zip(*halves):
        pass


def _const_spec(shape):
    nd = len(shape)
    return pl.BlockSpec(shape, lambda *_: (0,) * nd, pipeline_mode=pl.Buffered(1))


def _ffn_in_consts(p, rows):
    gbias = jnp.broadcast_to(p["gbias"][:, None], (2 * N_HEADS, rows))
    return [p["ffn1_pre_g"], p["ffn1_wg"], p["ffn1_wu"], p["ffn1_wd"], p["ffn1_post_g"], p["mix_pre_g"],
            p["w_qvo"], p["w_ktg"], p["w_conv"], gbias]


def _ffn_out_consts(p):
    return [p["w_out"], p["mix_post_g"], p["ffn2_pre_g"], p["ffn2_wg"], p["ffn2_wu"], p["ffn2_wd"],
            p["ffn2_post_g"], p["final_g"]]


def _ffn_in_mlstm(x2d, p, batch, seq, tm):
    m = batch * seq
    nt = seq // tm
    ntiles = m // tm
    cur = lambda w: pl.BlockSpec((tm, w), lambda g: (jnp.minimum(g, ntiles - 1), 0))
    prev = lambda w: pl.BlockSpec((tm, w), lambda g: (jnp.maximum(g - 1, 0), 0))
    prev_seq = lambda g: jnp.maximum(g - 1, 0) // nt
    consts = _ffn_in_consts(p, FFN_ROWS)
    return pl.pallas_call(
        functools.partial(_ffn_in_mlstm_body, tm=tm, nt=nt),
        grid=(ntiles + 1,),
        in_specs=[cur(D_MODEL)] + [_const_spec(c.shape) for c in consts],
        out_specs=[cur(D_MODEL), cur(CONV_CH), prev(M_WIDTH),
                   pl.BlockSpec((1, N_HEADS, HEAD_DIM, HEAD_DIM), lambda g: (prev_seq(g), 0, 0, 0)),
                   pl.BlockSpec((1, N_HEADS, 1, HEAD_DIM), lambda g: (prev_seq(g), 0, 0, 0)),
                   pl.BlockSpec((1, N_HEADS, 1, HEAD_DIM), lambda g: (prev_seq(g), 0, 0, 0))],
        out_shape=[jax.ShapeDtypeStruct((m, D_MODEL), F32), jax.ShapeDtypeStruct((m, CONV_CH), F32),
                   jax.ShapeDtypeStruct((m, M_WIDTH), F32),
                   jax.ShapeDtypeStruct((batch, N_HEADS, HEAD_DIM, HEAD_DIM), F32),
                   jax.ShapeDtypeStruct((batch, N_HEADS, 1, HEAD_DIM), F32),
                   jax.ShapeDtypeStruct((batch, N_HEADS, 1, HEAD_DIM), F32)],
        scratch_shapes=[pltpu.VMEM((2, tm, M_WIDTH), BF16), pltpu.VMEM((2, M_WIDTH, tm), BF16),
                        pltpu.VMEM((2, tm, M_WIDTH), F32), pltpu.VMEM((2, tm, M_WIDTH), F32),
                        pltpu.VMEM((2, 2 * N_HEADS, tm), F32),
                        pltpu.VMEM((N_HEADS, HEAD_DIM, AUG), F32), pltpu.VMEM((N_HEADS, 1, HEAD_DIM), F32)],
        compiler_params=pltpu.CompilerParams(dimension_semantics=("arbitrary",),
                                             vmem_limit_bytes=VMEM_LIMIT_TOKENWISE),
        name="ffn_in_mlstm",
    )(x2d, *consts)


def _conv_ffn_out(u, ym, x1, p, batch, seq, tm):
    m = batch * seq
    nt = seq // tm
    row = lambda w: pl.BlockSpec((tm, w), lambda g: (g, 0))
    consts = [p["conv_w"], p["conv_b"], p["conv_ln_g"], p["conv_ln_b"]] + _ffn_out_consts(p)
    return pl.pallas_call(
        functools.partial(_conv_ffn_out_body, tm=tm, nt=nt),
        grid=(m // tm,),
        in_specs=[row(CONV_CH), row(M_WIDTH), row(D_MODEL)] + [_const_spec(c.shape) for c in consts],
        out_specs=[row(D_MODEL), pl.BlockSpec((1, HIST_PAD, CONV_CH), lambda g: (g // nt, 0, 0))],
        out_shape=[jax.ShapeDtypeStruct((m, D_MODEL), F32), jax.ShapeDtypeStruct((batch, HIST_PAD, CONV_CH), F32)],
        scratch_shapes=[pltpu.VMEM((CONV_CH // LANES, HIST_PAD + tm, LANES), F32),
                        pltpu.VMEM((CONV_CH // LANES, tm, LANES), F32)],
        compiler_params=pltpu.CompilerParams(dimension_semantics=("arbitrary",),
                                             vmem_limit_bytes=VMEM_LIMIT_TOKENWISE),
        name="conv_ffn_out",
    )(u, ym, x1, *consts)


def _ffn_in_body(x_ref, g1_ref, wg_ref, wu_ref, wd_ref, gpost_ref, gmix_ref, wqvo_ref, wktg_ref, wconv_ref,
                 gbias_ref, x1_ref, q_ref, kt_ref, v_ref, og_ref, u_ref, gt_ref):
    outs = _ffn_in_values(x_ref[...], g1_ref, wg_ref, wu_ref, wd_ref, gpost_ref, gmix_ref, wqvo_ref, wktg_ref,
                          wconv_ref, gbias_ref)
    for ref, val in zip((x1_ref, q_ref, kt_ref, v_ref, og_ref, u_ref, gt_ref), outs):
        ref[...] = val


def _ffn_out_body(mix_ref, x1_ref, wout_ref, gmp_ref, g2_ref, wg_ref, wu_ref, wd_ref, gpost_ref, gfin_ref, y_ref):
    y_ref[...] = _ffn_out_values(mix_ref[:, :M_WIDTH], mix_ref[:, M_WIDTH:], x1_ref[...], wout_ref, gmp_ref, g2_ref,
                                 wg_ref, wu_ref, wd_ref, gpost_ref, gfin_ref)


def _mixer_sample_body(q_ref, kt_ref, v_ref, og_ref, u_ref, gt_ref, c0_ref, n0_ref, m0_ref, hist0_ref,
                       cw_ref, cb_ref, lg_ref, lb_ref, mix_ref, caug_out, m_out, hist_out, ubuf, *, nseq, L):
    R = nseq * L
    r = lax.broadcasted_iota(jnp.int32, (R, R), 0)
    c = lax.broadcasted_iota(jnp.int32, (R, R), 1)
    same = lax.div(r, L) == lax.div(c, L)
    caus = same & (c <= r)
    segtriu = jnp.where(same & (r <= c), 1.0, 0.0).astype(F32)
    segones = jnp.where(same, 1.0, 0.0).astype(F32)
    rowseq = lax.div(lax.broadcasted_iota(jnp.int32, (R, AUG), 0), L)
    ones = jnp.ones((R, HEAD_DIM), F32)
    gt = gt_ref[...]
    b_rows = jnp.dot(gt, segtriu, precision=HIGHEST, preferred_element_type=F32)
    tot_rows = jnp.dot(gt, segones, precision=HIGHEST, preferred_element_type=F32)
    for h in range(N_HEADS):
        hs = slice(h * HEAD_DIM, (h + 1) * HEAD_DIM)
        b_r = b_rows[N_HEADS + h:N_HEADS + h + 1, :]
        a_r = gt[h:h + 1, :] - b_r
        t_r = tot_rows[N_HEADS + h:N_HEADS + h + 1, :]
        arow = jnp.broadcast_to(a_r, (R, R))
        bcol = jnp.broadcast_to(b_r, (R, R)).T
        acol = arow.T
        tcol = jnp.broadcast_to(t_r, (R, R)).T
        mprev = m0_ref[h]
        mprev = jnp.concatenate([mprev, mprev], axis=1)
        dmat = jnp.where(caus, bcol + arow, -jnp.inf)
        m_t = jnp.maximum(bcol + mprev, jnp.max(dmat, axis=1, keepdims=True))
        dend = jnp.where(same, tcol + arow, -jnp.inf)
        m_new = jnp.maximum(tcol + mprev, jnp.max(dend, axis=1, keepdims=True))
        qh = q_ref[:, hs].astype(BF16)
        kth = kt_ref[hs, :].astype(BF16)
        s = _dot(qh, kth) * jnp.exp(dmat - m_t)
        vaug = jnp.concatenate([v_ref[:, hs], ones], axis=1)
        w_inter = jnp.exp(bcol + mprev - m_t)
        caug0 = [jnp.concatenate([c0_ref[0, i, h], jnp.broadcast_to(n0_ref[0, i, h:h + 1, :], (HEAD_DIM, HEAD_DIM)).T],
                                 axis=1) for i in range(nseq)]
        qc = jnp.zeros((R, AUG), F32)
        for i in range(nseq):
            qc = jnp.where(rowseq == i, _dot(qh, caug0[i].astype(BF16)), qc)
        sv = _dot(s.astype(BF16), vaug.astype(BF16))
        num = w_inter[:, :HEAD_DIM] * qc[:, :HEAD_DIM] + sv[:, :HEAD_DIM]
        den = w_inter[:, :HEAD_DIM] * qc[:, HEAD_DIM:] + sv[:, HEAD_DIM:]
        hh = num / jnp.maximum(jnp.abs(den), jnp.exp(-m_t[:, :HEAD_DIM]))
        mix_ref[:, hs] = og_ref[:, hs] * hh
        g_state = jnp.exp(tcol + mprev - m_new)
        g_rows = jnp.exp(tcol + acol - m_new)
        gv = g_rows * vaug
        for i in range(nseq):
            gvi = jnp.where(rowseq == i, gv, 0.0).astype(BF16)
            caug_out[i, h] = g_state[i * L:i * L + 1, :] * caug0[i] + _dot(kth, gvi)
            m_out[i, h] = m_new[i * L:i * L + 1, :HEAD_DIM]

    for i in range(nseq):
        ubuf[0:HIST_PAD, :] = hist0_ref[i]
        ubuf[HIST_PAD:HIST_PAD + L, :] = u_ref[i * L:(i + 1) * L, :]
        acc = _conv_taps(ubuf, 0, L, slice(0, CONV_CH), cw_ref, cb_ref)
        mix_ref[i * L:(i + 1) * L, M_WIDTH:] = _ln_swish(acc, lg_ref, lb_ref)
        hist_out[i] = ubuf[L:L + HIST_PAD, :]


def _ffn_in(x2d, p, tm):
    m = x2d.shape[0]
    row = lambda w: pl.BlockSpec((tm, w), lambda i: (i, 0))
    col = lambda h: pl.BlockSpec((h, tm), lambda i: (0, i))
    consts = _ffn_in_consts(p, tm)
    return pl.pallas_call(
        _ffn_in_body,
        grid=(m // tm,),
        in_specs=[row(D_MODEL)] + [_const_spec(c.shape) for c in consts],
        out_specs=[row(D_MODEL), row(M_WIDTH), col(M_WIDTH), row(M_WIDTH), row(M_WIDTH), row(CONV_CH),
                   col(2 * N_HEADS)],
        out_shape=[jax.ShapeDtypeStruct((m, D_MODEL), F32), jax.ShapeDtypeStruct((m, M_WIDTH), F32),
                   jax.ShapeDtypeStruct((M_WIDTH, m), F32), jax.ShapeDtypeStruct((m, M_WIDTH), F32),
                   jax.ShapeDtypeStruct((m, M_WIDTH), F32), jax.ShapeDtypeStruct((m, CONV_CH), F32),
                   jax.ShapeDtypeStruct((2 * N_HEADS, m), F32)],
        compiler_params=pltpu.CompilerParams(dimension_semantics=("arbitrary",),
                                             vmem_limit_bytes=VMEM_LIMIT_TOKENWISE),
        name="ffn_in",
    )(x2d, *consts)


def _ffn_out(mix, x1, p, tm):
    m = mix.shape[0]
    row = pl.BlockSpec((tm, D_MODEL), lambda i: (i, 0))
    consts = _ffn_out_consts(p)
    return pl.pallas_call(
        _ffn_out_body,
        grid=(m // tm,),
        in_specs=[row, row] + [_const_spec(c.shape) for c in consts],
        out_specs=row,
        out_shape=jax.ShapeDtypeStruct((m, D_MODEL), F32),
        compiler_params=pltpu.CompilerParams(dimension_semantics=("arbitrary",),
                                             vmem_limit_bytes=VMEM_LIMIT_TOKENWISE),
        name="ffn_out",
    )(mix, x1, *consts)


def _mixer_sample(q, kt, v, og, u, gt, c_all, n_all, layer, m0, hist0, p, nseq, L):
    rows = nseq * L
    args = [q, kt, v, og, u, gt, c_all, n_all, m0, hist0, p["conv_w"], p["conv_b"], p["conv_ln_g"], p["conv_ln_b"]]
    full = lambda a: pl.BlockSpec(a.shape, lambda i, nd=a.ndim: (0,) * nd)
    of_layer = lambda a: pl.BlockSpec((1,) + a.shape[1:], lambda i, nd=a.ndim: (layer,) + (0,) * (nd - 1))
    out_shape = [jax.ShapeDtypeStruct((rows, D_MODEL), F32),
                 jax.ShapeDtypeStruct((nseq, N_HEADS, HEAD_DIM, AUG), F32),
                 jax.ShapeDtypeStruct((nseq, N_HEADS, 1, HEAD_DIM), F32),
                 jax.ShapeDtypeStruct((nseq, HIST_PAD, CONV_CH), F32)]
    return pl.pallas_call(
        functools.partial(_mixer_sample_body, nseq=nseq, L=L),
        grid=(1,),
        in_specs=[of_layer(a) if a is c_all or a is n_all else full(a) for a in args],
        out_specs=[full(s) for s in out_shape],
        out_shape=out_shape,
        scratch_shapes=[pltpu.VMEM((HIST_PAD + L, CONV_CH), F32)],
        compiler_params=pltpu.CompilerParams(dimension_semantics=("arbitrary",),
                                             vmem_limit_bytes=VMEM_LIMIT_MIXER),
        name="mixer_sample",
    )(*args)


def _layer_params(l, ffn1_pre_g, ffn1_wg, ffn1_wu, ffn1_wd, ffn1_post_g, mix_pre_g, w_in, b_igate, b_fgate,
                  conv_w, conv_b, conv_ln_g, conv_ln_b, w_out, mix_post_g, ffn2_pre_g, ffn2_wg, ffn2_wu, ffn2_wd,
                  ffn2_post_g, final_g):
    vec = lambda a: a[l].astype(F32).reshape(1, -1)
    w = w_in[l]
    cuts = [0, M_WIDTH, 2 * M_WIDTH, 3 * M_WIDTH, 4 * M_WIDTH, 4 * M_WIDTH + N_HEADS, 4 * M_WIDTH + 2 * N_HEADS,
            4 * M_WIDTH + 2 * N_HEADS + CONV_CH, 4 * M_WIDTH + 2 * N_HEADS + 2 * CONV_CH]
    wq, wk, wv, wo, wi, wf, wcv, wcg = [w[:, a:b] for a, b in zip(cuts[:-1], cuts[1:])]
    return {
        "ffn1_pre_g": vec(ffn1_pre_g), "ffn1_post_g": vec(ffn1_post_g), "mix_pre_g": vec(mix_pre_g),
        "mix_post_g": vec(mix_post_g), "ffn2_pre_g": vec(ffn2_pre_g), "ffn2_post_g": vec(ffn2_post_g),
        "final_g": vec(final_g),
        "ffn1_wg": ffn1_wg[l].astype(BF16), "ffn1_wu": ffn1_wu[l].astype(BF16), "ffn1_wd": ffn1_wd[l].astype(BF16),
        "ffn2_wg": ffn2_wg[l].astype(BF16), "ffn2_wu": ffn2_wu[l].astype(BF16), "ffn2_wd": ffn2_wd[l].astype(BF16),
        "w_qvo": jnp.concatenate([wq, wv, wo], axis=1).astype(BF16),
        "w_ktg": jnp.concatenate([wk, wi, wf], axis=1).T.astype(BF16),
        "w_conv": jnp.concatenate([wcv, wcg], axis=1).astype(BF16),
        "gbias": jnp.concatenate([b_igate[l], b_fgate[l]]).astype(F32),
        "w_out": w_out[l].astype(BF16),
        "conv_w": conv_w[l].astype(F32), "conv_b": vec(conv_b), "conv_ln_g": vec(conv_ln_g),
        "conv_ln_b": vec(conv_ln_b),
    }


def kernel(x_prompt, x_sample, state_mlstm_C, state_mlstm_n, state_mlstm_m, cache_conv, ffn1_pre_g, ffn1_wg,
           ffn1_wu, ffn1_wd, ffn1_post_g, mix_pre_g, w_in, b_igate, b_fgate, conv_w, conv_b, conv_ln_g, conv_ln_b,
           w_out, mix_post_g, ffn2_pre_g, ffn2_wg, ffn2_wu, ffn2_wd, ffn2_post_g, final_g):
    batch, seq, _ = x_prompt.shape
    nseq, dseq, _ = x_sample.shape
    depth = w_in.shape[0]
    assert seq % PROMPT_TILE == 0 and PROMPT_TILE % FFN_ROWS == 0
    assert FFN_ROWS % CHUNK == 0 and FFN_ROWS % CONV_ROWS == 0 and PROMPT_TILE >= HIST_PAD
    assert nseq * dseq == SAMPLE_TILE and dseq <= HIST_PAD
    yp = x_prompt.reshape(batch * seq, D_MODEL)
    ys = x_sample.reshape(nseq * dseq, D_MODEL)
    outs_p, outs_s = [], []
    for l in range(depth):
        p = _layer_params(l, ffn1_pre_g, ffn1_wg, ffn1_wu, ffn1_wd, ffn1_post_g, mix_pre_g, w_in, b_igate, b_fgate,
                          conv_w, conv_b, conv_ln_g, conv_ln_b, w_out, mix_post_g, ffn2_pre_g, ffn2_wg, ffn2_wu,
                          ffn2_wd, ffn2_post_g, final_g)
        x1, u, ym, c, n, m = _ffn_in_mlstm(yp, p, batch, seq, PROMPT_TILE)
        yp, hist = _conv_ffn_out(u, ym, x1, p, batch, seq, PROMPT_TILE)
        outs_p.append((c, n[:, :, 0, :], m[:, :, 0, 0], hist[:, HIST_PAD - HIST:, :]))
        x1, q, kt, v, og, u, gt = _ffn_in(ys, p, SAMPLE_TILE)
        m0 = jnp.broadcast_to(state_mlstm_m[l].astype(F32).T[:, :, None, None], (N_HEADS, nseq, dseq, HEAD_DIM))
        m0 = m0.reshape(N_HEADS, nseq * dseq, HEAD_DIM)
        hist0 = jnp.pad(cache_conv[l].astype(F32), ((0, 0), (HIST_PAD - HIST, 0), (0, 0)))
        mix, caug, m, hist = _mixer_sample(q, kt, v, og, u, gt, state_mlstm_C.astype(F32), state_mlstm_n.astype(F32),
                                           l, m0, hist0, p, nseq, dseq)
        ys = _ffn_out(mix, x1, p, SAMPLE_TILE)
        outs_s.append((caug[..., :HEAD_DIM], caug[..., HEAD_DIM], m[:, :, 0, 0], hist[:, HIST_PAD - HIST:, :]))
    stack = lambda outs, k: jnp.stack([o[k] for o in outs])
    return (yp.reshape(batch, seq, D_MODEL), ys.reshape(nseq, dseq, D_MODEL),
            stack(outs_p, 0), stack(outs_p, 1), stack(outs_p, 2), stack(outs_p, 3),
            stack(outs_s, 0), stack(outs_s, 1), stack(outs_s, 2), stack(outs_s, 3))
```

```python
import functools

import jax
import jax.numpy as jnp
from jax import lax
from jax.experimental import pallas as pl
from jax.experimental.pallas import tpu as pltpu

D_MODEL = 1024
D_FF = 2816
N_HEADS = 4
HEAD_DIM = 128
M_WIDTH = N_HEADS * HEAD_DIM
CONV_CH = 512
CONV_WIDTH = 31
HIST = CONV_WIDTH - 1
HIST_PAD = 32
SUBLANES = 8
LANES = 128
EPS = 1e-6
CHUNK = 128
AUG = 2 * HEAD_DIM
CONV_ROWS = 128
LN_ROWS = 32

F32 = jnp.float32
BF16 = jnp.bfloat16
HIGHEST = lax.Precision.HIGHEST
NT_DIMS = (((1,), (1,)), ((), ()))

SAMPLE_TILE = 256
PROMPT_TILE = 512
FFN_ROWS = 256
FFN_CHUNK = 256
VMEM_LIMIT_TOKENWISE = 56 * 1024 * 1024
VMEM_LIMIT_MIXER = 40 * 1024 * 1024


def _rms(x, g):
    return x * lax.rsqrt(jnp.mean(x * x, axis=-1, keepdims=True) + EPS) * g


def _dot(a, b):
    return jnp.dot(a, b, preferred_element_type=F32)


def _swiglu(h, wg_ref, wu_ref, wd_ref):
    a = _dot(h, wg_ref[...])
    b = _dot(h, wu_ref[...])
    s = (a * jax.nn.sigmoid(a) * b).astype(BF16)
    return _dot(s, wd_ref[...])


def _gate_rows(pre, gbias_ref):
    g = pre + gbias_ref[...]
    row = lax.broadcasted_iota(jnp.int32, g.shape, 0)
    return jnp.where(row < N_HEADS, g, jax.nn.log_sigmoid(g))


def _ffn_in_values(x, g1_ref, wg_ref, wu_ref, wd_ref, gpost_ref, gmix_ref, wqvo_ref, wktg_ref, wconv_ref, gbias_ref):
    h = _rms(x, g1_ref[...]).astype(BF16)
    d = _swiglu(h, wg_ref, wu_ref, wd_ref)
    x1 = x + 0.5 * _rms(d, gpost_ref[...])
    h2 = _rms(x1, gmix_ref[...]).astype(BF16)
    qvo = _dot(h2, wqvo_ref[...])
    q = qvo[:, :M_WIDTH]
    v = qvo[:, M_WIDTH:2 * M_WIDTH]
    og = jax.nn.sigmoid(qvo[:, 2 * M_WIDTH:])
    ktg = lax.dot_general(wktg_ref[...], h2, NT_DIMS, preferred_element_type=F32)
    kt = ktg[:M_WIDTH] * (HEAD_DIM ** -0.5)
    cc = _dot(h2, wconv_ref[...])
    u = cc[:, :CONV_CH] * jax.nn.sigmoid(cc[:, CONV_CH:])
    return x1, q, kt, v, og, u, _gate_rows(ktg[M_WIDTH:], gbias_ref)


def _ffn_out_values(ym, c, x1, wout_ref, gmp_ref, g2_ref, wg_ref, wu_ref, wd_ref, gpost_ref, gfin_ref):
    o = _dot(ym.astype(BF16), wout_ref[0:M_WIDTH, :]) + _dot(c.astype(BF16), wout_ref[M_WIDTH:, :])
    x2 = x1 + _rms(o, gmp_ref[...])
    h = _rms(x2, g2_ref[...]).astype(BF16)
    d = _swiglu(h, wg_ref, wu_ref, wd_ref)
    x3 = x2 + 0.5 * _rms(d, gpost_ref[...])
    return _rms(x3, gfin_ref[...])


def _ffn_out_phases(ym, c, x1, y_ref, rows, wout_ref, gmp_ref, g2_ref, wg_ref, wu_ref, wd_ref, gpost_ref, gfin_ref):
    o = _dot(ym.astype(BF16), wout_ref[0:M_WIDTH, :]) + _dot(c.astype(BF16), wout_ref[M_WIDTH:, :])
    yield o
    x2 = x1 + _rms(o, gmp_ref[...])
    h = _rms(x2, g2_ref[...]).astype(BF16)
    a = _dot(h, wg_ref[...])
    yield a
    b = _dot(h, wu_ref[...])
    yield b
    s = (a * jax.nn.sigmoid(a) * b).astype(BF16)
    d = _dot(s, wd_ref[...])
    yield d
    x3 = x2 + 0.5 * _rms(d, gpost_ref[...])
    y = _rms(x3, gfin_ref[...])
    y_ref[rows, :] = y
    yield y


def _conv_taps(ubuf, row0, nrows, lanes, cw_ref, cb_ref):
    acc = jnp.broadcast_to(cb_ref[:, lanes], (nrows, lanes.stop - lanes.start))
    first = HIST_PAD - HIST
    for res in range(SUBLANES):
        taps = [j for j in range(CONV_WIDTH) if (first + j) % SUBLANES == res]
        lo = (first + taps[0]) // SUBLANES * SUBLANES
        hi = (first + taps[-1]) // SUBLANES * SUBLANES
        win = ubuf[row0 + lo:row0 + hi + nrows + (SUBLANES if res else 0), lanes]
        if res:
            win = pltpu.roll(win, win.shape[0] - res, 0)
        for j in taps:
            off = first + j - res - lo
            acc = acc + win[off:off + nrows, :] * cw_ref[j:j + 1, lanes]
    return acc


def _conv_taps_interleaved(uslab, row0, nrows, lanes, cw_ref, cb_ref):
    first = HIST_PAD - HIST
    half = nrows // 2
    acc = [jnp.broadcast_to(cb_ref[:, lanes], (half, LANES))] * 2
    for j in range(CONV_WIDTH):
        for parity in range(2):
            x = uslab[pl.ds(row0 + first + j + parity, half, stride=2), :]
            acc[parity] = acc[parity] + x * cw_ref[j:j + 1, lanes]
    return acc


def _ln_swish(acc, lg_ref, lb_ref):
    mu = jnp.mean(acc, axis=-1, keepdims=True)
    xc = acc - mu
    var = jnp.mean(xc * xc, axis=-1, keepdims=True)
    y = xc * lax.rsqrt(var + EPS) * lg_ref[...] + lb_ref[...]
    return y * jax.nn.sigmoid(y)


def _mlstm_gates(b_r, a_r, caus, qh, kth):
    L = b_r.shape[1]
    arow = jnp.broadcast_to(a_r, (L, L))
    bcol = jnp.broadcast_to(b_r, (L, L)).T
    acol = arow.T
    dmat = jnp.where(caus, bcol + arow, -jnp.inf)
    return dict(bcol=bcol, acol=acol, dmat=dmat, rowmax=jnp.max(dmat, axis=1, keepdims=True), s_raw=_dot(qh, kth))


def _mlstm_scores(st, mprev, vh):
    L = st["bcol"].shape[0]
    m_t = jnp.maximum(st["bcol"] + mprev, st["rowmax"])
    s = st["s_raw"] * jnp.exp(st["dmat"] - m_t)
    vaug = jnp.concatenate([vh, jnp.ones_like(vh)], axis=1)
    m_new = m_t[L - 1:L, :]
    b_last = st["bcol"][L - 1:L, :]
    g_rows = jnp.exp(b_last + st["acol"] - m_new)
    new = dict(m_t=m_t, sv=_dot(s.astype(BF16), vaug.astype(BF16)), w_inter=jnp.exp(st["bcol"] + mprev - m_t),
               g_state=jnp.exp(b_last + mprev - m_new),
               gv=(jnp.concatenate([g_rows, g_rows], axis=1) * vaug).astype(BF16))
    st.clear()
    st.update(new)
    return m_new


def _mlstm_output(st, qh, kth, caug):
    m_t, sv, w_inter, g_state = st["m_t"], st["sv"], st["w_inter"], st["g_state"]
    qc = _dot(qh, caug.astype(BF16))
    caug_new = jnp.concatenate([g_state, g_state], axis=1) * caug + _dot(kth, st["gv"])
    num = w_inter * qc[:, :HEAD_DIM] + sv[:, :HEAD_DIM]
    den = w_inter * qc[:, HEAD_DIM:] + sv[:, HEAD_DIM:]
    hh = num / jnp.maximum(jnp.abs(den), jnp.exp(-m_t))
    return hh, caug_new


def _store_state(c_out, n_out, i, h, caug):
    c_out[i, h] = caug[:, :HEAD_DIM]
    n_out[i, h] = caug[:, HEAD_DIM:].T[0:1, :]


def _mlstm_pieces(rslot, fresh, nck, q_st, kt_st, v_st, og_st, g_st, caug_s, m_s, ym_ref, c_out, n_out, m_out):
    L = CHUNK
    r = lax.broadcasted_iota(jnp.int32, (L, L), 0)
    c = lax.broadcasted_iota(jnp.int32, (L, L), 1)
    caus = c <= r
    triu = (r <= c).astype(BF16)
    stages = {}

    def stage1(ck):
        rows = slice(ck * L, (ck + 1) * L)
        gt = g_st[rslot, :, rows]
        hi = gt.astype(BF16)
        mid = (gt - hi.astype(F32)).astype(BF16)
        lo = (gt - hi.astype(F32) - mid.astype(F32)).astype(BF16)
        parts = _dot(jnp.concatenate([hi, mid, lo], axis=0), triu)
        b_rows = parts[0:2 * N_HEADS] + parts[2 * N_HEADS:4 * N_HEADS] + parts[4 * N_HEADS:]
        for h in range(N_HEADS):
            hs = slice(h * HEAD_DIM, (h + 1) * HEAD_DIM)
            b_r = b_rows[N_HEADS + h:N_HEADS + h + 1, :]
            stages[ck, h] = _mlstm_gates(b_r, gt[h:h + 1, :] - b_r, caus, q_st[rslot, rows, hs], kt_st[rslot, hs, rows])

    def stage2(ck):
        rows = slice(ck * L, (ck + 1) * L)
        for h in range(N_HEADS):
            mprev = m_s[h]
            if ck == 0:
                mprev = jnp.where(fresh, 0.0, mprev)
            m_new = _mlstm_scores(stages[ck, h], mprev, v_st[rslot, rows, slice(h * HEAD_DIM, (h + 1) * HEAD_DIM)])
            m_s[h] = m_new
            if ck == nck - 1:
                m_out[0, h] = m_new

    def stage3(ck):
        rows = slice(ck * L, (ck + 1) * L)
        for h in range(N_HEADS):
            hs = slice(h * HEAD_DIM, (h + 1) * HEAD_DIM)
            caug = caug_s[h]
            if ck == 0:
                caug = jnp.where(fresh, 0.0, caug)
            hh, caug = _mlstm_output(stages.pop((ck, h)), q_st[rslot, rows, hs], kt_st[rslot, hs, rows], caug)
            ym_ref[rows, hs] = og_st[rslot, rows, hs] * hh
            caug_s[h] = caug
            if ck == nck - 1:
                _store_state(c_out, n_out, 0, h, caug)

    for t in range(nck + 2):
        for lag, stage in enumerate((stage1, stage2, stage3)):
            if 0 <= t - lag < nck:
                stage(t - lag)
        yield


def _ffn_in_block(x_ref, rows, wslot, g1_ref, wg_ref, wu_ref, wd_ref, gpost_ref, gmix_ref, wqvo_ref, wktg_ref, wconv_ref,
                  gbias_ref, x1_ref, u_ref, q_st, kt_st, v_st, og_st, g_st):
    x = x_ref[rows, :]
    h = _rms(x, g1_ref[...]).astype(BF16)
    a = _dot(h, wg_ref[...])
    yield
    b = _dot(h, wu_ref[...])
    yield
    s = (a * jax.nn.sigmoid(a) * b).astype(BF16)
    d = _dot(s, wd_ref[...])
    yield
    x1 = x + 0.5 * _rms(d, gpost_ref[...])
    x1_ref[rows, :] = x1
    h2 = _rms(x1, gmix_ref[...]).astype(BF16)
    qvo = _dot(h2, wqvo_ref[...])
    q_st[wslot, rows, :] = qvo[:, :M_WIDTH].astype(BF16)
    v_st[wslot, rows, :] = qvo[:, M_WIDTH:2 * M_WIDTH]
    og_st[wslot, rows, :] = jax.nn.sigmoid(qvo[:, 2 * M_WIDTH:])
    yield
    ktg = lax.dot_general(wktg_ref[...], h2, NT_DIMS, preferred_element_type=F32)
    kt_st[wslot, :, rows] = (ktg[:M_WIDTH] * (HEAD_DIM ** -0.5)).astype(BF16)
    g_st[wslot, :, rows] = _gate_rows(ktg[M_WIDTH:], gbias_ref)
    yield
    cc = _dot(h2, wconv_ref[...])
    u_ref[rows, :] = cc[:, :CONV_CH] * jax.nn.sigmoid(cc[:, CONV_CH:])
    yield


def _ffn_in_mlstm_body(x_ref, g1_ref, wg_ref, wu_ref, wd_ref, gpost_ref, gmix_ref, wqvo_ref, wktg_ref, wconv_ref,
                       gbias_ref, x1_ref, u_ref, ym_ref, c_out, n_out, m_out,
                       q_st, kt_st, v_st, og_st, g_st, caug_s, m_s, *, tm, nt):
    g = pl.program_id(0)
    wslot = lax.rem(g, 2)
    rslot = 1 - wslot

    @pl.when(g == 0)
    def _init():
        for st in (q_st, kt_st, v_st, og_st, g_st, caug_s, m_s):
            st[...] = jnp.zeros_like(st)

    pieces = _mlstm_pieces(rslot, lax.rem(g - 1, nt) == 0, tm // CHUNK, q_st, kt_st, v_st, og_st, g_st, caug_s, m_s,
                           ym_ref, c_out, n_out, m_out)
    blocks = [_ffn_in_block(x_ref, slice(r0, r0 + FFN_ROWS), wslot, g1_ref, wg_ref, wu_ref, wd_ref, gpost_ref,
                            gmix_ref, wqvo_ref, wktg_ref, wconv_ref, gbias_ref, x1_ref, u_ref, q_st, kt_st, v_st,
                            og_st, g_st) for r0 in range(0, tm, FFN_ROWS)]
    next(pieces)
    while blocks:
        for block in list(blocks):
            if next(block, blocks) is blocks:
                blocks.remove(block)
            else:
                next(pieces, None)
    for _ in pieces:
        pass


def _conv_ffn_out_body(u_ref, ym_ref, x1_ref, cw_ref, cb_ref, lg_ref, lb_ref, wout_ref, gmp_ref, g2_ref,
                       wg_ref, wu_ref, wd_ref, gpost_ref, gfin_ref, y_ref, hist_out, ubuf, cbuf, *, tm, nt):
    nslab = CONV_CH // LANES

    @pl.when(lax.rem(pl.program_id(0), nt) == 0)
    def _new_sequence():
        ubuf[:, 0:HIST_PAD, :] = jnp.zeros((nslab, HIST_PAD, LANES), F32)

    for lb in range(nslab):
        ubuf[lb, HIST_PAD:HIST_PAD + tm, :] = u_ref[:, lb * LANES:(lb + 1) * LANES]
    for k in range(tm // CONV_ROWS):
        for lb in range(nslab):
            acc = _conv_taps_interleaved(ubuf.at[lb], k * CONV_ROWS, CONV_ROWS, slice(lb * LANES, (lb + 1) * LANES),
                                         cw_ref, cb_ref)
            for parity in range(2):
                cbuf[lb, pl.ds(k * CONV_ROWS + parity, CONV_ROWS // 2, stride=2), :] = acc[parity]
    tail = jnp.concatenate([ubuf[lb, tm:tm + HIST_PAD, :] for lb in range(nslab)], axis=1)
    hist_out[0] = tail
    for lb in range(nslab):
        ubuf[lb, 0:HIST_PAD, :] = tail[:, lb * LANES:(lb + 1) * LANES]

    halves = []
    for r0 in range(0, tm, FFN_ROWS):
        c = []
        for k in range(FFN_ROWS // LN_ROWS):
            rows = slice(r0 + k * LN_ROWS, r0 + (k + 1) * LN_ROWS)
            c.append(_ln_swish(jnp.concatenate([cbuf[lb, rows, :] for lb in range(nslab)], axis=1), lg_ref, lb_ref))
        rows = slice(r0, r0 + FFN_ROWS)
        halves.append(_ffn_out_phases(ym_ref[rows, :], jnp.concatenate(c, axis=0), x1_ref[rows, :], y_ref, rows,
                                      wout_ref, gmp_ref, g2_ref, wg_ref, wu_ref, wd_ref, gpost_ref, gfin_ref))
    for _ in zip(*halves):
        pass


def _const_spec(shape):
    nd = len(shape)
    return pl.BlockSpec(shape, lambda *_: (0,) * nd, pipeline_mode=pl.Buffered(1))


def _ffn_in_consts(p, rows):
    gbias = jnp.broadcast_to(p["gbias"][:, None], (2 * N_HEADS, rows))
    return [p["ffn1_pre_g"], p["ffn1_wg"], p["ffn1_wu"], p["ffn1_wd"], p["ffn1_post_g"], p["mix_pre_g"],
            p["w_qvo"], p["w_ktg"], p["w_conv"], gbias]


def _ffn_out_consts(p):
    return [p["w_out"], p["mix_post_g"], p["ffn2_pre_g"], p["ffn2_wg"], p["ffn2_wu"], p["ffn2_wd"],
            p["ffn2_post_g"], p["final_g"]]


def _ffn_in_mlstm(x2d, p, batch, seq, tm):
    m = batch * seq
    nt = seq // tm
    ntiles = m // tm
    cur = lambda w: pl.BlockSpec((tm, w), lambda g: (jnp.minimum(g, ntiles - 1), 0))
    prev = lambda w: pl.BlockSpec((tm, w), lambda g: (jnp.maximum(g - 1, 0), 0))
    prev_seq = lambda g: jnp.maximum(g - 1, 0) // nt
    consts = _ffn_in_consts(p, FFN_ROWS)
    return pl.pallas_call(
        functools.partial(_ffn_in_mlstm_body, tm=tm, nt=nt),
        grid=(ntiles + 1,),
        in_specs=[cur(D_MODEL)] + [_const_spec(c.shape) for c in consts],
        out_specs=[cur(D_MODEL), cur(CONV_CH), prev(M_WIDTH),
                   pl.BlockSpec((1, N_HEADS, HEAD_DIM, HEAD_DIM), lambda g: (prev_seq(g), 0, 0, 0)),
                   pl.BlockSpec((1, N_HEADS, 1, HEAD_DIM), lambda g: (prev_seq(g), 0, 0, 0)),
                   pl.BlockSpec((1, N_HEADS, 1, HEAD_DIM), lambda g: (prev_seq(g), 0, 0, 0))],
        out_shape=[jax.ShapeDtypeStruct((m, D_MODEL), F32), jax.ShapeDtypeStruct((m, CONV_CH), F32),
                   jax.ShapeDtypeStruct((m, M_WIDTH), F32),
                   jax.ShapeDtypeStruct((batch, N_HEADS, HEAD_DIM, HEAD_DIM), F32),
                   jax.ShapeDtypeStruct((batch, N_HEADS, 1, HEAD_DIM), F32),
                   jax.ShapeDtypeStruct((batch, N_HEADS, 1, HEAD_DIM), F32)],
        scratch_shapes=[pltpu.VMEM((2, tm, M_WIDTH), BF16), pltpu.VMEM((2, M_WIDTH, tm), BF16),
                        pltpu.VMEM((2, tm, M_WIDTH), F32), pltpu.VMEM((2, tm, M_WIDTH), F32),
                        pltpu.VMEM((2, 2 * N_HEADS, tm), F32),
                        pltpu.VMEM((N_HEADS, HEAD_DIM, AUG), F32), pltpu.VMEM((N_HEADS, 1, HEAD_DIM), F32)],
        compiler_params=pltpu.CompilerParams(dimension_semantics=("arbitrary",),
                                             vmem_limit_bytes=VMEM_LIMIT_TOKENWISE),
        name="ffn_in_mlstm",
    )(x2d, *consts)


def _conv_ffn_out(u, ym, x1, p, batch, seq, tm):
    m = batch * seq
    nt = seq // tm
    row = lambda w: pl.BlockSpec((tm, w), lambda g: (g, 0))
    consts = [p["conv_w"], p["conv_b"], p["conv_ln_g"], p["conv_ln_b"]] + _ffn_out_consts(p)
    return pl.pallas_call(
        functools.partial(_conv_ffn_out_body, tm=tm, nt=nt),
        grid=(m // tm,),
        in_specs=[row(CONV_CH), row(M_WIDTH), row(D_MODEL)] + [_const_spec(c.shape) for c in consts],
        out_specs=[row(D_MODEL), pl.BlockSpec((1, HIST_PAD, CONV_CH), lambda g: (g // nt, 0, 0))],
        out_shape=[jax.ShapeDtypeStruct((m, D_MODEL), F32), jax.ShapeDtypeStruct((batch, HIST_PAD, CONV_CH), F32)],
        scratch_shapes=[pltpu.VMEM((CONV_CH // LANES, HIST_PAD + tm, LANES), F32),
                        pltpu.VMEM((CONV_CH // LANES, tm, LANES), F32)],
        compiler_params=pltpu.CompilerParams(dimension_semantics=("arbitrary",),
                                             vmem_limit_bytes=VMEM_LIMIT_TOKENWISE),
        name="conv_ffn_out",
    )(u, ym, x1, *consts)


def _swiglu_step(h_s, acc_s, wg_ref, wu_ref, wd_ref):
    a = _dot(h_s[...], wg_ref[...])
    b = _dot(h_s[...], wu_ref[...])
    acc_s[...] += _dot((a * jax.nn.sigmoid(a) * b).astype(BF16), wd_ref[...])


def _ffn_in_body(x_ref, g1_ref, wg_ref, wu_ref, wd_ref, gpost_ref, gmix_ref, wqvo_ref, wktg_ref, wconv_ref,
                 gbias_ref, x1_ref, q_ref, kt_ref, v_ref, og_ref, u_ref, gt_ref, h_s, acc_s):
    step = pl.program_id(0)

    @pl.when(step == 0)
    def _first():
        h_s[...] = _rms(x_ref[...], g1_ref[...]).astype(BF16)
        acc_s[...] = jnp.zeros_like(acc_s)

    _swiglu_step(h_s, acc_s, wg_ref, wu_ref, wd_ref)

    @pl.when(step == pl.num_programs(0) - 1)
    def _last():
        x1 = x_ref[...] + 0.5 * _rms(acc_s[...], gpost_ref[...])
        h2 = _rms(x1, gmix_ref[...]).astype(BF16)
        qvo = _dot(h2, wqvo_ref[...])
        ktg = lax.dot_general(wktg_ref[...], h2, NT_DIMS, preferred_element_type=F32)
        cc = _dot(h2, wconv_ref[...])
        x1_ref[...] = x1
        q_ref[...] = qvo[:, :M_WIDTH]
        kt_ref[...] = ktg[:M_WIDTH] * (HEAD_DIM ** -0.5)
        v_ref[...] = qvo[:, M_WIDTH:2 * M_WIDTH]
        og_ref[...] = jax.nn.sigmoid(qvo[:, 2 * M_WIDTH:])
        u_ref[...] = cc[:, :CONV_CH] * jax.nn.sigmoid(cc[:, CONV_CH:])
        gt_ref[...] = _gate_rows(ktg[M_WIDTH:], gbias_ref)


def _ffn_out_body(mix_ref, x1_ref, wout_ref, gmp_ref, g2_ref, wg_ref, wu_ref, wd_ref, gpost_ref, gfin_ref, y_ref,
                  h_s, acc_s, x2_s):
    step = pl.program_id(0)

    @pl.when(step == 0)
    def _first():
        o = (_dot(mix_ref[:, :M_WIDTH].astype(BF16), wout_ref[0:M_WIDTH, :])
             + _dot(mix_ref[:, M_WIDTH:].astype(BF16), wout_ref[M_WIDTH:, :]))
        x2 = x1_ref[...] + _rms(o, gmp_ref[...])
        x2_s[...] = x2
        h_s[...] = _rms(x2, g2_ref[...]).astype(BF16)
        acc_s[...] = jnp.zeros_like(acc_s)

    _swiglu_step(h_s, acc_s, wg_ref, wu_ref, wd_ref)

    @pl.when(step == pl.num_programs(0) - 1)
    def _last():
        x3 = x2_s[...] + 0.5 * _rms(acc_s[...], gpost_ref[...])
        y_ref[...] = _rms(x3, gfin_ref[...])


def _mixer_sample_body(q_ref, kt_ref, v_ref, og_ref, u_ref, gt_ref, c0_ref, n0_ref, m0_ref, hist0_ref,
                       cw_ref, cb_ref, lg_ref, lb_ref, mix_ref, caug_out, m_out, hist_out, ubuf, *, nseq, L):
    R = nseq * L
    r = lax.broadcasted_iota(jnp.int32, (R, R), 0)
    c = lax.broadcasted_iota(jnp.int32, (R, R), 1)
    same = lax.div(r, L) == lax.div(c, L)
    caus = same & (c <= r)
    segtriu = jnp.where(same & (r <= c), 1.0, 0.0).astype(F32)
    segones = jnp.where(same, 1.0, 0.0).astype(F32)
    rowseq = lax.div(lax.broadcasted_iota(jnp.int32, (R, AUG), 0), L)
    ones = jnp.ones((R, HEAD_DIM), F32)
    gt = gt_ref[...]
    b_rows = jnp.dot(gt, segtriu, precision=HIGHEST, preferred_element_type=F32)
    tot_rows = jnp.dot(gt, segones, precision=HIGHEST, preferred_element_type=F32)
    for h in range(N_HEADS):
        hs = slice(h * HEAD_DIM, (h + 1) * HEAD_DIM)
        b_r = b_rows[N_HEADS + h:N_HEADS + h + 1, :]
        a_r = gt[h:h + 1, :] - b_r
        t_r = tot_rows[N_HEADS + h:N_HEADS + h + 1, :]
        arow = jnp.broadcast_to(a_r, (R, R))
        bcol = jnp.broadcast_to(b_r, (R, R)).T
        acol = arow.T
        tcol = jnp.broadcast_to(t_r, (R, R)).T
        mprev = m0_ref[h]
        mprev = jnp.concatenate([mprev, mprev], axis=1)
        dmat = jnp.where(caus, bcol + arow, -jnp.inf)
        m_t = jnp.maximum(bcol + mprev, jnp.max(dmat, axis=1, keepdims=True))
        dend = jnp.where(same, tcol + arow, -jnp.inf)
        m_new = jnp.maximum(tcol + mprev, jnp.max(dend, axis=1, keepdims=True))
        qh = q_ref[:, hs].astype(BF16)
        kth = kt_ref[hs, :].astype(BF16)
        s = _dot(qh, kth) * jnp.exp(dmat - m_t)
        vaug = jnp.concatenate([v_ref[:, hs], ones], axis=1)
        w_inter = jnp.exp(bcol + mprev - m_t)
        caug0 = [jnp.concatenate([c0_ref[0, i, h], jnp.broadcast_to(n0_ref[0, i, h:h + 1, :], (HEAD_DIM, HEAD_DIM)).T],
                                 axis=1) for i in range(nseq)]
        qc = jnp.zeros((R, AUG), F32)
        for i in range(nseq):
            qc = jnp.where(rowseq == i, _dot(qh, caug0[i].astype(BF16)), qc)
        sv = _dot(s.astype(BF16), vaug.astype(BF16))
        num = w_inter[:, :HEAD_DIM] * qc[:, :HEAD_DIM] + sv[:, :HEAD_DIM]
        den = w_inter[:, :HEAD_DIM] * qc[:, HEAD_DIM:] + sv[:, HEAD_DIM:]
        hh = num / jnp.maximum(jnp.abs(den), jnp.exp(-m_t[:, :HEAD_DIM]))
        mix_ref[:, hs] = og_ref[:, hs] * hh
        g_state = jnp.exp(tcol + mprev - m_new)
        g_rows = jnp.exp(tcol + acol - m_new)
        gv = g_rows * vaug
        for i in range(nseq):
            gvi = jnp.where(rowseq == i, gv, 0.0).astype(BF16)
            caug_out[i, h] = g_state[i * L:i * L + 1, :] * caug0[i] + _dot(kth, gvi)
            m_out[i, h] = m_new[i * L:i * L + 1, :HEAD_DIM]

    for i in range(nseq):
        ubuf[0:HIST_PAD, :] = hist0_ref[i]
        ubuf[HIST_PAD:HIST_PAD + L, :] = u_ref[i * L:(i + 1) * L, :]
        acc = _conv_taps(ubuf, 0, L, slice(0, CONV_CH), cw_ref, cb_ref)
        mix_ref[i * L:(i + 1) * L, M_WIDTH:] = _ln_swish(acc, lg_ref, lb_ref)
        hist_out[i] = ubuf[L:L + HIST_PAD, :]


def _streamed_ffn_specs(consts, wg, wu, wd):
    def spec(c):
        if c is wg or c is wu:
            return pl.BlockSpec((D_MODEL, FFN_CHUNK), lambda i: (0, i))
        if c is wd:
            return pl.BlockSpec((FFN_CHUNK, D_MODEL), lambda i: (i, 0))
        return _const_spec(c.shape)
    return [spec(c) for c in consts]


def _ffn_in(x2d, p):
    m = x2d.shape[0]
    full = lambda *shape: pl.BlockSpec(shape, lambda i: (0,) * len(shape))
    consts = _ffn_in_consts(p, m)
    return pl.pallas_call(
        _ffn_in_body,
        grid=(D_FF // FFN_CHUNK,),
        in_specs=[full(m, D_MODEL)] + _streamed_ffn_specs(consts, p["ffn1_wg"], p["ffn1_wu"], p["ffn1_wd"]),
        out_specs=[full(m, D_MODEL), full(m, M_WIDTH), full(M_WIDTH, m), full(m, M_WIDTH), full(m, M_WIDTH),
                   full(m, CONV_CH), full(2 * N_HEADS, m)],
        out_shape=[jax.ShapeDtypeStruct((m, D_MODEL), F32), jax.ShapeDtypeStruct((m, M_WIDTH), F32),
                   jax.ShapeDtypeStruct((M_WIDTH, m), F32), jax.ShapeDtypeStruct((m, M_WIDTH), F32),
                   jax.ShapeDtypeStruct((m, M_WIDTH), F32), jax.ShapeDtypeStruct((m, CONV_CH), F32),
                   jax.ShapeDtypeStruct((2 * N_HEADS, m), F32)],
        scratch_shapes=[pltpu.VMEM((m, D_MODEL), BF16), pltpu.VMEM((m, D_MODEL), F32)],
        compiler_params=pltpu.CompilerParams(dimension_semantics=("arbitrary",),
                                             vmem_limit_bytes=VMEM_LIMIT_TOKENWISE),
        name="ffn_in",
    )(x2d, *consts)


def _ffn_out(mix, x1, p):
    m = mix.shape[0]
    full = pl.BlockSpec((m, D_MODEL), lambda i: (0, 0))
    consts = _ffn_out_consts(p)
    return pl.pallas_call(
        _ffn_out_body,
        grid=(D_FF // FFN_CHUNK,),
        in_specs=[full, full] + _streamed_ffn_specs(consts, p["ffn2_wg"], p["ffn2_wu"], p["ffn2_wd"]),
        out_specs=full,
        out_shape=jax.ShapeDtypeStruct((m, D_MODEL), F32),
        scratch_shapes=[pltpu.VMEM((m, D_MODEL), BF16), pltpu.VMEM((m, D_MODEL), F32), pltpu.VMEM((m, D_MODEL), F32)],
        compiler_params=pltpu.CompilerParams(dimension_semantics=("arbitrary",),
                                             vmem_limit_bytes=VMEM_LIMIT_TOKENWISE),
        name="ffn_out",
    )(mix, x1, *consts)


def _mixer_sample(q, kt, v, og, u, gt, c_all, n_all, layer, m0, hist0, p, nseq, L):
    rows = nseq * L
    args = [q, kt, v, og, u, gt, c_all, n_all, m0, hist0, p["conv_w"], p["conv_b"], p["conv_ln_g"], p["conv_ln_b"]]
    full = lambda a: pl.BlockSpec(a.shape, lambda i, nd=a.ndim: (0,) * nd)
    of_layer = lambda a: pl.BlockSpec((1,) + a.shape[1:], lambda i, nd=a.ndim: (layer,) + (0,) * (nd - 1))
    out_shape = [jax.ShapeDtypeStruct((rows, D_MODEL), F32),
                 jax.ShapeDtypeStruct((nseq, N_HEADS, HEAD_DIM, AUG), F32),
                 jax.ShapeDtypeStruct((nseq, N_HEADS, 1, HEAD_DIM), F32),
                 jax.ShapeDtypeStruct((nseq, HIST_PAD, CONV_CH), F32)]
    return pl.pallas_call(
        functools.partial(_mixer_sample_body, nseq=nseq, L=L),
        grid=(1,),
        in_specs=[of_layer(a) if a is c_all or a is n_all else full(a) for a in args],
        out_specs=[full(s) for s in out_shape],
        out_shape=out_shape,
        scratch_shapes=[pltpu.VMEM((HIST_PAD + L, CONV_CH), F32)],
        compiler_params=pltpu.CompilerParams(dimension_semantics=("arbitrary",),
                                             vmem_limit_bytes=VMEM_LIMIT_MIXER),
        name="mixer_sample",
    )(*args)


def _layer_params(l, ffn1_pre_g, ffn1_wg, ffn1_wu, ffn1_wd, ffn1_post_g, mix_pre_g, w_in, b_igate, b_fgate,
                  conv_w, conv_b, conv_ln_g, conv_ln_b, w_out, mix_post_g, ffn2_pre_g, ffn2_wg, ffn2_wu, ffn2_wd,
                  ffn2_post_g, final_g):
    vec = lambda a: a[l].astype(F32).reshape(1, -1)
    w = w_in[l]
    cuts = [0, M_WIDTH, 2 * M_WIDTH, 3 * M_WIDTH, 4 * M_WIDTH, 4 * M_WIDTH + N_HEADS, 4 * M_WIDTH + 2 * N_HEADS,
            4 * M_WIDTH + 2 * N_HEADS + CONV_CH, 4 * M_WIDTH + 2 * N_HEADS + 2 * CONV_CH]
    wq, wk, wv, wo, wi, wf, wcv, wcg = [w[:, a:b] for a, b in zip(cuts[:-1], cuts[1:])]
    return {
        "ffn1_pre_g": vec(ffn1_pre_g), "ffn1_post_g": vec(ffn1_post_g), "mix_pre_g": vec(mix_pre_g),
        "mix_post_g": vec(mix_post_g), "ffn2_pre_g": vec(ffn2_pre_g), "ffn2_post_g": vec(ffn2_post_g),
        "final_g": vec(final_g),
        "ffn1_wg": ffn1_wg[l].astype(BF16), "ffn1_wu": ffn1_wu[l].astype(BF16), "ffn1_wd": ffn1_wd[l].astype(BF16),
        "ffn2_wg": ffn2_wg[l].astype(BF16), "ffn2_wu": ffn2_wu[l].astype(BF16), "ffn2_wd": ffn2_wd[l].astype(BF16),
        "w_qvo": jnp.concatenate([wq, wv, wo], axis=1).astype(BF16),
        "w_ktg": jnp.concatenate([wk, wi, wf], axis=1).T.astype(BF16),
        "w_conv": jnp.concatenate([wcv, wcg], axis=1).astype(BF16),
        "gbias": jnp.concatenate([b_igate[l], b_fgate[l]]).astype(F32),
        "w_out": w_out[l].astype(BF16),
        "conv_w": conv_w[l].astype(F32), "conv_b": vec(conv_b), "conv_ln_g": vec(conv_ln_g),
        "conv_ln_b": vec(conv_ln_b),
    }


def kernel(x_prompt, x_sample, state_mlstm_C, state_mlstm_n, state_mlstm_m, cache_conv, ffn1_pre_g, ffn1_wg,
           ffn1_wu, ffn1_wd, ffn1_post_g, mix_pre_g, w_in, b_igate, b_fgate, conv_w, conv_b, conv_ln_g, conv_ln_b,
           w_out, mix_post_g, ffn2_pre_g, ffn2_wg, ffn2_wu, ffn2_wd, ffn2_post_g, final_g):
    batch, seq, _ = x_prompt.shape
    nseq, dseq, _ = x_sample.shape
    depth = w_in.shape[0]
    assert seq % PROMPT_TILE == 0 and PROMPT_TILE % FFN_ROWS == 0
    assert FFN_ROWS % CHUNK == 0 and FFN_ROWS % CONV_ROWS == 0 and PROMPT_TILE >= HIST_PAD
    assert nseq * dseq == SAMPLE_TILE and dseq <= HIST_PAD
    yp = x_prompt.reshape(batch * seq, D_MODEL)
    ys = x_sample.reshape(nseq * dseq, D_MODEL)
    outs_p, outs_s = [], []
    for l in range(depth):
        p = _layer_params(l, ffn1_pre_g, ffn1_wg, ffn1_wu, ffn1_wd, ffn1_post_g, mix_pre_g, w_in, b_igate, b_fgate,
                          conv_w, conv_b, conv_ln_g, conv_ln_b, w_out, mix_post_g, ffn2_pre_g, ffn2_wg, ffn2_wu,
                          ffn2_wd, ffn2_post_g, final_g)
        x1, u, ym, c, n, m = _ffn_in_mlstm(yp, p, batch, seq, PROMPT_TILE)
        yp, hist = _conv_ffn_out(u, ym, x1, p, batch, seq, PROMPT_TILE)
        outs_p.append((c, n[:, :, 0, :], m[:, :, 0, 0], hist[:, HIST_PAD - HIST:, :]))
        x1, q, kt, v, og, u, gt = _ffn_in(ys, p)
        m0 = jnp.broadcast_to(state_mlstm_m[l].astype(F32).T[:, :, None, None], (N_HEADS, nseq, dseq, HEAD_DIM))
        m0 = m0.reshape(N_HEADS, nseq * dseq, HEAD_DIM)
        hist0 = jnp.pad(cache_conv[l].astype(F32), ((0, 0), (HIST_PAD - HIST, 0), (0, 0)))
        mix, caug, m, hist = _mixer_sample(q, kt, v, og, u, gt, state_mlstm_C.astype(F32), state_mlstm_n.astype(F32),
                                           l, m0, hist0, p, nseq, dseq)
        ys = _ffn_out(mix, x1, p)
        outs_s.append((caug[..., :HEAD_DIM], caug[..., HEAD_DIM], m[:, :, 0, 0], hist[:, HIST_PAD - HIST:, :]))
    stack = lambda outs, k: jnp.stack([o[k] for o in outs])
    return (yp.reshape(batch, seq, D_MODEL), ys.reshape(nseq, dseq, D_MODEL),
            stack(outs_p, 0), stack(outs_p, 1), stack(outs_p, 2), stack(outs_p, 3),
            stack(outs_s, 0), stack(outs_s, 1), stack(outs_s, 2), stack(outs_s, 3))
```

```python
import functools

import jax
import jax.numpy as jnp
from jax import lax
from jax.experimental import pallas as pl
from jax.experimental.pallas import tpu as pltpu

D_MODEL = 1024
D_FF = 2816
N_HEADS = 4
HEAD_DIM = 128
M_WIDTH = N_HEADS * HEAD_DIM
CONV_CH = 512
CONV_WIDTH = 31
HIST = CONV_WIDTH - 1
HIST_PAD = 32
SUBLANES = 8
LANES = 128
EPS = 1e-6
CHUNK = 128
AUG = 2 * HEAD_DIM
CONV_ROWS = 128
LN_ROWS = 32

F32 = jnp.float32
BF16 = jnp.bfloat16
HIGHEST = lax.Precision.HIGHEST
NT_DIMS = (((1,), (1,)), ((), ()))

SAMPLE_TILE = 256
PROMPT_TILE = 512
FFN_ROWS = 256
FFN_CHUNK = 1408
VMEM_LIMIT_TOKENWISE = 56 * 1024 * 1024
VMEM_LIMIT_MIXER = 40 * 1024 * 1024


def _rms(x, g):
    return x * lax.rsqrt(jnp.mean(x * x, axis=-1, keepdims=True) + EPS) * g


def _dot(a, b):
    return jnp.dot(a, b, preferred_element_type=F32)


def _swiglu(h, wg_ref, wu_ref, wd_ref):
    a = _dot(h, wg_ref[...])
    b = _dot(h, wu_ref[...])
    s = (a * jax.nn.sigmoid(a) * b).astype(BF16)
    return _dot(s, wd_ref[...])


def _gate_rows(pre, gbias_ref):
    g = pre + gbias_ref[...]
    row = lax.broadcasted_iota(jnp.int32, g.shape, 0)
    return jnp.where(row < N_HEADS, g, jax.nn.log_sigmoid(g))


def _ffn_in_values(x, g1_ref, wg_ref, wu_ref, wd_ref, gpost_ref, gmix_ref, wqvo_ref, wktg_ref, wconv_ref, gbias_ref):
    h = _rms(x, g1_ref[...]).astype(BF16)
    d = _swiglu(h, wg_ref, wu_ref, wd_ref)
    x1 = x + 0.5 * _rms(d, gpost_ref[...])
    h2 = _rms(x1, gmix_ref[...]).astype(BF16)
    qvo = _dot(h2, wqvo_ref[...])
    q = qvo[:, :M_WIDTH]
    v = qvo[:, M_WIDTH:2 * M_WIDTH]
    og = jax.nn.sigmoid(qvo[:, 2 * M_WIDTH:])
    ktg = lax.dot_general(wktg_ref[...], h2, NT_DIMS, preferred_element_type=F32)
    kt = ktg[:M_WIDTH] * (HEAD_DIM ** -0.5)
    cc = _dot(h2, wconv_ref[...])
    u = cc[:, :CONV_CH] * jax.nn.sigmoid(cc[:, CONV_CH:])
    return x1, q, kt, v, og, u, _gate_rows(ktg[M_WIDTH:], gbias_ref)


def _ffn_out_values(ym, c, x1, wout_ref, gmp_ref, g2_ref, wg_ref, wu_ref, wd_ref, gpost_ref, gfin_ref):
    o = _dot(ym.astype(BF16), wout_ref[0:M_WIDTH, :]) + _dot(c.astype(BF16), wout_ref[M_WIDTH:, :])
    x2 = x1 + _rms(o, gmp_ref[...])
    h = _rms(x2, g2_ref[...]).astype(BF16)
    d = _swiglu(h, wg_ref, wu_ref, wd_ref)
    x3 = x2 + 0.5 * _rms(d, gpost_ref[...])
    return _rms(x3, gfin_ref[...])


def _ffn_out_phases(ym, c, x1, y_ref, rows, wout_ref, gmp_ref, g2_ref, wg_ref, wu_ref, wd_ref, gpost_ref, gfin_ref):
    o = _dot(ym.astype(BF16), wout_ref[0:M_WIDTH, :]) + _dot(c.astype(BF16), wout_ref[M_WIDTH:, :])
    yield o
    x2 = x1 + _rms(o, gmp_ref[...])
    h = _rms(x2, g2_ref[...]).astype(BF16)
    a = _dot(h, wg_ref[...])
    yield a
    b = _dot(h, wu_ref[...])
    yield b
    s = (a * jax.nn.sigmoid(a) * b).astype(BF16)
    d = _dot(s, wd_ref[...])
    yield d
    x3 = x2 + 0.5 * _rms(d, gpost_ref[...])
    y = _rms(x3, gfin_ref[...])
    y_ref[rows, :] = y
    yield y


def _conv_taps(ubuf, row0, nrows, lanes, cw_ref, cb_ref):
    acc = jnp.broadcast_to(cb_ref[:, lanes], (nrows, lanes.stop - lanes.start))
    first = HIST_PAD - HIST
    for res in range(SUBLANES):
        taps = [j for j in range(CONV_WIDTH) if (first + j) % SUBLANES == res]
        lo = (first + taps[0]) // SUBLANES * SUBLANES
        hi = (first + taps[-1]) // SUBLANES * SUBLANES
        win = ubuf[row0 + lo:row0 + hi + nrows + (SUBLANES if res else 0), lanes]
        if res:
            win = pltpu.roll(win, win.shape[0] - res, 0)
        for j in taps:
            off = first + j - res - lo
            acc = acc + win[off:off + nrows, :] * cw_ref[j:j + 1, lanes]
    return acc


def _conv_taps_interleaved(uslab, row0, nrows, lanes, cw_ref, cb_ref):
    first = HIST_PAD - HIST
    half = nrows // 2
    acc = [jnp.broadcast_to(cb_ref[:, lanes], (half, LANES))] * 2
    for j in range(CONV_WIDTH):
        for parity in range(2):
            x = uslab[pl.ds(row0 + first + j + parity, half, stride=2), :]
            acc[parity] = acc[parity] + x * cw_ref[j:j + 1, lanes]
    return acc


def _ln_swish(acc, lg_ref, lb_ref):
    mu = jnp.mean(acc, axis=-1, keepdims=True)
    xc = acc - mu
    var = jnp.mean(xc * xc, axis=-1, keepdims=True)
    y = xc * lax.rsqrt(var + EPS) * lg_ref[...] + lb_ref[...]
    return y * jax.nn.sigmoid(y)


def _mlstm_gates(b_r, a_r, caus, qh, kth):
    L = b_r.shape[1]
    arow = jnp.broadcast_to(a_r, (L, L))
    bcol = jnp.broadcast_to(b_r, (L, L)).T
    acol = arow.T
    dmat = jnp.where(caus, bcol + arow, -jnp.inf)
    return dict(bcol=bcol, acol=acol, dmat=dmat, rowmax=jnp.max(dmat, axis=1, keepdims=True), s_raw=_dot(qh, kth))


def _mlstm_scores(st, mprev, vh):
    L = st["bcol"].shape[0]
    m_t = jnp.maximum(st["bcol"] + mprev, st["rowmax"])
    s = st["s_raw"] * jnp.exp(st["dmat"] - m_t)
    vaug = jnp.concatenate([vh, jnp.ones_like(vh)], axis=1)
    m_new = m_t[L - 1:L, :]
    b_last = st["bcol"][L - 1:L, :]
    g_rows = jnp.exp(b_last + st["acol"] - m_new)
    new = dict(m_t=m_t, sv=_dot(s.astype(BF16), vaug.astype(BF16)), w_inter=jnp.exp(st["bcol"] + mprev - m_t),
               g_state=jnp.exp(b_last + mprev - m_new),
               gv=(jnp.concatenate([g_rows, g_rows], axis=1) * vaug).astype(BF16))
    st.clear()
    st.update(new)
    return m_new


def _mlstm_output(st, qh, kth, caug):
    m_t, sv, w_inter, g_state = st["m_t"], st["sv"], st["w_inter"], st["g_state"]
    qc = _dot(qh, caug.astype(BF16))
    caug_new = jnp.concatenate([g_state, g_state], axis=1) * caug + _dot(kth, st["gv"])
    num = w_inter * qc[:, :HEAD_DIM] + sv[:, :HEAD_DIM]
    den = w_inter * qc[:, HEAD_DIM:] + sv[:, HEAD_DIM:]
    hh = num / jnp.maximum(jnp.abs(den), jnp.exp(-m_t))
    return hh, caug_new


def _store_state(c_out, n_out, i, h, caug):
    c_out[i, h] = caug[:, :HEAD_DIM]
    n_out[i, h] = caug[:, HEAD_DIM:].T[0:1, :]


def _mlstm_pieces(rslot, fresh, nck, q_st, kt_st, v_st, og_st, g_st, caug_s, m_s, ym_ref, c_out, n_out, m_out):
    L = CHUNK
    r = lax.broadcasted_iota(jnp.int32, (L, L), 0)
    c = lax.broadcasted_iota(jnp.int32, (L, L), 1)
    caus = c <= r
    triu = (r <= c).astype(BF16)
    stages = {}

    def stage1(ck):
        rows = slice(ck * L, (ck + 1) * L)
        gt = g_st[rslot, :, rows]
        hi = gt.astype(BF16)
        mid = (gt - hi.astype(F32)).astype(BF16)
        lo = (gt - hi.astype(F32) - mid.astype(F32)).astype(BF16)
        parts = _dot(jnp.concatenate([hi, mid, lo], axis=0), triu)
        b_rows = parts[0:2 * N_HEADS] + parts[2 * N_HEADS:4 * N_HEADS] + parts[4 * N_HEADS:]
        for h in range(N_HEADS):
            hs = slice(h * HEAD_DIM, (h + 1) * HEAD_DIM)
            b_r = b_rows[N_HEADS + h:N_HEADS + h + 1, :]
            stages[ck, h] = _mlstm_gates(b_r, gt[h:h + 1, :] - b_r, caus, q_st[rslot, rows, hs], kt_st[rslot, hs, rows])

    def stage2(ck):
        rows = slice(ck * L, (ck + 1) * L)
        for h in range(N_HEADS):
            mprev = m_s[h]
            if ck == 0:
                mprev = jnp.where(fresh, 0.0, mprev)
            m_new = _mlstm_scores(stages[ck, h], mprev, v_st[rslot, rows, slice(h * HEAD_DIM, (h + 1) * HEAD_DIM)])
            m_s[h] = m_new
            if ck == nck - 1:
                m_out[0, h] = m_new

    def stage3(ck):
        rows = slice(ck * L, (ck + 1) * L)
        for h in range(N_HEADS):
            hs = slice(h * HEAD_DIM, (h + 1) * HEAD_DIM)
            caug = caug_s[h]
            if ck == 0:
                caug = jnp.where(fresh, 0.0, caug)
            hh, caug = _mlstm_output(stages.pop((ck, h)), q_st[rslot, rows, hs], kt_st[rslot, hs, rows], caug)
            ym_ref[rows, hs] = og_st[rslot, rows, hs] * hh
            caug_s[h] = caug
            if ck == nck - 1:
                _store_state(c_out, n_out, 0, h, caug)

    for t in range(nck + 2):
        for lag, stage in enumerate((stage1, stage2, stage3)):
            if 0 <= t - lag < nck:
                stage(t - lag)
        yield


def _ffn_in_block(x_ref, rows, wslot, g1_ref, wg_ref, wu_ref, wd_ref, gpost_ref, gmix_ref, wqvo_ref, wktg_ref, wconv_ref,
                  gbias_ref, x1_ref, u_ref, q_st, kt_st, v_st, og_st, g_st):
    x = x_ref[rows, :]
    h = _rms(x, g1_ref[...]).astype(BF16)
    a = _dot(h, wg_ref[...])
    yield
    b = _dot(h, wu_ref[...])
    yield
    s = (a * jax.nn.sigmoid(a) * b).astype(BF16)
    d = _dot(s, wd_ref[...])
    yield
    x1 = x + 0.5 * _rms(d, gpost_ref[...])
    x1_ref[rows, :] = x1
    h2 = _rms(x1, gmix_ref[...]).astype(BF16)
    qvo = _dot(h2, wqvo_ref[...])
    q_st[wslot, rows, :] = qvo[:, :M_WIDTH].astype(BF16)
    v_st[wslot, rows, :] = qvo[:, M_WIDTH:2 * M_WIDTH]
    og_st[wslot, rows, :] = jax.nn.sigmoid(qvo[:, 2 * M_WIDTH:])
    yield
    ktg = lax.dot_general(wktg_ref[...], h2, NT_DIMS, preferred_element_type=F32)
    kt_st[wslot, :, rows] = (ktg[:M_WIDTH] * (HEAD_DIM ** -0.5)).astype(BF16)
    g_st[wslot, :, rows] = _gate_rows(ktg[M_WIDTH:], gbias_ref)
    yield
    cc = _dot(h2, wconv_ref[...])
    u_ref[rows, :] = cc[:, :CONV_CH] * jax.nn.sigmoid(cc[:, CONV_CH:])
    yield


def _ffn_in_mlstm_body(x_ref, g1_ref, wg_ref, wu_ref, wd_ref, gpost_ref, gmix_ref, wqvo_ref, wktg_ref, wconv_ref,
                       gbias_ref, x1_ref, u_ref, ym_ref, c_out, n_out, m_out,
                       q_st, kt_st, v_st, og_st, g_st, caug_s, m_s, *, tm, nt):
    g = pl.program_id(0)
    wslot = lax.rem(g, 2)
    rslot = 1 - wslot

    @pl.when(g == 0)
    def _init():
        for st in (q_st, kt_st, v_st, og_st, g_st, caug_s, m_s):
            st[...] = jnp.zeros_like(st)

    pieces = _mlstm_pieces(rslot, lax.rem(g - 1, nt) == 0, tm // CHUNK, q_st, kt_st, v_st, og_st, g_st, caug_s, m_s,
                           ym_ref, c_out, n_out, m_out)
    blocks = [_ffn_in_block(x_ref, slice(r0, r0 + FFN_ROWS), wslot, g1_ref, wg_ref, wu_ref, wd_ref, gpost_ref,
                            gmix_ref, wqvo_ref, wktg_ref, wconv_ref, gbias_ref, x1_ref, u_ref, q_st, kt_st, v_st,
                            og_st, g_st) for r0 in range(0, tm, FFN_ROWS)]
    next(pieces)
    while blocks:
        for block in list(blocks):
            if next(block, blocks) is blocks:
                blocks.remove(block)
            else:
                next(pieces, None)
    for _ in pieces:
        pass


def _conv_ffn_out_body(u_ref, ym_ref, x1_ref, cw_ref, cb_ref, lg_ref, lb_ref, wout_ref, gmp_ref, g2_ref,
                       wg_ref, wu_ref, wd_ref, gpost_ref, gfin_ref, y_ref, hist_out, ubuf, cbuf, *, tm, nt):
    nslab = CONV_CH // LANES

    @pl.when(lax.rem(pl.program_id(0), nt) == 0)
    def _new_sequence():
        ubuf[:, 0:HIST_PAD, :] = jnp.zeros((nslab, HIST_PAD, LANES), F32)

    for lb in range(nslab):
        ubuf[lb, HIST_PAD:HIST_PAD + tm, :] = u_ref[:, lb * LANES:(lb + 1) * LANES]
    for k in range(tm // CONV_ROWS):
        for lb in range(nslab):
            acc = _conv_taps_interleaved(ubuf.at[lb], k * CONV_ROWS, CONV_ROWS, slice(lb * LANES, (lb + 1) * LANES),
                                         cw_ref, cb_ref)
            for parity in range(2):
                cbuf[lb, pl.ds(k * CONV_ROWS + parity, CONV_ROWS // 2, stride=2), :] = acc[parity]
    tail = jnp.concatenate([ubuf[lb, tm:tm + HIST_PAD, :] for lb in range(nslab)], axis=1)
    hist_out[0] = tail
    for lb in range(nslab):
        ubuf[lb, 0:HIST_PAD, :] = tail[:, lb * LANES:(lb + 1) * LANES]

    halves = []
    for r0 in range(0, tm, FFN_ROWS):
        c = []
        for k in range(FFN_ROWS // LN_ROWS):
            rows = slice(r0 + k * LN_ROWS, r0 + (k + 1) * LN_ROWS)
            c.append(_ln_swish(jnp.concatenate([cbuf[lb, rows, :] for lb in range(nslab)], axis=1), lg_ref, lb_ref))
        rows = slice(r0, r0 + FFN_ROWS)
        halves.append(_ffn_out_phases(ym_ref[rows, :], jnp.concatenate(c, axis=0), x1_ref[rows, :], y_ref, rows,
                                      wout_ref, gmp_ref, g2_ref, wg_ref, wu_ref, wd_ref, gpost_ref, gfin_ref))
    for _ in zip(*halves):
        pass


def _const_spec(shape):
    nd = len(shape)
    return pl.BlockSpec(shape, lambda *_: (0,) * nd, pipeline_mode=pl.Buffered(1))


def _ffn_in_consts(p, rows):
    gbias = jnp.broadcast_to(p["gbias"][:, None], (2 * N_HEADS, rows))
    return [p["ffn1_pre_g"], p["ffn1_wg"], p["ffn1_wu"], p["ffn1_wd"], p["ffn1_post_g"], p["mix_pre_g"],
            p["w_qvo"], p["w_ktg"], p["w_conv"], gbias]


def _ffn_out_consts(p):
    return [p["w_out"], p["mix_post_g"], p["ffn2_pre_g"], p["ffn2_wg"], p["ffn2_wu"], p["ffn2_wd"],
            p["ffn2_post_g"], p["final_g"]]


def _ffn_in_mlstm(x2d, p, batch, seq, tm):
    m = batch * seq
    nt = seq // tm
    ntiles = m // tm
    cur = lambda w: pl.BlockSpec((tm, w), lambda g: (jnp.minimum(g, ntiles - 1), 0))
    prev = lambda w: pl.BlockSpec((tm, w), lambda g: (jnp.maximum(g - 1, 0), 0))
    prev_seq = lambda g: jnp.maximum(g - 1, 0) // nt
    consts = _ffn_in_consts(p, FFN_ROWS)
    return pl.pallas_call(
        functools.partial(_ffn_in_mlstm_body, tm=tm, nt=nt),
        grid=(ntiles + 1,),
        in_specs=[cur(D_MODEL)] + [_const_spec(c.shape) for c in consts],
        out_specs=[cur(D_MODEL), cur(CONV_CH), prev(M_WIDTH),
                   pl.BlockSpec((1, N_HEADS, HEAD_DIM, HEAD_DIM), lambda g: (prev_seq(g), 0, 0, 0)),
                   pl.BlockSpec((1, N_HEADS, 1, HEAD_DIM), lambda g: (prev_seq(g), 0, 0, 0)),
                   pl.BlockSpec((1, N_HEADS, 1, HEAD_DIM), lambda g: (prev_seq(g), 0, 0, 0))],
        out_shape=[jax.ShapeDtypeStruct((m, D_MODEL), F32), jax.ShapeDtypeStruct((m, CONV_CH), F32),
                   jax.ShapeDtypeStruct((m, M_WIDTH), F32),
                   jax.ShapeDtypeStruct((batch, N_HEADS, HEAD_DIM, HEAD_DIM), F32),
                   jax.ShapeDtypeStruct((batch, N_HEADS, 1, HEAD_DIM), F32),
                   jax.ShapeDtypeStruct((batch, N_HEADS, 1, HEAD_DIM), F32)],
        scratch_shapes=[pltpu.VMEM((2, tm, M_WIDTH), BF16), pltpu.VMEM((2, M_WIDTH, tm), BF16),
                        pltpu.VMEM((2, tm, M_WIDTH), F32), pltpu.VMEM((2, tm, M_WIDTH), F32),
                        pltpu.VMEM((2, 2 * N_HEADS, tm), F32),
                        pltpu.VMEM((N_HEADS, HEAD_DIM, AUG), F32), pltpu.VMEM((N_HEADS, 1, HEAD_DIM), F32)],
        compiler_params=pltpu.CompilerParams(dimension_semantics=("arbitrary",),
                                             vmem_limit_bytes=VMEM_LIMIT_TOKENWISE),
        name="ffn_in_mlstm",
    )(x2d, *consts)


def _conv_ffn_out(u, ym, x1, p, batch, seq, tm):
    m = batch * seq
    nt = seq // tm
    row = lambda w: pl.BlockSpec((tm, w), lambda g: (g, 0))
    consts = [p["conv_w"], p["conv_b"], p["conv_ln_g"], p["conv_ln_b"]] + _ffn_out_consts(p)
    return pl.pallas_call(
        functools.partial(_conv_ffn_out_body, tm=tm, nt=nt),
        grid=(m // tm,),
        in_specs=[row(CONV_CH), row(M_WIDTH), row(D_MODEL)] + [_const_spec(c.shape) for c in consts],
        out_specs=[row(D_MODEL), pl.BlockSpec((1, HIST_PAD, CONV_CH), lambda g: (g // nt, 0, 0))],
        out_shape=[jax.ShapeDtypeStruct((m, D_MODEL), F32), jax.ShapeDtypeStruct((batch, HIST_PAD, CONV_CH), F32)],
        scratch_shapes=[pltpu.VMEM((CONV_CH // LANES, HIST_PAD + tm, LANES), F32),
                        pltpu.VMEM((CONV_CH // LANES, tm, LANES), F32)],
        compiler_params=pltpu.CompilerParams(dimension_semantics=("arbitrary",),
                                             vmem_limit_bytes=VMEM_LIMIT_TOKENWISE),
        name="conv_ffn_out",
    )(u, ym, x1, *consts)


def _swiglu_step(h_s, acc_s, wg_ref, wu_ref, wd_ref):
    a = _dot(h_s[...], wg_ref[...])
    b = _dot(h_s[...], wu_ref[...])
    acc_s[...] += _dot((a * jax.nn.sigmoid(a) * b).astype(BF16), wd_ref[...])


def _ffn_in_body(x_ref, g1_ref, wg_ref, wu_ref, wd_ref, gpost_ref, gmix_ref, wqvo_ref, wktg_ref, wconv_ref,
                 gbias_ref, x1_ref, q_ref, kt_ref, v_ref, og_ref, u_ref, gt_ref, h_s, acc_s):
    step = pl.program_id(0)

    @pl.when(step == 0)
    def _first():
        h_s[...] = _rms(x_ref[...], g1_ref[...]).astype(BF16)
        acc_s[...] = jnp.zeros_like(acc_s)

    _swiglu_step(h_s, acc_s, wg_ref, wu_ref, wd_ref)

    @pl.when(step == pl.num_programs(0) - 1)
    def _last():
        x1 = x_ref[...] + 0.5 * _rms(acc_s[...], gpost_ref[...])
        h2 = _rms(x1, gmix_ref[...]).astype(BF16)
        qvo = _dot(h2, wqvo_ref[...])
        ktg = lax.dot_general(wktg_ref[...], h2, NT_DIMS, preferred_element_type=F32)
        cc = _dot(h2, wconv_ref[...])
        x1_ref[...] = x1
        q_ref[...] = qvo[:, :M_WIDTH]
        kt_ref[...] = ktg[:M_WIDTH] * (HEAD_DIM ** -0.5)
        v_ref[...] = qvo[:, M_WIDTH:2 * M_WIDTH]
        og_ref[...] = jax.nn.sigmoid(qvo[:, 2 * M_WIDTH:])
        u_ref[...] = cc[:, :CONV_CH] * jax.nn.sigmoid(cc[:, CONV_CH:])
        gt_ref[...] = _gate_rows(ktg[M_WIDTH:], gbias_ref)


def _ffn_out_body(mix_ref, x1_ref, wout_ref, gmp_ref, g2_ref, wg_ref, wu_ref, wd_ref, gpost_ref, gfin_ref, y_ref,
                  h_s, acc_s, x2_s):
    step = pl.program_id(0)

    @pl.when(step == 0)
    def _first():
        o = (_dot(mix_ref[:, :M_WIDTH].astype(BF16), wout_ref[0:M_WIDTH, :])
             + _dot(mix_ref[:, M_WIDTH:].astype(BF16), wout_ref[M_WIDTH:, :]))
        x2 = x1_ref[...] + _rms(o, gmp_ref[...])
        x2_s[...] = x2
        h_s[...] = _rms(x2, g2_ref[...]).astype(BF16)
        acc_s[...] = jnp.zeros_like(acc_s)

    _swiglu_step(h_s, acc_s, wg_ref, wu_ref, wd_ref)

    @pl.when(step == pl.num_programs(0) - 1)
    def _last():
        x3 = x2_s[...] + 0.5 * _rms(acc_s[...], gpost_ref[...])
        y_ref[...] = _rms(x3, gfin_ref[...])


def _mixer_sample_body(q_ref, kt_ref, v_ref, og_ref, u_ref, gt_ref, c0_ref, n0_ref, m0_ref, hist0_ref,
                       cw_ref, cb_ref, lg_ref, lb_ref, mix_ref, caug_out, m_out, hist_out, ubuf, *, nseq, L):
    R = nseq * L
    r = lax.broadcasted_iota(jnp.int32, (R, R), 0)
    c = lax.broadcasted_iota(jnp.int32, (R, R), 1)
    same = lax.div(r, L) == lax.div(c, L)
    caus = same & (c <= r)
    segtriu = jnp.where(same & (r <= c), 1.0, 0.0).astype(F32)
    segones = jnp.where(same, 1.0, 0.0).astype(F32)
    rowseq = lax.div(lax.broadcasted_iota(jnp.int32, (R, AUG), 0), L)
    ones = jnp.ones((R, HEAD_DIM), F32)
    gt = gt_ref[...]
    b_rows = jnp.dot(gt, segtriu, precision=HIGHEST, preferred_element_type=F32)
    tot_rows = jnp.dot(gt, segones, precision=HIGHEST, preferred_element_type=F32)
    for h in range(N_HEADS):
        hs = slice(h * HEAD_DIM, (h + 1) * HEAD_DIM)
        b_r = b_rows[N_HEADS + h:N_HEADS + h + 1, :]
        a_r = gt[h:h + 1, :] - b_r
        t_r = tot_rows[N_HEADS + h:N_HEADS + h + 1, :]
        arow = jnp.broadcast_to(a_r, (R, R))
        bcol = jnp.broadcast_to(b_r, (R, R)).T
        acol = arow.T
        tcol = jnp.broadcast_to(t_r, (R, R)).T
        mprev = m0_ref[h]
        mprev = jnp.concatenate([mprev, mprev], axis=1)
        dmat = jnp.where(caus, bcol + arow, -jnp.inf)
        m_t = jnp.maximum(bcol + mprev, jnp.max(dmat, axis=1, keepdims=True))
        dend = jnp.where(same, tcol + arow, -jnp.inf)
        m_new = jnp.maximum(tcol + mprev, jnp.max(dend, axis=1, keepdims=True))
        qh = q_ref[:, hs].astype(BF16)
        kth = kt_ref[hs, :].astype(BF16)
        s = _dot(qh, kth) * jnp.exp(dmat - m_t)
        vaug = jnp.concatenate([v_ref[:, hs], ones], axis=1)
        w_inter = jnp.exp(bcol + mprev - m_t)
        caug0 = [jnp.concatenate([c0_ref[0, i, h], jnp.broadcast_to(n0_ref[0, i, h:h + 1, :], (HEAD_DIM, HEAD_DIM)).T],
                                 axis=1) for i in range(nseq)]
        qc = jnp.zeros((R, AUG), F32)
        for i in range(nseq):
            qc = jnp.where(rowseq == i, _dot(qh, caug0[i].astype(BF16)), qc)
        sv = _dot(s.astype(BF16), vaug.astype(BF16))
        num = w_inter[:, :HEAD_DIM] * qc[:, :HEAD_DIM] + sv[:, :HEAD_DIM]
        den = w_inter[:, :HEAD_DIM] * qc[:, HEAD_DIM:] + sv[:, HEAD_DIM:]
        hh = num / jnp.maximum(jnp.abs(den), jnp.exp(-m_t[:, :HEAD_DIM]))
        mix_ref[:, hs] = og_ref[:, hs] * hh
        g_state = jnp.exp(tcol + mprev - m_new)
        g_rows = jnp.exp(tcol + acol - m_new)
        gv = g_rows * vaug
        for i in range(nseq):
            gvi = jnp.where(rowseq == i, gv, 0.0).astype(BF16)
            caug_out[i, h] = g_state[i * L:i * L + 1, :] * caug0[i] + _dot(kth, gvi)
            m_out[i, h] = m_new[i * L:i * L + 1, :HEAD_DIM]

    for i in range(nseq):
        ubuf[0:HIST_PAD, :] = hist0_ref[i]
        ubuf[HIST_PAD:HIST_PAD + L, :] = u_ref[i * L:(i + 1) * L, :]
        acc = _conv_taps(ubuf, 0, L, slice(0, CONV_CH), cw_ref, cb_ref)
        mix_ref[i * L:(i + 1) * L, M_WIDTH:] = _ln_swish(acc, lg_ref, lb_ref)
        hist_out[i] = ubuf[L:L + HIST_PAD, :]


def _streamed_ffn_specs(consts, wg, wu, wd):
    def spec(c):
        if c is wg or c is wu:
            return pl.BlockSpec((D_MODEL, FFN_CHUNK), lambda i: (0, i))
        if c is wd:
            return pl.BlockSpec((FFN_CHUNK, D_MODEL), lambda i: (i, 0))
        return _const_spec(c.shape)
    return [spec(c) for c in consts]


def _ffn_in(x2d, p):
    m = x2d.shape[0]
    full = lambda *shape: pl.BlockSpec(shape, lambda i: (0,) * len(shape))
    consts = _ffn_in_consts(p, m)
    return pl.pallas_call(
        _ffn_in_body,
        grid=(D_FF // FFN_CHUNK,),
        in_specs=[full(m, D_MODEL)] + _streamed_ffn_specs(consts, p["ffn1_wg"], p["ffn1_wu"], p["ffn1_wd"]),
        out_specs=[full(m, D_MODEL), full(m, M_WIDTH), full(M_WIDTH, m), full(m, M_WIDTH), full(m, M_WIDTH),
                   full(m, CONV_CH), full(2 * N_HEADS, m)],
        out_shape=[jax.ShapeDtypeStruct((m, D_MODEL), F32), jax.ShapeDtypeStruct((m, M_WIDTH), F32),
                   jax.ShapeDtypeStruct((M_WIDTH, m), F32), jax.ShapeDtypeStruct((m, M_WIDTH), F32),
                   jax.ShapeDtypeStruct((m, M_WIDTH), F32), jax.ShapeDtypeStruct((m, CONV_CH), F32),
                   jax.ShapeDtypeStruct((2 * N_HEADS, m), F32)],
        scratch_shapes=[pltpu.VMEM((m, D_MODEL), BF16), pltpu.VMEM((m, D_MODEL), F32)],
        compiler_params=pltpu.CompilerParams(dimension_semantics=("arbitrary",),
                                             vmem_limit_bytes=VMEM_LIMIT_TOKENWISE),
        name="ffn_in",
    )(x2d, *consts)


def _ffn_out(mix, x1, p):
    m = mix.shape[0]
    full = pl.BlockSpec((m, D_MODEL), lambda i: (0, 0))
    consts = _ffn_out_consts(p)
    return pl.pallas_call(
        _ffn_out_body,
        grid=(D_FF // FFN_CHUNK,),
        in_specs=[full, full] + _streamed_ffn_specs(consts, p["ffn2_wg"], p["ffn2_wu"], p["ffn2_wd"]),
        out_specs=full,
        out_shape=jax.ShapeDtypeStruct((m, D_MODEL), F32),
        scratch_shapes=[pltpu.VMEM((m, D_MODEL), BF16), pltpu.VMEM((m, D_MODEL), F32), pltpu.VMEM((m, D_MODEL), F32)],
        compiler_params=pltpu.CompilerParams(dimension_semantics=("arbitrary",),
                                             vmem_limit_bytes=VMEM_LIMIT_TOKENWISE),
        name="ffn_out",
    )(mix, x1, *consts)


def _mixer_sample(q, kt, v, og, u, gt, c_all, n_all, layer, m0, hist0, p, nseq, L):
    rows = nseq * L
    args = [q, kt, v, og, u, gt, c_all, n_all, m0, hist0, p["conv_w"], p["conv_b"], p["conv_ln_g"], p["conv_ln_b"]]
    full = lambda a: pl.BlockSpec(a.shape, lambda i, nd=a.ndim: (0,) * nd)
    of_layer = lambda a: pl.BlockSpec((1,) + a.shape[1:], lambda i, nd=a.ndim: (layer,) + (0,) * (nd - 1))
    out_shape = [jax.ShapeDtypeStruct((rows, D_MODEL), F32),
                 jax.ShapeDtypeStruct((nseq, N_HEADS, HEAD_DIM, AUG), F32),
                 jax.ShapeDtypeStruct((nseq, N_HEADS, 1, HEAD_DIM), F32),
                 jax.ShapeDtypeStruct((nseq, HIST_PAD, CONV_CH), F32)]
    return pl.pallas_call(
        functools.partial(_mixer_sample_body, nseq=nseq, L=L),
        grid=(1,),
        in_specs=[of_layer(a) if a is c_all or a is n_all else full(a) for a in args],
        out_specs=[full(s) for s in out_shape],
        out_shape=out_shape,
        scratch_shapes=[pltpu.VMEM((HIST_PAD + L, CONV_CH), F32)],
        compiler_params=pltpu.CompilerParams(dimension_semantics=("arbitrary",),
                                             vmem_limit_bytes=VMEM_LIMIT_MIXER),
        name="mixer_sample",
    )(*args)


def _layer_params(l, ffn1_pre_g, ffn1_wg, ffn1_wu, ffn1_wd, ffn1_post_g, mix_pre_g, w_in, b_igate, b_fgate,
                  conv_w, conv_b, conv_ln_g, conv_ln_b, w_out, mix_post_g, ffn2_pre_g, ffn2_wg, ffn2_wu, ffn2_wd,
                  ffn2_post_g, final_g):
    vec = lambda a: a[l].astype(F32).reshape(1, -1)
    w = w_in[l]
    cuts = [0, M_WIDTH, 2 * M_WIDTH, 3 * M_WIDTH, 4 * M_WIDTH, 4 * M_WIDTH + N_HEADS, 4 * M_WIDTH + 2 * N_HEADS,
            4 * M_WIDTH + 2 * N_HEADS + CONV_CH, 4 * M_WIDTH + 2 * N_HEADS + 2 * CONV_CH]
    wq, wk, wv, wo, wi, wf, wcv, wcg = [w[:, a:b] for a, b in zip(cuts[:-1], cuts[1:])]
    return {
        "ffn1_pre_g": vec(ffn1_pre_g), "ffn1_post_g": vec(ffn1_post_g), "mix_pre_g": vec(mix_pre_g),
        "mix_post_g": vec(mix_post_g), "ffn2_pre_g": vec(ffn2_pre_g), "ffn2_post_g": vec(ffn2_post_g),
        "final_g": vec(final_g),
        "ffn1_wg": ffn1_wg[l].astype(BF16), "ffn1_wu": ffn1_wu[l].astype(BF16), "ffn1_wd": ffn1_wd[l].astype(BF16),
        "ffn2_wg": ffn2_wg[l].astype(BF16), "ffn2_wu": ffn2_wu[l].astype(BF16), "ffn2_wd": ffn2_wd[l].astype(BF16),
        "w_qvo": jnp.concatenate([wq, wv, wo], axis=1).astype(BF16),
        "w_ktg": jnp.concatenate([wk, wi, wf], axis=1).T.astype(BF16),
        "w_conv": jnp.concatenate([wcv, wcg], axis=1).astype(BF16),
        "gbias": jnp.concatenate([b_igate[l], b_fgate[l]]).astype(F32),
        "w_out": w_out[l].astype(BF16),
        "conv_w": conv_w[l].astype(F32), "conv_b": vec(conv_b), "conv_ln_g": vec(conv_ln_g),
        "conv_ln_b": vec(conv_ln_b),
    }


def kernel(x_prompt, x_sample, state_mlstm_C, state_mlstm_n, state_mlstm_m, cache_conv, ffn1_pre_g, ffn1_wg,
           ffn1_wu, ffn1_wd, ffn1_post_g, mix_pre_g, w_in, b_igate, b_fgate, conv_w, conv_b, conv_ln_g, conv_ln_b,
           w_out, mix_post_g, ffn2_pre_g, ffn2_wg, ffn2_wu, ffn2_wd, ffn2_post_g, final_g):
    batch, seq, _ = x_prompt.shape
    nseq, dseq, _ = x_sample.shape
    depth = w_in.shape[0]
    assert seq % PROMPT_TILE == 0 and PROMPT_TILE % FFN_ROWS == 0
    assert FFN_ROWS % CHUNK == 0 and FFN_ROWS % CONV_ROWS == 0 and PROMPT_TILE >= HIST_PAD
    assert nseq * dseq == SAMPLE_TILE and dseq <= HIST_PAD
    yp = x_prompt.reshape(batch * seq, D_MODEL)
    ys = x_sample.reshape(nseq * dseq, D_MODEL)
    outs_p, outs_s = [], []
    for l in range(depth):
        p = _layer_params(l, ffn1_pre_g, ffn1_wg, ffn1_wu, ffn1_wd, ffn1_post_g, mix_pre_g, w_in, b_igate, b_fgate,
                          conv_w, conv_b, conv_ln_g, conv_ln_b, w_out, mix_post_g, ffn2_pre_g, ffn2_wg, ffn2_wu,
                          ffn2_wd, ffn2_post_g, final_g)
        x1, u, ym, c, n, m = _ffn_in_mlstm(yp, p, batch, seq, PROMPT_TILE)
        yp, hist = _conv_ffn_out(u, ym, x1, p, batch, seq, PROMPT_TILE)
        outs_p.append((c, n[:, :, 0, :], m[:, :, 0, 0], hist[:, HIST_PAD - HIST:, :]))
        x1, q, kt, v, og, u, gt = _ffn_in(ys, p)
        m0 = jnp.broadcast_to(state_mlstm_m[l].astype(F32).T[:, :, None, None], (N_HEADS, nseq, dseq, HEAD_DIM))
        m0 = m0.reshape(N_HEADS, nseq * dseq, HEAD_DIM)
        hist0 = jnp.pad(cache_conv[l].astype(F32), ((0, 0), (HIST_PAD - HIST, 0), (0, 0)))
        mix, caug, m, hist = _mixer_sample(q, kt, v, og, u, gt, state_mlstm_C.astype(F32), state_mlstm_n.astype(F32),
                                           l, m0, hist0, p, nseq, dseq)
        ys = _ffn_out(mix, x1, p)
        outs_s.append((caug[..., :HEAD_DIM], caug[..., HEAD_DIM], m[:, :, 0, 0], hist[:, HIST_PAD - HIST:, :]))
    stack = lambda outs, k: jnp.stack([o[k] for o in outs])
    return (yp.reshape(batch, seq, D_MODEL), ys.reshape(nseq, dseq, D_MODEL),
            stack(outs_p, 0), stack(outs_p, 1), stack(outs_p, 2), stack(outs_p, 3),
            stack(outs_s, 0), stack(outs_s, 1), stack(outs_s, 2), stack(outs_s, 3))
```

```python
import functools

import jax
import jax.numpy as jnp
from jax import lax
from jax.experimental import pallas as pl
from jax.experimental.pallas import tpu as pltpu

D_MODEL = 1024
D_FF = 2816
N_HEADS = 4
HEAD_DIM = 128
M_WIDTH = N_HEADS * HEAD_DIM
CONV_CH = 512
CONV_WIDTH = 31
HIST = CONV_WIDTH - 1
HIST_PAD = 32
SUBLANES = 8
LANES = 128
EPS = 1e-6
CHUNK = 128
AUG = 2 * HEAD_DIM
CONV_ROWS = 128
LN_ROWS = 32

F32 = jnp.float32
BF16 = jnp.bfloat16
HIGHEST = lax.Precision.HIGHEST
NT_DIMS = (((1,), (1,)), ((), ()))

SAMPLE_TILE = 256
PROMPT_TILE = 512
FFN_ROWS = 256
WEIGHT_STAGE_ROWS = 128
VMEM_LIMIT_TOKENWISE = 60 * 1024 * 1024
VMEM_LIMIT_MIXER = 40 * 1024 * 1024


def _rms(x, g):
    return x * lax.rsqrt(jnp.mean(x * x, axis=-1, keepdims=True) + EPS) * g


def _dot(a, b):
    return jnp.dot(a, b, preferred_element_type=F32)


def _swiglu(h, wg_ref, wu_ref, wd_ref):
    a = _dot(h, wg_ref[...])
    b = _dot(h, wu_ref[...])
    s = (a * jax.nn.sigmoid(a) * b).astype(BF16)
    return _dot(s, wd_ref[...])


def _gate_rows(pre, gbias_ref):
    g = pre + gbias_ref[...]
    row = lax.broadcasted_iota(jnp.int32, g.shape, 0)
    return jnp.where(row < N_HEADS, g, jax.nn.log_sigmoid(g))


def _ffn_in_values(x, g1_ref, wg_ref, wu_ref, wd_ref, gpost_ref, gmix_ref, wqvo_ref, wktg_ref, wconv_ref, gbias_ref):
    h = _rms(x, g1_ref[...]).astype(BF16)
    d = _swiglu(h, wg_ref, wu_ref, wd_ref)
    x1 = x + 0.5 * _rms(d, gpost_ref[...])
    h2 = _rms(x1, gmix_ref[...]).astype(BF16)
    qvo = _dot(h2, wqvo_ref[...])
    q = qvo[:, :M_WIDTH]
    v = qvo[:, M_WIDTH:2 * M_WIDTH]
    og = jax.nn.sigmoid(qvo[:, 2 * M_WIDTH:])
    ktg = lax.dot_general(wktg_ref[...], h2, NT_DIMS, preferred_element_type=F32)
    kt = ktg[:M_WIDTH] * (HEAD_DIM ** -0.5)
    cc = _dot(h2, wconv_ref[...])
    u = cc[:, :CONV_CH] * jax.nn.sigmoid(cc[:, CONV_CH:])
    return x1, q, kt, v, og, u, _gate_rows(ktg[M_WIDTH:], gbias_ref)


def _ffn_out_values(ym, c, x1, wout_ref, gmp_ref, g2_ref, wg_ref, wu_ref, wd_ref, gpost_ref, gfin_ref):
    o = _dot(ym.astype(BF16), wout_ref[0:M_WIDTH, :]) + _dot(c.astype(BF16), wout_ref[M_WIDTH:, :])
    x2 = x1 + _rms(o, gmp_ref[...])
    h = _rms(x2, g2_ref[...]).astype(BF16)
    d = _swiglu(h, wg_ref, wu_ref, wd_ref)
    x3 = x2 + 0.5 * _rms(d, gpost_ref[...])
    return _rms(x3, gfin_ref[...])


def _ffn_out_phases(ym, c, x1, y_ref, rows, wout_ref, gmp_ref, g2_ref, wg_ref, wu_ref, wd_ref, gpost_ref, gfin_ref):
    o = _dot(ym.astype(BF16), wout_ref[0:M_WIDTH, :]) + _dot(c.astype(BF16), wout_ref[M_WIDTH:, :])
    yield o
    x2 = x1 + _rms(o, gmp_ref[...])
    h = _rms(x2, g2_ref[...]).astype(BF16)
    a = _dot(h, wg_ref[...])
    yield a
    b = _dot(h, wu_ref[...])
    yield b
    s = (a * jax.nn.sigmoid(a) * b).astype(BF16)
    d = _dot(s, wd_ref[...])
    yield d
    x3 = x2 + 0.5 * _rms(d, gpost_ref[...])
    y = _rms(x3, gfin_ref[...])
    y_ref[rows, :] = y
    yield y


def _conv_taps(ubuf, row0, nrows, lanes, cw_ref, cb_ref):
    acc = jnp.broadcast_to(cb_ref[:, lanes], (nrows, lanes.stop - lanes.start))
    first = HIST_PAD - HIST
    for res in range(SUBLANES):
        taps = [j for j in range(CONV_WIDTH) if (first + j) % SUBLANES == res]
        lo = (first + taps[0]) // SUBLANES * SUBLANES
        hi = (first + taps[-1]) // SUBLANES * SUBLANES
        win = ubuf[row0 + lo:row0 + hi + nrows + (SUBLANES if res else 0), lanes]
        if res:
            win = pltpu.roll(win, win.shape[0] - res, 0)
        for j in taps:
            off = first + j - res - lo
            acc = acc + win[off:off + nrows, :] * cw_ref[j:j + 1, lanes]
    return acc


def _conv_taps_interleaved(uslab, row0, nrows, lanes, cw_ref, cb_ref):
    first = HIST_PAD - HIST
    half = nrows // 2
    acc = [jnp.broadcast_to(cb_ref[:, lanes], (half, LANES))] * 2
    for j in range(CONV_WIDTH):
        for parity in range(2):
            x = uslab[pl.ds(row0 + first + j + parity, half, stride=2), :]
            acc[parity] = acc[parity] + x * cw_ref[j:j + 1, lanes]
    return acc


def _ln_swish(acc, lg_ref, lb_ref):
    mu = jnp.mean(acc, axis=-1, keepdims=True)
    xc = acc - mu
    var = jnp.mean(xc * xc, axis=-1, keepdims=True)
    y = xc * lax.rsqrt(var + EPS) * lg_ref[...] + lb_ref[...]
    return y * jax.nn.sigmoid(y)


def _mlstm_gates(b_r, a_r, caus, qh, kth):
    L = b_r.shape[1]
    arow = jnp.broadcast_to(a_r, (L, L))
    bcol = jnp.broadcast_to(b_r, (L, L)).T
    acol = arow.T
    dmat = jnp.where(caus, bcol + arow, -jnp.inf)
    return dict(bcol=bcol, acol=acol, dmat=dmat, rowmax=jnp.max(dmat, axis=1, keepdims=True), s_raw=_dot(qh, kth))


def _mlstm_scores(st, mprev, vh):
    L = st["bcol"].shape[0]
    m_t = jnp.maximum(st["bcol"] + mprev, st["rowmax"])
    s = st["s_raw"] * jnp.exp(st["dmat"] - m_t)
    vaug = jnp.concatenate([vh, jnp.ones_like(vh)], axis=1)
    m_new = m_t[L - 1:L, :]
    b_last = st["bcol"][L - 1:L, :]
    g_rows = jnp.exp(b_last + st["acol"] - m_new)
    new = dict(m_t=m_t, sv=_dot(s.astype(BF16), vaug.astype(BF16)), w_inter=jnp.exp(st["bcol"] + mprev - m_t),
               g_state=jnp.exp(b_last + mprev - m_new),
               gv=(jnp.concatenate([g_rows, g_rows], axis=1) * vaug).astype(BF16))
    st.clear()
    st.update(new)
    return m_new


def _mlstm_output(st, qh, kth, caug):
    m_t, sv, w_inter, g_state = st["m_t"], st["sv"], st["w_inter"], st["g_state"]
    qc = _dot(qh, caug.astype(BF16))
    caug_new = jnp.concatenate([g_state, g_state], axis=1) * caug + _dot(kth, st["gv"])
    num = w_inter * qc[:, :HEAD_DIM] + sv[:, :HEAD_DIM]
    den = w_inter * qc[:, HEAD_DIM:] + sv[:, HEAD_DIM:]
    hh = num / jnp.maximum(jnp.abs(den), jnp.exp(-m_t))
    return hh, caug_new


def _store_state(c_out, n_out, i, h, caug):
    c_out[i, h] = caug[:, :HEAD_DIM]
    n_out[i, h] = caug[:, HEAD_DIM:].T[0:1, :]


def _weight_out_copies(w_s, w_out, sem_out):
    return [pltpu.make_async_copy(s, o, sem_out.at[k]) for k, (s, o) in enumerate(zip(w_s, w_out))]


def _stage_weights(layer, w_hbm, w_s, w_out, stages, sem_in, sem_out):
    for src, dst, stage in zip(w_hbm, w_s, stages):
        rb = stage.shape[1]
        n = dst.shape[0] // rb
        cp = lambda i: pltpu.make_async_copy(src.at[layer, pl.ds(i * rb, rb), :], stage.at[i % 2], sem_in.at[i % 2])
        cp(0).start()
        for i in range(n):
            if i + 1 < n:
                cp(i + 1).start()
            cp(i).wait()
            dst[i * rb:(i + 1) * rb, :] = stage[i % 2].astype(BF16)
    for copy in _weight_out_copies(w_s, w_out, sem_out):
        copy.start()


def _mlstm_pieces(rslot, fresh, nck, q_st, kt_st, v_st, og_st, g_st, caug_s, m_s, ym_ref, c_out, n_out, m_out):
    L = CHUNK
    r = lax.broadcasted_iota(jnp.int32, (L, L), 0)
    c = lax.broadcasted_iota(jnp.int32, (L, L), 1)
    caus = c <= r
    triu = (r <= c).astype(BF16)
    stages = {}

    def stage1(ck):
        rows = slice(ck * L, (ck + 1) * L)
        gt = g_st[rslot, :, rows]
        hi = gt.astype(BF16)
        mid = (gt - hi.astype(F32)).astype(BF16)
        lo = (gt - hi.astype(F32) - mid.astype(F32)).astype(BF16)
        parts = _dot(jnp.concatenate([hi, mid, lo], axis=0), triu)
        b_rows = parts[0:2 * N_HEADS] + parts[2 * N_HEADS:4 * N_HEADS] + parts[4 * N_HEADS:]
        for h in range(N_HEADS):
            hs = slice(h * HEAD_DIM, (h + 1) * HEAD_DIM)
            b_r = b_rows[N_HEADS + h:N_HEADS + h + 1, :]
            stages[ck, h] = _mlstm_gates(b_r, gt[h:h + 1, :] - b_r, caus, q_st[rslot, rows, hs], kt_st[rslot, hs, rows])

    def stage2(ck):
        rows = slice(ck * L, (ck + 1) * L)
        for h in range(N_HEADS):
            mprev = m_s[h]
            if ck == 0:
                mprev = jnp.where(fresh, 0.0, mprev)
            m_new = _mlstm_scores(stages[ck, h], mprev, v_st[rslot, rows, slice(h * HEAD_DIM, (h + 1) * HEAD_DIM)])
            m_s[h] = m_new
            if ck == nck - 1:
                m_out[0, h] = m_new

    def stage3(ck):
        rows = slice(ck * L, (ck + 1) * L)
        for h in range(N_HEADS):
            hs = slice(h * HEAD_DIM, (h + 1) * HEAD_DIM)
            caug = caug_s[h]
            if ck == 0:
                caug = jnp.where(fresh, 0.0, caug)
            hh, caug = _mlstm_output(stages.pop((ck, h)), q_st[rslot, rows, hs], kt_st[rslot, hs, rows], caug)
            ym_ref[rows, hs] = og_st[rslot, rows, hs] * hh
            caug_s[h] = caug
            if ck == nck - 1:
                _store_state(c_out, n_out, 0, h, caug)

    for t in range(nck + 2):
        for lag, stage in enumerate((stage1, stage2, stage3)):
            if 0 <= t - lag < nck:
                stage(t - lag)
        yield


def _ffn_in_block(x_ref, rows, wslot, g1_ref, wg_ref, wu_ref, wd_ref, gpost_ref, gmix_ref, wqvo_ref, wktg_ref, wconv_ref,
                  gbias_ref, x1_ref, u_ref, q_st, kt_st, v_st, og_st, g_st):
    x = x_ref[rows, :]
    h = _rms(x, g1_ref[...]).astype(BF16)
    a = _dot(h, wg_ref[...])
    yield
    b = _dot(h, wu_ref[...])
    yield
    s = (a * jax.nn.sigmoid(a) * b).astype(BF16)
    d = _dot(s, wd_ref[...])
    yield
    x1 = x + 0.5 * _rms(d, gpost_ref[...])
    x1_ref[rows, :] = x1
    h2 = _rms(x1, gmix_ref[...]).astype(BF16)
    qvo = _dot(h2, wqvo_ref[...])
    q_st[wslot, rows, :] = qvo[:, :M_WIDTH].astype(BF16)
    v_st[wslot, rows, :] = qvo[:, M_WIDTH:2 * M_WIDTH]
    og_st[wslot, rows, :] = jax.nn.sigmoid(qvo[:, 2 * M_WIDTH:])
    yield
    ktg = lax.dot_general(wktg_ref[...], h2, NT_DIMS, preferred_element_type=F32)
    kt_st[wslot, :, rows] = (ktg[:M_WIDTH] * (HEAD_DIM ** -0.5)).astype(BF16)
    g_st[wslot, :, rows] = _gate_rows(ktg[M_WIDTH:], gbias_ref)
    yield
    cc = _dot(h2, wconv_ref[...])
    u_ref[rows, :] = cc[:, :CONV_CH] * jax.nn.sigmoid(cc[:, CONV_CH:])
    yield


def _ffn_in_mlstm_body(x_ref, g1_ref, wg_hbm, wu_hbm, wd_hbm, gpost_ref, gmix_ref, wqvo_ref, wktg_ref, wconv_ref,
                       gbias_ref, x1_ref, u_ref, ym_ref, c_out, n_out, m_out, wg_o, wu_o, wd_o,
                       q_st, kt_st, v_st, og_st, g_st, caug_s, m_s, wg_ref, wu_ref, wd_ref, stage_w, stage_d,
                       sem_in, sem_out, *, tm, nt, layer):
    g = pl.program_id(0)
    wslot = lax.rem(g, 2)
    rslot = 1 - wslot
    w_s, w_o = (wg_ref, wu_ref, wd_ref), (wg_o, wu_o, wd_o)

    @pl.when(g == 0)
    def _init():
        for st in (q_st, kt_st, v_st, og_st, g_st, caug_s, m_s):
            st[...] = jnp.zeros_like(st)
        _stage_weights(layer, (wg_hbm, wu_hbm, wd_hbm), w_s, w_o, (stage_w, stage_w, stage_d), sem_in, sem_out)

    @pl.when(g == pl.num_programs(0) - 1)
    def _weights_written():
        for copy in _weight_out_copies(w_s, w_o, sem_out):
            copy.wait()

    pieces = _mlstm_pieces(rslot, lax.rem(g - 1, nt) == 0, tm // CHUNK, q_st, kt_st, v_st, og_st, g_st, caug_s, m_s,
                           ym_ref, c_out, n_out, m_out)
    blocks = [_ffn_in_block(x_ref, slice(r0, r0 + FFN_ROWS), wslot, g1_ref, wg_ref, wu_ref, wd_ref, gpost_ref,
                            gmix_ref, wqvo_ref, wktg_ref, wconv_ref, gbias_ref, x1_ref, u_ref, q_st, kt_st, v_st,
                            og_st, g_st) for r0 in range(0, tm, FFN_ROWS)]
    next(pieces)
    while blocks:
        for block in list(blocks):
            if next(block, blocks) is blocks:
                blocks.remove(block)
            else:
                next(pieces, None)
    for _ in pieces:
        pass


def _conv_ffn_out_body(u_ref, ym_ref, x1_ref, cw_ref, cb_ref, lg_ref, lb_ref, wout_ref, gmp_ref, g2_ref,
                       wg_hbm, wu_hbm, wd_hbm, gpost_ref, gfin_ref, y_ref, hist_out, wg_o, wu_o, wd_o, ubuf, cbuf,
                       wg_ref, wu_ref, wd_ref, stage_w, stage_d, sem_in, sem_out, *, tm, nt, layer):
    nslab = CONV_CH // LANES
    w_s, w_o = (wg_ref, wu_ref, wd_ref), (wg_o, wu_o, wd_o)

    @pl.when(pl.program_id(0) == 0)
    def _weights():
        _stage_weights(layer, (wg_hbm, wu_hbm, wd_hbm), w_s, w_o, (stage_w, stage_w, stage_d), sem_in, sem_out)

    @pl.when(pl.program_id(0) == pl.num_programs(0) - 1)
    def _weights_written():
        for copy in _weight_out_copies(w_s, w_o, sem_out):
            copy.wait()

    @pl.when(lax.rem(pl.program_id(0), nt) == 0)
    def _new_sequence():
        ubuf[:, 0:HIST_PAD, :] = jnp.zeros((nslab, HIST_PAD, LANES), F32)

    for lb in range(nslab):
        ubuf[lb, HIST_PAD:HIST_PAD + tm, :] = u_ref[:, lb * LANES:(lb + 1) * LANES]
    for k in range(tm // CONV_ROWS):
        for lb in range(nslab):
            acc = _conv_taps_interleaved(ubuf.at[lb], k * CONV_ROWS, CONV_ROWS, slice(lb * LANES, (lb + 1) * LANES),
                                         cw_ref, cb_ref)
            for parity in range(2):
                cbuf[lb, pl.ds(k * CONV_ROWS + parity, CONV_ROWS // 2, stride=2), :] = acc[parity]
    tail = jnp.concatenate([ubuf[lb, tm:tm + HIST_PAD, :] for lb in range(nslab)], axis=1)
    hist_out[0] = tail
    for lb in range(nslab):
        ubuf[lb, 0:HIST_PAD, :] = tail[:, lb * LANES:(lb + 1) * LANES]

    halves = []
    for r0 in range(0, tm, FFN_ROWS):
        c = []
        for k in range(FFN_ROWS // LN_ROWS):
            rows = slice(r0 + k * LN_ROWS, r0 + (k + 1) * LN_ROWS)
            c.append(_ln_swish(jnp.concatenate([cbuf[lb, rows, :] for lb in range(nslab)], axis=1), lg_ref, lb_ref))
        rows = slice(r0, r0 + FFN_ROWS)
        halves.append(_ffn_out_phases(ym_ref[rows, :], jnp.concatenate(c, axis=0), x1_ref[rows, :], y_ref, rows,
                                      wout_ref, gmp_ref, g2_ref, wg_ref, wu_ref, wd_ref, gpost_ref, gfin_ref))
    for _ in zip(*halves):
        pass


def _const_spec(shape):
    nd = len(shape)
    return pl.BlockSpec(shape, lambda *_: (0,) * nd, pipeline_mode=pl.Buffered(1))


def _ffn_in_consts(p, rows):
    gbias = jnp.broadcast_to(p["gbias"][:, None], (2 * N_HEADS, rows))
    return [p["ffn1_pre_g"], p["ffn1_wg"], p["ffn1_wu"], p["ffn1_wd"], p["ffn1_post_g"], p["mix_pre_g"],
            p["w_qvo"], p["w_ktg"], p["w_conv"], gbias]


def _ffn_out_consts(p):
    return [p["w_out"], p["mix_post_g"], p["ffn2_pre_g"], p["ffn2_wg"], p["ffn2_wu"], p["ffn2_wd"],
            p["ffn2_post_g"], p["final_g"]]


def _weight_spec(c):
    return pl.BlockSpec(memory_space=pl.ANY) if c.ndim == 3 else _const_spec(c.shape)


_WEIGHT_OUT_SHAPES = [jax.ShapeDtypeStruct(s, BF16) for s in ((D_MODEL, D_FF), (D_MODEL, D_FF), (D_FF, D_MODEL))]


def _weight_scratch(dtype):
    return [pltpu.VMEM((D_MODEL, D_FF), BF16), pltpu.VMEM((D_MODEL, D_FF), BF16), pltpu.VMEM((D_FF, D_MODEL), BF16),
            pltpu.VMEM((2, WEIGHT_STAGE_ROWS, D_FF), dtype), pltpu.VMEM((2, WEIGHT_STAGE_ROWS, D_MODEL), dtype),
            pltpu.SemaphoreType.DMA((2,)), pltpu.SemaphoreType.DMA((3,))]


def _ffn_in_mlstm(x2d, p, layer, batch, seq, tm):
    m = batch * seq
    nt = seq // tm
    ntiles = m // tm
    cur = lambda w: pl.BlockSpec((tm, w), lambda g: (jnp.minimum(g, ntiles - 1), 0))
    prev = lambda w: pl.BlockSpec((tm, w), lambda g: (jnp.maximum(g - 1, 0), 0))
    prev_seq = lambda g: jnp.maximum(g - 1, 0) // nt
    consts = _ffn_in_consts(p, FFN_ROWS)
    return pl.pallas_call(
        functools.partial(_ffn_in_mlstm_body, tm=tm, nt=nt, layer=layer),
        grid=(ntiles + 1,),
        in_specs=[cur(D_MODEL)] + [_weight_spec(c) for c in consts],
        out_specs=[cur(D_MODEL), cur(CONV_CH), prev(M_WIDTH),
                   pl.BlockSpec((1, N_HEADS, HEAD_DIM, HEAD_DIM), lambda g: (prev_seq(g), 0, 0, 0)),
                   pl.BlockSpec((1, N_HEADS, 1, HEAD_DIM), lambda g: (prev_seq(g), 0, 0, 0)),
                   pl.BlockSpec((1, N_HEADS, 1, HEAD_DIM), lambda g: (prev_seq(g), 0, 0, 0))]
                  + [pl.BlockSpec(memory_space=pl.ANY)] * 3,
        out_shape=[jax.ShapeDtypeStruct((m, D_MODEL), F32), jax.ShapeDtypeStruct((m, CONV_CH), F32),
                   jax.ShapeDtypeStruct((m, M_WIDTH), F32),
                   jax.ShapeDtypeStruct((batch, N_HEADS, HEAD_DIM, HEAD_DIM), F32),
                   jax.ShapeDtypeStruct((batch, N_HEADS, 1, HEAD_DIM), F32),
                   jax.ShapeDtypeStruct((batch, N_HEADS, 1, HEAD_DIM), F32)] + _WEIGHT_OUT_SHAPES,
        scratch_shapes=[pltpu.VMEM((2, tm, M_WIDTH), BF16), pltpu.VMEM((2, M_WIDTH, tm), BF16),
                        pltpu.VMEM((2, tm, M_WIDTH), F32), pltpu.VMEM((2, tm, M_WIDTH), F32),
                        pltpu.VMEM((2, 2 * N_HEADS, tm), F32),
                        pltpu.VMEM((N_HEADS, HEAD_DIM, AUG), F32), pltpu.VMEM((N_HEADS, 1, HEAD_DIM), F32)]
                       + _weight_scratch(p["ffn1_wg"].dtype),
        compiler_params=pltpu.CompilerParams(dimension_semantics=("arbitrary",),
                                             vmem_limit_bytes=VMEM_LIMIT_TOKENWISE),
        name="ffn_in_mlstm",
    )(x2d, *consts)


def _conv_ffn_out(u, ym, x1, p, layer, batch, seq, tm):
    m = batch * seq
    nt = seq // tm
    row = lambda w: pl.BlockSpec((tm, w), lambda g: (g, 0))
    consts = [p["conv_w"], p["conv_b"], p["conv_ln_g"], p["conv_ln_b"]] + _ffn_out_consts(p)
    return pl.pallas_call(
        functools.partial(_conv_ffn_out_body, tm=tm, nt=nt, layer=layer),
        grid=(m // tm,),
        in_specs=[row(CONV_CH), row(M_WIDTH), row(D_MODEL)] + [_weight_spec(c) for c in consts],
        out_specs=[row(D_MODEL), pl.BlockSpec((1, HIST_PAD, CONV_CH), lambda g: (g // nt, 0, 0))]
                  + [pl.BlockSpec(memory_space=pl.ANY)] * 3,
        out_shape=[jax.ShapeDtypeStruct((m, D_MODEL), F32), jax.ShapeDtypeStruct((batch, HIST_PAD, CONV_CH), F32)]
                  + _WEIGHT_OUT_SHAPES,
        scratch_shapes=[pltpu.VMEM((CONV_CH // LANES, HIST_PAD + tm, LANES), F32),
                        pltpu.VMEM((CONV_CH // LANES, tm, LANES), F32)] + _weight_scratch(p["ffn2_wg"].dtype),
        compiler_params=pltpu.CompilerParams(dimension_semantics=("arbitrary",),
                                             vmem_limit_bytes=VMEM_LIMIT_TOKENWISE),
        name="conv_ffn_out",
    )(u, ym, x1, *consts)


def _ffn_in_body(x_ref, g1_ref, wg_ref, wu_ref, wd_ref, gpost_ref, gmix_ref, wqvo_ref, wktg_ref, wconv_ref,
                 gbias_ref, x1_ref, q_ref, kt_ref, v_ref, og_ref, u_ref, gt_ref):
    outs = _ffn_in_values(x_ref[...], g1_ref, wg_ref, wu_ref, wd_ref, gpost_ref, gmix_ref, wqvo_ref, wktg_ref,
                          wconv_ref, gbias_ref)
    for ref, val in zip((x1_ref, q_ref, kt_ref, v_ref, og_ref, u_ref, gt_ref), outs):
        ref[...] = val


def _ffn_out_body(mix_ref, x1_ref, wout_ref, gmp_ref, g2_ref, wg_ref, wu_ref, wd_ref, gpost_ref, gfin_ref, y_ref):
    y_ref[...] = _ffn_out_values(mix_ref[:, :M_WIDTH], mix_ref[:, M_WIDTH:], x1_ref[...], wout_ref, gmp_ref, g2_ref,
                                 wg_ref, wu_ref, wd_ref, gpost_ref, gfin_ref)


def _mixer_sample_body(q_ref, kt_ref, v_ref, og_ref, u_ref, gt_ref, c0_ref, n0_ref, m0_ref, hist0_ref,
                       cw_ref, cb_ref, lg_ref, lb_ref, mix_ref, caug_out, m_out, hist_out, ubuf, *, nseq, L):
    R = nseq * L
    r = lax.broadcasted_iota(jnp.int32, (R, R), 0)
    c = lax.broadcasted_iota(jnp.int32, (R, R), 1)
    same = lax.div(r, L) == lax.div(c, L)
    caus = same & (c <= r)
    segtriu = jnp.where(same & (r <= c), 1.0, 0.0).astype(F32)
    segones = jnp.where(same, 1.0, 0.0).astype(F32)
    rowseq = lax.div(lax.broadcasted_iota(jnp.int32, (R, AUG), 0), L)
    ones = jnp.ones((R, HEAD_DIM), F32)
    gt = gt_ref[...]
    b_rows = jnp.dot(gt, segtriu, precision=HIGHEST, preferred_element_type=F32)
    tot_rows = jnp.dot(gt, segones, precision=HIGHEST, preferred_element_type=F32)
    for h in range(N_HEADS):
        hs = slice(h * HEAD_DIM, (h + 1) * HEAD_DIM)
        b_r = b_rows[N_HEADS + h:N_HEADS + h + 1, :]
        a_r = gt[h:h + 1, :] - b_r
        t_r = tot_rows[N_HEADS + h:N_HEADS + h + 1, :]
        arow = jnp.broadcast_to(a_r, (R, R))
        bcol = jnp.broadcast_to(b_r, (R, R)).T
        acol = arow.T
        tcol = jnp.broadcast_to(t_r, (R, R)).T
        mprev = m0_ref[h]
        mprev = jnp.concatenate([mprev, mprev], axis=1)
        dmat = jnp.where(caus, bcol + arow, -jnp.inf)
        m_t = jnp.maximum(bcol + mprev, jnp.max(dmat, axis=1, keepdims=True))
        dend = jnp.where(same, tcol + arow, -jnp.inf)
        m_new = jnp.maximum(tcol + mprev, jnp.max(dend, axis=1, keepdims=True))
        qh = q_ref[:, hs].astype(BF16)
        kth = kt_ref[hs, :].astype(BF16)
        s = _dot(qh, kth) * jnp.exp(dmat - m_t)
        vaug = jnp.concatenate([v_ref[:, hs], ones], axis=1)
        w_inter = jnp.exp(bcol + mprev - m_t)
        caug0 = [jnp.concatenate([c0_ref[0, i, h], jnp.broadcast_to(n0_ref[0, i, h:h + 1, :], (HEAD_DIM, HEAD_DIM)).T],
                                 axis=1) for i in range(nseq)]
        qc = jnp.zeros((R, AUG), F32)
        for i in range(nseq):
            qc = jnp.where(rowseq == i, _dot(qh, caug0[i].astype(BF16)), qc)
        sv = _dot(s.astype(BF16), vaug.astype(BF16))
        num = w_inter[:, :HEAD_DIM] * qc[:, :HEAD_DIM] + sv[:, :HEAD_DIM]
        den = w_inter[:, :HEAD_DIM] * qc[:, HEAD_DIM:] + sv[:, HEAD_DIM:]
        hh = num / jnp.maximum(jnp.abs(den), jnp.exp(-m_t[:, :HEAD_DIM]))
        mix_ref[:, hs] = og_ref[:, hs] * hh
        g_state = jnp.exp(tcol + mprev - m_new)
        g_rows = jnp.exp(tcol + acol - m_new)
        gv = g_rows * vaug
        for i in range(nseq):
            gvi = jnp.where(rowseq == i, gv, 0.0).astype(BF16)
            caug_out[i, h] = g_state[i * L:i * L + 1, :] * caug0[i] + _dot(kth, gvi)
            m_out[i, h] = m_new[i * L:i * L + 1, :HEAD_DIM]

    for i in range(nseq):
        ubuf[0:HIST_PAD, :] = hist0_ref[i]
        ubuf[HIST_PAD:HIST_PAD + L, :] = u_ref[i * L:(i + 1) * L, :]
        acc = _conv_taps(ubuf, 0, L, slice(0, CONV_CH), cw_ref, cb_ref)
        mix_ref[i * L:(i + 1) * L, M_WIDTH:] = _ln_swish(acc, lg_ref, lb_ref)
        hist_out[i] = ubuf[L:L + HIST_PAD, :]


def _ffn_in(x2d, p, tm):
    m = x2d.shape[0]
    row = lambda w: pl.BlockSpec((tm, w), lambda i: (i, 0))
    col = lambda h: pl.BlockSpec((h, tm), lambda i: (0, i))
    consts = _ffn_in_consts(p, tm)
    return pl.pallas_call(
        _ffn_in_body,
        grid=(m // tm,),
        in_specs=[row(D_MODEL)] + [_const_spec(c.shape) for c in consts],
        out_specs=[row(D_MODEL), row(M_WIDTH), col(M_WIDTH), row(M_WIDTH), row(M_WIDTH), row(CONV_CH),
                   col(2 * N_HEADS)],
        out_shape=[jax.ShapeDtypeStruct((m, D_MODEL), F32), jax.ShapeDtypeStruct((m, M_WIDTH), F32),
                   jax.ShapeDtypeStruct((M_WIDTH, m), F32), jax.ShapeDtypeStruct((m, M_WIDTH), F32),
                   jax.ShapeDtypeStruct((m, M_WIDTH), F32), jax.ShapeDtypeStruct((m, CONV_CH), F32),
                   jax.ShapeDtypeStruct((2 * N_HEADS, m), F32)],
        compiler_params=pltpu.CompilerParams(dimension_semantics=("arbitrary",),
                                             vmem_limit_bytes=VMEM_LIMIT_TOKENWISE),
        name="ffn_in",
    )(x2d, *consts)


def _ffn_out(mix, x1, p, tm):
    m = mix.shape[0]
    row = pl.BlockSpec((tm, D_MODEL), lambda i: (i, 0))
    consts = _ffn_out_consts(p)
    return pl.pallas_call(
        _ffn_out_body,
        grid=(m // tm,),
        in_specs=[row, row] + [_const_spec(c.shape) for c in consts],
        out_specs=row,
        out_shape=jax.ShapeDtypeStruct((m, D_MODEL), F32),
        compiler_params=pltpu.CompilerParams(dimension_semantics=("arbitrary",),
                                             vmem_limit_bytes=VMEM_LIMIT_TOKENWISE),
        name="ffn_out",
    )(mix, x1, *consts)


def _mixer_sample(q, kt, v, og, u, gt, c_all, n_all, layer, m0, hist0, p, nseq, L):
    rows = nseq * L
    args = [q, kt, v, og, u, gt, c_all, n_all, m0, hist0, p["conv_w"], p["conv_b"], p["conv_ln_g"], p["conv_ln_b"]]
    full = lambda a: pl.BlockSpec(a.shape, lambda i, nd=a.ndim: (0,) * nd)
    of_layer = lambda a: pl.BlockSpec((1,) + a.shape[1:], lambda i, nd=a.ndim: (layer,) + (0,) * (nd - 1))
    out_shape = [jax.ShapeDtypeStruct((rows, D_MODEL), F32),
                 jax.ShapeDtypeStruct((nseq, N_HEADS, HEAD_DIM, AUG), F32),
                 jax.ShapeDtypeStruct((nseq, N_HEADS, 1, HEAD_DIM), F32),
                 jax.ShapeDtypeStruct((nseq, HIST_PAD, CONV_CH), F32)]
    return pl.pallas_call(
        functools.partial(_mixer_sample_body, nseq=nseq, L=L),
        grid=(1,),
        in_specs=[of_layer(a) if a is c_all or a is n_all else full(a) for a in args],
        out_specs=[full(s) for s in out_shape],
        out_shape=out_shape,
        scratch_shapes=[pltpu.VMEM((HIST_PAD + L, CONV_CH), F32)],
        compiler_params=pltpu.CompilerParams(dimension_semantics=("arbitrary",),
                                             vmem_limit_bytes=VMEM_LIMIT_MIXER),
        name="mixer_sample",
    )(*args)


def _layer_params(l, ffn1_pre_g, ffn1_wg, ffn1_wu, ffn1_wd, ffn1_post_g, mix_pre_g, w_in, b_igate, b_fgate,
                  conv_w, conv_b, conv_ln_g, conv_ln_b, w_out, mix_post_g, ffn2_pre_g, ffn2_wg, ffn2_wu, ffn2_wd,
                  ffn2_post_g, final_g):
    vec = lambda a: a[l].astype(F32).reshape(1, -1)
    w = w_in[l]
    cuts = [0, M_WIDTH, 2 * M_WIDTH, 3 * M_WIDTH, 4 * M_WIDTH, 4 * M_WIDTH + N_HEADS, 4 * M_WIDTH + 2 * N_HEADS,
            4 * M_WIDTH + 2 * N_HEADS + CONV_CH, 4 * M_WIDTH + 2 * N_HEADS + 2 * CONV_CH]
    wq, wk, wv, wo, wi, wf, wcv, wcg = [w[:, a:b] for a, b in zip(cuts[:-1], cuts[1:])]
    return {
        "ffn1_pre_g": vec(ffn1_pre_g), "ffn1_post_g": vec(ffn1_post_g), "mix_pre_g": vec(mix_pre_g),
        "mix_post_g": vec(mix_post_g), "ffn2_pre_g": vec(ffn2_pre_g), "ffn2_post_g": vec(ffn2_post_g),
        "final_g": vec(final_g),
        "ffn1_wg": ffn1_wg, "ffn1_wu": ffn1_wu, "ffn1_wd": ffn1_wd,
        "ffn2_wg": ffn2_wg, "ffn2_wu": ffn2_wu, "ffn2_wd": ffn2_wd,
        "w_qvo": jnp.concatenate([wq, wv, wo], axis=1).astype(BF16),
        "w_ktg": jnp.concatenate([wk, wi, wf], axis=1).T.astype(BF16),
        "w_conv": jnp.concatenate([wcv, wcg], axis=1).astype(BF16),
        "gbias": jnp.concatenate([b_igate[l], b_fgate[l]]).astype(F32),
        "w_out": w_out[l].astype(BF16),
        "conv_w": conv_w[l].astype(F32), "conv_b": vec(conv_b), "conv_ln_g": vec(conv_ln_g),
        "conv_ln_b": vec(conv_ln_b),
    }


def kernel(x_prompt, x_sample, state_mlstm_C, state_mlstm_n, state_mlstm_m, cache_conv, ffn1_pre_g, ffn1_wg,
           ffn1_wu, ffn1_wd, ffn1_post_g, mix_pre_g, w_in, b_igate, b_fgate, conv_w, conv_b, conv_ln_g, conv_ln_b,
           w_out, mix_post_g, ffn2_pre_g, ffn2_wg, ffn2_wu, ffn2_wd, ffn2_post_g, final_g):
    batch, seq, _ = x_prompt.shape
    nseq, dseq, _ = x_sample.shape
    depth = w_in.shape[0]
    assert seq % PROMPT_TILE == 0 and PROMPT_TILE % FFN_ROWS == 0
    assert FFN_ROWS % CHUNK == 0 and FFN_ROWS % CONV_ROWS == 0 and PROMPT_TILE >= HIST_PAD
    assert nseq * dseq == SAMPLE_TILE and dseq <= HIST_PAD
    yp = x_prompt.reshape(batch * seq, D_MODEL)
    ys = x_sample.reshape(nseq * dseq, D_MODEL)
    outs_p, outs_s = [], []
    for l in range(depth):
        p = _layer_params(l, ffn1_pre_g, ffn1_wg, ffn1_wu, ffn1_wd, ffn1_post_g, mix_pre_g, w_in, b_igate, b_fgate,
                          conv_w, conv_b, conv_ln_g, conv_ln_b, w_out, mix_post_g, ffn2_pre_g, ffn2_wg, ffn2_wu,
                          ffn2_wd, ffn2_post_g, final_g)
        x1, u, ym, c, n, m, *ffn1_w = _ffn_in_mlstm(yp, p, l, batch, seq, PROMPT_TILE)
        yp, hist, *ffn2_w = _conv_ffn_out(u, ym, x1, p, l, batch, seq, PROMPT_TILE)
        p.update(zip(("ffn1_wg", "ffn1_wu", "ffn1_wd", "ffn2_wg", "ffn2_wu", "ffn2_wd"), ffn1_w + ffn2_w))
        outs_p.append((c, n[:, :, 0, :], m[:, :, 0, 0], hist[:, HIST_PAD - HIST:, :]))
        x1, q, kt, v, og, u, gt = _ffn_in(ys, p, SAMPLE_TILE)
        m0 = jnp.broadcast_to(state_mlstm_m[l].astype(F32).T[:, :, None, None], (N_HEADS, nseq, dseq, HEAD_DIM))
        m0 = m0.reshape(N_HEADS, nseq * dseq, HEAD_DIM)
        hist0 = jnp.pad(cache_conv[l].astype(F32), ((0, 0), (HIST_PAD - HIST, 0), (0, 0)))
        mix, caug, m, hist = _mixer_sample(q, kt, v, og, u, gt, state_mlstm_C.astype(F32), state_mlstm_n.astype(F32),
                                           l, m0, hist0, p, nseq, dseq)
        ys = _ffn_out(mix, x1, p, SAMPLE_TILE)
        outs_s.append((caug[..., :HEAD_DIM], caug[..., HEAD_DIM], m[:, :, 0, 0], hist[:, HIST_PAD - HIST:, :]))
    stack = lambda outs, k: jnp.stack([o[k] for o in outs])
    return (yp.reshape(batch, seq, D_MODEL), ys.reshape(nseq, dseq, D_MODEL),
            stack(outs_p, 0), stack(outs_p, 1), stack(outs_p, 2), stack(outs_p, 3),
            stack(outs_s, 0), stack(outs_s, 1), stack(outs_s, 2), stack(outs_s, 3))
```

```python
import functools

import jax
import jax.numpy as jnp
from jax import lax
from jax.experimental import pallas as pl
from jax.experimental.pallas import tpu as pltpu

D_MODEL = 1024
D_FF = 2816
N_HEADS = 4
HEAD_DIM = 128
M_WIDTH = N_HEADS * HEAD_DIM
CONV_CH = 512
CONV_WIDTH = 31
HIST = CONV_WIDTH - 1
HIST_PAD = 32
SUBLANES = 8
LANES = 128
EPS = 1e-6
CHUNK = 128
AUG = 2 * HEAD_DIM
CONV_ROWS = 128
LN_ROWS = 32

F32 = jnp.float32
BF16 = jnp.bfloat16
HIGHEST = lax.Precision.HIGHEST
NT_DIMS = (((1,), (1,)), ((), ()))

SAMPLE_TILE = 256
PROMPT_TILE = 512
FFN_ROWS = 256
WEIGHT_STAGE_ROWS = 64
WEIGHT_STAGE_SLOTS = 4
VMEM_LIMIT_TOKENWISE = 60 * 1024 * 1024
VMEM_LIMIT_MIXER = 40 * 1024 * 1024


def _rms(x, g):
    return x * lax.rsqrt(jnp.mean(x * x, axis=-1, keepdims=True) + EPS) * g


def _dot(a, b):
    return jnp.dot(a, b, preferred_element_type=F32)


def _swiglu(h, wg_ref, wu_ref, wd_ref):
    a = _dot(h, wg_ref[...])
    b = _dot(h, wu_ref[...])
    s = (a * jax.nn.sigmoid(a) * b).astype(BF16)
    return _dot(s, wd_ref[...])


def _gate_rows(pre, gbias_ref):
    g = pre + gbias_ref[...]
    row = lax.broadcasted_iota(jnp.int32, g.shape, 0)
    return jnp.where(row < N_HEADS, g, jax.nn.log_sigmoid(g))


def _ffn_in_values(x, g1_ref, wg_ref, wu_ref, wd_ref, gpost_ref, gmix_ref, wqvo_ref, wktg_ref, wconv_ref, gbias_ref):
    h = _rms(x, g1_ref[...]).astype(BF16)
    d = _swiglu(h, wg_ref, wu_ref, wd_ref)
    x1 = x + 0.5 * _rms(d, gpost_ref[...])
    h2 = _rms(x1, gmix_ref[...]).astype(BF16)
    qvo = _dot(h2, wqvo_ref[...])
    q = qvo[:, :M_WIDTH]
    v = qvo[:, M_WIDTH:2 * M_WIDTH]
    og = jax.nn.sigmoid(qvo[:, 2 * M_WIDTH:])
    ktg = lax.dot_general(wktg_ref[...], h2, NT_DIMS, preferred_element_type=F32)
    kt = ktg[:M_WIDTH] * (HEAD_DIM ** -0.5)
    cc = _dot(h2, wconv_ref[...])
    u = cc[:, :CONV_CH] * jax.nn.sigmoid(cc[:, CONV_CH:])
    return x1, q, kt, v, og, u, _gate_rows(ktg[M_WIDTH:], gbias_ref)


def _ffn_out_values(ym, c, x1, wout_ref, gmp_ref, g2_ref, wg_ref, wu_ref, wd_ref, gpost_ref, gfin_ref):
    o = _dot(ym.astype(BF16), wout_ref[0:M_WIDTH, :]) + _dot(c.astype(BF16), wout_ref[M_WIDTH:, :])
    x2 = x1 + _rms(o, gmp_ref[...])
    h = _rms(x2, g2_ref[...]).astype(BF16)
    d = _swiglu(h, wg_ref, wu_ref, wd_ref)
    x3 = x2 + 0.5 * _rms(d, gpost_ref[...])
    return _rms(x3, gfin_ref[...])


def _ffn_out_phases(ym, c, x1, y_ref, rows, wout_ref, gmp_ref, g2_ref, wg_ref, wu_ref, wd_ref, gpost_ref, gfin_ref):
    o = _dot(ym.astype(BF16), wout_ref[0:M_WIDTH, :]) + _dot(c.astype(BF16), wout_ref[M_WIDTH:, :])
    yield o
    x2 = x1 + _rms(o, gmp_ref[...])
    h = _rms(x2, g2_ref[...]).astype(BF16)
    a = _dot(h, wg_ref[...])
    yield a
    b = _dot(h, wu_ref[...])
    yield b
    s = (a * jax.nn.sigmoid(a) * b).astype(BF16)
    d = _dot(s, wd_ref[...])
    yield d
    x3 = x2 + 0.5 * _rms(d, gpost_ref[...])
    y = _rms(x3, gfin_ref[...])
    y_ref[rows, :] = y
    yield y


def _conv_taps(ubuf, row0, nrows, lanes, cw_ref, cb_ref):
    acc = jnp.broadcast_to(cb_ref[:, lanes], (nrows, lanes.stop - lanes.start))
    first = HIST_PAD - HIST
    for res in range(SUBLANES):
        taps = [j for j in range(CONV_WIDTH) if (first + j) % SUBLANES == res]
        lo = (first + taps[0]) // SUBLANES * SUBLANES
        hi = (first + taps[-1]) // SUBLANES * SUBLANES
        win = ubuf[row0 + lo:row0 + hi + nrows + (SUBLANES if res else 0), lanes]
        if res:
            win = pltpu.roll(win, win.shape[0] - res, 0)
        for j in taps:
            off = first + j - res - lo
            acc = acc + win[off:off + nrows, :] * cw_ref[j:j + 1, lanes]
    return acc


def _conv_taps_interleaved(uslab, row0, nrows, lanes, cw_ref, cb_ref):
    first = HIST_PAD - HIST
    half = nrows // 2
    acc = [jnp.broadcast_to(cb_ref[:, lanes], (half, LANES))] * 2
    for j in range(CONV_WIDTH):
        for parity in range(2):
            x = uslab[pl.ds(row0 + first + j + parity, half, stride=2), :]
            acc[parity] = acc[parity] + x * cw_ref[j:j + 1, lanes]
    return acc


def _ln_swish(acc, lg_ref, lb_ref):
    mu = jnp.mean(acc, axis=-1, keepdims=True)
    xc = acc - mu
    var = jnp.mean(xc * xc, axis=-1, keepdims=True)
    y = xc * lax.rsqrt(var + EPS) * lg_ref[...] + lb_ref[...]
    return y * jax.nn.sigmoid(y)


def _mlstm_gates(b_r, a_r, caus, qh, kth):
    L = b_r.shape[1]
    arow = jnp.broadcast_to(a_r, (L, L))
    bcol = jnp.broadcast_to(b_r, (L, L)).T
    acol = arow.T
    dmat = jnp.where(caus, bcol + arow, -jnp.inf)
    return dict(bcol=bcol, acol=acol, dmat=dmat, rowmax=jnp.max(dmat, axis=1, keepdims=True), s_raw=_dot(qh, kth))


def _mlstm_scores(st, mprev, vh):
    L = st["bcol"].shape[0]
    m_t = jnp.maximum(st["bcol"] + mprev, st["rowmax"])
    s = st["s_raw"] * jnp.exp(st["dmat"] - m_t)
    vaug = jnp.concatenate([vh, jnp.ones_like(vh)], axis=1)
    m_new = m_t[L - 1:L, :]
    b_last = st["bcol"][L - 1:L, :]
    g_rows = jnp.exp(b_last + st["acol"] - m_new)
    new = dict(m_t=m_t, sv=_dot(s.astype(BF16), vaug.astype(BF16)), w_inter=jnp.exp(st["bcol"] + mprev - m_t),
               g_state=jnp.exp(b_last + mprev - m_new),
               gv=(jnp.concatenate([g_rows, g_rows], axis=1) * vaug).astype(BF16))
    st.clear()
    st.update(new)
    return m_new


def _mlstm_output(st, qh, kth, caug):
    m_t, sv, w_inter, g_state = st["m_t"], st["sv"], st["w_inter"], st["g_state"]
    qc = _dot(qh, caug.astype(BF16))
    caug_new = jnp.concatenate([g_state, g_state], axis=1) * caug + _dot(kth, st["gv"])
    num = w_inter * qc[:, :HEAD_DIM] + sv[:, :HEAD_DIM]
    den = w_inter * qc[:, HEAD_DIM:] + sv[:, HEAD_DIM:]
    hh = num / jnp.maximum(jnp.abs(den), jnp.exp(-m_t))
    return hh, caug_new


def _store_state(c_out, n_out, i, h, caug):
    c_out[i, h] = caug[:, :HEAD_DIM]
    n_out[i, h] = caug[:, HEAD_DIM:].T[0:1, :]


def _weight_out_copies(w_s, w_out, sem_out):
    return [pltpu.make_async_copy(s, o, sem_out.at[k]) for k, (s, o) in enumerate(zip(w_s, w_out))]


def _stage_weights(layer, w_hbm, w_s, w_out, stages, sem_in, sem_out):
    for src, dst, stage in zip(w_hbm, w_s, stages):
        slots, rb = stage.shape[0], stage.shape[1]
        n = dst.shape[0] // rb
        cp = lambda i: pltpu.make_async_copy(src.at[layer, pl.ds(i * rb, rb), :], stage.at[i % slots],
                                             sem_in.at[i % slots])
        for i in range(slots - 1):
            cp(i).start()
        for i in range(n):
            if i + slots - 1 < n:
                cp(i + slots - 1).start()
            cp(i).wait()
            dst[i * rb:(i + 1) * rb, :] = stage[i % slots].astype(BF16)
    for copy in _weight_out_copies(w_s, w_out, sem_out):
        copy.start()


def _mlstm_pieces(rslot, fresh, nck, q_st, kt_st, v_st, og_st, g_st, caug_s, m_s, ym_ref, c_out, n_out, m_out):
    L = CHUNK
    r = lax.broadcasted_iota(jnp.int32, (L, L), 0)
    c = lax.broadcasted_iota(jnp.int32, (L, L), 1)
    caus = c <= r
    triu = (r <= c).astype(BF16)
    stages = {}

    def stage1(ck):
        rows = slice(ck * L, (ck + 1) * L)
        gt = g_st[rslot, :, rows]
        hi = gt.astype(BF16)
        mid = (gt - hi.astype(F32)).astype(BF16)
        lo = (gt - hi.astype(F32) - mid.astype(F32)).astype(BF16)
        parts = _dot(jnp.concatenate([hi, mid, lo], axis=0), triu)
        b_rows = parts[0:2 * N_HEADS] + parts[2 * N_HEADS:4 * N_HEADS] + parts[4 * N_HEADS:]
        for h in range(N_HEADS):
            hs = slice(h * HEAD_DIM, (h + 1) * HEAD_DIM)
            b_r = b_rows[N_HEADS + h:N_HEADS + h + 1, :]
            stages[ck, h] = _mlstm_gates(b_r, gt[h:h + 1, :] - b_r, caus, q_st[rslot, rows, hs], kt_st[rslot, hs, rows])

    def stage2(ck):
        rows = slice(ck * L, (ck + 1) * L)
        for h in range(N_HEADS):
            mprev = m_s[h]
            if ck == 0:
                mprev = jnp.where(fresh, 0.0, mprev)
            m_new = _mlstm_scores(stages[ck, h], mprev, v_st[rslot, rows, slice(h * HEAD_DIM, (h + 1) * HEAD_DIM)])
            m_s[h] = m_new
            if ck == nck - 1:
                m_out[0, h] = m_new

    def stage3(ck):
        rows = slice(ck * L, (ck + 1) * L)
        for h in range(N_HEADS):
            hs = slice(h * HEAD_DIM, (h + 1) * HEAD_DIM)
            caug = caug_s[h]
            if ck == 0:
                caug = jnp.where(fresh, 0.0, caug)
            hh, caug = _mlstm_output(stages.pop((ck, h)), q_st[rslot, rows, hs], kt_st[rslot, hs, rows], caug)
            ym_ref[rows, hs] = og_st[rslot, rows, hs] * hh
            caug_s[h] = caug
            if ck == nck - 1:
                _store_state(c_out, n_out, 0, h, caug)

    for t in range(nck + 2):
        for lag, stage in enumerate((stage1, stage2, stage3)):
            if 0 <= t - lag < nck:
                stage(t - lag)
        yield


def _ffn_in_block(x_ref, rows, wslot, g1_ref, wg_ref, wu_ref, wd_ref, gpost_ref, gmix_ref, wqvo_ref, wktg_ref, wconv_ref,
                  gbias_ref, x1_ref, u_ref, q_st, kt_st, v_st, og_st, g_st):
    x = x_ref[rows, :]
    h = _rms(x, g1_ref[...]).astype(BF16)
    a = _dot(h, wg_ref[...])
    yield
    b = _dot(h, wu_ref[...])
    yield
    s = (a * jax.nn.sigmoid(a) * b).astype(BF16)
    d = _dot(s, wd_ref[...])
    yield
    x1 = x + 0.5 * _rms(d, gpost_ref[...])
    x1_ref[rows, :] = x1
    h2 = _rms(x1, gmix_ref[...]).astype(BF16)
    qvo = _dot(h2, wqvo_ref[...])
    q_st[wslot, rows, :] = qvo[:, :M_WIDTH].astype(BF16)
    v_st[wslot, rows, :] = qvo[:, M_WIDTH:2 * M_WIDTH]
    og_st[wslot, rows, :] = jax.nn.sigmoid(qvo[:, 2 * M_WIDTH:])
    yield
    ktg = lax.dot_general(wktg_ref[...], h2, NT_DIMS, preferred_element_type=F32)
    kt_st[wslot, :, rows] = (ktg[:M_WIDTH] * (HEAD_DIM ** -0.5)).astype(BF16)
    g_st[wslot, :, rows] = _gate_rows(ktg[M_WIDTH:], gbias_ref)
    yield
    cc = _dot(h2, wconv_ref[...])
    u_ref[rows, :] = cc[:, :CONV_CH] * jax.nn.sigmoid(cc[:, CONV_CH:])
    yield


def _ffn_in_mlstm_body(x_ref, g1_ref, wg_hbm, wu_hbm, wd_hbm, gpost_ref, gmix_ref, wqvo_ref, wktg_ref, wconv_ref,
                       gbias_ref, x1_ref, u_ref, ym_ref, c_out, n_out, m_out, wg_o, wu_o, wd_o,
                       q_st, kt_st, v_st, og_st, g_st, caug_s, m_s, wg_ref, wu_ref, wd_ref, stage_w, stage_d,
                       sem_in, sem_out, *, tm, nt, layer):
    g = pl.program_id(0)
    wslot = lax.rem(g, 2)
    rslot = 1 - wslot
    w_s, w_o = (wg_ref, wu_ref, wd_ref), (wg_o, wu_o, wd_o)

    @pl.when(g == 0)
    def _init():
        for st in (q_st, kt_st, v_st, og_st, g_st, caug_s, m_s):
            st[...] = jnp.zeros_like(st)
        _stage_weights(layer, (wg_hbm, wu_hbm, wd_hbm), w_s, w_o, (stage_w, stage_w, stage_d), sem_in, sem_out)

    @pl.when(g == pl.num_programs(0) - 1)
    def _weights_written():
        for copy in _weight_out_copies(w_s, w_o, sem_out):
            copy.wait()

    pieces = _mlstm_pieces(rslot, lax.rem(g - 1, nt) == 0, tm // CHUNK, q_st, kt_st, v_st, og_st, g_st, caug_s, m_s,
                           ym_ref, c_out, n_out, m_out)
    blocks = [_ffn_in_block(x_ref, slice(r0, r0 + FFN_ROWS), wslot, g1_ref, wg_ref, wu_ref, wd_ref, gpost_ref,
                            gmix_ref, wqvo_ref, wktg_ref, wconv_ref, gbias_ref, x1_ref, u_ref, q_st, kt_st, v_st,
                            og_st, g_st) for r0 in range(0, tm, FFN_ROWS)]
    next(pieces)
    while blocks:
        for block in list(blocks):
            if next(block, blocks) is blocks:
                blocks.remove(block)
            else:
                next(pieces, None)
    for _ in pieces:
        pass


def _conv_ffn_out_body(u_ref, ym_ref, x1_ref, cw_ref, cb_ref, lg_ref, lb_ref, wout_ref, gmp_ref, g2_ref,
                       wg_hbm, wu_hbm, wd_hbm, gpost_ref, gfin_ref, y_ref, hist_out, wg_o, wu_o, wd_o, ubuf, cbuf,
                       wg_ref, wu_ref, wd_ref, stage_w, stage_d, sem_in, sem_out, *, tm, nt, layer):
    nslab = CONV_CH // LANES
    w_s, w_o = (wg_ref, wu_ref, wd_ref), (wg_o, wu_o, wd_o)

    @pl.when(pl.program_id(0) == 0)
    def _weights():
        _stage_weights(layer, (wg_hbm, wu_hbm, wd_hbm), w_s, w_o, (stage_w, stage_w, stage_d), sem_in, sem_out)

    @pl.when(pl.program_id(0) == pl.num_programs(0) - 1)
    def _weights_written():
        for copy in _weight_out_copies(w_s, w_o, sem_out):
            copy.wait()

    @pl.when(lax.rem(pl.program_id(0), nt) == 0)
    def _new_sequence():
        ubuf[:, 0:HIST_PAD, :] = jnp.zeros((nslab, HIST_PAD, LANES), F32)

    for lb in range(nslab):
        ubuf[lb, HIST_PAD:HIST_PAD + tm, :] = u_ref[:, lb * LANES:(lb + 1) * LANES]
    for k in range(tm // CONV_ROWS):
        for lb in range(nslab):
            acc = _conv_taps_interleaved(ubuf.at[lb], k * CONV_ROWS, CONV_ROWS, slice(lb * LANES, (lb + 1) * LANES),
                                         cw_ref, cb_ref)
            for parity in range(2):
                cbuf[lb, pl.ds(k * CONV_ROWS + parity, CONV_ROWS // 2, stride=2), :] = acc[parity]
    tail = jnp.concatenate([ubuf[lb, tm:tm + HIST_PAD, :] for lb in range(nslab)], axis=1)
    hist_out[0] = tail
    for lb in range(nslab):
        ubuf[lb, 0:HIST_PAD, :] = tail[:, lb * LANES:(lb + 1) * LANES]

    halves = []
    for r0 in range(0, tm, FFN_ROWS):
        c = []
        for k in range(FFN_ROWS // LN_ROWS):
            rows = slice(r0 + k * LN_ROWS, r0 + (k + 1) * LN_ROWS)
            c.append(_ln_swish(jnp.concatenate([cbuf[lb, rows, :] for lb in range(nslab)], axis=1), lg_ref, lb_ref))
        rows = slice(r0, r0 + FFN_ROWS)
        halves.append(_ffn_out_phases(ym_ref[rows, :], jnp.concatenate(c, axis=0), x1_ref[rows, :], y_ref, rows,
                                      wout_ref, gmp_ref, g2_ref, wg_ref, wu_ref, wd_ref, gpost_ref, gfin_ref))
    for _ in zip(*halves):
        pass


def _const_spec(shape):
    nd = len(shape)
    return pl.BlockSpec(shape, lambda *_: (0,) * nd, pipeline_mode=pl.Buffered(1))


def _ffn_in_consts(p, rows):
    gbias = jnp.broadcast_to(p["gbias"][:, None], (2 * N_HEADS, rows))
    return [p["ffn1_pre_g"], p["ffn1_wg"], p["ffn1_wu"], p["ffn1_wd"], p["ffn1_post_g"], p["mix_pre_g"],
            p["w_qvo"], p["w_ktg"], p["w_conv"], gbias]


def _ffn_out_consts(p):
    return [p["w_out"], p["mix_post_g"], p["ffn2_pre_g"], p["ffn2_wg"], p["ffn2_wu"], p["ffn2_wd"],
            p["ffn2_post_g"], p["final_g"]]


def _weight_spec(c):
    return pl.BlockSpec(memory_space=pl.ANY) if c.ndim == 3 else _const_spec(c.shape)


_WEIGHT_OUT_SHAPES = [jax.ShapeDtypeStruct(s, BF16) for s in ((D_MODEL, D_FF), (D_MODEL, D_FF), (D_FF, D_MODEL))]


def _weight_scratch(dtype):
    return [pltpu.VMEM((D_MODEL, D_FF), BF16), pltpu.VMEM((D_MODEL, D_FF), BF16), pltpu.VMEM((D_FF, D_MODEL), BF16),
            pltpu.VMEM((WEIGHT_STAGE_SLOTS, WEIGHT_STAGE_ROWS, D_FF), dtype),
            pltpu.VMEM((WEIGHT_STAGE_SLOTS, WEIGHT_STAGE_ROWS, D_MODEL), dtype),
            pltpu.SemaphoreType.DMA((WEIGHT_STAGE_SLOTS,)), pltpu.SemaphoreType.DMA((3,))]


def _ffn_in_mlstm(x2d, p, layer, batch, seq, tm):
    m = batch * seq
    nt = seq // tm
    ntiles = m // tm
    cur = lambda w: pl.BlockSpec((tm, w), lambda g: (jnp.minimum(g, ntiles - 1), 0))
    prev = lambda w: pl.BlockSpec((tm, w), lambda g: (jnp.maximum(g - 1, 0), 0))
    prev_seq = lambda g: jnp.maximum(g - 1, 0) // nt
    consts = _ffn_in_consts(p, FFN_ROWS)
    return pl.pallas_call(
        functools.partial(_ffn_in_mlstm_body, tm=tm, nt=nt, layer=layer),
        grid=(ntiles + 1,),
        in_specs=[cur(D_MODEL)] + [_weight_spec(c) for c in consts],
        out_specs=[cur(D_MODEL), cur(CONV_CH), prev(M_WIDTH),
                   pl.BlockSpec((1, N_HEADS, HEAD_DIM, HEAD_DIM), lambda g: (prev_seq(g), 0, 0, 0)),
                   pl.BlockSpec((1, N_HEADS, 1, HEAD_DIM), lambda g: (prev_seq(g), 0, 0, 0)),
                   pl.BlockSpec((1, N_HEADS, 1, HEAD_DIM), lambda g: (prev_seq(g), 0, 0, 0))]
                  + [pl.BlockSpec(memory_space=pl.ANY)] * 3,
        out_shape=[jax.ShapeDtypeStruct((m, D_MODEL), F32), jax.ShapeDtypeStruct((m, CONV_CH), F32),
                   jax.ShapeDtypeStruct((m, M_WIDTH), F32),
                   jax.ShapeDtypeStruct((batch, N_HEADS, HEAD_DIM, HEAD_DIM), F32),
                   jax.ShapeDtypeStruct((batch, N_HEADS, 1, HEAD_DIM), F32),
                   jax.ShapeDtypeStruct((batch, N_HEADS, 1, HEAD_DIM), F32)] + _WEIGHT_OUT_SHAPES,
        scratch_shapes=[pltpu.VMEM((2, tm, M_WIDTH), BF16), pltpu.VMEM((2, M_WIDTH, tm), BF16),
                        pltpu.VMEM((2, tm, M_WIDTH), F32), pltpu.VMEM((2, tm, M_WIDTH), F32),
                        pltpu.VMEM((2, 2 * N_HEADS, tm), F32),
                        pltpu.VMEM((N_HEADS, HEAD_DIM, AUG), F32), pltpu.VMEM((N_HEADS, 1, HEAD_DIM), F32)]
                       + _weight_scratch(p["ffn1_wg"].dtype),
        compiler_params=pltpu.CompilerParams(dimension_semantics=("arbitrary",),
                                             vmem_limit_bytes=VMEM_LIMIT_TOKENWISE),
        name="ffn_in_mlstm",
    )(x2d, *consts)


def _conv_ffn_out(u, ym, x1, p, layer, batch, seq, tm):
    m = batch * seq
    nt = seq // tm
    row = lambda w: pl.BlockSpec((tm, w), lambda g: (g, 0))
    consts = [p["conv_w"], p["conv_b"], p["conv_ln_g"], p["conv_ln_b"]] + _ffn_out_consts(p)
    return pl.pallas_call(
        functools.partial(_conv_ffn_out_body, tm=tm, nt=nt, layer=layer),
        grid=(m // tm,),
        in_specs=[row(CONV_CH), row(M_WIDTH), row(D_MODEL)] + [_weight_spec(c) for c in consts],
        out_specs=[row(D_MODEL), pl.BlockSpec((1, HIST_PAD, CONV_CH), lambda g: (g // nt, 0, 0))]
                  + [pl.BlockSpec(memory_space=pl.ANY)] * 3,
        out_shape=[jax.ShapeDtypeStruct((m, D_MODEL), F32), jax.ShapeDtypeStruct((batch, HIST_PAD, CONV_CH), F32)]
                  + _WEIGHT_OUT_SHAPES,
        scratch_shapes=[pltpu.VMEM((CONV_CH // LANES, HIST_PAD + tm, LANES), F32),
                        pltpu.VMEM((CONV_CH // LANES, tm, LANES), F32)] + _weight_scratch(p["ffn2_wg"].dtype),
        compiler_params=pltpu.CompilerParams(dimension_semantics=("arbitrary",),
                                             vmem_limit_bytes=VMEM_LIMIT_TOKENWISE),
        name="conv_ffn_out",
    )(u, ym, x1, *consts)


def _ffn_in_body(x_ref, g1_ref, wg_ref, wu_ref, wd_ref, gpost_ref, gmix_ref, wqvo_ref, wktg_ref, wconv_ref,
                 gbias_ref, x1_ref, q_ref, kt_ref, v_ref, og_ref, u_ref, gt_ref):
    outs = _ffn_in_values(x_ref[...], g1_ref, wg_ref, wu_ref, wd_ref, gpost_ref, gmix_ref, wqvo_ref, wktg_ref,
                          wconv_ref, gbias_ref)
    for ref, val in zip((x1_ref, q_ref, kt_ref, v_ref, og_ref, u_ref, gt_ref), outs):
        ref[...] = val


def _ffn_out_body(mix_ref, x1_ref, wout_ref, gmp_ref, g2_ref, wg_ref, wu_ref, wd_ref, gpost_ref, gfin_ref, y_ref):
    y_ref[...] = _ffn_out_values(mix_ref[:, :M_WIDTH], mix_ref[:, M_WIDTH:], x1_ref[...], wout_ref, gmp_ref, g2_ref,
                                 wg_ref, wu_ref, wd_ref, gpost_ref, gfin_ref)


def _mixer_sample_body(q_ref, kt_ref, v_ref, og_ref, u_ref, gt_ref, c0_ref, n0_ref, m0_ref, hist0_ref,
                       cw_ref, cb_ref, lg_ref, lb_ref, mix_ref, caug_out, m_out, hist_out, ubuf, *, nseq, L):
    R = nseq * L
    r = lax.broadcasted_iota(jnp.int32, (R, R), 0)
    c = lax.broadcasted_iota(jnp.int32, (R, R), 1)
    same = lax.div(r, L) == lax.div(c, L)
    caus = same & (c <= r)
    segtriu = jnp.where(same & (r <= c), 1.0, 0.0).astype(F32)
    segones = jnp.where(same, 1.0, 0.0).astype(F32)
    rowseq = lax.div(lax.broadcasted_iota(jnp.int32, (R, AUG), 0), L)
    ones = jnp.ones((R, HEAD_DIM), F32)
    gt = gt_ref[...]
    b_rows = jnp.dot(gt, segtriu, precision=HIGHEST, preferred_element_type=F32)
    tot_rows = jnp.dot(gt, segones, precision=HIGHEST, preferred_element_type=F32)
    for h in range(N_HEADS):
        hs = slice(h * HEAD_DIM, (h + 1) * HEAD_DIM)
        b_r = b_rows[N_HEADS + h:N_HEADS + h + 1, :]
        a_r = gt[h:h + 1, :] - b_r
        t_r = tot_rows[N_HEADS + h:N_HEADS + h + 1, :]
        arow = jnp.broadcast_to(a_r, (R, R))
        bcol = jnp.broadcast_to(b_r, (R, R)).T
        acol = arow.T
        tcol = jnp.broadcast_to(t_r, (R, R)).T
        mprev = m0_ref[h]
        mprev = jnp.concatenate([mprev, mprev], axis=1)
        dmat = jnp.where(caus, bcol + arow, -jnp.inf)
        m_t = jnp.maximum(bcol + mprev, jnp.max(dmat, axis=1, keepdims=True))
        dend = jnp.where(same, tcol + arow, -jnp.inf)
        m_new = jnp.maximum(tcol + mprev, jnp.max(dend, axis=1, keepdims=True))
        qh = q_ref[:, hs].astype(BF16)
        kth = kt_ref[hs, :].astype(BF16)
        s = _dot(qh, kth) * jnp.exp(dmat - m_t)
        vaug = jnp.concatenate([v_ref[:, hs], ones], axis=1)
        w_inter = jnp.exp(bcol + mprev - m_t)
        caug0 = [jnp.concatenate([c0_ref[0, i, h], jnp.broadcast_to(n0_ref[0, i, h:h + 1, :], (HEAD_DIM, HEAD_DIM)).T],
                                 axis=1) for i in range(nseq)]
        qc = jnp.zeros((R, AUG), F32)
        for i in range(nseq):
            qc = jnp.where(rowseq == i, _dot(qh, caug0[i].astype(BF16)), qc)
        sv = _dot(s.astype(BF16), vaug.astype(BF16))
        num = w_inter[:, :HEAD_DIM] * qc[:, :HEAD_DIM] + sv[:, :HEAD_DIM]
        den = w_inter[:, :HEAD_DIM] * qc[:, HEAD_DIM:] + sv[:, HEAD_DIM:]
        hh = num / jnp.maximum(jnp.abs(den), jnp.exp(-m_t[:, :HEAD_DIM]))
        mix_ref[:, hs] = og_ref[:, hs] * hh
        g_state = jnp.exp(tcol + mprev - m_new)
        g_rows = jnp.exp(tcol + acol - m_new)
        gv = g_rows * vaug
        for i in range(nseq):
            gvi = jnp.where(rowseq == i, gv, 0.0).astype(BF16)
            caug_out[i, h] = g_state[i * L:i * L + 1, :] * caug0[i] + _dot(kth, gvi)
            m_out[i, h] = m_new[i * L:i * L + 1, :HEAD_DIM]

    for i in range(nseq):
        ubuf[0:HIST_PAD, :] = hist0_ref[i]
        ubuf[HIST_PAD:HIST_PAD + L, :] = u_ref[i * L:(i + 1) * L, :]
        acc = _conv_taps(ubuf, 0, L, slice(0, CONV_CH), cw_ref, cb_ref)
        mix_ref[i * L:(i + 1) * L, M_WIDTH:] = _ln_swish(acc, lg_ref, lb_ref)
        hist_out[i] = ubuf[L:L + HIST_PAD, :]


def _ffn_in(x2d, p, tm):
    m = x2d.shape[0]
    row = lambda w: pl.BlockSpec((tm, w), lambda i: (i, 0))
    col = lambda h: pl.BlockSpec((h, tm), lambda i: (0, i))
    consts = _ffn_in_consts(p, tm)
    return pl.pallas_call(
        _ffn_in_body,
        grid=(m // tm,),
        in_specs=[row(D_MODEL)] + [_const_spec(c.shape) for c in consts],
        out_specs=[row(D_MODEL), row(M_WIDTH), col(M_WIDTH), row(M_WIDTH), row(M_WIDTH), row(CONV_CH),
                   col(2 * N_HEADS)],
        out_shape=[jax.ShapeDtypeStruct((m, D_MODEL), F32), jax.ShapeDtypeStruct((m, M_WIDTH), F32),
                   jax.ShapeDtypeStruct((M_WIDTH, m), F32), jax.ShapeDtypeStruct((m, M_WIDTH), F32),
                   jax.ShapeDtypeStruct((m, M_WIDTH), F32), jax.ShapeDtypeStruct((m, CONV_CH), F32),
                   jax.ShapeDtypeStruct((2 * N_HEADS, m), F32)],
        compiler_params=pltpu.CompilerParams(dimension_semantics=("arbitrary",),
                                             vmem_limit_bytes=VMEM_LIMIT_TOKENWISE),
        name="ffn_in",
    )(x2d, *consts)


def _ffn_out(mix, x1, p, tm):
    m = mix.shape[0]
    row = pl.BlockSpec((tm, D_MODEL), lambda i: (i, 0))
    consts = _ffn_out_consts(p)
    return pl.pallas_call(
        _ffn_out_body,
        grid=(m // tm,),
        in_specs=[row, row] + [_const_spec(c.shape) for c in consts],
        out_specs=row,
        out_shape=jax.ShapeDtypeStruct((m, D_MODEL), F32),
        compiler_params=pltpu.CompilerParams(dimension_semantics=("arbitrary",),
                                             vmem_limit_bytes=VMEM_LIMIT_TOKENWISE),
        name="ffn_out",
    )(mix, x1, *consts)


def _mixer_sample(q, kt, v, og, u, gt, c_all, n_all, layer, m0, hist0, p, nseq, L):
    rows = nseq * L
    args = [q, kt, v, og, u, gt, c_all, n_all, m0, hist0, p["conv_w"], p["conv_b"], p["conv_ln_g"], p["conv_ln_b"]]
    full = lambda a: pl.BlockSpec(a.shape, lambda i, nd=a.ndim: (0,) * nd)
    of_layer = lambda a: pl.BlockSpec((1,) + a.shape[1:], lambda i, nd=a.ndim: (layer,) + (0,) * (nd - 1))
    out_shape = [jax.ShapeDtypeStruct((rows, D_MODEL), F32),
                 jax.ShapeDtypeStruct((nseq, N_HEADS, HEAD_DIM, AUG), F32),
                 jax.ShapeDtypeStruct((nseq, N_HEADS, 1, HEAD_DIM), F32),
                 jax.ShapeDtypeStruct((nseq, HIST_PAD, CONV_CH), F32)]
    return pl.pallas_call(
        functools.partial(_mixer_sample_body, nseq=nseq, L=L),
        grid=(1,),
        in_specs=[of_layer(a) if a is c_all or a is n_all else full(a) for a in args],
        out_specs=[full(s) for s in out_shape],
        out_shape=out_shape,
        scratch_shapes=[pltpu.VMEM((HIST_PAD + L, CONV_CH), F32)],
        compiler_params=pltpu.CompilerParams(dimension_semantics=("arbitrary",),
                                             vmem_limit_bytes=VMEM_LIMIT_MIXER),
        name="mixer_sample",
    )(*args)


def _layer_params(l, ffn1_pre_g, ffn1_wg, ffn1_wu, ffn1_wd, ffn1_post_g, mix_pre_g, w_in, b_igate, b_fgate,
                  conv_w, conv_b, conv_ln_g, conv_ln_b, w_out, mix_post_g, ffn2_pre_g, ffn2_wg, ffn2_wu, ffn2_wd,
                  ffn2_post_g, final_g):
    vec = lambda a: a[l].astype(F32).reshape(1, -1)
    w = w_in[l]
    cuts = [0, M_WIDTH, 2 * M_WIDTH, 3 * M_WIDTH, 4 * M_WIDTH, 4 * M_WIDTH + N_HEADS, 4 * M_WIDTH + 2 * N_HEADS,
            4 * M_WIDTH + 2 * N_HEADS + CONV_CH, 4 * M_WIDTH + 2 * N_HEADS + 2 * CONV_CH]
    wq, wk, wv, wo, wi, wf, wcv, wcg = [w[:, a:b] for a, b in zip(cuts[:-1], cuts[1:])]
    return {
        "ffn1_pre_g": vec(ffn1_pre_g), "ffn1_post_g": vec(ffn1_post_g), "mix_pre_g": vec(mix_pre_g),
        "mix_post_g": vec(mix_post_g), "ffn2_pre_g": vec(ffn2_pre_g), "ffn2_post_g": vec(ffn2_post_g),
        "final_g": vec(final_g),
        "ffn1_wg": ffn1_wg, "ffn1_wu": ffn1_wu, "ffn1_wd": ffn1_wd,
        "ffn2_wg": ffn2_wg, "ffn2_wu": ffn2_wu, "ffn2_wd": ffn2_wd,
        "w_qvo": jnp.concatenate([wq, wv, wo], axis=1).astype(BF16),
        "w_ktg": jnp.concatenate([wk, wi, wf], axis=1).T.astype(BF16),
        "w_conv": jnp.concatenate([wcv, wcg], axis=1).astype(BF16),
        "gbias": jnp.concatenate([b_igate[l], b_fgate[l]]).astype(F32),
        "w_out": w_out[l].astype(BF16),
        "conv_w": conv_w[l].astype(F32), "conv_b": vec(conv_b), "conv_ln_g": vec(conv_ln_g),
        "conv_ln_b": vec(conv_ln_b),
    }


def kernel(x_prompt, x_sample, state_mlstm_C, state_mlstm_n, state_mlstm_m, cache_conv, ffn1_pre_g, ffn1_wg,
           ffn1_wu, ffn1_wd, ffn1_post_g, mix_pre_g, w_in, b_igate, b_fgate, conv_w, conv_b, conv_ln_g, conv_ln_b,
           w_out, mix_post_g, ffn2_pre_g, ffn2_wg, ffn2_wu, ffn2_wd, ffn2_post_g, final_g):
    batch, seq, _ = x_prompt.shape
    nseq, dseq, _ = x_sample.shape
    depth = w_in.shape[0]
    assert seq % PROMPT_TILE == 0 and PROMPT_TILE % FFN_ROWS == 0
    assert FFN_ROWS % CHUNK == 0 and FFN_ROWS % CONV_ROWS == 0 and PROMPT_TILE >= HIST_PAD
    assert nseq * dseq == SAMPLE_TILE and dseq <= HIST_PAD
    yp = x_prompt.reshape(batch * seq, D_MODEL)
    ys = x_sample.reshape(nseq * dseq, D_MODEL)
    outs_p, outs_s = [], []
    for l in range(depth):
        p = _layer_params(l, ffn1_pre_g, ffn1_wg, ffn1_wu, ffn1_wd, ffn1_post_g, mix_pre_g, w_in, b_igate, b_fgate,
                          conv_w, conv_b, conv_ln_g, conv_ln_b, w_out, mix_post_g, ffn2_pre_g, ffn2_wg, ffn2_wu,
                          ffn2_wd, ffn2_post_g, final_g)
        x1, u, ym, c, n, m, *ffn1_w = _ffn_in_mlstm(yp, p, l, batch, seq, PROMPT_TILE)
        yp, hist, *ffn2_w = _conv_ffn_out(u, ym, x1, p, l, batch, seq, PROMPT_TILE)
        p.update(zip(("ffn1_wg", "ffn1_wu", "ffn1_wd", "ffn2_wg", "ffn2_wu", "ffn2_wd"), ffn1_w + ffn2_w))
        outs_p.append((c, n[:, :, 0, :], m[:, :, 0, 0], hist[:, HIST_PAD - HIST:, :]))
        x1, q, kt, v, og, u, gt = _ffn_in(ys, p, SAMPLE_TILE)
        m0 = jnp.broadcast_to(state_mlstm_m[l].astype(F32).T[:, :, None, None], (N_HEADS, nseq, dseq, HEAD_DIM))
        m0 = m0.reshape(N_HEADS, nseq * dseq, HEAD_DIM)
        hist0 = jnp.pad(cache_conv[l].astype(F32), ((0, 0), (HIST_PAD - HIST, 0), (0, 0)))
        mix, caug, m, hist = _mixer_sample(q, kt, v, og, u, gt, state_mlstm_C.astype(F32), state_mlstm_n.astype(F32),
                                           l, m0, hist0, p, nseq, dseq)
        ys = _ffn_out(mix, x1, p, SAMPLE_TILE)
        outs_s.append((caug[..., :HEAD_DIM], caug[..., HEAD_DIM], m[:, :, 0, 0], hist[:, HIST_PAD - HIST:, :]))
    stack = lambda outs, k: jnp.stack([o[k] for o in outs])
    return (yp.reshape(batch, seq, D_MODEL), ys.reshape(nseq, dseq, D_MODEL),
            stack(outs_p, 0), stack(outs_p, 1), stack(outs_p, 2), stack(outs_p, 3),
            stack(outs_s, 0), stack(outs_s, 1), stack(outs_s, 2), stack(outs_s, 3))
```
